```python
import jax
import jax.numpy as jnp
from jax import lax
import numpy as np

D_MODEL = 1024
BATCH = 4
SEQ = 4096
DEPTH = 2
DEC_BATCH = 128
DEC_SEQ = 8
PAST_LEN = 16384
PAGE_SIZE = 128

RET_HEADS = 8
RET_DK = 64
RET_DV = 128
RET_CHUNK = 128
SWA_HEADS = 8
SWA_KV_HEADS = 2
SWA_GROUP = SWA_HEADS // SWA_KV_HEADS
SWA_HD = 64
WINDOW = 128
MEM_HEADS = 4
MEM_HD = 128
N_MEM = 256
D_FF = 4 * D_MODEL
ROPE_THETA = 10000.0
LN_EPS = 1e-5
GN_EPS = 1e-5
ALPHA = (2 * DEPTH) ** 0.25
BETA = (8 * DEPTH) ** -0.25

RET_QK_W = RET_HEADS * RET_DK
RET_V_W = RET_HEADS * RET_DV
SWA_Q_W = SWA_HEADS * SWA_HD
SWA_KV_W = SWA_KV_HEADS * SWA_HD
MEM_W = MEM_HEADS * MEM_HD
IN_WIDTHS = (RET_QK_W, RET_QK_W, RET_V_W, RET_V_W, SWA_Q_W, SWA_KV_W, SWA_KV_W, MEM_W, D_MODEL, D_MODEL, D_MODEL)
IN_IS_VALUE = (False, False, True, False, False, False, True, False, False, False, False)
IN_W = sum(IN_WIDTHS)
IN_OFFSETS = [int(o) for o in np.cumsum(IN_WIDTHS)[:-1]]

kernel_name = 'hybrid_retention_sinkswa_memxattn_step'

F32 = jnp.float32


def layer_norm(x, g, b):
    xf = x.astype(F32)
    mu = jnp.mean(xf, -1, keepdims=True)
    var = jnp.mean(jnp.square(xf - mu), -1, keepdims=True)
    return ((xf - mu) * lax.rsqrt(var + LN_EPS) * g.astype(F32) + b.astype(F32)).astype(x.dtype)


def rope(x, pos):
    half = x.shape[-1] // 2
    inv = jnp.power(ROPE_THETA, -jnp.arange(half, dtype=F32) / half)
    ang = pos.astype(F32)[:, None] * inv[None, :]
    cos = jnp.cos(ang)[:, None, :]
    sin = jnp.sin(ang)[:, None, :]
    xf = x.astype(F32)
    x1, x2 = xf[..., :half], xf[..., half:]
    return jnp.concatenate([x1 * cos - x2 * sin, x2 * cos + x1 * sin], -1).astype(x.dtype)


def ret_log_gamma():
    return jnp.log1p(-jnp.exp2(-5.0 - jnp.arange(RET_HEADS, dtype=F32)))


def retention_chunk(q, k, v, s0):
    L = q.shape[1]
    lg = ret_log_gamma()
    idx = jnp.arange(L, dtype=F32)
    diff = idx[:, None] - idx[None, :]
    decay = jnp.where(diff >= 0, jnp.exp(lg[:, None, None] * jnp.maximum(diff, 0.0)), 0.0)
    qf = q.astype(F32)
    kf = k.astype(F32) * (RET_DK ** -0.5)
    vf = v.astype(F32)
    s0 = s0.astype(F32)
    scores = jnp.einsum('bihd,bjhd->bhij', qf, kf) * decay
    o = jnp.einsum('bhij,bjhe->bihe', scores, vf)
    o = o + jnp.einsum('bihd,bhde->bihe', qf, s0) * jnp.exp(lg[None, :] * (idx[:, None] + 1.0))[None, :, :, None]
    w_end = jnp.exp(lg[None, :] * (L - 1.0 - idx[:, None]))
    s_new = jnp.exp(lg * L)[None, :, None, None] * s0 + jnp.einsum('bjhd,bjhe,jh->bhde', kf, vf, w_end)
    return o, s_new


def retention_prompt(q, k, v):
    B, S = q.shape[:2]
    n = S // RET_CHUNK

    def blocks(t):
        return jnp.moveaxis(t.reshape(B, n, RET_CHUNK, *t.shape[2:]), 1, 0)

    def step(s, qkv):
        o, s = retention_chunk(qkv[0], qkv[1], qkv[2], s)
        return s, o

    s0 = jnp.zeros((B, RET_HEADS, RET_DK, RET_DV), F32)
    s_end, o = lax.scan(step, s0, (blocks(q), blocks(k), blocks(v)))
    return jnp.moveaxis(o, 0, 1).reshape(B, S, RET_HEADS, RET_DV), s_end


def group_norm(o, g):
    mu = jnp.mean(o, -1, keepdims=True)
    var = jnp.mean(jnp.square(o - mu), -1, keepdims=True)
    return (o - mu) * lax.rsqrt(var + GN_EPS) * g.astype(F32)


def band_mask(tq, tk, off):
    i = jnp.arange(tq)[:, None]
    j = jnp.arange(tk)[None, :]
    return (j <= i + off) & (j > i + off - WINDOW)


def sink_window_attend(q, k, v, valid, sinks):
    s = jnp.einsum('bnqhgd,bnkhd->bnhgqk', q.astype(F32), k.astype(F32)) * (SWA_HD ** -0.5)
    s = jnp.where(valid[None, :, None, None], s, -jnp.inf)
    sink = jnp.broadcast_to(sinks.astype(F32).reshape(SWA_KV_HEADS, SWA_GROUP)[None, None, :, :, None, None],
                            s.shape[:-1] + (1,))
    p = jax.nn.softmax(jnp.concatenate([s, sink], -1), axis=-1)[..., :-1]
    return jnp.einsum('bnhgqk,bnkhd->bnqhgd', p, v.astype(F32))


def swa_prompt(q, k, v, sinks):
    B, S = q.shape[:2]
    n = S // WINDOW
    qb = q.reshape(B, n, WINDOW, SWA_KV_HEADS, SWA_GROUP, SWA_HD)
    kb = k.reshape(B, n, WINDOW, SWA_KV_HEADS, SWA_HD)
    vb = v.reshape(B, n, WINDOW, SWA_KV_HEADS, SWA_HD)

    def with_prev(t):
        prev = jnp.concatenate([jnp.zeros_like(t[:, :1]), t[:, :-1]], axis=1)
        return jnp.concatenate([prev, t], axis=2)

    j = jnp.arange(2 * WINDOW)
    not_pad = (jnp.arange(n)[:, None, None] > 0) | (j >= WINDOW)[None, None, :]
    valid = band_mask(WINDOW, 2 * WINDOW, WINDOW)[None] & not_pad
    o = sink_window_attend(qb, with_prev(kb), with_prev(vb), valid, sinks)
    return o.reshape(B, S, SWA_KV_HEADS, SWA_GROUP, SWA_HD)


def swa_sample(q, k_ctx, v_ctx, sinks):
    T = q.shape[1]
    lb = k_ctx.shape[1] - T
    valid = band_mask(T, lb + T, lb)[None]
    o = sink_window_attend(q[:, None], k_ctx[:, None], v_ctx[:, None], valid, sinks)
    return o[:, 0]


def mem_kv(mem, w):
    B, M, _ = mem.shape
    k, v = jnp.split(mem @ w, 2, axis=-1)
    return k.reshape(B, M, MEM_HEADS, MEM_HD), v.reshape(B, M, MEM_HEADS, MEM_HD)


def mem_attend(q, km, vm):
    s = jnp.einsum('bthd,bmhd->bhtm', q.astype(F32), km.astype(F32)) * (MEM_HD ** -0.5)
    p = jax.nn.softmax(s, axis=-1)
    return jnp.einsum('bhtm,bmhd->bthd', p, vm.astype(F32))


def in_proj(x, w_in, pos):
    B, T, _ = x.shape
    rq, rk, rv, rg, sq, sk, sv, mq, g_ret, g_swa, g_mem = jnp.split(x @ w_in, IN_OFFSETS, axis=-1)
    rq = rope(rq.reshape(B, T, RET_HEADS, RET_DK), pos)
    rk = rope(rk.reshape(B, T, RET_HEADS, RET_DK), pos)
    rv = rv.reshape(B, T, RET_HEADS, RET_DV)
    sq = rope(sq.reshape(B, T, SWA_HEADS, SWA_HD), pos).reshape(B, T, SWA_KV_HEADS, SWA_GROUP, SWA_HD)
    sk = rope(sk.reshape(B, T, SWA_KV_HEADS, SWA_HD), pos)
    sv = sv.reshape(B, T, SWA_KV_HEADS, SWA_HD)
    mq = mq.reshape(B, T, MEM_HEADS, MEM_HD)
    return rq, rk, rv, rg, sq, sk, sv, mq, g_ret, g_swa, g_mem


def finish_layer(x, ret_o, rg, g_ret, swa_o, g_swa, mem_o, g_mem,
                 gn_g, w_br_ret, w_br_swa, w_br_mem, w_out, ln1_g, ln1_b, w_up, w_down, ln2_g, ln2_b):
    B, T, _ = x.shape
    dt = x.dtype
    ret_b = (jax.nn.silu(rg.astype(F32)) * group_norm(ret_o, gn_g).reshape(B, T, RET_V_W)).astype(dt) @ w_br_ret
    swa_b = swa_o.reshape(B, T, SWA_Q_W).astype(dt) @ w_br_swa
    mem_b = mem_o.reshape(B, T, MEM_W).astype(dt) @ w_br_mem
    merged = jax.nn.sigmoid(g_ret) * ret_b + jax.nn.sigmoid(g_swa) * swa_b + jax.nn.sigmoid(g_mem) * mem_b
    x = layer_norm(ALPHA * x + merged @ w_out, ln1_g, ln1_b)
    h = jnp.square(jax.nn.relu(x @ w_up))
    return layer_norm(ALPHA * x + h @ w_down, ln2_g, ln2_b)


def setup_inputs(seed: int = 0) -> dict:
    key = jax.random.key(seed)
    ks = jax.random.split(key, 24)

    def nrm(k, shape, s=1.0):
        return jax.random.normal(k, shape, F32) * s

    buf = min(WINDOW, PAST_LEN)
    col_scale = jnp.asarray(np.concatenate(
        [np.full((w,), BETA if isv else 1.0, np.float32) for w, isv in zip(IN_WIDTHS, IN_IS_VALUE)]))
    mem_scale = jnp.asarray(np.concatenate([np.ones((MEM_W,), np.float32), np.full((MEM_W,), BETA, np.float32)]))
    return {
        'x_prompt': nrm(ks[0], (BATCH, SEQ, D_MODEL)),
        'x_sample': nrm(ks[1], (DEC_BATCH, DEC_SEQ, D_MODEL)),
        'state_ret': nrm(ks[2], (DEPTH, DEC_BATCH, RET_HEADS, RET_DK, RET_DV), 0.5),
        'cache_swa_k': nrm(ks[3], (DEPTH, DEC_BATCH, buf, SWA_KV_HEADS, SWA_HD)),
        'cache_swa_v': nrm(ks[4], (DEPTH, DEC_BATCH, buf, SWA_KV_HEADS, SWA_HD), BETA),
        'cache_mem_k': nrm(ks[5], (DEPTH, DEC_BATCH, N_MEM, MEM_HEADS, MEM_HD)),
        'cache_mem_v': nrm(ks[6], (DEPTH, DEC_BATCH, N_MEM, MEM_HEADS, MEM_HD), BETA),
        'mem_prompt': nrm(ks[7], (BATCH, N_MEM, D_MODEL)),
        'w_in': nrm(ks[8], (DEPTH, D_MODEL, IN_W), D_MODEL ** -0.5) * col_scale,
        'w_br_ret': nrm(ks[9], (DEPTH, RET_V_W, D_MODEL), RET_V_W ** -0.5 * BETA),
        'w_br_swa': nrm(ks[10], (DEPTH, SWA_Q_W, D_MODEL), SWA_Q_W ** -0.5 * BETA),
        'w_br_mem': nrm(ks[11], (DEPTH, MEM_W, D_MODEL), MEM_W ** -0.5 * BETA),
        'w_out': nrm(ks[12], (DEPTH, D_MODEL, D_MODEL), D_MODEL ** -0.5 * BETA),
        'w_mem_kv': nrm(ks[13], (DEPTH, D_MODEL, 2 * MEM_W), D_MODEL ** -0.5) * mem_scale,
        'attn_sinks': nrm(ks[14], (DEPTH, SWA_HEADS), 0.5),
        'ret_gn_g': 1.0 + nrm(ks[15], (DEPTH, RET_HEADS, RET_DV), 0.02),
        'ln1_g': 1.0 + nrm(ks[16], (DEPTH, D_MODEL), 0.02),
        'ln1_b': nrm(ks[17], (DEPTH, D_MODEL), 0.02),
        'w_up': nrm(ks[18], (DEPTH, D_MODEL, D_FF), D_MODEL ** -0.5),
        'w_down': nrm(ks[19], (DEPTH, D_FF, D_MODEL), D_FF ** -0.5 * BETA),
        'ln2_g': 1.0 + nrm(ks[20], (DEPTH, D_MODEL), 0.02),
        'ln2_b': nrm(ks[21], (DEPTH, D_MODEL), 0.02),
    }


def reference(x_prompt, x_sample, state_ret, cache_swa_k, cache_swa_v, cache_mem_k, cache_mem_v, mem_prompt,
              w_in, w_br_ret, w_br_swa, w_br_mem, w_out, w_mem_kv, attn_sinks, ret_gn_g,
              ln1_g, ln1_b, w_up, w_down, ln2_g, ln2_b):
    pos_p = jnp.arange(x_prompt.shape[1], dtype=jnp.int32)
    pos_s = PAST_LEN + jnp.arange(x_sample.shape[1], dtype=jnp.int32)
    xp, xs = x_prompt, x_sample
    ret_p, swk_p, swv_p, mk_p, mv_p = [], [], [], [], []
    ret_s, swk_s, swv_s = [], [], []
    for l in range(DEPTH):
        lw = (ret_gn_g[l], w_br_ret[l], w_br_swa[l], w_br_mem[l], w_out[l],
              ln1_g[l], ln1_b[l], w_up[l], w_down[l], ln2_g[l], ln2_b[l])
        rq, rk, rv, rg, sq, sk, sv, mq, g_r, g_s, g_m = in_proj(xp, w_in[l], pos_p)
        ret_o, s_p = retention_prompt(rq, rk, rv)
        swa_o = swa_prompt(sq, sk, sv, attn_sinks[l])
        mk, mv = mem_kv(mem_prompt, w_mem_kv[l])
        mem_o = mem_attend(mq, mk, mv)
        xp = finish_layer(xp, ret_o, rg, g_r, swa_o, g_s, mem_o, g_m, *lw)
        nbp = min(WINDOW, sk.shape[1])
        ret_p.append(s_p)
        swk_p.append(sk[:, -nbp:])
        swv_p.append(sv[:, -nbp:])
        mk_p.append(mk)
        mv_p.append(mv)
        rq, rk, rv, rg, sq, sk, sv, mq, g_r, g_s, g_m = in_proj(xs, w_in[l], pos_s)
        ret_o, s_s = retention_chunk(rq, rk, rv, state_ret[l])
        k_ctx = jnp.concatenate([cache_swa_k[l].astype(sk.dtype), sk], axis=1)
        v_ctx = jnp.concatenate([cache_swa_v[l].astype(sv.dtype), sv], axis=1)
        swa_o = swa_sample(sq, k_ctx, v_ctx, attn_sinks[l])
        mem_o = mem_attend(mq, cache_mem_k[l], cache_mem_v[l])
        xs = finish_layer(xs, ret_o, rg, g_r, swa_o, g_s, mem_o, g_m, *lw)
        nbs = min(WINDOW, k_ctx.shape[1])
        ret_s.append(s_s)
        swk_s.append(k_ctx[:, -nbs:])
        swv_s.append(v_ctx[:, -nbs:])
    return (xp, xs, jnp.stack(ret_p), jnp.stack(swk_p), jnp.stack(swv_p), jnp.stack(mk_p), jnp.stack(mv_p),
            jnp.stack(ret_s), jnp.stack(swk_s), jnp.stack(swv_s))
```

```python
import functools

import jax
import jax.numpy as jnp
from jax import lax
from jax.experimental import pallas as pl
from jax.experimental.pallas import tpu as pltpu

F32 = jnp.float32
BF16 = jnp.bfloat16

D_MODEL = 1024
DEPTH = 2
PAST_LEN = 16384
RET_HEADS = 8
RET_DK = 64
RET_DV = 128
RET_CHUNK = 128
SWA_HEADS = 8
SWA_KV_HEADS = 2
SWA_GROUP = SWA_HEADS // SWA_KV_HEADS
SWA_HD = 64
WINDOW = 128
MEM_HEADS = 4
MEM_HD = 128
N_MEM = 256
D_FF = 4 * D_MODEL
ROPE_THETA = 10000.0
LN_EPS = 1e-5
GN_EPS = 1e-5
ALPHA = (2 * DEPTH) ** 0.25

RET_QK_W = RET_HEADS * RET_DK
RET_V_W = RET_HEADS * RET_DV
SWA_Q_W = SWA_HEADS * SWA_HD
SWA_KV_W = SWA_KV_HEADS * SWA_HD
MEM_W = MEM_HEADS * MEM_HD
OFF_RQ = 0
OFF_RK = OFF_RQ + RET_QK_W
OFF_RV = OFF_RK + RET_QK_W
OFF_RG = OFF_RV + RET_V_W
OFF_SQ = OFF_RG + RET_V_W
OFF_SK = OFF_SQ + SWA_Q_W
OFF_SV = OFF_SK + SWA_KV_W
OFF_MQ = OFF_SV + SWA_KV_W
OFF_GR = OFF_MQ + MEM_W
OFF_GS = OFF_GR + D_MODEL
OFF_GM = OFF_GS + D_MODEL
IN_W = OFF_GM + D_MODEL

LANES = 128
V7X_VMEM_LIMIT = 56 * 1024 * 1024


def _const_spec(shape):
    nd = len(shape)
    return pl.BlockSpec(shape, lambda *_: (0,) * nd, pipeline_mode=pl.Buffered(1))


def _params(n_grid):
    return pltpu.CompilerParams(dimension_semantics=("arbitrary",) * n_grid, vmem_limit_bytes=V7X_VMEM_LIMIT)


def _inproj_kernel(x_ref, w_ref, cos_ref, sin_ref,
                   rq_ref, rk_ref, rv_ref, rg_ref, sq_ref, sk_ref, sv_ref, mq_ref, gr_ref, gs_ref, gm_ref):
    xb = x_ref[...].astype(BF16)
    cos = cos_ref[...]
    sin = sin_ref[...]
    lane = lax.broadcasted_iota(jnp.int32, cos.shape, 1)
    first_half = (lane & (SWA_HD // 2)) == 0

    def proj(off, width):
        return jnp.dot(xb, w_ref[:, off:off + width], preferred_element_type=F32)

    def rope_store(off, width, out_ref, scale):
        y = proj(off, width)
        for j in range(width // LANES):
            yj = y[:, j * LANES:(j + 1) * LANES]
            sw = jnp.where(first_half, pltpu.roll(yj, LANES - SWA_HD // 2, 1), pltpu.roll(yj, SWA_HD // 2, 1))
            r = yj * cos + sw * sin
            if scale != 1.0:
                r = r * scale
            out_ref[:, j * LANES:(j + 1) * LANES] = r.astype(out_ref.dtype)

    def plain_store(off, width, out_ref):
        out_ref[...] = proj(off, width).astype(out_ref.dtype)

    rope_store(OFF_RQ, RET_QK_W, rq_ref, 1.0)
    rope_store(OFF_RK, RET_QK_W, rk_ref, RET_DK ** -0.5)
    plain_store(OFF_RV, RET_V_W, rv_ref)
    plain_store(OFF_RG, RET_V_W, rg_ref)
    rope_store(OFF_SQ, SWA_Q_W, sq_ref, SWA_HD ** -0.5)
    rope_store(OFF_SK, SWA_KV_W, sk_ref, 1.0)
    plain_store(OFF_SV, SWA_KV_W, sv_ref)
    plain_store(OFF_MQ, MEM_W, mq_ref)
    plain_store(OFF_GR, D_MODEL, gr_ref)
    plain_store(OFF_GS, D_MODEL, gs_ref)
    plain_store(OFF_GM, D_MODEL, gm_ref)


def _inproj(x2d, w_bf, cos_tab, sin_tab, tm, qkv_dtype):
    m = x2d.shape[0]
    n_tab = cos_tab.shape[0] // tm
    row = lambda w: pl.BlockSpec((tm, w), lambda i: (i, 0))
    tab = pl.BlockSpec((tm, LANES), lambda i: (i % n_tab, 0))
    widths_dtypes = [(RET_QK_W, qkv_dtype), (RET_QK_W, qkv_dtype), (RET_V_W, qkv_dtype), (RET_V_W, F32),
                     (SWA_Q_W, qkv_dtype), (SWA_KV_W, F32), (SWA_KV_W, F32), (MEM_W, qkv_dtype),
                     (D_MODEL, F32), (D_MODEL, F32), (D_MODEL, F32)]
    return pl.pallas_call(
        _inproj_kernel,
        grid=(m // tm,),
        in_specs=[row(D_MODEL), _const_spec(w_bf.shape), tab, tab],
        out_specs=[row(w) for w, _ in widths_dtypes],
        out_shape=[jax.ShapeDtypeStruct((m, w), dt) for w, dt in widths_dtypes],
        compiler_params=_params(1),
        name="inproj",
    )(x2d, w_bf, cos_tab, sin_tab)


def _group_norm(o, g_row):
    mu = jnp.mean(o, -1, keepdims=True)
    d = o - mu
    var = jnp.mean(d * d, -1, keepdims=True)
    return d * lax.rsqrt(var + GN_EPS) * g_row


def _ret_prompt_kernel(q_ref, k_ref, v_ref, decay_ref, rowdec_ref, wend_ref, gl_ref, gn_ref,
                       o_ref, s_out_ref, s_scr):
    c = pl.program_id(1)

    @pl.when(c == 0)
    def _():
        s_scr[...] = jnp.zeros_like(s_scr)

    for h in range(RET_HEADS):
        q = q_ref[:, h * RET_DK:(h + 1) * RET_DK]
        k = k_ref[:, h * RET_DK:(h + 1) * RET_DK]
        v = v_ref[:, h * RET_DV:(h + 1) * RET_DV]
        s0 = s_scr[h]
        sc = lax.dot_general(q, k, (((1,), (1,)), ((), ())), preferred_element_type=F32) * decay_ref[h]
        o = jnp.dot(sc.astype(BF16), v, preferred_element_type=F32)
        o = o + jnp.dot(q, s0.astype(BF16), preferred_element_type=F32) * rowdec_ref[h]
        kw = (k.astype(F32) * wend_ref[h]).astype(BF16)
        s_scr[h] = gl_ref[h] * s0 + lax.dot_general(kw, v, (((0,), (0,)), ((), ())), preferred_element_type=F32)
        o_ref[:, h * RET_DV:(h + 1) * RET_DV] = _group_norm(o, gn_ref[h:h + 1, :])

    @pl.when(c == pl.num_programs(1) - 1)
    def _():
        s_out_ref[0] = s_scr[...]


def _ret_tables(lg, n_rows, period):
    r = jnp.arange(n_rows)
    t = (r % period).astype(F32)
    same = (r[:, None] // period) == (r[None, :] // period)
    diff = t[:, None] - t[None, :]
    decay = jnp.where((diff >= 0) & same, jnp.exp(lg[:, None, None] * jnp.maximum(diff, 0.0)), 0.0)
    rowdec = jnp.exp(lg[:, None] * (t[None, :] + 1.0))
    wend = jnp.exp(lg[:, None] * (period - 1.0 - t[None, :]))
    gl = jnp.exp(lg * period)
    rowdec = jnp.broadcast_to(rowdec[:, :, None], (RET_HEADS, n_rows, RET_DV))
    wend = jnp.broadcast_to(wend[:, :, None], (RET_HEADS, n_rows, RET_DK))
    gl = jnp.broadcast_to(gl[:, None, None], (RET_HEADS, 1, RET_DV))
    return decay, rowdec, wend, gl


def _ret_prompt(rq, rk, rv, tables, gn_g, batch):
    m = rq.shape[0]
    n = m // batch // RET_CHUNK
    decay, rowdec, wend, gl = tables
    row = lambda w: pl.BlockSpec((RET_CHUNK, w), lambda b, c: (b * n + c, 0))
    return pl.pallas_call(
        _ret_prompt_kernel,
        grid=(batch, n),
        in_specs=[row(RET_QK_W), row(RET_QK_W), row(RET_V_W),
                  _const_spec(decay.shape), _const_spec(rowdec.shape), _const_spec(wend.shape),
                  _const_spec(gl.shape), _const_spec(gn_g.shape)],
        out_specs=[row(RET_V_W), pl.BlockSpec((1, RET_HEADS, RET_DK, RET_DV), lambda b, c: (b, 0, 0, 0))],
        out_shape=[jax.ShapeDtypeStruct((m, RET_V_W), F32),
                   jax.ShapeDtypeStruct((batch, RET_HEADS, RET_DK, RET_DV), F32)],
        scratch_shapes=[pltpu.VMEM((RET_HEADS, RET_DK, RET_DV), F32)],
        compiler_params=_params(2),
        name="ret_prompt",
    )(rq, rk, rv, decay, rowdec, wend, gl, gn_g)


def _ret_sample_kernel(n_seq, t_len, q_ref, k_ref, v_ref, s_ref, decay_ref, rowdec_ref, wend_ref, gl_ref, gn_ref,
                       o_ref, s_out_ref):
    rows = n_seq * t_len
    seq_of_row = lax.broadcasted_iota(jnp.int32, (rows, 1), 0) // t_len
    for h in range(RET_HEADS):
        q = q_ref[:, h * RET_DK:(h + 1) * RET_DK].astype(BF16)
        kf = k_ref[:, h * RET_DK:(h + 1) * RET_DK]
        k = kf.astype(BF16)
        v = v_ref[:, h * RET_DV:(h + 1) * RET_DV].astype(BF16)
        sc = lax.dot_general(q, k, (((1,), (1,)), ((), ())), preferred_element_type=F32) * decay_ref[h]
        o = jnp.dot(sc.astype(BF16), v, preferred_element_type=F32)
        kw = kf * wend_ref[h]
        o_state = jnp.zeros((rows, RET_DV), F32)
        for b in range(n_seq):
            mine = seq_of_row == b
            s0 = s_ref[b, h]
            o_state = o_state + jnp.where(mine, jnp.dot(q, s0.astype(BF16), preferred_element_type=F32), 0.0)
            kw_b = jnp.where(mine, kw, 0.0).astype(BF16)
            s_out_ref[b, h] = gl_ref[h] * s0 + lax.dot_general(
                kw_b, v, (((0,), (0,)), ((), ())), preferred_element_type=F32)
        o = o + o_state * rowdec_ref[h]
        o_ref[:, h * RET_DV:(h + 1) * RET_DV] = _group_norm(o, gn_ref[h:h + 1, :])


def _ret_sample(rq, rk, rv, state, tables, gn_g, t_len, n_seq):
    m = rq.shape[0]
    rows = n_seq * t_len
    decay, rowdec, wend, gl = tables
    row = lambda w: pl.BlockSpec((rows, w), lambda i: (i, 0))
    st = pl.BlockSpec((n_seq, RET_HEADS, RET_DK, RET_DV), lambda i: (i, 0, 0, 0))
    return pl.pallas_call(
        functools.partial(_ret_sample_kernel, n_seq, t_len),
        grid=(m // rows,),
        in_specs=[row(RET_QK_W), row(RET_QK_W), row(RET_V_W), st,
                  _const_spec(decay.shape), _const_spec(rowdec.shape), _const_spec(wend.shape),
                  _const_spec(gl.shape), _const_spec(gn_g.shape)],
        out_specs=[row(RET_V_W), st],
        out_shape=[jax.ShapeDtypeStruct((m, RET_V_W), F32), jax.ShapeDtypeStruct(state.shape, F32)],
        compiler_params=_params(1),
        name="ret_sample",
    )(rq, rk, rv, state, decay, rowdec, wend, gl, gn_g)


def _swa_core(q, kc, vc, valid, sinks_ref):
    outs = []
    for h in range(SWA_HEADS):
        kvh = h // SWA_GROUP
        qh = q[:, h * SWA_HD:(h + 1) * SWA_HD]
        kh = kc[:, kvh * SWA_HD:(kvh + 1) * SWA_HD]
        vh = vc[:, kvh * SWA_HD:(kvh + 1) * SWA_HD]
        s = lax.dot_general(qh, kh, (((1,), (1,)), ((), ())), preferred_element_type=F32)
        s = jnp.where(valid, s, -jnp.inf)
        sink = sinks_ref[h]
        m = jnp.maximum(jnp.max(s, -1, keepdims=True), sink)
        e = jnp.exp(s - m)
        den = jnp.sum(e, -1, keepdims=True) + jnp.exp(sink - m)
        p = e / den
        outs.append(jnp.dot(p.astype(BF16), vh, preferred_element_type=F32))
    return jnp.concatenate(outs, -1)


def _swa_prompt_kernel(sinks_ref, q_ref, kp_ref, kc_ref, vp_ref, vc_ref, o_ref):
    n = pl.program_id(1)
    i = lax.broadcasted_iota(jnp.int32, (WINDOW, 2 * WINDOW), 0)
    j = lax.broadcasted_iota(jnp.int32, (WINDOW, 2 * WINDOW), 1)
    valid = (j <= i + WINDOW) & (j > i) & ((n > 0) | (j >= WINDOW))
    kc = jnp.concatenate([kp_ref[...], kc_ref[...]], 0).astype(BF16)
    vc = jnp.concatenate([vp_ref[...], vc_ref[...]], 0).astype(BF16)
    o_ref[...] = _swa_core(q_ref[...], kc, vc, valid, sinks_ref).astype(o_ref.dtype)


def _swa_prompt(sq, sk, sv, sinks, batch):
    m = sq.shape[0]
    n = m // batch // WINDOW
    cur = lambda w: pl.BlockSpec((WINDOW, w), lambda b, c: (b * n + c, 0))
    prev = lambda w: pl.BlockSpec((WINDOW, w), lambda b, c: (b * n + jnp.maximum(c - 1, 0), 0))
    return pl.pallas_call(
        _swa_prompt_kernel,
        grid=(batch, n),
        in_specs=[pl.BlockSpec(memory_space=pltpu.SMEM),
                  cur(SWA_Q_W), prev(SWA_KV_W), cur(SWA_KV_W), prev(SWA_KV_W), cur(SWA_KV_W)],
        out_specs=cur(SWA_Q_W),
        out_shape=jax.ShapeDtypeStruct((m, SWA_Q_W), BF16),
        compiler_params=_params(2),
        name="swa_prompt",
    )(sinks, sq, sk, sk, sv, sv)


def _swa_sample_kernel(n_seq, t_len, sinks_ref, q_ref, kn_ref, vn_ref, kb_ref, vb_ref, o_ref, ko_ref, vo_ref):
    lb = kb_ref.shape[1]
    tk = 2 * WINDOW
    i = lax.broadcasted_iota(jnp.int32, (t_len, tk), 0)
    j = lax.broadcasted_iota(jnp.int32, (t_len, tk), 1)
    valid = (j <= i + lb) & (j > i + lb - WINDOW)
    pad = jnp.zeros((tk - lb - t_len, SWA_KV_W), F32)

    def body(b, carry):
        kb, vb, kn, vn = kb_ref[b], vb_ref[b], kn_ref[b], vn_ref[b]
        kc = jnp.concatenate([kb, kn, pad], 0).astype(BF16)
        vc = jnp.concatenate([vb, vn, pad], 0).astype(BF16)
        o_ref[b] = _swa_core(q_ref[b].astype(BF16), kc, vc, valid, sinks_ref).astype(o_ref.dtype)
        ko_ref[b] = jnp.concatenate([kb[t_len:], kn], 0)
        vo_ref[b] = jnp.concatenate([vb[t_len:], vn], 0)
        return carry

    lax.fori_loop(0, n_seq, body, 0)


def _swa_sample(sq, sk, sv, cache_k, cache_v, sinks, n_seq):
    nb, t_len, _ = sq.shape
    lb = cache_k.shape[1]
    blk = lambda t, w: pl.BlockSpec((n_seq, t, w), lambda i: (i, 0, 0))
    return pl.pallas_call(
        functools.partial(_swa_sample_kernel, n_seq, t_len),
        grid=(nb // n_seq,),
        in_specs=[pl.BlockSpec(memory_space=pltpu.SMEM),
                  blk(t_len, SWA_Q_W), blk(t_len, SWA_KV_W), blk(t_len, SWA_KV_W),
                  blk(lb, SWA_KV_W), blk(lb, SWA_KV_W)],
        out_specs=[blk(t_len, SWA_Q_W), blk(WINDOW, SWA_KV_W), blk(WINDOW, SWA_KV_W)],
        out_shape=[jax.ShapeDtypeStruct((nb, t_len, SWA_Q_W), BF16),
                   jax.ShapeDtypeStruct((nb, WINDOW, SWA_KV_W), F32),
                   jax.ShapeDtypeStruct((nb, WINDOW, SWA_KV_W), F32)],
        compiler_params=_params(1),
        name="swa_sample",
    )(sinks, sq, sk, sv, cache_k, cache_v)


def _mem_attn_kernel(n_seq, q_ref, k_ref, v_ref, o_ref):
    def body(b, carry):
        q = q_ref[b].astype(BF16)
        outs = []
        for h in range(MEM_HEADS):
            sl = slice(h * MEM_HD, (h + 1) * MEM_HD)
            kh = k_ref[b, :, sl].astype(BF16)
            vh = v_ref[b, :, sl].astype(BF16)
            s = lax.dot_general(q[:, sl], kh, (((1,), (1,)), ((), ())), preferred_element_type=F32) * (MEM_HD ** -0.5)
            m = jnp.max(s, -1, keepdims=True)
            e = jnp.exp(s - m)
            p = e / jnp.sum(e, -1, keepdims=True)
            outs.append(jnp.dot(p.astype(BF16), vh, preferred_element_type=F32))
        o_ref[b] = jnp.concatenate(outs, -1).astype(o_ref.dtype)
        return carry

    lax.fori_loop(0, n_seq, body, 0)


def _mem_attn(mq, mk, mv, n_seq, tq):
    nb, t, _ = mq.shape
    qblk = pl.BlockSpec((n_seq, tq, MEM_W), lambda i, j: (i, j, 0))
    kvblk = pl.BlockSpec((n_seq, N_MEM, MEM_W), lambda i, j: (i, 0, 0))
    return pl.pallas_call(
        functools.partial(_mem_attn_kernel, n_seq),
        grid=(nb // n_seq, t // tq),
        in_specs=[qblk, kvblk, kvblk],
        out_specs=qblk,
        out_shape=jax.ShapeDtypeStruct((nb, t, MEM_W), BF16),
        compiler_params=_params(2),
        name="mem_attn",
    )(mq, mk, mv)


def _matmul_kernel(x_ref, w_ref, o_ref):
    o_ref[...] = jnp.dot(x_ref[...].astype(BF16), w_ref[...], preferred_element_type=F32)


def _matmul(x2d, w_bf, tm):
    m, k = x2d.shape
    n = w_bf.shape[1]
    return pl.pallas_call(
        _matmul_kernel,
        grid=(m // tm,),
        in_specs=[pl.BlockSpec((tm, k), lambda i: (i, 0)), _const_spec(w_bf.shape)],
        out_specs=pl.BlockSpec((tm, n), lambda i: (i, 0)),
        out_shape=jax.ShapeDtypeStruct((m, n), F32),
        compiler_params=_params(1),
        name="mem_kv",
    )(x2d, w_bf)


def _layer_norm(x, g, b):
    mu = jnp.mean(x, -1, keepdims=True)
    d = x - mu
    var = jnp.mean(d * d, -1, keepdims=True)
    return d * lax.rsqrt(var + LN_EPS) * g + b


def _finish_kernel(x_ref, gn_ref, rg_ref, swa_ref, mem_ref, gr_ref, gs_ref, gm_ref,
                   wr_ref, ws_ref, wm_ref, wo_ref, l1g_ref, l1b_ref, wu_ref, wd_ref, l2g_ref, l2b_ref, o_ref):
    rg = rg_ref[...]
    ret_in = (rg * jax.nn.sigmoid(rg) * gn_ref[...]).astype(BF16)
    ret_b = jnp.dot(ret_in, wr_ref[...], preferred_element_type=F32)
    swa_b = jnp.dot(swa_ref[...], ws_ref[...], preferred_element_type=F32)
    mem_b = jnp.dot(mem_ref[...], wm_ref[...], preferred_element_type=F32)
    merged = (jax.nn.sigmoid(gr_ref[...]) * ret_b + jax.nn.sigmoid(gs_ref[...]) * swa_b
              + jax.nn.sigmoid(gm_ref[...]) * mem_b)
    y = jnp.dot(merged.astype(BF16), wo_ref[...], preferred_element_type=F32)
    x1 = _layer_norm(ALPHA * x_ref[...] + y, l1g_ref[...], l1b_ref[...])
    x1b = x1.astype(BF16)
    ff_chunk = D_FF // 4
    acc = jnp.zeros_like(x1)
    for c in range(D_FF // ff_chunk):
        h = jnp.dot(x1b, wu_ref[:, c * ff_chunk:(c + 1) * ff_chunk], preferred_element_type=F32)
        h = jnp.square(jnp.maximum(h, 0.0)).astype(BF16)
        acc = acc + jnp.dot(h, wd_ref[c * ff_chunk:(c + 1) * ff_chunk, :], preferred_element_type=F32)
    o_ref[...] = _layer_norm(ALPHA * x1 + acc, l2g_ref[...], l2b_ref[...])


def _finish(x2d, gn, rg, swa_o, mem_o, g_r, g_s, g_m, lw, tm):
    m = x2d.shape[0]
    row = lambda w: pl.BlockSpec((tm, w), lambda i: (i, 0))
    return pl.pallas_call(
        _finish_kernel,
        grid=(m // tm,),
        in_specs=[row(D_MODEL), row(RET_V_W), row(RET_V_W), row(SWA_Q_W), row(MEM_W),
                  row(D_MODEL), row(D_MODEL), row(D_MODEL)] + [_const_spec(w.shape) for w in lw],
        out_specs=row(D_MODEL),
        out_shape=jax.ShapeDtypeStruct((m, D_MODEL), F32),
        compiler_params=_params(1),
        name="finish",
    )(x2d, gn, rg, swa_o, mem_o, g_r, g_s, g_m, *lw)


def _rope_tables(pos):
    half = SWA_HD // 2
    inv = jnp.power(ROPE_THETA, -jnp.arange(half, dtype=F32) / half)
    ang = pos.astype(F32)[:, None] * inv[None, :]
    c, s = jnp.cos(ang), jnp.sin(ang)
    return jnp.concatenate([c, c, c, c], -1), jnp.concatenate([-s, s, -s, s], -1)


def kernel(x_prompt, x_sample, state_ret, cache_swa_k, cache_swa_v, cache_mem_k, cache_mem_v, mem_prompt,
           w_in, w_br_ret, w_br_swa, w_br_mem, w_out, w_mem_kv, attn_sinks, ret_gn_g,
           ln1_g, ln1_b, w_up, w_down, ln2_g, ln2_b):
    batch, seq, _ = x_prompt.shape
    dec_b, dec_t, _ = x_sample.shape
    tm_p, tm_s = 256, 256
    ret_seqs = RET_CHUNK // dec_t

    cos_p, sin_p = _rope_tables(jnp.arange(seq, dtype=jnp.int32))
    cos_s, sin_s = _rope_tables(PAST_LEN + jnp.arange(dec_t, dtype=jnp.int32))
    cos_s, sin_s = jnp.tile(cos_s, (tm_s // dec_t, 1)), jnp.tile(sin_s, (tm_s // dec_t, 1))
    lg = jnp.log1p(-jnp.exp2(-5.0 - jnp.arange(RET_HEADS, dtype=F32)))
    tab_p = _ret_tables(lg, RET_CHUNK, RET_CHUNK)
    tab_s = _ret_tables(lg, RET_CHUNK, dec_t)

    xp = x_prompt.reshape(batch * seq, D_MODEL)
    xs = x_sample.reshape(dec_b * dec_t, D_MODEL)
    mem2d = mem_prompt.reshape(batch * N_MEM, D_MODEL)
    lb = cache_swa_k.shape[2]

    ret_p, swk_p, swv_p, mk_p, mv_p, ret_s, swk_s, swv_s = [], [], [], [], [], [], [], []
    for l in range(DEPTH):
        w_in_bf = w_in[l].astype(BF16)
        row = lambda a: a[l].reshape(1, D_MODEL)
        lw = (w_br_ret[l].astype(BF16), w_br_swa[l].astype(BF16), w_br_mem[l].astype(BF16), w_out[l].astype(BF16),
              row(ln1_g), row(ln1_b), w_up[l].astype(BF16), w_down[l].astype(BF16), row(ln2_g), row(ln2_b))
        gn_g = ret_gn_g[l]
        sinks = attn_sinks[l]

        rq, rk, rv, rg, sq, sk, sv, mq, g_r, g_s, g_m = _inproj(xp, w_in_bf, cos_p, sin_p, tm_p, BF16)
        gn, s_p = _ret_prompt(rq, rk, rv, tab_p, gn_g, batch)
        swa_o = _swa_prompt(sq, sk, sv, sinks, batch)
        mkv = _matmul(mem2d, w_mem_kv[l].astype(BF16), 256)
        mk = mkv[:, :MEM_W].reshape(batch, N_MEM, MEM_W)
        mv = mkv[:, MEM_W:].reshape(batch, N_MEM, MEM_W)
        mem_o = _mem_attn(mq.reshape(batch, seq, MEM_W), mk, mv, 1, 512).reshape(batch * seq, MEM_W)
        xp = _finish(xp, gn, rg, swa_o, mem_o, g_r, g_s, g_m, lw, tm_p)
        ret_p.append(s_p)
        swk_p.append(sk.reshape(batch, seq, SWA_KV_HEADS, SWA_HD)[:, -WINDOW:])
        swv_p.append(sv.reshape(batch, seq, SWA_KV_HEADS, SWA_HD)[:, -WINDOW:])
        mk_p.append(mk.reshape(batch, N_MEM, MEM_HEADS, MEM_HD))
        mv_p.append(mv.reshape(batch, N_MEM, MEM_HEADS, MEM_HD))

        rq, rk, rv, rg, sq, sk, sv, mq, g_r, g_s, g_m = _inproj(xs, w_in_bf, cos_s, sin_s, tm_s, F32)
        gn, s_s = _ret_sample(rq, rk, rv, state_ret[l], tab_s, gn_g, dec_t, ret_seqs)
        swa_o, ko, vo = _swa_sample(sq.reshape(dec_b, dec_t, SWA_Q_W), sk.reshape(dec_b, dec_t, SWA_KV_W),
                                    sv.reshape(dec_b, dec_t, SWA_KV_W),
                                    cache_swa_k[l].reshape(dec_b, lb, SWA_KV_W),
                                    cache_swa_v[l].reshape(dec_b, lb, SWA_KV_W), sinks, 8)
        mem_o = _mem_attn(mq.reshape(dec_b, dec_t, MEM_W), cache_mem_k[l].reshape(dec_b, N_MEM, MEM_W),
                          cache_mem_v[l].reshape(dec_b, N_MEM, MEM_W), 8, dec_t)
        xs = _finish(xs, gn, rg, swa_o.reshape(dec_b * dec_t, SWA_Q_W), mem_o.reshape(dec_b * dec_t, MEM_W),
                     g_r, g_s, g_m, lw, tm_s)
        ret_s.append(s_s)
        swk_s.append(ko.reshape(dec_b, WINDOW, SWA_KV_HEADS, SWA_HD))
        swv_s.append(vo.reshape(dec_b, WINDOW, SWA_KV_HEADS, SWA_HD))

    return (xp.reshape(batch, seq, D_MODEL), xs.reshape(dec_b, dec_t, D_MODEL),
            jnp.stack(ret_p), jnp.stack(swk_p), jnp.stack(swv_p), jnp.stack(mk_p), jnp.stack(mv_p),
            jnp.stack(ret_s), jnp.stack(swk_s), jnp.stack(swv_s))
```

```python
import functools

import jax
import jax.numpy as jnp
from jax import lax
from jax.experimental import pallas as pl
from jax.experimental.pallas import tpu as pltpu

F32 = jnp.float32
BF16 = jnp.bfloat16

D_MODEL = 1024
DEPTH = 2
PAST_LEN = 16384
RET_HEADS = 8
RET_DK = 64
RET_DV = 128
RET_CHUNK = 128
SWA_HEADS = 8
SWA_KV_HEADS = 2
SWA_GROUP = SWA_HEADS // SWA_KV_HEADS
SWA_HD = 64
WINDOW = 128
MEM_HEADS = 4
MEM_HD = 128
N_MEM = 256
D_FF = 4 * D_MODEL
ROPE_THETA = 10000.0
LN_EPS = 1e-5
GN_EPS = 1e-5
ALPHA = (2 * DEPTH) ** 0.25

RET_QK_W = RET_HEADS * RET_DK
RET_V_W = RET_HEADS * RET_DV
SWA_Q_W = SWA_HEADS * SWA_HD
SWA_KV_W = SWA_KV_HEADS * SWA_HD
MEM_W = MEM_HEADS * MEM_HD
OFF_RQ = 0
OFF_RK = OFF_RQ + RET_QK_W
OFF_RV = OFF_RK + RET_QK_W
OFF_RG = OFF_RV + RET_V_W
OFF_SQ = OFF_RG + RET_V_W
OFF_SK = OFF_SQ + SWA_Q_W
OFF_SV = OFF_SK + SWA_KV_W
OFF_MQ = OFF_SV + SWA_KV_W
OFF_GR = OFF_MQ + MEM_W
OFF_GS = OFF_GR + D_MODEL
OFF_GM = OFF_GS + D_MODEL
IN_W = OFF_GM + D_MODEL

LANES = 128
V7X_VMEM_LIMIT = 56 * 1024 * 1024


def _const_spec(shape):
    nd = len(shape)
    return pl.BlockSpec(shape, lambda *_: (0,) * nd, pipeline_mode=pl.Buffered(1))


def _params(n_grid):
    return pltpu.CompilerParams(dimension_semantics=("arbitrary",) * n_grid, vmem_limit_bytes=V7X_VMEM_LIMIT)


def _inproj_kernel(x_ref, w_ref, cos_ref, sin_ref,
                   rq_ref, rk_ref, rv_ref, rg_ref, sq_ref, sk_ref, sv_ref, mq_ref, gr_ref, gs_ref, gm_ref):
    xb = x_ref[...].astype(BF16)
    cos = cos_ref[...]
    sin = sin_ref[...]
    lane = lax.broadcasted_iota(jnp.int32, cos.shape, 1)
    first_half = (lane & (SWA_HD // 2)) == 0

    def proj(off, width):
        return jnp.dot(xb, w_ref[:, off:off + width], preferred_element_type=F32)

    def rope_store(off, width, out_ref, scale):
        y = proj(off, width)
        for j in range(width // LANES):
            yj = y[:, j * LANES:(j + 1) * LANES]
            sw = jnp.where(first_half, pltpu.roll(yj, LANES - SWA_HD // 2, 1), pltpu.roll(yj, SWA_HD // 2, 1))
            r = yj * cos + sw * sin
            if scale != 1.0:
                r = r * scale
            out_ref[:, j * LANES:(j + 1) * LANES] = r.astype(out_ref.dtype)

    def plain_store(off, width, out_ref):
        out_ref[...] = proj(off, width).astype(out_ref.dtype)

    rope_store(OFF_RQ, RET_QK_W, rq_ref, 1.0)
    rope_store(OFF_RK, RET_QK_W, rk_ref, RET_DK ** -0.5)
    plain_store(OFF_RV, RET_V_W, rv_ref)
    plain_store(OFF_RG, RET_V_W, rg_ref)
    rope_store(OFF_SQ, SWA_Q_W, sq_ref, SWA_HD ** -0.5)
    rope_store(OFF_SK, SWA_KV_W, sk_ref, 1.0)
    plain_store(OFF_SV, SWA_KV_W, sv_ref)
    plain_store(OFF_MQ, MEM_W, mq_ref)
    plain_store(OFF_GR, D_MODEL, gr_ref)
    plain_store(OFF_GS, D_MODEL, gs_ref)
    plain_store(OFF_GM, D_MODEL, gm_ref)


def _inproj(x2d, w_bf, cos_tab, sin_tab, tm, qkv_dtype):
    m = x2d.shape[0]
    n_tab = cos_tab.shape[0] // tm
    row = lambda w: pl.BlockSpec((tm, w), lambda i: (i, 0))
    tab = pl.BlockSpec((tm, LANES), lambda i: (i % n_tab, 0))
    widths_dtypes = [(RET_QK_W, qkv_dtype), (RET_QK_W, qkv_dtype), (RET_V_W, qkv_dtype), (RET_V_W, F32),
                     (SWA_Q_W, qkv_dtype), (SWA_KV_W, F32), (SWA_KV_W, F32), (MEM_W, qkv_dtype),
                     (D_MODEL, F32), (D_MODEL, F32), (D_MODEL, F32)]
    return pl.pallas_call(
        _inproj_kernel,
        grid=(m // tm,),
        in_specs=[row(D_MODEL), _const_spec(w_bf.shape), tab, tab],
        out_specs=[row(w) for w, _ in widths_dtypes],
        out_shape=[jax.ShapeDtypeStruct((m, w), dt) for w, dt in widths_dtypes],
        compiler_params=_params(1),
        name="inproj",
    )(x2d, w_bf, cos_tab, sin_tab)


def _group_norm(o, g_row):
    mu = jnp.mean(o, -1, keepdims=True)
    d = o - mu
    var = jnp.mean(d * d, -1, keepdims=True)
    return d * lax.rsqrt(var + GN_EPS) * g_row


def _ret_prompt_kernel(q_ref, k_ref, v_ref, decay_ref, rowdec_ref, wend_ref, gl_ref, gn_ref,
                       o_ref, s_out_ref, s_scr):
    c = pl.program_id(1)

    @pl.when(c == 0)
    def _():
        s_scr[...] = jnp.zeros_like(s_scr)

    for h in range(RET_HEADS):
        q = q_ref[:, h * RET_DK:(h + 1) * RET_DK]
        k = k_ref[:, h * RET_DK:(h + 1) * RET_DK]
        v = v_ref[:, h * RET_DV:(h + 1) * RET_DV]
        s0 = s_scr[h]
        sc = lax.dot_general(q, k, (((1,), (1,)), ((), ())), preferred_element_type=F32) * decay_ref[h]
        o = jnp.dot(sc.astype(BF16), v, preferred_element_type=F32)
        o = o + jnp.dot(q, s0.astype(BF16), preferred_element_type=F32) * rowdec_ref[h]
        kw = (k.astype(F32) * wend_ref[h]).astype(BF16)
        s_scr[h] = gl_ref[h] * s0 + lax.dot_general(kw, v, (((0,), (0,)), ((), ())), preferred_element_type=F32)
        o_ref[:, h * RET_DV:(h + 1) * RET_DV] = _group_norm(o, gn_ref[h:h + 1, :])

    @pl.when(c == pl.num_programs(1) - 1)
    def _():
        s_out_ref[0] = s_scr[...]


def _ret_tables(lg, n_rows, period):
    r = jnp.arange(n_rows)
    t = (r % period).astype(F32)
    same = (r[:, None] // period) == (r[None, :] // period)
    diff = t[:, None] - t[None, :]
    decay = jnp.where((diff >= 0) & same, jnp.exp(lg[:, None, None] * jnp.maximum(diff, 0.0)), 0.0)
    rowdec = jnp.exp(lg[:, None] * (t[None, :] + 1.0))
    wend = jnp.exp(lg[:, None] * (period - 1.0 - t[None, :]))
    gl = jnp.exp(lg * period)
    rowdec = jnp.broadcast_to(rowdec[:, :, None], (RET_HEADS, n_rows, RET_DV))
    wend = jnp.broadcast_to(wend[:, :, None], (RET_HEADS, n_rows, RET_DK))
    gl = jnp.broadcast_to(gl[:, None, None], (RET_HEADS, 1, RET_DV))
    return decay, rowdec, wend, gl


def _ret_prompt(rq, rk, rv, tables, gn_g, batch):
    m = rq.shape[0]
    n = m // batch // RET_CHUNK
    decay, rowdec, wend, gl = tables
    row = lambda w: pl.BlockSpec((RET_CHUNK, w), lambda b, c: (b * n + c, 0))
    return pl.pallas_call(
        _ret_prompt_kernel,
        grid=(batch, n),
        in_specs=[row(RET_QK_W), row(RET_QK_W), row(RET_V_W),
                  _const_spec(decay.shape), _const_spec(rowdec.shape), _const_spec(wend.shape),
                  _const_spec(gl.shape), _const_spec(gn_g.shape)],
        out_specs=[row(RET_V_W), pl.BlockSpec((1, RET_HEADS, RET_DK, RET_DV), lambda b, c: (b, 0, 0, 0))],
        out_shape=[jax.ShapeDtypeStruct((m, RET_V_W), F32),
                   jax.ShapeDtypeStruct((batch, RET_HEADS, RET_DK, RET_DV), F32)],
        scratch_shapes=[pltpu.VMEM((RET_HEADS, RET_DK, RET_DV), F32)],
        compiler_params=_params(2),
        name="ret_prompt",
    )(rq, rk, rv, decay, rowdec, wend, gl, gn_g)


def _ret_sample_kernel(n_seq, t_len, q_ref, k_ref, v_ref, s_ref, decay_ref, rowdec_ref, wend_ref, gl_ref, gn_ref,
                       o_ref, s_out_ref):
    rows = n_seq * t_len
    seq_of_row = lax.broadcasted_iota(jnp.int32, (rows, 1), 0) // t_len
    for h in range(RET_HEADS):
        q = q_ref[:, h * RET_DK:(h + 1) * RET_DK].astype(BF16)
        kf = k_ref[:, h * RET_DK:(h + 1) * RET_DK]
        k = kf.astype(BF16)
        v = v_ref[:, h * RET_DV:(h + 1) * RET_DV].astype(BF16)
        sc = lax.dot_general(q, k, (((1,), (1,)), ((), ())), preferred_element_type=F32) * decay_ref[h]
        o = jnp.dot(sc.astype(BF16), v, preferred_element_type=F32)
        kw = kf * wend_ref[h]
        o_state = jnp.zeros((rows, RET_DV), F32)
        for b in range(n_seq):
            mine = seq_of_row == b
            s0 = s_ref[0, b, h]
            o_state = o_state + jnp.where(mine, jnp.dot(q, s0.astype(BF16), preferred_element_type=F32), 0.0)
            kw_b = jnp.where(mine, kw, 0.0).astype(BF16)
            s_out_ref[b, h] = gl_ref[h] * s0 + lax.dot_general(
                kw_b, v, (((0,), (0,)), ((), ())), preferred_element_type=F32)
        o = o + o_state * rowdec_ref[h]
        o_ref[:, h * RET_DV:(h + 1) * RET_DV] = _group_norm(o, gn_ref[h:h + 1, :])


def _ret_sample(rq, rk, rv, state, layer, tables, gn_g, t_len, n_seq):
    m = rq.shape[0]
    rows = n_seq * t_len
    decay, rowdec, wend, gl = tables
    row = lambda w: pl.BlockSpec((rows, w), lambda i: (i, 0))
    st_in = pl.BlockSpec((1, n_seq, RET_HEADS, RET_DK, RET_DV), lambda i: (layer, i, 0, 0, 0))
    st_out = pl.BlockSpec((n_seq, RET_HEADS, RET_DK, RET_DV), lambda i: (i, 0, 0, 0))
    return pl.pallas_call(
        functools.partial(_ret_sample_kernel, n_seq, t_len),
        grid=(m // rows,),
        in_specs=[row(RET_QK_W), row(RET_QK_W), row(RET_V_W), st_in,
                  _const_spec(decay.shape), _const_spec(rowdec.shape), _const_spec(wend.shape),
                  _const_spec(gl.shape), _const_spec(gn_g.shape)],
        out_specs=[row(RET_V_W), st_out],
        out_shape=[jax.ShapeDtypeStruct((m, RET_V_W), F32), jax.ShapeDtypeStruct(state.shape[1:], F32)],
        compiler_params=_params(1),
        name="ret_sample",
    )(rq, rk, rv, state, decay, rowdec, wend, gl, gn_g)


def _swa_core(q, kc, vc, valid, sinks_ref):
    outs = []
    for h in range(SWA_HEADS):
        kvh = h // SWA_GROUP
        qh = q[:, h * SWA_HD:(h + 1) * SWA_HD]
        kh = kc[:, kvh * SWA_HD:(kvh + 1) * SWA_HD]
        vh = vc[:, kvh * SWA_HD:(kvh + 1) * SWA_HD]
        s = lax.dot_general(qh, kh, (((1,), (1,)), ((), ())), preferred_element_type=F32)
        s = jnp.where(valid, s, -jnp.inf)
        sink = sinks_ref[h]
        m = jnp.maximum(jnp.max(s, -1, keepdims=True), sink)
        e = jnp.exp(s - m)
        den = jnp.sum(e, -1, keepdims=True) + jnp.exp(sink - m)
        p = e / den
        outs.append(jnp.dot(p.astype(BF16), vh, preferred_element_type=F32))
    return jnp.concatenate(outs, -1)


def _swa_prompt_kernel(sinks_ref, q_ref, kp_ref, kc_ref, vp_ref, vc_ref, o_ref):
    n = pl.program_id(1)
    i = lax.broadcasted_iota(jnp.int32, (WINDOW, 2 * WINDOW), 0)
    j = lax.broadcasted_iota(jnp.int32, (WINDOW, 2 * WINDOW), 1)
    valid = (j <= i + WINDOW) & (j > i) & ((n > 0) | (j >= WINDOW))
    kc = jnp.concatenate([kp_ref[...], kc_ref[...]], 0).astype(BF16)
    vc = jnp.concatenate([vp_ref[...], vc_ref[...]], 0).astype(BF16)
    o_ref[...] = _swa_core(q_ref[...], kc, vc, valid, sinks_ref).astype(o_ref.dtype)


def _swa_prompt(sq, sk, sv, sinks, batch):
    m = sq.shape[0]
    n = m // batch // WINDOW
    cur = lambda w: pl.BlockSpec((WINDOW, w), lambda b, c: (b * n + c, 0))
    prev = lambda w: pl.BlockSpec((WINDOW, w), lambda b, c: (b * n + jnp.maximum(c - 1, 0), 0))
    return pl.pallas_call(
        _swa_prompt_kernel,
        grid=(batch, n),
        in_specs=[pl.BlockSpec(memory_space=pltpu.SMEM),
                  cur(SWA_Q_W), prev(SWA_KV_W), cur(SWA_KV_W), prev(SWA_KV_W), cur(SWA_KV_W)],
        out_specs=cur(SWA_Q_W),
        out_shape=jax.ShapeDtypeStruct((m, SWA_Q_W), BF16),
        compiler_params=_params(2),
        name="swa_prompt",
    )(sinks, sq, sk, sk, sv, sv)


def _swa_sample_kernel(n_seq, t_len, sinks_ref, q_ref, kn_ref, vn_ref, kt_ref, vt_ref, o_ref, kto_ref, vto_ref):
    grp_rows = SWA_GROUP * t_len
    n_all = n_seq * grp_rows
    q = q_ref[...]
    kn = kn_ref[...]
    vn = vn_ref[...]
    kn_t = kn.T
    vn_t = vn.T
    r = lax.broadcasted_iota(jnp.int32, (n_all, 1), 0)
    t_q = r % t_len
    g_row = (r // t_len) % SWA_GROUP
    b_row = r // grp_rows
    c = lax.broadcasted_iota(jnp.int32, (1, WINDOW), 1)
    valid_cache = c > t_q
    valid_new = ((c // t_len) == b_row) & ((c % t_len) <= t_q)
    lane = lax.broadcasted_iota(jnp.int32, (SWA_HD, WINDOW), 1)
    is_new_lane = lane >= WINDOW - t_len
    pieces = []
    for kvh in range(SWA_KV_HEADS):
        hd = slice(kvh * SWA_HD, (kvh + 1) * SWA_HD)
        qg = [q[:, (kvh * SWA_GROUP + g) * SWA_HD:(kvh * SWA_GROUP + g + 1) * SWA_HD] for g in range(SWA_GROUP)]
        q_all = jnp.concatenate([qg[g][b * t_len:(b + 1) * t_len] for b in range(n_seq) for g in range(SWA_GROUP)],
                                0).astype(BF16)
        s_new = jnp.dot(q_all, kn_t[hd].astype(BF16), preferred_element_type=F32)
        s_cache = jnp.concatenate(
            [jnp.dot(q_all[b * grp_rows:(b + 1) * grp_rows], kt_ref[0, b, kvh].astype(BF16),
                     preferred_element_type=F32) for b in range(n_seq)], 0)
        s_new = jnp.where(valid_new, s_new, -jnp.inf)
        s_cache = jnp.where(valid_cache, s_cache, -jnp.inf)
        sink = jnp.full((n_all, 1), sinks_ref[kvh * SWA_GROUP], F32)
        for g in range(1, SWA_GROUP):
            sink = jnp.where(g_row == g, sinks_ref[kvh * SWA_GROUP + g], sink)
        m = jnp.maximum(jnp.maximum(jnp.max(s_new, -1, keepdims=True), jnp.max(s_cache, -1, keepdims=True)), sink)
        e_new = jnp.exp(s_new - m)
        e_cache = jnp.exp(s_cache - m)
        den = jnp.sum(e_new, -1, keepdims=True) + jnp.sum(e_cache, -1, keepdims=True) + jnp.exp(sink - m)
        p_new = (e_new / den).astype(BF16)
        p_cache = (e_cache / den).astype(BF16)
        o = jnp.dot(p_new, vn[:, hd].astype(BF16), preferred_element_type=F32)
        o = o + jnp.concatenate(
            [lax.dot_general(p_cache[b * grp_rows:(b + 1) * grp_rows], vt_ref[0, b, kvh].astype(BF16),
                             (((1,), (1,)), ((), ())), preferred_element_type=F32) for b in range(n_seq)], 0)
        for g in range(SWA_GROUP):
            pieces.append(jnp.concatenate(
                [o[b * grp_rows + g * t_len:b * grp_rows + (g + 1) * t_len] for b in range(n_seq)], 0))
        for b in range(n_seq):
            shift_new = (WINDOW - t_len - b * t_len) % WINDOW
            kto_ref[b, kvh] = jnp.where(is_new_lane, pltpu.roll(kn_t[hd], shift_new, 1),
                                        pltpu.roll(kt_ref[0, b, kvh], WINDOW - t_len, 1))
            vto_ref[b, kvh] = jnp.where(is_new_lane, pltpu.roll(vn_t[hd], shift_new, 1),
                                        pltpu.roll(vt_ref[0, b, kvh], WINDOW - t_len, 1))
    o_ref[...] = jnp.concatenate(pieces, -1).astype(o_ref.dtype)


def _swa_sample(sq, sk, sv, cache_kt, cache_vt, sinks, layer, t_len, n_seq):
    m = sq.shape[0]
    rows = n_seq * t_len
    assert rows == WINDOW and cache_kt.shape[-1] == WINDOW
    row = lambda w: pl.BlockSpec((rows, w), lambda i: (i, 0))
    cin = pl.BlockSpec((1, n_seq, SWA_KV_HEADS, SWA_HD, WINDOW), lambda i: (layer, i, 0, 0, 0))
    cout = pl.BlockSpec((n_seq, SWA_KV_HEADS, SWA_HD, WINDOW), lambda i: (i, 0, 0, 0))
    cshape = jax.ShapeDtypeStruct(cache_kt.shape[1:], F32)
    return pl.pallas_call(
        functools.partial(_swa_sample_kernel, n_seq, t_len),
        grid=(m // rows,),
        in_specs=[pl.BlockSpec(memory_space=pltpu.SMEM), row(SWA_Q_W), row(SWA_KV_W), row(SWA_KV_W), cin, cin],
        out_specs=[row(SWA_Q_W), cout, cout],
        out_shape=[jax.ShapeDtypeStruct((m, SWA_Q_W), BF16), cshape, cshape],
        compiler_params=_params(1),
        name="swa_sample",
    )(sinks, sq, sk, sv, cache_kt, cache_vt)


def _mem_attn_kernel(n_seq, q_ref, k_ref, v_ref, o_ref):
    def body(b, carry):
        q = q_ref[b].astype(BF16)
        outs = []
        for h in range(MEM_HEADS):
            sl = slice(h * MEM_HD, (h + 1) * MEM_HD)
            kh = k_ref[b, :, sl].astype(BF16)
            vh = v_ref[b, :, sl].astype(BF16)
            s = lax.dot_general(q[:, sl], kh, (((1,), (1,)), ((), ())), preferred_element_type=F32) * (MEM_HD ** -0.5)
            m = jnp.max(s, -1, keepdims=True)
            e = jnp.exp(s - m)
            p = e / jnp.sum(e, -1, keepdims=True)
            outs.append(jnp.dot(p.astype(BF16), vh, preferred_element_type=F32))
        o_ref[b] = jnp.concatenate(outs, -1).astype(o_ref.dtype)
        return carry

    lax.fori_loop(0, n_seq, body, 0)


def _mem_attn(mq, mk, mv, n_seq, tq):
    nb, t, _ = mq.shape
    qblk = pl.BlockSpec((n_seq, tq, MEM_W), lambda i, j: (i, j, 0))
    kvblk = pl.BlockSpec((n_seq, N_MEM, MEM_W), lambda i, j: (i, 0, 0))
    return pl.pallas_call(
        functools.partial(_mem_attn_kernel, n_seq),
        grid=(nb // n_seq, t // tq),
        in_specs=[qblk, kvblk, kvblk],
        out_specs=qblk,
        out_shape=jax.ShapeDtypeStruct((nb, t, MEM_W), BF16),
        compiler_params=_params(2),
        name="mem_attn",
    )(mq, mk, mv)


def _mem_sample_kernel(n_seq, t_len, q_ref, k_ref, v_ref, o_ref):
    rows = MEM_HEADS * t_len
    head_of_row = lax.broadcasted_iota(jnp.int32, (rows, 1), 0) // t_len
    head_of_col = lax.broadcasted_iota(jnp.int32, (1, N_MEM * MEM_HEADS), 1) % MEM_HEADS
    valid = head_of_row == head_of_col
    outs = []
    for b in range(n_seq):
        qb = q_ref[b * t_len:(b + 1) * t_len, :]
        q_all = jnp.concatenate([qb[:, h * MEM_HD:(h + 1) * MEM_HD] for h in range(MEM_HEADS)], 0).astype(BF16)
        s = lax.dot_general(q_all, k_ref[0, b].astype(BF16), (((1,), (1,)), ((), ())),
                            preferred_element_type=F32) * (MEM_HD ** -0.5)
        s = jnp.where(valid, s, -jnp.inf)
        m = jnp.max(s, -1, keepdims=True)
        e = jnp.exp(s - m)
        p = e / jnp.sum(e, -1, keepdims=True)
        o = jnp.dot(p.astype(BF16), v_ref[0, b].astype(BF16), preferred_element_type=F32)
        outs.append(jnp.concatenate([o[h * t_len:(h + 1) * t_len] for h in range(MEM_HEADS)], -1))
    o_ref[...] = jnp.concatenate(outs, 0).astype(o_ref.dtype)


def _mem_sample(mq, cache_k, cache_v, layer, t_len, n_seq):
    m = mq.shape[0]
    rows = n_seq * t_len
    row = pl.BlockSpec((rows, MEM_W), lambda i: (i, 0))
    kv = pl.BlockSpec((1, n_seq, N_MEM * MEM_HEADS, MEM_HD), lambda i: (layer, i, 0, 0))
    return pl.pallas_call(
        functools.partial(_mem_sample_kernel, n_seq, t_len),
        grid=(m // rows,),
        in_specs=[row, kv, kv],
        out_specs=row,
        out_shape=jax.ShapeDtypeStruct((m, MEM_W), BF16),
        compiler_params=_params(1),
        name="mem_sample",
    )(mq, cache_k, cache_v)


def _matmul_kernel(x_ref, w_ref, o_ref):
    o_ref[...] = jnp.dot(x_ref[...].astype(BF16), w_ref[...], preferred_element_type=F32)


def _matmul(x2d, w_bf, tm):
    m, k = x2d.shape
    n = w_bf.shape[1]
    return pl.pallas_call(
        _matmul_kernel,
        grid=(m // tm,),
        in_specs=[pl.BlockSpec((tm, k), lambda i: (i, 0)), _const_spec(w_bf.shape)],
        out_specs=pl.BlockSpec((tm, n), lambda i: (i, 0)),
        out_shape=jax.ShapeDtypeStruct((m, n), F32),
        compiler_params=_params(1),
        name="mem_kv",
    )(x2d, w_bf)


def _layer_norm(x, g, b):
    mu = jnp.mean(x, -1, keepdims=True)
    d = x - mu
    var = jnp.mean(d * d, -1, keepdims=True)
    return d * lax.rsqrt(var + LN_EPS) * g + b


def _finish_kernel(x_ref, gn_ref, rg_ref, swa_ref, mem_ref, gr_ref, gs_ref, gm_ref,
                   wr_ref, ws_ref, wm_ref, wo_ref, l1g_ref, l1b_ref, wu_ref, wd_ref, l2g_ref, l2b_ref, o_ref):
    rg = rg_ref[...]
    ret_in = (rg * jax.nn.sigmoid(rg) * gn_ref[...]).astype(BF16)
    ret_b = jnp.dot(ret_in, wr_ref[...], preferred_element_type=F32)
    swa_b = jnp.dot(swa_ref[...], ws_ref[...], preferred_element_type=F32)
    mem_b = jnp.dot(mem_ref[...], wm_ref[...], preferred_element_type=F32)
    merged = (jax.nn.sigmoid(gr_ref[...]) * ret_b + jax.nn.sigmoid(gs_ref[...]) * swa_b
              + jax.nn.sigmoid(gm_ref[...]) * mem_b)
    y = jnp.dot(merged.astype(BF16), wo_ref[...], preferred_element_type=F32)
    x1 = _layer_norm(ALPHA * x_ref[...] + y, l1g_ref[...], l1b_ref[...])
    x1b = x1.astype(BF16)
    ff_chunk = D_FF // 4
    acc = jnp.zeros_like(x1)
    for c in range(D_FF // ff_chunk):
        h = jnp.dot(x1b, wu_ref[:, c * ff_chunk:(c + 1) * ff_chunk], preferred_element_type=F32)
        h = jnp.square(jnp.maximum(h, 0.0)).astype(BF16)
        acc = acc + jnp.dot(h, wd_ref[c * ff_chunk:(c + 1) * ff_chunk, :], preferred_element_type=F32)
    o_ref[...] = _layer_norm(ALPHA * x1 + acc, l2g_ref[...], l2b_ref[...])


def _finish(x2d, gn, rg, swa_o, mem_o, g_r, g_s, g_m, lw, tm):
    m = x2d.shape[0]
    row = lambda w: pl.BlockSpec((tm, w), lambda i: (i, 0))
    return pl.pallas_call(
        _finish_kernel,
        grid=(m // tm,),
        in_specs=[row(D_MODEL), row(RET_V_W), row(RET_V_W), row(SWA_Q_W), row(MEM_W),
                  row(D_MODEL), row(D_MODEL), row(D_MODEL)] + [_const_spec(w.shape) for w in lw],
        out_specs=row(D_MODEL),
        out_shape=jax.ShapeDtypeStruct((m, D_MODEL), F32),
        compiler_params=_params(1),
        name="finish",
    )(x2d, gn, rg, swa_o, mem_o, g_r, g_s, g_m, *lw)


def _rope_tables(pos):
    half = SWA_HD // 2
    inv = jnp.power(ROPE_THETA, -jnp.arange(half, dtype=F32) / half)
    ang = pos.astype(F32)[:, None] * inv[None, :]
    c, s = jnp.cos(ang), jnp.sin(ang)
    return jnp.concatenate([c, c, c, c], -1), jnp.concatenate([-s, s, -s, s], -1)


def kernel(x_prompt, x_sample, state_ret, cache_swa_k, cache_swa_v, cache_mem_k, cache_mem_v, mem_prompt,
           w_in, w_br_ret, w_br_swa, w_br_mem, w_out, w_mem_kv, attn_sinks, ret_gn_g,
           ln1_g, ln1_b, w_up, w_down, ln2_g, ln2_b):
    batch, seq, _ = x_prompt.shape
    dec_b, dec_t, _ = x_sample.shape
    tm_p, tm_s = 256, 256
    ret_seqs = RET_CHUNK // dec_t

    cos_p, sin_p = _rope_tables(jnp.arange(seq, dtype=jnp.int32))
    cos_s, sin_s = _rope_tables(PAST_LEN + jnp.arange(dec_t, dtype=jnp.int32))
    cos_s, sin_s = jnp.tile(cos_s, (tm_s // dec_t, 1)), jnp.tile(sin_s, (tm_s // dec_t, 1))
    lg = jnp.log1p(-jnp.exp2(-5.0 - jnp.arange(RET_HEADS, dtype=F32)))
    tab_p = _ret_tables(lg, RET_CHUNK, RET_CHUNK)
    tab_s = _ret_tables(lg, RET_CHUNK, dec_t)

    xp = x_prompt.reshape(batch * seq, D_MODEL)
    xs = x_sample.reshape(dec_b * dec_t, D_MODEL)
    mem2d = mem_prompt.reshape(batch * N_MEM, D_MODEL)
    cache_kt = jnp.transpose(cache_swa_k, (0, 1, 3, 4, 2))
    cache_vt = jnp.transpose(cache_swa_v, (0, 1, 3, 4, 2))
    cache_mk = cache_mem_k.reshape(DEPTH, dec_b, N_MEM * MEM_HEADS, MEM_HD)
    cache_mv = cache_mem_v.reshape(DEPTH, dec_b, N_MEM * MEM_HEADS, MEM_HD)

    ret_p, swk_p, swv_p, mk_p, mv_p, ret_s, swk_s, swv_s = [], [], [], [], [], [], [], []
    for l in range(DEPTH):
        w_in_bf = w_in[l].astype(BF16)
        row = lambda a: a[l].reshape(1, D_MODEL)
        lw = (w_br_ret[l].astype(BF16), w_br_swa[l].astype(BF16), w_br_mem[l].astype(BF16), w_out[l].astype(BF16),
              row(ln1_g), row(ln1_b), w_up[l].astype(BF16), w_down[l].astype(BF16), row(ln2_g), row(ln2_b))
        gn_g = ret_gn_g[l]
        sinks = attn_sinks[l]

        rq, rk, rv, rg, sq, sk, sv, mq, g_r, g_s, g_m = _inproj(xp, w_in_bf, cos_p, sin_p, tm_p, BF16)
        gn, s_p = _ret_prompt(rq, rk, rv, tab_p, gn_g, batch)
        swa_o = _swa_prompt(sq, sk, sv, sinks, batch)
        mkv = _matmul(mem2d, w_mem_kv[l].astype(BF16), 256)
        mk = mkv[:, :MEM_W].reshape(batch, N_MEM, MEM_W)
        mv = mkv[:, MEM_W:].reshape(batch, N_MEM, MEM_W)
        mem_o = _mem_attn(mq.reshape(batch, seq, MEM_W), mk, mv, 1, 512).reshape(batch * seq, MEM_W)
        xp = _finish(xp, gn, rg, swa_o, mem_o, g_r, g_s, g_m, lw, tm_p)
        ret_p.append(s_p)
        swk_p.append(sk.reshape(batch, seq, SWA_KV_W)[:, -WINDOW:].reshape(batch, WINDOW, SWA_KV_HEADS, SWA_HD))
        swv_p.append(sv.reshape(batch, seq, SWA_KV_W)[:, -WINDOW:].reshape(batch, WINDOW, SWA_KV_HEADS, SWA_HD))
        mk_p.append(mk.reshape(batch, N_MEM, MEM_HEADS, MEM_HD))
        mv_p.append(mv.reshape(batch, N_MEM, MEM_HEADS, MEM_HD))

        rq, rk, rv, rg, sq, sk, sv, mq, g_r, g_s, g_m = _inproj(xs, w_in_bf, cos_s, sin_s, tm_s, F32)
        gn, s_s = _ret_sample(rq, rk, rv, state_ret, l, tab_s, gn_g, dec_t, ret_seqs)
        swa_o, kto, vto = _swa_sample(sq, sk, sv, cache_kt, cache_vt, sinks, l, dec_t, WINDOW // dec_t)
        mem_o = _mem_sample(mq, cache_mk, cache_mv, l, dec_t, 8)
        xs = _finish(xs, gn, rg, swa_o, mem_o, g_r, g_s, g_m, lw, tm_s)
        ret_s.append(s_s)
        swk_s.append(kto)
        swv_s.append(vto)

    from_t = lambda a: jnp.transpose(jnp.stack(a), (0, 1, 4, 2, 3))
    return (xp.reshape(batch, seq, D_MODEL), xs.reshape(dec_b, dec_t, D_MODEL),
            jnp.stack(ret_p), jnp.stack(swk_p), jnp.stack(swv_p), jnp.stack(mk_p), jnp.stack(mv_p),
            jnp.stack(ret_s), from_t(swk_s), from_t(swv_s))
```

```python
import functools

import jax
import jax.numpy as jnp
from jax import lax
from jax.experimental import pallas as pl
from jax.experimental.pallas import tpu as pltpu

F32 = jnp.float32
BF16 = jnp.bfloat16

D_MODEL = 1024
DEPTH = 2
PAST_LEN = 16384
RET_HEADS = 8
RET_DK = 64
RET_DV = 128
RET_CHUNK = 128
SWA_HEADS = 8
SWA_KV_HEADS = 2
SWA_GROUP = SWA_HEADS // SWA_KV_HEADS
SWA_HD = 64
WINDOW = 128
MEM_HEADS = 4
MEM_HD = 128
N_MEM = 256
D_FF = 4 * D_MODEL
ROPE_THETA = 10000.0
LN_EPS = 1e-5
GN_EPS = 1e-5
ALPHA = (2 * DEPTH) ** 0.25

RET_QK_W = RET_HEADS * RET_DK
RET_V_W = RET_HEADS * RET_DV
SWA_Q_W = SWA_HEADS * SWA_HD
SWA_KV_W = SWA_KV_HEADS * SWA_HD
MEM_W = MEM_HEADS * MEM_HD
OFF_RQ = 0
OFF_RK = OFF_RQ + RET_QK_W
OFF_RV = OFF_RK + RET_QK_W
OFF_RG = OFF_RV + RET_V_W
OFF_SQ = OFF_RG + RET_V_W
OFF_SK = OFF_SQ + SWA_Q_W
OFF_SV = OFF_SK + SWA_KV_W
OFF_MQ = OFF_SV + SWA_KV_W
OFF_GR = OFF_MQ + MEM_W
OFF_GS = OFF_GR + D_MODEL
OFF_GM = OFF_GS + D_MODEL
IN_W = OFF_GM + D_MODEL

LANES = 128
V7X_VMEM_LIMIT = 56 * 1024 * 1024


def _const_spec(shape):
    nd = len(shape)
    return pl.BlockSpec(shape, lambda *_: (0,) * nd, pipeline_mode=pl.Buffered(1))


def _params(n_grid):
    return pltpu.CompilerParams(dimension_semantics=("arbitrary",) * n_grid, vmem_limit_bytes=V7X_VMEM_LIMIT)


def _inproj_kernel(x_ref, w_ref, cos_ref, sin_ref,
                   rq_ref, rk_ref, rv_ref, rg_ref, sq_ref, sk_ref, sv_ref, mq_ref, gr_ref, gs_ref, gm_ref):
    xb = x_ref[...].astype(BF16)
    cos = cos_ref[...]
    sin = sin_ref[...]
    lane = lax.broadcasted_iota(jnp.int32, cos.shape, 1)
    first_half = (lane & (SWA_HD // 2)) == 0

    def proj(off, width):
        return jnp.dot(xb, w_ref[:, off:off + width], preferred_element_type=F32)

    def rope_store(off, width, out_ref, scale):
        y = proj(off, width)
        for j in range(width // LANES):
            yj = y[:, j * LANES:(j + 1) * LANES]
            sw = jnp.where(first_half, pltpu.roll(yj, LANES - SWA_HD // 2, 1), pltpu.roll(yj, SWA_HD // 2, 1))
            r = yj * cos + sw * sin
            if scale != 1.0:
                r = r * scale
            out_ref[:, j * LANES:(j + 1) * LANES] = r.astype(out_ref.dtype)

    def plain_store(off, width, out_ref):
        out_ref[...] = proj(off, width).astype(out_ref.dtype)

    rope_store(OFF_RQ, RET_QK_W, rq_ref, 1.0)
    rope_store(OFF_RK, RET_QK_W, rk_ref, RET_DK ** -0.5)
    plain_store(OFF_RV, RET_V_W, rv_ref)
    plain_store(OFF_RG, RET_V_W, rg_ref)
    rope_store(OFF_SQ, SWA_Q_W, sq_ref, SWA_HD ** -0.5)
    rope_store(OFF_SK, SWA_KV_W, sk_ref, 1.0)
    plain_store(OFF_SV, SWA_KV_W, sv_ref)
    plain_store(OFF_MQ, MEM_W, mq_ref)
    plain_store(OFF_GR, D_MODEL, gr_ref)
    plain_store(OFF_GS, D_MODEL, gs_ref)
    plain_store(OFF_GM, D_MODEL, gm_ref)


def _inproj(x2d, w_bf, cos_tab, sin_tab, tm, qkv_dtype):
    m = x2d.shape[0]
    n_tab = cos_tab.shape[0] // tm
    row = lambda w: pl.BlockSpec((tm, w), lambda i: (i, 0))
    tab = pl.BlockSpec((tm, LANES), lambda i: (i % n_tab, 0))
    widths_dtypes = [(RET_QK_W, qkv_dtype), (RET_QK_W, qkv_dtype), (RET_V_W, qkv_dtype), (RET_V_W, F32),
                     (SWA_Q_W, qkv_dtype), (SWA_KV_W, F32), (SWA_KV_W, F32), (MEM_W, qkv_dtype),
                     (D_MODEL, F32), (D_MODEL, F32), (D_MODEL, F32)]
    return pl.pallas_call(
        _inproj_kernel,
        grid=(m // tm,),
        in_specs=[row(D_MODEL), _const_spec(w_bf.shape), tab, tab],
        out_specs=[row(w) for w, _ in widths_dtypes],
        out_shape=[jax.ShapeDtypeStruct((m, w), dt) for w, dt in widths_dtypes],
        compiler_params=_params(1),
        name="inproj",
    )(x2d, w_bf, cos_tab, sin_tab)


def _group_norm(o, g_row):
    mu = jnp.mean(o, -1, keepdims=True)
    d = o - mu
    var = jnp.mean(d * d, -1, keepdims=True)
    return d * lax.rsqrt(var + GN_EPS) * g_row


def _ret_tables(lg, n_rows, period):
    r = jnp.arange(n_rows)
    t = (r % period).astype(F32)
    same = (r[:, None] // period) == (r[None, :] // period)
    diff = t[:, None] - t[None, :]
    decay = jnp.where((diff >= 0) & same, jnp.exp(lg[:, None, None] * jnp.maximum(diff, 0.0)), 0.0)
    rowdec = jnp.exp(lg[:, None] * (t[None, :] + 1.0))
    wend = jnp.exp(lg[:, None] * (period - 1.0 - t[None, :]))
    gl = jnp.exp(lg * period)
    rowdec = jnp.broadcast_to(rowdec[:, :, None], (RET_HEADS, n_rows, RET_DV))
    wend = jnp.broadcast_to(wend[:, :, None], (RET_HEADS, n_rows, RET_DK))
    gl = jnp.broadcast_to(gl[:, None, None], (RET_HEADS, 1, RET_DV))
    return decay, rowdec, wend, gl


def _ret_sample_kernel(n_seq, t_len, q_ref, k_ref, v_ref, s_ref, decay_ref, rowdec_ref, wend_ref, gl_ref, gn_ref,
                       o_ref, s_out_ref):
    rows = n_seq * t_len
    seq_of_row = lax.broadcasted_iota(jnp.int32, (rows, 1), 0) // t_len
    for h in range(RET_HEADS):
        q = q_ref[:, h * RET_DK:(h + 1) * RET_DK].astype(BF16)
        kf = k_ref[:, h * RET_DK:(h + 1) * RET_DK]
        k = kf.astype(BF16)
        v = v_ref[:, h * RET_DV:(h + 1) * RET_DV].astype(BF16)
        sc = lax.dot_general(q, k, (((1,), (1,)), ((), ())), preferred_element_type=F32) * decay_ref[h]
        o = jnp.dot(sc.astype(BF16), v, preferred_element_type=F32)
        kw = kf * wend_ref[h]
        o_state = jnp.zeros((rows, RET_DV), F32)
        for b in range(n_seq):
            mine = seq_of_row == b
            s0 = s_ref[0, b, h]
            o_state = o_state + jnp.where(mine, jnp.dot(q, s0.astype(BF16), preferred_element_type=F32), 0.0)
            kw_b = jnp.where(mine, kw, 0.0).astype(BF16)
            s_out_ref[b, h] = gl_ref[h] * s0 + lax.dot_general(
                kw_b, v, (((0,), (0,)), ((), ())), preferred_element_type=F32)
        o = o + o_state * rowdec_ref[h]
        o_ref[:, h * RET_DV:(h + 1) * RET_DV] = _group_norm(o, gn_ref[h:h + 1, :])


def _ret_sample(rq, rk, rv, state, layer, tables, gn_g, t_len, n_seq):
    m = rq.shape[0]
    rows = n_seq * t_len
    decay, rowdec, wend, gl = tables
    row = lambda w: pl.BlockSpec((rows, w), lambda i: (i, 0))
    st_in = pl.BlockSpec((1, n_seq, RET_HEADS, RET_DK, RET_DV), lambda i: (layer, i, 0, 0, 0))
    st_out = pl.BlockSpec((n_seq, RET_HEADS, RET_DK, RET_DV), lambda i: (i, 0, 0, 0))
    return pl.pallas_call(
        functools.partial(_ret_sample_kernel, n_seq, t_len),
        grid=(m // rows,),
        in_specs=[row(RET_QK_W), row(RET_QK_W), row(RET_V_W), st_in,
                  _const_spec(decay.shape), _const_spec(rowdec.shape), _const_spec(wend.shape),
                  _const_spec(gl.shape), _const_spec(gn_g.shape)],
        out_specs=[row(RET_V_W), st_out],
        out_shape=[jax.ShapeDtypeStruct((m, RET_V_W), F32), jax.ShapeDtypeStruct(state.shape[1:], F32)],
        compiler_params=_params(1),
        name="ret_sample",
    )(rq, rk, rv, state, decay, rowdec, wend, gl, gn_g)


def _swa_sample_kernel(n_seq, t_len, sinks_ref, q_ref, kn_ref, vn_ref, kt_ref, vt_ref, o_ref, kto_ref, vto_ref):
    grp_rows = SWA_GROUP * t_len
    n_all = n_seq * grp_rows
    q = q_ref[...]
    kn = kn_ref[...]
    vn = vn_ref[...]
    kn_t = kn.T
    vn_t = vn.T
    r = lax.broadcasted_iota(jnp.int32, (n_all, 1), 0)
    t_q = r % t_len
    g_row = (r // t_len) % SWA_GROUP
    b_row = r // grp_rows
    c = lax.broadcasted_iota(jnp.int32, (1, WINDOW), 1)
    valid_cache = c > t_q
    valid_new = ((c // t_len) == b_row) & ((c % t_len) <= t_q)
    lane = lax.broadcasted_iota(jnp.int32, (SWA_HD, WINDOW), 1)
    is_new_lane = lane >= WINDOW - t_len
    pieces = []
    for kvh in range(SWA_KV_HEADS):
        hd = slice(kvh * SWA_HD, (kvh + 1) * SWA_HD)
        qg = [q[:, (kvh * SWA_GROUP + g) * SWA_HD:(kvh * SWA_GROUP + g + 1) * SWA_HD] for g in range(SWA_GROUP)]
        q_all = jnp.concatenate([qg[g][b * t_len:(b + 1) * t_len] for b in range(n_seq) for g in range(SWA_GROUP)],
                                0).astype(BF16)
        s_new = jnp.dot(q_all, kn_t[hd].astype(BF16), preferred_element_type=F32)
        s_cache = jnp.concatenate(
            [jnp.dot(q_all[b * grp_rows:(b + 1) * grp_rows], kt_ref[0, b, kvh].astype(BF16),
                     preferred_element_type=F32) for b in range(n_seq)], 0)
        s_new = jnp.where(valid_new, s_new, -jnp.inf)
        s_cache = jnp.where(valid_cache, s_cache, -jnp.inf)
        sink = jnp.full((n_all, 1), sinks_ref[kvh * SWA_GROUP], F32)
        for g in range(1, SWA_GROUP):
            sink = jnp.where(g_row == g, sinks_ref[kvh * SWA_GROUP + g], sink)
        m = jnp.maximum(jnp.maximum(jnp.max(s_new, -1, keepdims=True), jnp.max(s_cache, -1, keepdims=True)), sink)
        e_new = jnp.exp(s_new - m)
        e_cache = jnp.exp(s_cache - m)
        den = jnp.sum(e_new, -1, keepdims=True) + jnp.sum(e_cache, -1, keepdims=True) + jnp.exp(sink - m)
        p_new = (e_new / den).astype(BF16)
        p_cache = (e_cache / den).astype(BF16)
        o = jnp.dot(p_new, vn[:, hd].astype(BF16), preferred_element_type=F32)
        o = o + jnp.concatenate(
            [lax.dot_general(p_cache[b * grp_rows:(b + 1) * grp_rows], vt_ref[0, b, kvh].astype(BF16),
                             (((1,), (1,)), ((), ())), preferred_element_type=F32) for b in range(n_seq)], 0)
        for g in range(SWA_GROUP):
            pieces.append(jnp.concatenate(
                [o[b * grp_rows + g * t_len:b * grp_rows + (g + 1) * t_len] for b in range(n_seq)], 0))
        for b in range(n_seq):
            shift_new = (WINDOW - t_len - b * t_len) % WINDOW
            kto_ref[b, kvh] = jnp.where(is_new_lane, pltpu.roll(kn_t[hd], shift_new, 1),
                                        pltpu.roll(kt_ref[0, b, kvh], WINDOW - t_len, 1))
            vto_ref[b, kvh] = jnp.where(is_new_lane, pltpu.roll(vn_t[hd], shift_new, 1),
                                        pltpu.roll(vt_ref[0, b, kvh], WINDOW - t_len, 1))
    o_ref[...] = jnp.concatenate(pieces, -1).astype(o_ref.dtype)


def _swa_sample(sq, sk, sv, cache_kt, cache_vt, sinks, layer, t_len, n_seq):
    m = sq.shape[0]
    rows = n_seq * t_len
    assert rows == WINDOW and cache_kt.shape[-1] == WINDOW
    row = lambda w: pl.BlockSpec((rows, w), lambda i: (i, 0))
    cin = pl.BlockSpec((1, n_seq, SWA_KV_HEADS, SWA_HD, WINDOW), lambda i: (layer, i, 0, 0, 0))
    cout = pl.BlockSpec((n_seq, SWA_KV_HEADS, SWA_HD, WINDOW), lambda i: (i, 0, 0, 0))
    cshape = jax.ShapeDtypeStruct(cache_kt.shape[1:], F32)
    return pl.pallas_call(
        functools.partial(_swa_sample_kernel, n_seq, t_len),
        grid=(m // rows,),
        in_specs=[pl.BlockSpec(memory_space=pltpu.SMEM), row(SWA_Q_W), row(SWA_KV_W), row(SWA_KV_W), cin, cin],
        out_specs=[row(SWA_Q_W), cout, cout],
        out_shape=[jax.ShapeDtypeStruct((m, SWA_Q_W), BF16), cshape, cshape],
        compiler_params=_params(1),
        name="swa_sample",
    )(sinks, sq, sk, sv, cache_kt, cache_vt)


def _mem_sample_kernel(n_seq, t_len, q_ref, k_ref, v_ref, o_ref):
    rows = MEM_HEADS * t_len
    head_of_row = lax.broadcasted_iota(jnp.int32, (rows, 1), 0) // t_len
    head_of_col = lax.broadcasted_iota(jnp.int32, (1, N_MEM * MEM_HEADS), 1) % MEM_HEADS
    valid = head_of_row == head_of_col
    outs = []
    for b in range(n_seq):
        qb = q_ref[b * t_len:(b + 1) * t_len, :]
        q_all = jnp.concatenate([qb[:, h * MEM_HD:(h + 1) * MEM_HD] for h in range(MEM_HEADS)], 0).astype(BF16)
        s = lax.dot_general(q_all, k_ref[0, b].astype(BF16), (((1,), (1,)), ((), ())),
                            preferred_element_type=F32) * (MEM_HD ** -0.5)
        s = jnp.where(valid, s, -jnp.inf)
        m = jnp.max(s, -1, keepdims=True)
        e = jnp.exp(s - m)
        p = e / jnp.sum(e, -1, keepdims=True)
        o = jnp.dot(p.astype(BF16), v_ref[0, b].astype(BF16), preferred_element_type=F32)
        outs.append(jnp.concatenate([o[h * t_len:(h + 1) * t_len] for h in range(MEM_HEADS)], -1))
    o_ref[...] = jnp.concatenate(outs, 0).astype(o_ref.dtype)


def _mem_sample(mq, cache_k, cache_v, layer, t_len, n_seq):
    m = mq.shape[0]
    rows = n_seq * t_len
    row = pl.BlockSpec((rows, MEM_W), lambda i: (i, 0))
    kv = pl.BlockSpec((1, n_seq, N_MEM * MEM_HEADS, MEM_HD), lambda i: (layer, i, 0, 0))
    return pl.pallas_call(
        functools.partial(_mem_sample_kernel, n_seq, t_len),
        grid=(m // rows,),
        in_specs=[row, kv, kv],
        out_specs=row,
        out_shape=jax.ShapeDtypeStruct((m, MEM_W), BF16),
        compiler_params=_params(1),
        name="mem_sample",
    )(mq, cache_k, cache_v)


def _pair_tables(tables):
    decay, rowdec, wend, gl = tables
    pair = lambda a: jnp.concatenate([a[0::2], a[1::2]], -1)
    gl_rows = jnp.concatenate([jnp.broadcast_to(gl[0::2], (RET_HEADS // 2, RET_DK, RET_DV)),
                               jnp.broadcast_to(gl[1::2], (RET_HEADS // 2, RET_DK, RET_DV))], 1)
    return pair(decay), pair(rowdec), pair(wend), gl_rows


def _run_interleaved(tasks):
    pending, active = list(tasks), []
    while pending or active:
        if pending:
            active.append(pending.pop(0))
        for t in list(active):
            try:
                next(t)
            except StopIteration:
                active.remove(t)


def _mix_prompt_kernel(nb, sinks_ref, rq_ref, rk_ref, rv_ref, sq_ref, sk_ref, sv_ref, mq_ref, mk_ref, mv_ref,
                       decay_ref, rowdec_ref, wend_ref, gl_ref, gn_ref,
                       gn_out, swa_out, mem_out, s_out,
                       s_scr, kp_scr, kpr_scr, vp_scr, vpr_scr):
    c = pl.program_id(0)

    @pl.when(c == 0)
    def _():
        s_scr[...] = jnp.zeros_like(s_scr)
        for scr in (kp_scr, kpr_scr, vp_scr, vpr_scr):
            scr[...] = jnp.zeros_like(scr)

    pair_w = 2 * SWA_HD
    lane_lo = lax.broadcasted_iota(jnp.int32, (RET_CHUNK, pair_w), 1) < SWA_HD
    row_lo = lax.broadcasted_iota(jnp.int32, (2 * RET_DK, RET_DV), 0) < RET_DK
    lane_lo_kv = lax.broadcasted_iota(jnp.int32, (2 * WINDOW, pair_w), 1) < SWA_HD
    i = lax.broadcasted_iota(jnp.int32, (WINDOW, 2 * WINDOW), 0)
    j = lax.broadcasted_iota(jnp.int32, (WINDOW, 2 * WINDOW), 1)
    valid = (j <= i + WINDOW) & (j > i) & ((c > 0) | (j >= WINDOW))


    def ret_task(b, p):
        qk = slice(p * 2 * RET_DK, (p + 1) * 2 * RET_DK)
        vv = slice(p * 2 * RET_DV, (p + 1) * 2 * RET_DV)
        q2, k2, v2, s2 = rq_ref[b, :, qk], rk_ref[b, :, qk], rv_ref[b, :, vv], s_scr[b, p]
        zk = jnp.zeros_like(k2)
        k_rows = jnp.concatenate([jnp.where(lane_lo, k2, zk), jnp.where(lane_lo, zk, k2)], 0)
        sc_raw = lax.dot_general(q2, k_rows, (((1,), (1,)), ((), ())), preferred_element_type=F32)
        s2b = s2.astype(BF16)
        zs = jnp.zeros_like(s2b)
        s_bd = jnp.concatenate([jnp.where(row_lo, s2b, zs), jnp.where(row_lo, zs, s2b)], 1)
        os_raw = jnp.dot(q2, s_bd, preferred_element_type=F32)
        kw2 = (k2.astype(F32) * wend_ref[p]).astype(BF16)
        upd = lax.dot_general(kw2, v2, (((0,), (0,)), ((), ())), preferred_element_type=F32)
        yield
        zv = jnp.zeros((RET_CHUNK, RET_DV), v2.dtype)
        v_bd = jnp.concatenate([jnp.concatenate([v2[:, :RET_DV], zv], 1),
                                jnp.concatenate([zv, v2[:, RET_DV:]], 1)], 0)
        o_raw = jnp.dot((sc_raw * decay_ref[p]).astype(BF16), v_bd, preferred_element_type=F32)
        s_scr[b, p] = gl_ref[p] * s2 + jnp.where(row_lo, upd[:, :RET_DV], upd[:, RET_DV:])
        yield
        o2 = o_raw + os_raw * rowdec_ref[p]
        for u in range(2):
            lo = (2 * p + u) * RET_DV
            gn_out[b, :, lo:lo + RET_DV] = _group_norm(o2[:, u * RET_DV:(u + 1) * RET_DV],
                                                       gn_ref[2 * p + u:2 * p + u + 1, :])

    kv_ctx = {}

    def swa_prep(b):
        k_cur, v_cur = sk_ref[b], sv_ref[b]
        kb, kbr = k_cur.astype(BF16), pltpu.roll(k_cur, SWA_HD, 1).astype(BF16)
        vb, vbr = v_cur.astype(BF16), pltpu.roll(v_cur, SWA_HD, 1).astype(BF16)
        kv_ctx[b] = (jnp.concatenate([kp_scr[b], kb], 0), jnp.concatenate([kpr_scr[b], kbr], 0),
                     jnp.concatenate([vp_scr[b], vb], 0), jnp.concatenate([vpr_scr[b], vbr], 0))
        kp_scr[b], kpr_scr[b], vp_scr[b], vpr_scr[b] = kb, kbr, vb, vbr

    def swa_task(b, kvh):
        if b not in kv_ctx:
            swa_prep(b)
        kc, kcr, vc, vcr = kv_ctx[b]
        zkv = jnp.zeros_like(kc)
        k_lo, k_hi = (kc, kcr) if kvh == 0 else (kcr, kc)
        v_lo, v_hi = (vc, vcr) if kvh == 0 else (vcr, vc)
        k_rows = jnp.concatenate([jnp.where(lane_lo_kv, k_lo, zkv), jnp.where(lane_lo_kv, zkv, k_hi)], 0)
        v_rows = jnp.concatenate([jnp.where(lane_lo_kv, v_lo, zkv), jnp.where(lane_lo_kv, zkv, v_hi)], 0)
        n_pairs = SWA_GROUP // 2
        pairs = [kvh * n_pairs + jj for jj in range(n_pairs)]
        q4 = jnp.concatenate([sq_ref[b, :, pr * pair_w:(pr + 1) * pair_w] for pr in pairs], 0)
        s4 = lax.dot_general(q4, k_rows, (((1,), (1,)), ((), ())), preferred_element_type=F32)
        yield
        rows = []
        for jj, pr in enumerate(pairs):
            ps = []
            for u in range(2):
                s = jnp.where(valid, s4[jj * WINDOW:(jj + 1) * WINDOW, u * 2 * WINDOW:(u + 1) * 2 * WINDOW], -jnp.inf)
                sink = sinks_ref[2 * pr + u]
                m = jnp.maximum(jnp.max(s, -1, keepdims=True), sink)
                e = jnp.exp(s - m)
                den = jnp.sum(e, -1, keepdims=True) + jnp.exp(sink - m)
                ps.append((e / den).astype(BF16))
            rows.append(jnp.concatenate(ps, 1))
        o4 = jnp.dot(jnp.concatenate(rows, 0), v_rows, preferred_element_type=F32)
        yield
        for jj, pr in enumerate(pairs):
            swa_out[b, :, pr * pair_w:(pr + 1) * pair_w] = o4[jj * WINDOW:(jj + 1) * WINDOW].astype(swa_out.dtype)

    def mem_task(b, h):
        sl = slice(h * MEM_HD, (h + 1) * MEM_HD)
        s = lax.dot_general(mq_ref[b, :, sl], mk_ref[b, :, sl], (((1,), (1,)), ((), ())),
                            preferred_element_type=F32) * (MEM_HD ** -0.5)
        yield
        m = jnp.max(s, -1, keepdims=True)
        e = jnp.exp(s - m)
        p_att = (e / jnp.sum(e, -1, keepdims=True)).astype(BF16)
        o = jnp.dot(p_att, mv_ref[b, :, sl], preferred_element_type=F32)
        yield
        mem_out[b, :, sl] = o.astype(mem_out.dtype)

    def region(make_tasks):
        @pl.when(c >= 0)
        def _():
            _run_interleaved(make_tasks())

    for b in range(nb):
        region(lambda b=b: [ret_task(b, p) for p in range(RET_HEADS // 2)])
        region(lambda b=b: [swa_task(b, kvh) for kvh in range(SWA_KV_HEADS)])
        region(lambda b=b: [mem_task(b, h) for h in range(MEM_HEADS)])

    @pl.when(c == pl.num_programs(0) - 1)
    def _():
        s_out[...] = s_scr[...]


def _mix_prompt(rq, rk, rv, sq, sk, sv, mq, mk_bf, mv_bf, sinks, pair_tables, gn_g):
    nb, seq, _ = rq.shape
    decay2, rowdec2, wend2, gl2 = pair_tables
    chunk = lambda w: pl.BlockSpec((nb, RET_CHUNK, w), lambda c: (0, c, 0))
    st_shape = (nb, RET_HEADS // 2, 2 * RET_DK, RET_DV)
    kv_scr = pltpu.VMEM((nb, WINDOW, SWA_KV_W), BF16)
    return pl.pallas_call(
        functools.partial(_mix_prompt_kernel, nb),
        grid=(seq // RET_CHUNK,),
        in_specs=[pl.BlockSpec(memory_space=pltpu.SMEM),
                  chunk(RET_QK_W), chunk(RET_QK_W), chunk(RET_V_W), chunk(SWA_Q_W), chunk(SWA_KV_W), chunk(SWA_KV_W),
                  chunk(MEM_W), _const_spec(mk_bf.shape), _const_spec(mv_bf.shape),
                  _const_spec(decay2.shape), _const_spec(rowdec2.shape), _const_spec(wend2.shape),
                  _const_spec(gl2.shape), _const_spec(gn_g.shape)],
        out_specs=[chunk(RET_V_W), chunk(SWA_Q_W), chunk(MEM_W), _const_spec(st_shape)],
        out_shape=[jax.ShapeDtypeStruct((nb, seq, RET_V_W), F32), jax.ShapeDtypeStruct((nb, seq, SWA_Q_W), BF16),
                   jax.ShapeDtypeStruct((nb, seq, MEM_W), BF16), jax.ShapeDtypeStruct(st_shape, F32)],
        scratch_shapes=[pltpu.VMEM(st_shape, F32), kv_scr, kv_scr, kv_scr, kv_scr],
        compiler_params=_params(1),
        name="mix_prompt",
    )(sinks, rq, rk, rv, sq, sk, sv, mq, mk_bf, mv_bf, decay2, rowdec2, wend2, gl2, gn_g)


def _mem_kv_kernel(x_ref, w_ref, k_ref, v_ref, kb_ref, vb_ref):
    kv = jnp.dot(x_ref[...].astype(BF16), w_ref[...], preferred_element_type=F32)
    k_ref[...] = kv[:, :MEM_W]
    v_ref[...] = kv[:, MEM_W:]
    kb_ref[...] = kv[:, :MEM_W].astype(BF16)
    vb_ref[...] = kv[:, MEM_W:].astype(BF16)


def _mem_kv(mem2d, w_bf, tm):
    m, k = mem2d.shape
    out = pl.BlockSpec((tm, MEM_W), lambda i: (i, 0))
    return pl.pallas_call(
        _mem_kv_kernel,
        grid=(m // tm,),
        in_specs=[pl.BlockSpec((tm, k), lambda i: (i, 0)), _const_spec(w_bf.shape)],
        out_specs=[out, out, out, out],
        out_shape=[jax.ShapeDtypeStruct((m, MEM_W), F32), jax.ShapeDtypeStruct((m, MEM_W), F32),
                   jax.ShapeDtypeStruct((m, MEM_W), BF16), jax.ShapeDtypeStruct((m, MEM_W), BF16)],
        compiler_params=_params(1),
        name="mem_kv",
    )(mem2d, w_bf)


def _layer_norm(x, g, b):
    mu = jnp.mean(x, -1, keepdims=True)
    d = x - mu
    var = jnp.mean(d * d, -1, keepdims=True)
    return d * lax.rsqrt(var + LN_EPS) * g + b


def _finish_kernel(x_ref, gn_ref, rg_ref, swa_ref, mem_ref, gr_ref, gs_ref, gm_ref,
                   wr_ref, ws_ref, wm_ref, wo_ref, l1g_ref, l1b_ref, wu_ref, wd_ref, l2g_ref, l2b_ref, o_ref):
    rg = rg_ref[...]
    ret_in = (rg * jax.nn.sigmoid(rg) * gn_ref[...]).astype(BF16)
    ret_b = jnp.dot(ret_in, wr_ref[...], preferred_element_type=F32)
    swa_b = jnp.dot(swa_ref[...], ws_ref[...], preferred_element_type=F32)
    mem_b = jnp.dot(mem_ref[...], wm_ref[...], preferred_element_type=F32)
    merged = (jax.nn.sigmoid(gr_ref[...]) * ret_b + jax.nn.sigmoid(gs_ref[...]) * swa_b
              + jax.nn.sigmoid(gm_ref[...]) * mem_b)
    y = jnp.dot(merged.astype(BF16), wo_ref[...], preferred_element_type=F32)
    x1 = _layer_norm(ALPHA * x_ref[...] + y, l1g_ref[...], l1b_ref[...])
    x1b = x1.astype(BF16)
    ff_chunk = D_FF // 4
    acc = jnp.zeros_like(x1)
    for c in range(D_FF // ff_chunk):
        h = jnp.dot(x1b, wu_ref[:, c * ff_chunk:(c + 1) * ff_chunk], preferred_element_type=F32)
        h = jnp.square(jnp.maximum(h, 0.0)).astype(BF16)
        acc = acc + jnp.dot(h, wd_ref[c * ff_chunk:(c + 1) * ff_chunk, :], preferred_element_type=F32)
    o_ref[...] = _layer_norm(ALPHA * x1 + acc, l2g_ref[...], l2b_ref[...])


def _finish(x2d, gn, rg, swa_o, mem_o, g_r, g_s, g_m, lw, tm):
    m = x2d.shape[0]
    row = lambda w: pl.BlockSpec((tm, w), lambda i: (i, 0))
    return pl.pallas_call(
        _finish_kernel,
        grid=(m // tm,),
        in_specs=[row(D_MODEL), row(RET_V_W), row(RET_V_W), row(SWA_Q_W), row(MEM_W),
                  row(D_MODEL), row(D_MODEL), row(D_MODEL)] + [_const_spec(w.shape) for w in lw],
        out_specs=row(D_MODEL),
        out_shape=jax.ShapeDtypeStruct((m, D_MODEL), F32),
        compiler_params=_params(1),
        name="finish",
    )(x2d, gn, rg, swa_o, mem_o, g_r, g_s, g_m, *lw)


def _rope_tables(pos):
    half = SWA_HD // 2
    inv = jnp.power(ROPE_THETA, -jnp.arange(half, dtype=F32) / half)
    ang = pos.astype(F32)[:, None] * inv[None, :]
    c, s = jnp.cos(ang), jnp.sin(ang)
    return jnp.concatenate([c, c, c, c], -1), jnp.concatenate([-s, s, -s, s], -1)


def kernel(x_prompt, x_sample, state_ret, cache_swa_k, cache_swa_v, cache_mem_k, cache_mem_v, mem_prompt,
           w_in, w_br_ret, w_br_swa, w_br_mem, w_out, w_mem_kv, attn_sinks, ret_gn_g,
           ln1_g, ln1_b, w_up, w_down, ln2_g, ln2_b):
    batch, seq, _ = x_prompt.shape
    dec_b, dec_t, _ = x_sample.shape
    tm_p, tm_s = 256, 256
    ret_seqs = RET_CHUNK // dec_t

    cos_p, sin_p = _rope_tables(jnp.arange(seq, dtype=jnp.int32))
    cos_s, sin_s = _rope_tables(PAST_LEN + jnp.arange(dec_t, dtype=jnp.int32))
    cos_s, sin_s = jnp.tile(cos_s, (tm_s // dec_t, 1)), jnp.tile(sin_s, (tm_s // dec_t, 1))
    lg = jnp.log1p(-jnp.exp2(-5.0 - jnp.arange(RET_HEADS, dtype=F32)))
    tab_p = _ret_tables(lg, RET_CHUNK, RET_CHUNK)
    tab_p2 = _pair_tables(tab_p)
    tab_s = _ret_tables(lg, RET_CHUNK, dec_t)

    xp = x_prompt.reshape(batch * seq, D_MODEL)
    xs = x_sample.reshape(dec_b * dec_t, D_MODEL)
    mem2d = mem_prompt.reshape(batch * N_MEM, D_MODEL)
    cache_kt = jnp.transpose(cache_swa_k, (0, 1, 3, 4, 2))
    cache_vt = jnp.transpose(cache_swa_v, (0, 1, 3, 4, 2))
    cache_mk = cache_mem_k.reshape(DEPTH, dec_b, N_MEM * MEM_HEADS, MEM_HD)
    cache_mv = cache_mem_v.reshape(DEPTH, dec_b, N_MEM * MEM_HEADS, MEM_HD)

    ret_p, swk_p, swv_p, mk_p, mv_p, ret_s, swk_s, swv_s = [], [], [], [], [], [], [], []
    for l in range(DEPTH):
        w_in_bf = w_in[l].astype(BF16)
        row = lambda a: a[l].reshape(1, D_MODEL)
        lw = (w_br_ret[l].astype(BF16), w_br_swa[l].astype(BF16), w_br_mem[l].astype(BF16), w_out[l].astype(BF16),
              row(ln1_g), row(ln1_b), w_up[l].astype(BF16), w_down[l].astype(BF16), row(ln2_g), row(ln2_b))
        gn_g = ret_gn_g[l]
        sinks = attn_sinks[l]

        rq, rk, rv, rg, sq, sk, sv, mq, g_r, g_s, g_m = _inproj(xp, w_in_bf, cos_p, sin_p, tm_p, BF16)
        mk, mv, mk_bf, mv_bf = _mem_kv(mem2d, w_mem_kv[l].astype(BF16), 256)
        by_seq = lambda a: a.reshape(batch, seq, a.shape[-1])
        by_mem = lambda a: a.reshape(batch, N_MEM, MEM_W)
        gn, swa_o, mem_o, s_p = _mix_prompt(by_seq(rq), by_seq(rk), by_seq(rv), by_seq(sq), by_seq(sk), by_seq(sv),
                                            by_seq(mq), by_mem(mk_bf), by_mem(mv_bf), sinks, tab_p2, gn_g)
        flat = lambda a: a.reshape(batch * seq, a.shape[-1])
        xp = _finish(xp, flat(gn), rg, flat(swa_o), flat(mem_o), g_r, g_s, g_m, lw, tm_p)
        ret_p.append(s_p.reshape(batch, RET_HEADS, RET_DK, RET_DV))
        swk_p.append(sk.reshape(batch, seq, SWA_KV_W)[:, -WINDOW:].reshape(batch, WINDOW, SWA_KV_HEADS, SWA_HD))
        swv_p.append(sv.reshape(batch, seq, SWA_KV_W)[:, -WINDOW:].reshape(batch, WINDOW, SWA_KV_HEADS, SWA_HD))
        mk_p.append(mk.reshape(batch, N_MEM, MEM_HEADS, MEM_HD))
        mv_p.append(mv.reshape(batch, N_MEM, MEM_HEADS, MEM_HD))

        rq, rk, rv, rg, sq, sk, sv, mq, g_r, g_s, g_m = _inproj(xs, w_in_bf, cos_s, sin_s, tm_s, F32)
        gn, s_s = _ret_sample(rq, rk, rv, state_ret, l, tab_s, gn_g, dec_t, ret_seqs)
        swa_o, kto, vto = _swa_sample(sq, sk, sv, cache_kt, cache_vt, sinks, l, dec_t, WINDOW // dec_t)
        mem_o = _mem_sample(mq, cache_mk, cache_mv, l, dec_t, 8)
        xs = _finish(xs, gn, rg, swa_o, mem_o, g_r, g_s, g_m, lw, tm_s)
        ret_s.append(s_s)
        swk_s.append(kto)
        swv_s.append(vto)

    from_t = lambda a: jnp.transpose(jnp.stack(a), (0, 1, 4, 2, 3))
    return (xp.reshape(batch, seq, D_MODEL), xs.reshape(dec_b, dec_t, D_MODEL),
            jnp.stack(ret_p), jnp.stack(swk_p), jnp.stack(swv_p), jnp.stack(mk_p), jnp.stack(mv_p),
            jnp.stack(ret_s), from_t(swk_s), from_t(swv_s))
```

```python
import functools

import jax
import jax.numpy as jnp
from jax import lax
from jax.experimental import pallas as pl
from jax.experimental.pallas import tpu as pltpu

F32 = jnp.float32
BF16 = jnp.bfloat16

D_MODEL = 1024
DEPTH = 2
PAST_LEN = 16384
RET_HEADS = 8
RET_DK = 64
RET_DV = 128
RET_CHUNK = 128
SWA_HEADS = 8
SWA_KV_HEADS = 2
SWA_GROUP = SWA_HEADS // SWA_KV_HEADS
SWA_HD = 64
WINDOW = 128
MEM_HEADS = 4
MEM_HD = 128
N_MEM = 256
D_FF = 4 * D_MODEL
ROPE_THETA = 10000.0
LN_EPS = 1e-5
GN_EPS = 1e-5
ALPHA = (2 * DEPTH) ** 0.25

RET_QK_W = RET_HEADS * RET_DK
RET_V_W = RET_HEADS * RET_DV
SWA_Q_W = SWA_HEADS * SWA_HD
SWA_KV_W = SWA_KV_HEADS * SWA_HD
MEM_W = MEM_HEADS * MEM_HD
OFF_RQ = 0
OFF_RK = OFF_RQ + RET_QK_W
OFF_RV = OFF_RK + RET_QK_W
OFF_RG = OFF_RV + RET_V_W
OFF_SQ = OFF_RG + RET_V_W
OFF_SK = OFF_SQ + SWA_Q_W
OFF_SV = OFF_SK + SWA_KV_W
OFF_MQ = OFF_SV + SWA_KV_W
OFF_GR = OFF_MQ + MEM_W
OFF_GS = OFF_GR + D_MODEL
OFF_GM = OFF_GS + D_MODEL
IN_W = OFF_GM + D_MODEL

LANES = 128
V7X_VMEM_LIMIT = 56 * 1024 * 1024


def _const_spec(shape):
    nd = len(shape)
    return pl.BlockSpec(shape, lambda *_: (0,) * nd, pipeline_mode=pl.Buffered(1))


def _layer_spec(shape, layer):
    nd = len(shape)
    return pl.BlockSpec((1,) + tuple(shape[1:]), lambda *_: (layer,) + (0,) * (nd - 1), pipeline_mode=pl.Buffered(1))


def _params(n_grid):
    return pltpu.CompilerParams(dimension_semantics=("arbitrary",) * n_grid, vmem_limit_bytes=V7X_VMEM_LIMIT)


def _inproj_kernel(x_ref, w_ref, cos_ref, sin_ref,
                   rq_ref, rk_ref, rv_ref, rg_ref, sq_ref, sk_ref, sv_ref, mq_ref, gr_ref, gs_ref, gm_ref):
    xb = x_ref[...].astype(BF16)
    cos = cos_ref[...]
    sin = sin_ref[...]
    lane = lax.broadcasted_iota(jnp.int32, cos.shape, 1)
    first_half = (lane & (SWA_HD // 2)) == 0

    def proj(off, width):
        return jnp.dot(xb, w_ref[0, :, off:off + width], preferred_element_type=F32)

    def rope_store(off, width, out_ref, scale):
        y = proj(off, width)
        for j in range(width // LANES):
            yj = y[:, j * LANES:(j + 1) * LANES]
            sw = jnp.where(first_half, pltpu.roll(yj, LANES - SWA_HD // 2, 1), pltpu.roll(yj, SWA_HD // 2, 1))
            r = yj * cos + sw * sin
            if scale != 1.0:
                r = r * scale
            out_ref[:, j * LANES:(j + 1) * LANES] = r.astype(out_ref.dtype)

    def plain_store(off, width, out_ref):
        out_ref[...] = proj(off, width).astype(out_ref.dtype)

    rope_store(OFF_RQ, RET_QK_W, rq_ref, 1.0)
    rope_store(OFF_RK, RET_QK_W, rk_ref, RET_DK ** -0.5)
    plain_store(OFF_RV, RET_V_W, rv_ref)
    plain_store(OFF_RG, RET_V_W, rg_ref)
    rope_store(OFF_SQ, SWA_Q_W, sq_ref, SWA_HD ** -0.5)
    rope_store(OFF_SK, SWA_KV_W, sk_ref, 1.0)
    plain_store(OFF_SV, SWA_KV_W, sv_ref)
    plain_store(OFF_MQ, MEM_W, mq_ref)
    plain_store(OFF_GR, D_MODEL, gr_ref)
    plain_store(OFF_GS, D_MODEL, gs_ref)
    plain_store(OFF_GM, D_MODEL, gm_ref)


def _inproj(x2d, w_bf, layer, cos_tab, sin_tab, tm, qkv_dtype):
    m = x2d.shape[0]
    n_tab = cos_tab.shape[0] // tm
    row = lambda w: pl.BlockSpec((tm, w), lambda i: (i, 0))
    tab = pl.BlockSpec((tm, LANES), lambda i: (i % n_tab, 0))
    widths_dtypes = [(RET_QK_W, qkv_dtype), (RET_QK_W, qkv_dtype), (RET_V_W, qkv_dtype), (RET_V_W, F32),
                     (SWA_Q_W, qkv_dtype), (SWA_KV_W, F32), (SWA_KV_W, F32), (MEM_W, qkv_dtype),
                     (D_MODEL, F32), (D_MODEL, F32), (D_MODEL, F32)]
    return pl.pallas_call(
        _inproj_kernel,
        grid=(m // tm,),
        in_specs=[row(D_MODEL), _layer_spec(w_bf.shape, layer), tab, tab],
        out_specs=[row(w) for w, _ in widths_dtypes],
        out_shape=[jax.ShapeDtypeStruct((m, w), dt) for w, dt in widths_dtypes],
        compiler_params=_params(1),
        name="inproj",
    )(x2d, w_bf, cos_tab, sin_tab)


def _group_norm(o, g_row):
    mu = jnp.mean(o, -1, keepdims=True)
    d = o - mu
    var = jnp.mean(d * d, -1, keepdims=True)
    return d * lax.rsqrt(var + GN_EPS) * g_row


def _ret_tables(lg, n_rows, period):
    r = jnp.arange(n_rows)
    t = (r % period).astype(F32)
    same = (r[:, None] // period) == (r[None, :] // period)
    diff = t[:, None] - t[None, :]
    decay = jnp.where((diff >= 0) & same, jnp.exp(lg[:, None, None] * jnp.maximum(diff, 0.0)), 0.0)
    rowdec = jnp.exp(lg[:, None] * (t[None, :] + 1.0))
    wend = jnp.exp(lg[:, None] * (period - 1.0 - t[None, :]))
    gl = jnp.exp(lg * period)
    rowdec = jnp.broadcast_to(rowdec[:, :, None], (RET_HEADS, n_rows, RET_DV))
    wend = jnp.broadcast_to(wend[:, :, None], (RET_HEADS, n_rows, RET_DK))
    gl = jnp.broadcast_to(gl[:, None, None], (RET_HEADS, 1, RET_DV))
    return decay, rowdec, wend, gl


def _ret_sample_kernel(n_seq, t_len, q_ref, k_ref, v_ref, s_ref, decay_ref, rowdec_ref, wend_ref, gl_ref, gn_ref,
                       o_ref, s_out_ref):
    rows = n_seq * t_len
    seq_of_row = lax.broadcasted_iota(jnp.int32, (rows, 1), 0) // t_len
    for h in range(RET_HEADS):
        q = q_ref[:, h * RET_DK:(h + 1) * RET_DK].astype(BF16)
        kf = k_ref[:, h * RET_DK:(h + 1) * RET_DK]
        k = kf.astype(BF16)
        v = v_ref[:, h * RET_DV:(h + 1) * RET_DV].astype(BF16)
        sc = lax.dot_general(q, k, (((1,), (1,)), ((), ())), preferred_element_type=F32) * decay_ref[h]
        o = jnp.dot(sc.astype(BF16), v, preferred_element_type=F32)
        kw = kf * wend_ref[h]
        o_state = jnp.zeros((rows, RET_DV), F32)
        for b in range(n_seq):
            mine = seq_of_row == b
            s0 = s_ref[0, b, h]
            o_state = o_state + jnp.where(mine, jnp.dot(q, s0.astype(BF16), preferred_element_type=F32), 0.0)
            kw_b = jnp.where(mine, kw, 0.0).astype(BF16)
            s_out_ref[b, h] = gl_ref[h] * s0 + lax.dot_general(
                kw_b, v, (((0,), (0,)), ((), ())), preferred_element_type=F32)
        o = o + o_state * rowdec_ref[h]
        o_ref[:, h * RET_DV:(h + 1) * RET_DV] = _group_norm(o, gn_ref[h:h + 1, :])


def _ret_sample(rq, rk, rv, state, layer, tables, gn_g, t_len, n_seq):
    m = rq.shape[0]
    rows = n_seq * t_len
    decay, rowdec, wend, gl = tables
    row = lambda w: pl.BlockSpec((rows, w), lambda i: (i, 0))
    st_in = pl.BlockSpec((1, n_seq, RET_HEADS, RET_DK, RET_DV), lambda i: (layer, i, 0, 0, 0))
    st_out = pl.BlockSpec((n_seq, RET_HEADS, RET_DK, RET_DV), lambda i: (i, 0, 0, 0))
    return pl.pallas_call(
        functools.partial(_ret_sample_kernel, n_seq, t_len),
        grid=(m // rows,),
        in_specs=[row(RET_QK_W), row(RET_QK_W), row(RET_V_W), st_in,
                  _const_spec(decay.shape), _const_spec(rowdec.shape), _const_spec(wend.shape),
                  _const_spec(gl.shape), _const_spec(gn_g.shape)],
        out_specs=[row(RET_V_W), st_out],
        out_shape=[jax.ShapeDtypeStruct((m, RET_V_W), F32), jax.ShapeDtypeStruct(state.shape[1:], F32)],
        compiler_params=_params(1),
        name="ret_sample",
    )(rq, rk, rv, state, decay, rowdec, wend, gl, gn_g)


def _swa_sample_kernel(n_seq, t_len, sinks_ref, q_ref, kn_ref, vn_ref, kt_ref, vt_ref, o_ref, kto_ref, vto_ref):
    grp_rows = SWA_GROUP * t_len
    n_all = n_seq * grp_rows
    q = q_ref[...]
    kn = kn_ref[...]
    vn = vn_ref[...]
    kn_t = kn.T
    vn_t = vn.T
    r = lax.broadcasted_iota(jnp.int32, (n_all, 1), 0)
    t_q = r % t_len
    g_row = (r // t_len) % SWA_GROUP
    b_row = r // grp_rows
    c = lax.broadcasted_iota(jnp.int32, (1, WINDOW), 1)
    valid_cache = c > t_q
    valid_new = ((c // t_len) == b_row) & ((c % t_len) <= t_q)
    lane = lax.broadcasted_iota(jnp.int32, (SWA_HD, WINDOW), 1)
    is_new_lane = lane >= WINDOW - t_len
    pieces = []
    for kvh in range(SWA_KV_HEADS):
        hd = slice(kvh * SWA_HD, (kvh + 1) * SWA_HD)
        qg = [q[:, (kvh * SWA_GROUP + g) * SWA_HD:(kvh * SWA_GROUP + g + 1) * SWA_HD] for g in range(SWA_GROUP)]
        q_all = jnp.concatenate([qg[g][b * t_len:(b + 1) * t_len] for b in range(n_seq) for g in range(SWA_GROUP)],
                                0).astype(BF16)
        s_new = jnp.dot(q_all, kn_t[hd].astype(BF16), preferred_element_type=F32)
        s_cache = jnp.concatenate(
            [jnp.dot(q_all[b * grp_rows:(b + 1) * grp_rows], kt_ref[0, b, kvh].astype(BF16),
                     preferred_element_type=F32) for b in range(n_seq)], 0)
        s_new = jnp.where(valid_new, s_new, -jnp.inf)
        s_cache = jnp.where(valid_cache, s_cache, -jnp.inf)
        sink = jnp.full((n_all, 1), sinks_ref[kvh * SWA_GROUP], F32)
        for g in range(1, SWA_GROUP):
            sink = jnp.where(g_row == g, sinks_ref[kvh * SWA_GROUP + g], sink)
        m = jnp.maximum(jnp.maximum(jnp.max(s_new, -1, keepdims=True), jnp.max(s_cache, -1, keepdims=True)), sink)
        e_new = jnp.exp(s_new - m)
        e_cache = jnp.exp(s_cache - m)
        den = jnp.sum(e_new, -1, keepdims=True) + jnp.sum(e_cache, -1, keepdims=True) + jnp.exp(sink - m)
        p_new = (e_new / den).astype(BF16)
        p_cache = (e_cache / den).astype(BF16)
        o = jnp.dot(p_new, vn[:, hd].astype(BF16), preferred_element_type=F32)
        o = o + jnp.concatenate(
            [lax.dot_general(p_cache[b * grp_rows:(b + 1) * grp_rows], vt_ref[0, b, kvh].astype(BF16),
                             (((1,), (1,)), ((), ())), preferred_element_type=F32) for b in range(n_seq)], 0)
        for g in range(SWA_GROUP):
            pieces.append(jnp.concatenate(
                [o[b * grp_rows + g * t_len:b * grp_rows + (g + 1) * t_len] for b in range(n_seq)], 0))
        for b in range(n_seq):
            shift_new = (WINDOW - t_len - b * t_len) % WINDOW
            kto_ref[b, kvh] = jnp.where(is_new_lane, pltpu.roll(kn_t[hd], shift_new, 1),
                                        pltpu.roll(kt_ref[0, b, kvh], WINDOW - t_len, 1))
            vto_ref[b, kvh] = jnp.where(is_new_lane, pltpu.roll(vn_t[hd], shift_new, 1),
                                        pltpu.roll(vt_ref[0, b, kvh], WINDOW - t_len, 1))
    o_ref[...] = jnp.concatenate(pieces, -1).astype(o_ref.dtype)


def _swa_sample(sq, sk, sv, cache_kt, cache_vt, sinks, layer, t_len, n_seq):
    m = sq.shape[0]
    rows = n_seq * t_len
    assert rows == WINDOW and cache_kt.shape[-1] == WINDOW
    row = lambda w: pl.BlockSpec((rows, w), lambda i: (i, 0))
    cin = pl.BlockSpec((1, n_seq, SWA_KV_HEADS, SWA_HD, WINDOW), lambda i: (layer, i, 0, 0, 0))
    cout = pl.BlockSpec((n_seq, SWA_KV_HEADS, SWA_HD, WINDOW), lambda i: (i, 0, 0, 0))
    cshape = jax.ShapeDtypeStruct(cache_kt.shape[1:], F32)
    return pl.pallas_call(
        functools.partial(_swa_sample_kernel, n_seq, t_len),
        grid=(m // rows,),
        in_specs=[pl.BlockSpec(memory_space=pltpu.SMEM), row(SWA_Q_W), row(SWA_KV_W), row(SWA_KV_W), cin, cin],
        out_specs=[row(SWA_Q_W), cout, cout],
        out_shape=[jax.ShapeDtypeStruct((m, SWA_Q_W), BF16), cshape, cshape],
        compiler_params=_params(1),
        name="swa_sample",
    )(sinks, sq, sk, sv, cache_kt, cache_vt)


def _mem_sample_kernel(n_seq, t_len, q_ref, k_ref, v_ref, o_ref):
    rows = MEM_HEADS * t_len
    head_of_row = lax.broadcasted_iota(jnp.int32, (rows, 1), 0) // t_len
    head_of_col = lax.broadcasted_iota(jnp.int32, (1, N_MEM * MEM_HEADS), 1) % MEM_HEADS
    valid = head_of_row == head_of_col
    outs = []
    for b in range(n_seq):
        qb = q_ref[b * t_len:(b + 1) * t_len, :]
        q_all = jnp.concatenate([qb[:, h * MEM_HD:(h + 1) * MEM_HD] for h in range(MEM_HEADS)], 0).astype(BF16)
        s = lax.dot_general(q_all, k_ref[0, b].astype(BF16), (((1,), (1,)), ((), ())),
                            preferred_element_type=F32) * (MEM_HD ** -0.5)
        s = jnp.where(valid, s, -jnp.inf)
        m = jnp.max(s, -1, keepdims=True)
        e = jnp.exp(s - m)
        p = e / jnp.sum(e, -1, keepdims=True)
        o = jnp.dot(p.astype(BF16), v_ref[0, b].astype(BF16), preferred_element_type=F32)
        outs.append(jnp.concatenate([o[h * t_len:(h + 1) * t_len] for h in range(MEM_HEADS)], -1))
    o_ref[...] = jnp.concatenate(outs, 0).astype(o_ref.dtype)


def _mem_sample(mq, cache_k, cache_v, layer, t_len, n_seq):
    m = mq.shape[0]
    rows = n_seq * t_len
    row = pl.BlockSpec((rows, MEM_W), lambda i: (i, 0))
    kv = pl.BlockSpec((1, n_seq, N_MEM * MEM_HEADS, MEM_HD), lambda i: (layer, i, 0, 0))
    return pl.pallas_call(
        functools.partial(_mem_sample_kernel, n_seq, t_len),
        grid=(m // rows,),
        in_specs=[row, kv, kv],
        out_specs=row,
        out_shape=jax.ShapeDtypeStruct((m, MEM_W), BF16),
        compiler_params=_params(1),
        name="mem_sample",
    )(mq, cache_k, cache_v)


def _pair_tables(tables):
    decay, rowdec, wend, gl = tables
    pair = lambda a: jnp.concatenate([a[0::2], a[1::2]], -1)
    gl_rows = jnp.concatenate([jnp.broadcast_to(gl[0::2], (RET_HEADS // 2, RET_DK, RET_DV)),
                               jnp.broadcast_to(gl[1::2], (RET_HEADS // 2, RET_DK, RET_DV))], 1)
    return pair(decay), pair(rowdec), pair(wend), gl_rows


def _run_interleaved(tasks):
    pending, active = list(tasks), []
    while pending or active:
        if pending:
            active.append(pending.pop(0))
        for t in list(active):
            try:
                next(t)
            except StopIteration:
                active.remove(t)


def _mix_prompt_kernel(nb, sinks_ref, rq_ref, rk_ref, rv_ref, sq_ref, sk_ref, sv_ref, mq_ref, mk_ref, mv_ref,
                       decay_ref, rowdec_ref, wend_ref, gl_ref, gn_ref,
                       gn_out, swa_out, mem_out, s_out,
                       s_scr, kp_scr, kpr_scr, vp_scr, vpr_scr):
    c = pl.program_id(0)

    @pl.when(c == 0)
    def _():
        s_scr[...] = jnp.zeros_like(s_scr)
        for scr in (kp_scr, kpr_scr, vp_scr, vpr_scr):
            scr[...] = jnp.zeros_like(scr)

    pair_w = 2 * SWA_HD
    lane_lo = lax.broadcasted_iota(jnp.int32, (RET_CHUNK, pair_w), 1) < SWA_HD
    row_lo = lax.broadcasted_iota(jnp.int32, (2 * RET_DK, RET_DV), 0) < RET_DK
    lane_lo_kv = lax.broadcasted_iota(jnp.int32, (2 * WINDOW, pair_w), 1) < SWA_HD
    i = lax.broadcasted_iota(jnp.int32, (WINDOW, 2 * WINDOW), 0)
    j = lax.broadcasted_iota(jnp.int32, (WINDOW, 2 * WINDOW), 1)
    valid = (j <= i + WINDOW) & (j > i) & ((c > 0) | (j >= WINDOW))


    def ret_task(b, p):
        qk = slice(p * 2 * RET_DK, (p + 1) * 2 * RET_DK)
        vv = slice(p * 2 * RET_DV, (p + 1) * 2 * RET_DV)
        q2, k2, v2, s2 = rq_ref[b, :, qk], rk_ref[b, :, qk], rv_ref[b, :, vv], s_scr[b, p]
        zk = jnp.zeros_like(k2)
        k_rows = jnp.concatenate([jnp.where(lane_lo, k2, zk), jnp.where(lane_lo, zk, k2)], 0)
        sc_raw = lax.dot_general(q2, k_rows, (((1,), (1,)), ((), ())), preferred_element_type=F32)
        s2b = s2.astype(BF16)
        zs = jnp.zeros_like(s2b)
        s_bd = jnp.concatenate([jnp.where(row_lo, s2b, zs), jnp.where(row_lo, zs, s2b)], 1)
        os_raw = jnp.dot(q2, s_bd, preferred_element_type=F32)
        kw2 = (k2.astype(F32) * wend_ref[p]).astype(BF16)
        upd = lax.dot_general(kw2, v2, (((0,), (0,)), ((), ())), preferred_element_type=F32)
        yield
        zv = jnp.zeros((RET_CHUNK, RET_DV), v2.dtype)
        v_bd = jnp.concatenate([jnp.concatenate([v2[:, :RET_DV], zv], 1),
                                jnp.concatenate([zv, v2[:, RET_DV:]], 1)], 0)
        o_raw = jnp.dot((sc_raw * decay_ref[p]).astype(BF16), v_bd, preferred_element_type=F32)
        s_scr[b, p] = gl_ref[p] * s2 + jnp.where(row_lo, upd[:, :RET_DV], upd[:, RET_DV:])
        yield
        o2 = o_raw + os_raw * rowdec_ref[p]
        for u in range(2):
            lo = (2 * p + u) * RET_DV
            gn_out[b, :, lo:lo + RET_DV] = _group_norm(o2[:, u * RET_DV:(u + 1) * RET_DV],
                                                       gn_ref[2 * p + u:2 * p + u + 1, :])

    kv_ctx = {}

    def swa_prep(b):
        k_cur, v_cur = sk_ref[b], sv_ref[b]
        kb, kbr = k_cur.astype(BF16), pltpu.roll(k_cur, SWA_HD, 1).astype(BF16)
        vb, vbr = v_cur.astype(BF16), pltpu.roll(v_cur, SWA_HD, 1).astype(BF16)
        kv_ctx[b] = (jnp.concatenate([kp_scr[b], kb], 0), jnp.concatenate([kpr_scr[b], kbr], 0),
                     jnp.concatenate([vp_scr[b], vb], 0), jnp.concatenate([vpr_scr[b], vbr], 0))
        kp_scr[b], kpr_scr[b], vp_scr[b], vpr_scr[b] = kb, kbr, vb, vbr

    def swa_task(b, kvh):
        if b not in kv_ctx:
            swa_prep(b)
        kc, kcr, vc, vcr = kv_ctx[b]
        zkv = jnp.zeros_like(kc)
        k_lo, k_hi = (kc, kcr) if kvh == 0 else (kcr, kc)
        v_lo, v_hi = (vc, vcr) if kvh == 0 else (vcr, vc)
        k_rows = jnp.concatenate([jnp.where(lane_lo_kv, k_lo, zkv), jnp.where(lane_lo_kv, zkv, k_hi)], 0)
        v_rows = jnp.concatenate([jnp.where(lane_lo_kv, v_lo, zkv), jnp.where(lane_lo_kv, zkv, v_hi)], 0)
        n_pairs = SWA_GROUP // 2
        pairs = [kvh * n_pairs + jj for jj in range(n_pairs)]
        q4 = jnp.concatenate([sq_ref[b, :, pr * pair_w:(pr + 1) * pair_w] for pr in pairs], 0)
        s4 = lax.dot_general(q4, k_rows, (((1,), (1,)), ((), ())), preferred_element_type=F32)
        yield
        rows = []
        for jj, pr in enumerate(pairs):
            ps = []
            for u in range(2):
                s = jnp.where(valid, s4[jj * WINDOW:(jj + 1) * WINDOW, u * 2 * WINDOW:(u + 1) * 2 * WINDOW], -jnp.inf)
                sink = sinks_ref[2 * pr + u]
                m = jnp.maximum(jnp.max(s, -1, keepdims=True), sink)
                e = jnp.exp(s - m)
                den = jnp.sum(e, -1, keepdims=True) + jnp.exp(sink - m)
                ps.append((e / den).astype(BF16))
            rows.append(jnp.concatenate(ps, 1))
        o4 = jnp.dot(jnp.concatenate(rows, 0), v_rows, preferred_element_type=F32)
        yield
        for jj, pr in enumerate(pairs):
            swa_out[b, :, pr * pair_w:(pr + 1) * pair_w] = o4[jj * WINDOW:(jj + 1) * WINDOW].astype(swa_out.dtype)

    def mem_task(b, h):
        sl = slice(h * MEM_HD, (h + 1) * MEM_HD)
        s = lax.dot_general(mq_ref[b, :, sl], mk_ref[b, :, sl], (((1,), (1,)), ((), ())),
                            preferred_element_type=F32) * (MEM_HD ** -0.5)
        yield
        m = jnp.max(s, -1, keepdims=True)
        e = jnp.exp(s - m)
        p_att = (e / jnp.sum(e, -1, keepdims=True)).astype(BF16)
        o = jnp.dot(p_att, mv_ref[b, :, sl], preferred_element_type=F32)
        yield
        mem_out[b, :, sl] = o.astype(mem_out.dtype)

    def region(make_tasks):
        @pl.when(c >= 0)
        def _():
            _run_interleaved(make_tasks())

    for b in range(nb):
        region(lambda b=b: [ret_task(b, p) for p in range(RET_HEADS // 2)])
        region(lambda b=b: [swa_task(b, kvh) for kvh in range(SWA_KV_HEADS)])
        region(lambda b=b: [mem_task(b, h) for h in range(MEM_HEADS)])

    @pl.when(c == pl.num_programs(0) - 1)
    def _():
        s_out[...] = s_scr[...]


def _mix_prompt(rq, rk, rv, sq, sk, sv, mq, mk_bf, mv_bf, sinks, pair_tables, gn_g):
    nb, seq, _ = rq.shape
    decay2, rowdec2, wend2, gl2 = pair_tables
    chunk = lambda w: pl.BlockSpec((nb, RET_CHUNK, w), lambda c: (0, c, 0))
    st_shape = (nb, RET_HEADS // 2, 2 * RET_DK, RET_DV)
    kv_scr = pltpu.VMEM((nb, WINDOW, SWA_KV_W), BF16)
    return pl.pallas_call(
        functools.partial(_mix_prompt_kernel, nb),
        grid=(seq // RET_CHUNK,),
        in_specs=[pl.BlockSpec(memory_space=pltpu.SMEM),
                  chunk(RET_QK_W), chunk(RET_QK_W), chunk(RET_V_W), chunk(SWA_Q_W), chunk(SWA_KV_W), chunk(SWA_KV_W),
                  chunk(MEM_W), _const_spec(mk_bf.shape), _const_spec(mv_bf.shape),
                  _const_spec(decay2.shape), _const_spec(rowdec2.shape), _const_spec(wend2.shape),
                  _const_spec(gl2.shape), _const_spec(gn_g.shape)],
        out_specs=[chunk(RET_V_W), chunk(SWA_Q_W), chunk(MEM_W), _const_spec(st_shape)],
        out_shape=[jax.ShapeDtypeStruct((nb, seq, RET_V_W), F32), jax.ShapeDtypeStruct((nb, seq, SWA_Q_W), BF16),
                   jax.ShapeDtypeStruct((nb, seq, MEM_W), BF16), jax.ShapeDtypeStruct(st_shape, F32)],
        scratch_shapes=[pltpu.VMEM(st_shape, F32), kv_scr, kv_scr, kv_scr, kv_scr],
        compiler_params=_params(1),
        name="mix_prompt",
    )(sinks, rq, rk, rv, sq, sk, sv, mq, mk_bf, mv_bf, decay2, rowdec2, wend2, gl2, gn_g)


def _mem_kv_kernel(x_ref, w_ref, k_ref, v_ref, kb_ref, vb_ref):
    kv = jnp.dot(x_ref[...].astype(BF16), w_ref[0], preferred_element_type=F32)
    k_ref[...] = kv[:, :MEM_W]
    v_ref[...] = kv[:, MEM_W:]
    kb_ref[...] = kv[:, :MEM_W].astype(BF16)
    vb_ref[...] = kv[:, MEM_W:].astype(BF16)


def _mem_kv(mem2d, w_bf, layer, tm):
    m, k = mem2d.shape
    out = pl.BlockSpec((tm, MEM_W), lambda i: (i, 0))
    return pl.pallas_call(
        _mem_kv_kernel,
        grid=(m // tm,),
        in_specs=[pl.BlockSpec((tm, k), lambda i: (i, 0)), _layer_spec(w_bf.shape, layer)],
        out_specs=[out, out, out, out],
        out_shape=[jax.ShapeDtypeStruct((m, MEM_W), F32), jax.ShapeDtypeStruct((m, MEM_W), F32),
                   jax.ShapeDtypeStruct((m, MEM_W), BF16), jax.ShapeDtypeStruct((m, MEM_W), BF16)],
        compiler_params=_params(1),
        name="mem_kv",
    )(mem2d, w_bf)


def _layer_norm(x, g, b):
    mu = jnp.mean(x, -1, keepdims=True)
    d = x - mu
    var = jnp.mean(d * d, -1, keepdims=True)
    return d * lax.rsqrt(var + LN_EPS) * g + b


def _merge_kernel(x_ref, gn_ref, rg_ref, swa_ref, mem_ref, gr_ref, gs_ref, gm_ref,
                  wr_ref, ws_ref, wm_ref, wo_ref, l1g_ref, l1b_ref, o_ref):
    rg = rg_ref[...]
    ret_in = (rg * jax.nn.sigmoid(rg) * gn_ref[...]).astype(BF16)
    ret_b = jnp.dot(ret_in, wr_ref[0], preferred_element_type=F32)
    swa_b = jnp.dot(swa_ref[...], ws_ref[0], preferred_element_type=F32)
    mem_b = jnp.dot(mem_ref[...], wm_ref[0], preferred_element_type=F32)
    merged = (jax.nn.sigmoid(gr_ref[...]) * ret_b + jax.nn.sigmoid(gs_ref[...]) * swa_b
              + jax.nn.sigmoid(gm_ref[...]) * mem_b)
    y = jnp.dot(merged.astype(BF16), wo_ref[0], preferred_element_type=F32)
    o_ref[...] = _layer_norm(ALPHA * x_ref[...] + y, l1g_ref[0], l1b_ref[0])


def _mlp_kernel(x_ref, wu_ref, wd_ref, l2g_ref, l2b_ref, o_ref):
    x1 = x_ref[...]
    x1b = x1.astype(BF16)
    ff_chunk = D_FF // 4
    acc = jnp.zeros_like(x1)
    for c in range(D_FF // ff_chunk):
        h = jnp.dot(x1b, wu_ref[0, :, c * ff_chunk:(c + 1) * ff_chunk], preferred_element_type=F32)
        h = jnp.square(jnp.maximum(h, 0.0)).astype(BF16)
        acc = acc + jnp.dot(h, wd_ref[0, c * ff_chunk:(c + 1) * ff_chunk, :], preferred_element_type=F32)
    o_ref[...] = _layer_norm(ALPHA * x1 + acc, l2g_ref[0], l2b_ref[0])


def _finish(x2d, gn, rg, swa_o, mem_o, g_r, g_s, g_m, lw, layer, tm):
    w_br_ret, w_br_swa, w_br_mem, w_out, ln1_g, ln1_b, w_up, w_down, ln2_g, ln2_b = lw
    m = x2d.shape[0]
    row = lambda w: pl.BlockSpec((tm, w), lambda i: (i, 0))
    lspec = lambda a: _layer_spec(a.shape, layer)
    x1 = pl.pallas_call(
        _merge_kernel,
        grid=(m // tm,),
        in_specs=[row(D_MODEL), row(RET_V_W), row(RET_V_W), row(SWA_Q_W), row(MEM_W),
                  row(D_MODEL), row(D_MODEL), row(D_MODEL),
                  lspec(w_br_ret), lspec(w_br_swa), lspec(w_br_mem), lspec(w_out), lspec(ln1_g), lspec(ln1_b)],
        out_specs=row(D_MODEL),
        out_shape=jax.ShapeDtypeStruct((m, D_MODEL), F32),
        compiler_params=_params(1),
        name="merge",
    )(x2d, gn, rg, swa_o, mem_o, g_r, g_s, g_m, w_br_ret, w_br_swa, w_br_mem, w_out, ln1_g, ln1_b)
    return pl.pallas_call(
        _mlp_kernel,
        grid=(m // tm,),
        in_specs=[row(D_MODEL), lspec(w_up), lspec(w_down), lspec(ln2_g), lspec(ln2_b)],
        out_specs=row(D_MODEL),
        out_shape=jax.ShapeDtypeStruct((m, D_MODEL), F32),
        compiler_params=_params(1),
        name="mlp",
    )(x1, w_up, w_down, ln2_g, ln2_b)


def _rope_tables(pos):
    half = SWA_HD // 2
    inv = jnp.power(ROPE_THETA, -jnp.arange(half, dtype=F32) / half)
    ang = pos.astype(F32)[:, None] * inv[None, :]
    c, s = jnp.cos(ang), jnp.sin(ang)
    return jnp.concatenate([c, c, c, c], -1), jnp.concatenate([-s, s, -s, s], -1)


def kernel(x_prompt, x_sample, state_ret, cache_swa_k, cache_swa_v, cache_mem_k, cache_mem_v, mem_prompt,
           w_in, w_br_ret, w_br_swa, w_br_mem, w_out, w_mem_kv, attn_sinks, ret_gn_g,
           ln1_g, ln1_b, w_up, w_down, ln2_g, ln2_b):
    batch, seq, _ = x_prompt.shape
    dec_b, dec_t, _ = x_sample.shape
    tm_p, tm_s = 512, 256
    ret_seqs = RET_CHUNK // dec_t

    cos_p, sin_p = _rope_tables(jnp.arange(seq, dtype=jnp.int32))
    cos_s, sin_s = _rope_tables(PAST_LEN + jnp.arange(dec_t, dtype=jnp.int32))
    cos_s, sin_s = jnp.tile(cos_s, (tm_s // dec_t, 1)), jnp.tile(sin_s, (tm_s // dec_t, 1))
    lg = jnp.log1p(-jnp.exp2(-5.0 - jnp.arange(RET_HEADS, dtype=F32)))
    tab_p = _ret_tables(lg, RET_CHUNK, RET_CHUNK)
    tab_p2 = _pair_tables(tab_p)
    tab_s = _ret_tables(lg, RET_CHUNK, dec_t)

    xp = x_prompt.reshape(batch * seq, D_MODEL)
    xs = x_sample.reshape(dec_b * dec_t, D_MODEL)
    mem2d = mem_prompt.reshape(batch * N_MEM, D_MODEL)
    cache_kt = jnp.transpose(cache_swa_k, (0, 1, 3, 4, 2))
    cache_vt = jnp.transpose(cache_swa_v, (0, 1, 3, 4, 2))
    cache_mk = cache_mem_k.reshape(DEPTH, dec_b, N_MEM * MEM_HEADS, MEM_HD)
    cache_mv = cache_mem_v.reshape(DEPTH, dec_b, N_MEM * MEM_HEADS, MEM_HD)

    w_in_bf = w_in.astype(BF16)
    w_mem_kv_bf = w_mem_kv.astype(BF16)
    ln_row = lambda a: a.reshape(DEPTH, 1, D_MODEL)
    lw = (w_br_ret.astype(BF16), w_br_swa.astype(BF16), w_br_mem.astype(BF16), w_out.astype(BF16),
          ln_row(ln1_g), ln_row(ln1_b), w_up.astype(BF16), w_down.astype(BF16), ln_row(ln2_g), ln_row(ln2_b))

    ret_p, swk_p, swv_p, mk_p, mv_p, ret_s, swk_s, swv_s = [], [], [], [], [], [], [], []
    for l in range(DEPTH):
        gn_g = ret_gn_g[l]
        sinks = attn_sinks[l]

        rq, rk, rv, rg, sq, sk, sv, mq, g_r, g_s, g_m = _inproj(xp, w_in_bf, l, cos_p, sin_p, tm_p, BF16)
        mk, mv, mk_bf, mv_bf = _mem_kv(mem2d, w_mem_kv_bf, l, 256)
        by_seq = lambda a: a.reshape(batch, seq, a.shape[-1])
        by_mem = lambda a: a.reshape(batch, N_MEM, MEM_W)
        gn, swa_o, mem_o, s_p = _mix_prompt(by_seq(rq), by_seq(rk), by_seq(rv), by_seq(sq), by_seq(sk), by_seq(sv),
                                            by_seq(mq), by_mem(mk_bf), by_mem(mv_bf), sinks, tab_p2, gn_g)
        flat = lambda a: a.reshape(batch * seq, a.shape[-1])
        xp = _finish(xp, flat(gn), rg, flat(swa_o), flat(mem_o), g_r, g_s, g_m, lw, l, tm_p)
        ret_p.append(s_p.reshape(batch, RET_HEADS, RET_DK, RET_DV))
        swk_p.append(sk.reshape(batch, seq, SWA_KV_W)[:, -WINDOW:].reshape(batch, WINDOW, SWA_KV_HEADS, SWA_HD))
        swv_p.append(sv.reshape(batch, seq, SWA_KV_W)[:, -WINDOW:].reshape(batch, WINDOW, SWA_KV_HEADS, SWA_HD))
        mk_p.append(mk.reshape(batch, N_MEM, MEM_HEADS, MEM_HD))
        mv_p.append(mv.reshape(batch, N_MEM, MEM_HEADS, MEM_HD))

        rq, rk, rv, rg, sq, sk, sv, mq, g_r, g_s, g_m = _inproj(xs, w_in_bf, l, cos_s, sin_s, tm_s, F32)
        gn, s_s = _ret_sample(rq, rk, rv, state_ret, l, tab_s, gn_g, dec_t, ret_seqs)
        swa_o, kto, vto = _swa_sample(sq, sk, sv, cache_kt, cache_vt, sinks, l, dec_t, WINDOW // dec_t)
        mem_o = _mem_sample(mq, cache_mk, cache_mv, l, dec_t, 8)
        xs = _finish(xs, gn, rg, swa_o, mem_o, g_r, g_s, g_m, lw, l, tm_s)
        ret_s.append(s_s)
        swk_s.append(kto)
        swv_s.append(vto)

    from_t = lambda a: jnp.transpose(jnp.stack(a), (0, 1, 4, 2, 3))
    return (xp.reshape(batch, seq, D_MODEL), xs.reshape(dec_b, dec_t, D_MODEL),
            jnp.stack(ret_p), jnp.stack(swk_p), jnp.stack(swv_p), jnp.stack(mk_p), jnp.stack(mv_p),
            jnp.stack(ret_s), from_t(swk_s), from_t(swv_s))
```

```python
import functools

import jax
import jax.numpy as jnp
from jax import lax
from jax.experimental import pallas as pl
from jax.experimental.pallas import tpu as pltpu

F32 = jnp.float32
BF16 = jnp.bfloat16

D_MODEL = 1024
DEPTH = 2
PAST_LEN = 16384
RET_HEADS = 8
RET_DK = 64
RET_DV = 128
RET_CHUNK = 128
SWA_HEADS = 8
SWA_KV_HEADS = 2
SWA_GROUP = SWA_HEADS // SWA_KV_HEADS
SWA_HD = 64
WINDOW = 128
MEM_HEADS = 4
MEM_HD = 128
N_MEM = 256
D_FF = 4 * D_MODEL
ROPE_THETA = 10000.0
LN_EPS = 1e-5
GN_EPS = 1e-5
ALPHA = (2 * DEPTH) ** 0.25

RET_QK_W = RET_HEADS * RET_DK
RET_V_W = RET_HEADS * RET_DV
SWA_Q_W = SWA_HEADS * SWA_HD
SWA_KV_W = SWA_KV_HEADS * SWA_HD
MEM_W = MEM_HEADS * MEM_HD
OFF_RQ = 0
OFF_RK = OFF_RQ + RET_QK_W
OFF_RV = OFF_RK + RET_QK_W
OFF_RG = OFF_RV + RET_V_W
OFF_SQ = OFF_RG + RET_V_W
OFF_SK = OFF_SQ + SWA_Q_W
OFF_SV = OFF_SK + SWA_KV_W
OFF_MQ = OFF_SV + SWA_KV_W
OFF_GR = OFF_MQ + MEM_W
OFF_GS = OFF_GR + D_MODEL
OFF_GM = OFF_GS + D_MODEL
IN_W = OFF_GM + D_MODEL

LANES = 128
V7X_VMEM_LIMIT = 56 * 1024 * 1024


def _const_spec(shape):
    nd = len(shape)
    return pl.BlockSpec(shape, lambda *_: (0,) * nd, pipeline_mode=pl.Buffered(1))


def _layer_spec(shape, layer):
    nd = len(shape)
    return pl.BlockSpec((1,) + tuple(shape[1:]), lambda *_: (layer,) + (0,) * (nd - 1), pipeline_mode=pl.Buffered(1))


def _params(n_grid):
    return pltpu.CompilerParams(dimension_semantics=("arbitrary",) * n_grid, vmem_limit_bytes=V7X_VMEM_LIMIT)


def _inproj_kernel(x_ref, w_ref, cos_ref, sin_ref,
                   rq_ref, rk_ref, rv_ref, rg_ref, sq_ref, sk_ref, sv_ref, mq_ref, gr_ref, gs_ref, gm_ref):
    xb = x_ref[...].astype(BF16)
    cos = cos_ref[...]
    sin = sin_ref[...]
    lane = lax.broadcasted_iota(jnp.int32, cos.shape, 1)
    first_half = (lane & (SWA_HD // 2)) == 0

    def proj(off, width):
        return jnp.dot(xb, w_ref[0, :, off:off + width], preferred_element_type=F32)

    def rope_store(off, width, out_ref, scale):
        y = proj(off, width)
        for j in range(width // LANES):
            yj = y[:, j * LANES:(j + 1) * LANES]
            sw = jnp.where(first_half, pltpu.roll(yj, LANES - SWA_HD // 2, 1), pltpu.roll(yj, SWA_HD // 2, 1))
            r = yj * cos + sw * sin
            if scale != 1.0:
                r = r * scale
            out_ref[:, j * LANES:(j + 1) * LANES] = r.astype(out_ref.dtype)

    def plain_store(off, width, out_ref):
        out_ref[...] = proj(off, width).astype(out_ref.dtype)

    rope_store(OFF_RQ, RET_QK_W, rq_ref, 1.0)
    rope_store(OFF_RK, RET_QK_W, rk_ref, RET_DK ** -0.5)
    plain_store(OFF_RV, RET_V_W, rv_ref)
    plain_store(OFF_RG, RET_V_W, rg_ref)
    rope_store(OFF_SQ, SWA_Q_W, sq_ref, SWA_HD ** -0.5)
    rope_store(OFF_SK, SWA_KV_W, sk_ref, 1.0)
    plain_store(OFF_SV, SWA_KV_W, sv_ref)
    plain_store(OFF_MQ, MEM_W, mq_ref)
    plain_store(OFF_GR, D_MODEL, gr_ref)
    plain_store(OFF_GS, D_MODEL, gs_ref)
    plain_store(OFF_GM, D_MODEL, gm_ref)


def _inproj(x2d, w_bf, layer, cos_tab, sin_tab, tm, qkv_dtype):
    m = x2d.shape[0]
    n_tab = cos_tab.shape[0] // tm
    row = lambda w: pl.BlockSpec((tm, w), lambda i: (i, 0))
    tab = pl.BlockSpec((tm, LANES), lambda i: (i % n_tab, 0))
    widths_dtypes = [(RET_QK_W, qkv_dtype), (RET_QK_W, qkv_dtype), (RET_V_W, qkv_dtype), (RET_V_W, F32),
                     (SWA_Q_W, qkv_dtype), (SWA_KV_W, F32), (SWA_KV_W, F32), (MEM_W, qkv_dtype),
                     (D_MODEL, F32), (D_MODEL, F32), (D_MODEL, F32)]
    return pl.pallas_call(
        _inproj_kernel,
        grid=(m // tm,),
        in_specs=[row(D_MODEL), _layer_spec(w_bf.shape, layer), tab, tab],
        out_specs=[row(w) for w, _ in widths_dtypes],
        out_shape=[jax.ShapeDtypeStruct((m, w), dt) for w, dt in widths_dtypes],
        compiler_params=_params(1),
        name="inproj",
    )(x2d, w_bf, cos_tab, sin_tab)


def _group_norm(o, g_row):
    mu = jnp.mean(o, -1, keepdims=True)
    d = o - mu
    var = jnp.mean(d * d, -1, keepdims=True)
    return d * lax.rsqrt(var + GN_EPS) * g_row


def _ret_tables(lg, n_rows, period):
    r = jnp.arange(n_rows)
    t = (r % period).astype(F32)
    same = (r[:, None] // period) == (r[None, :] // period)
    diff = t[:, None] - t[None, :]
    decay = jnp.where((diff >= 0) & same, jnp.exp(lg[:, None, None] * jnp.maximum(diff, 0.0)), 0.0)
    rowdec = jnp.exp(lg[:, None] * (t[None, :] + 1.0))
    wend = jnp.exp(lg[:, None] * (period - 1.0 - t[None, :]))
    gl = jnp.exp(lg * period)
    rowdec = jnp.broadcast_to(rowdec[:, :, None], (RET_HEADS, n_rows, RET_DV))
    wend = jnp.broadcast_to(wend[:, :, None], (RET_HEADS, n_rows, RET_DK))
    gl = jnp.broadcast_to(gl[:, None, None], (RET_HEADS, 1, RET_DV))
    return decay, rowdec, wend, gl


def _ret_sample_kernel(n_seq, t_len, has_prev, q_ref, k_ref, v_ref, s_ref, decay_ref, rowdec_ref, wend_ref, gl_ref,
                       gn_ref, *rest):
    o_ref, s_out_ref = rest[-2:]
    del has_prev
    rows = n_seq * t_len
    pair_dk, pair_dv = 2 * RET_DK, 2 * RET_DV
    lane_lo = lax.broadcasted_iota(jnp.int32, (rows, pair_dk), 1) < RET_DK
    row_lo = lax.broadcasted_iota(jnp.int32, (pair_dk, RET_DV), 0) < RET_DK
    for p in range(RET_HEADS // 2):
        qk = slice(p * pair_dk, (p + 1) * pair_dk)
        vv = slice(p * pair_dv, (p + 1) * pair_dv)
        q2f, k2f, v2f = q_ref[:, qk], k_ref[:, qk], v_ref[:, vv]
        q2, k2, v2 = q2f.astype(BF16), k2f.astype(BF16), v2f.astype(BF16)
        zk = jnp.zeros_like(k2)
        k_rows = jnp.concatenate([jnp.where(lane_lo, k2, zk), jnp.where(lane_lo, zk, k2)], 0)
        sc2 = lax.dot_general(q2, k_rows, (((1,), (1,)), ((), ())), preferred_element_type=F32) * decay_ref[p]
        zv = jnp.zeros((rows, RET_DV), BF16)
        v_bd = jnp.concatenate([jnp.concatenate([v2[:, :RET_DV], zv], 1),
                                jnp.concatenate([zv, v2[:, RET_DV:]], 1)], 0)
        o2 = jnp.dot(sc2.astype(BF16), v_bd, preferred_element_type=F32)
        kw2f = k2f * wend_ref[p]
        o_state = []
        for b in range(n_seq):
            r = slice(b * t_len, (b + 1) * t_len)
            s2 = s_ref[0, b, 2 * p:2 * p + 2].reshape(pair_dk, RET_DV)
            s2b = s2.astype(BF16)
            zs = jnp.zeros_like(s2b)
            s_bd = jnp.concatenate([jnp.where(row_lo, s2b, zs), jnp.where(row_lo, zs, s2b)], 1)
            o_state.append(jnp.dot(q2f[r].astype(BF16), s_bd, preferred_element_type=F32))
            upd = lax.dot_general(kw2f[r].astype(BF16), v2f[r].astype(BF16), (((0,), (0,)), ((), ())),
                                  preferred_element_type=F32)
            s_new = (gl_ref[p] * s2 + jnp.where(row_lo, upd[:, :RET_DV], upd[:, RET_DV:])).reshape(2, RET_DK, RET_DV)
            for d in range(s_out_ref.shape[0]):
                s_out_ref[d, b, 2 * p:2 * p + 2] = s_new
        o2 = o2 + jnp.concatenate(o_state, 0) * rowdec_ref[p]
        for u in range(2):
            lo = (2 * p + u) * RET_DV
            o_ref[:, lo:lo + RET_DV] = _group_norm(o2[:, u * RET_DV:(u + 1) * RET_DV],
                                                   gn_ref[2 * p + u:2 * p + u + 1, :])


def _stacked_out_specs(shape, layer, n_seq):
    tail = tuple(shape[2:])
    zeros = (0,) * len(tail)
    if layer == 0:
        return pl.BlockSpec((shape[0], n_seq) + tail, lambda i: (0, i) + zeros)
    return pl.BlockSpec((1, n_seq) + tail, lambda i: (layer, i) + zeros)


def _ret_sample(rq, rk, rv, state, prev_out, layer, pair_tables, gn_g, t_len, n_seq):
    m = rq.shape[0]
    rows = n_seq * t_len
    decay2, rowdec2, wend2, gl2 = pair_tables
    row = lambda w: pl.BlockSpec((rows, w), lambda i: (i, 0))
    st_in = pl.BlockSpec((1, n_seq, RET_HEADS, RET_DK, RET_DV), lambda i: (layer, i, 0, 0, 0))
    in_specs = [row(RET_QK_W), row(RET_QK_W), row(RET_V_W), st_in,
                _const_spec(decay2.shape), _const_spec(rowdec2.shape), _const_spec(wend2.shape),
                _const_spec(gl2.shape), _const_spec(gn_g.shape)]
    args = [rq, rk, rv, state, decay2, rowdec2, wend2, gl2, gn_g]
    aliases = {}
    if prev_out is not None:
        in_specs.append(pl.BlockSpec(memory_space=pl.ANY))
        args.append(prev_out)
        aliases = {len(args) - 1: 1}
    return pl.pallas_call(
        functools.partial(_ret_sample_kernel, n_seq, t_len, prev_out is not None),
        grid=(m // rows,),
        in_specs=in_specs,
        out_specs=[row(RET_V_W), _stacked_out_specs(state.shape, layer, n_seq)],
        out_shape=[jax.ShapeDtypeStruct((m, RET_V_W), F32), jax.ShapeDtypeStruct(state.shape, F32)],
        input_output_aliases=aliases,
        compiler_params=_params(1),
        name="ret_sample",
    )(*args)


def _swa_sample_kernel(n_seq, t_len, sinks_ref, q_ref, kn_ref, vn_ref, kt_ref, vt_ref, *rest):
    o_ref, kto_ref, vto_ref = rest[-3:]
    grp_rows = SWA_GROUP * t_len
    n_all = n_seq * grp_rows
    q = q_ref[...]
    kn = kn_ref[...]
    vn = vn_ref[...]
    kn_t = kn.T
    vn_t = vn.T
    r = lax.broadcasted_iota(jnp.int32, (n_all, 1), 0)
    t_q = r % t_len
    g_row = (r // t_len) % SWA_GROUP
    b_row = r // grp_rows
    c = lax.broadcasted_iota(jnp.int32, (1, WINDOW), 1)
    valid_cache = c > t_q
    valid_new = ((c // t_len) == b_row) & ((c % t_len) <= t_q)
    lane = lax.broadcasted_iota(jnp.int32, (SWA_HD, WINDOW), 1)
    is_new_lane = lane >= WINDOW - t_len
    pieces = []
    for kvh in range(SWA_KV_HEADS):
        hd = slice(kvh * SWA_HD, (kvh + 1) * SWA_HD)
        qg = [q[:, (kvh * SWA_GROUP + g) * SWA_HD:(kvh * SWA_GROUP + g + 1) * SWA_HD] for g in range(SWA_GROUP)]
        q_all = jnp.concatenate([qg[g][b * t_len:(b + 1) * t_len] for b in range(n_seq) for g in range(SWA_GROUP)],
                                0).astype(BF16)
        s_new = jnp.dot(q_all, kn_t[hd].astype(BF16), preferred_element_type=F32)
        s_cache = jnp.concatenate(
            [jnp.dot(q_all[b * grp_rows:(b + 1) * grp_rows], kt_ref[0, b, kvh].astype(BF16),
                     preferred_element_type=F32) for b in range(n_seq)], 0)
        s_new = jnp.where(valid_new, s_new, -jnp.inf)
        s_cache = jnp.where(valid_cache, s_cache, -jnp.inf)
        sink = jnp.full((n_all, 1), sinks_ref[kvh * SWA_GROUP], F32)
        for g in range(1, SWA_GROUP):
            sink = jnp.where(g_row == g, sinks_ref[kvh * SWA_GROUP + g], sink)
        m = jnp.maximum(jnp.maximum(jnp.max(s_new, -1, keepdims=True), jnp.max(s_cache, -1, keepdims=True)), sink)
        e_new = jnp.exp(s_new - m)
        e_cache = jnp.exp(s_cache - m)
        den = jnp.sum(e_new, -1, keepdims=True) + jnp.sum(e_cache, -1, keepdims=True) + jnp.exp(sink - m)
        p_new = (e_new / den).astype(BF16)
        p_cache = (e_cache / den).astype(BF16)
        o = jnp.dot(p_new, vn[:, hd].astype(BF16), preferred_element_type=F32)
        o = o + jnp.concatenate(
            [lax.dot_general(p_cache[b * grp_rows:(b + 1) * grp_rows], vt_ref[0, b, kvh].astype(BF16),
                             (((1,), (1,)), ((), ())), preferred_element_type=F32) for b in range(n_seq)], 0)
        for g in range(SWA_GROUP):
            pieces.append(jnp.concatenate(
                [o[b * grp_rows + g * t_len:b * grp_rows + (g + 1) * t_len] for b in range(n_seq)], 0))
        for b in range(n_seq):
            shift_new = (WINDOW - t_len - b * t_len) % WINDOW
            k_slid = jnp.where(is_new_lane, pltpu.roll(kn_t[hd], shift_new, 1),
                               pltpu.roll(kt_ref[0, b, kvh], WINDOW - t_len, 1))
            v_slid = jnp.where(is_new_lane, pltpu.roll(vn_t[hd], shift_new, 1),
                               pltpu.roll(vt_ref[0, b, kvh], WINDOW - t_len, 1))
            for d in range(kto_ref.shape[0]):
                kto_ref[d, b, kvh] = k_slid
                vto_ref[d, b, kvh] = v_slid
    o_ref[...] = jnp.concatenate(pieces, -1).astype(o_ref.dtype)


def _swa_sample(sq, sk, sv, cache_kt, cache_vt, prev_out, sinks, layer, t_len, n_seq):
    m = sq.shape[0]
    rows = n_seq * t_len
    assert rows == WINDOW and cache_kt.shape[-1] == WINDOW
    row = lambda w: pl.BlockSpec((rows, w), lambda i: (i, 0))
    cin = pl.BlockSpec((1, n_seq, SWA_KV_HEADS, SWA_HD, WINDOW), lambda i: (layer, i, 0, 0, 0))
    cout = _stacked_out_specs(cache_kt.shape, layer, n_seq)
    cshape = jax.ShapeDtypeStruct(cache_kt.shape, F32)
    in_specs = [pl.BlockSpec(memory_space=pltpu.SMEM), row(SWA_Q_W), row(SWA_KV_W), row(SWA_KV_W), cin, cin]
    args = [sinks, sq, sk, sv, cache_kt, cache_vt]
    aliases = {}
    if prev_out is not None:
        in_specs += [pl.BlockSpec(memory_space=pl.ANY)] * 2
        args += list(prev_out)
        aliases = {len(args) - 2: 1, len(args) - 1: 2}
    return pl.pallas_call(
        functools.partial(_swa_sample_kernel, n_seq, t_len),
        grid=(m // rows,),
        in_specs=in_specs,
        out_specs=[row(SWA_Q_W), cout, cout],
        out_shape=[jax.ShapeDtypeStruct((m, SWA_Q_W), BF16), cshape, cshape],
        input_output_aliases=aliases,
        compiler_params=_params(1),
        name="swa_sample",
    )(*args)


def _mem_sample_kernel(n_seq, t_len, q_ref, k_ref, v_ref, o_ref):
    rows = MEM_HEADS * t_len
    head_of_row = lax.broadcasted_iota(jnp.int32, (rows, 1), 0) // t_len
    head_of_col = lax.broadcasted_iota(jnp.int32, (1, N_MEM * MEM_HEADS), 1) % MEM_HEADS
    valid = head_of_row == head_of_col
    outs = []
    for b in range(n_seq):
        qb = q_ref[b * t_len:(b + 1) * t_len, :]
        q_all = jnp.concatenate([qb[:, h * MEM_HD:(h + 1) * MEM_HD] for h in range(MEM_HEADS)], 0).astype(BF16)
        s = lax.dot_general(q_all, k_ref[0, b].astype(BF16), (((1,), (1,)), ((), ())),
                            preferred_element_type=F32) * (MEM_HD ** -0.5)
        s = jnp.where(valid, s, -jnp.inf)
        m = jnp.max(s, -1, keepdims=True)
        e = jnp.exp(s - m)
        p = e / jnp.sum(e, -1, keepdims=True)
        o = jnp.dot(p.astype(BF16), v_ref[0, b].astype(BF16), preferred_element_type=F32)
        outs.append(jnp.concatenate([o[h * t_len:(h + 1) * t_len] for h in range(MEM_HEADS)], -1))
    o_ref[...] = jnp.concatenate(outs, 0).astype(o_ref.dtype)


def _mem_sample(mq, cache_k, cache_v, layer, t_len, n_seq):
    m = mq.shape[0]
    rows = n_seq * t_len
    row = pl.BlockSpec((rows, MEM_W), lambda i: (i, 0))
    kv = pl.BlockSpec((1, n_seq, N_MEM * MEM_HEADS, MEM_HD), lambda i: (layer, i, 0, 0))
    return pl.pallas_call(
        functools.partial(_mem_sample_kernel, n_seq, t_len),
        grid=(m // rows,),
        in_specs=[row, kv, kv],
        out_specs=row,
        out_shape=jax.ShapeDtypeStruct((m, MEM_W), BF16),
        compiler_params=_params(1),
        name="mem_sample",
    )(mq, cache_k, cache_v)


def _pair_tables(tables):
    decay, rowdec, wend, gl = tables
    pair = lambda a: jnp.concatenate([a[0::2], a[1::2]], -1)
    gl_rows = jnp.concatenate([jnp.broadcast_to(gl[0::2], (RET_HEADS // 2, RET_DK, RET_DV)),
                               jnp.broadcast_to(gl[1::2], (RET_HEADS // 2, RET_DK, RET_DV))], 1)
    return pair(decay), pair(rowdec), pair(wend), gl_rows


def _run_interleaved(tasks):
    pending, active = list(tasks), []
    while pending or active:
        if pending:
            active.append(pending.pop(0))
        for t in list(active):
            try:
                next(t)
            except StopIteration:
                active.remove(t)


def _mix_prompt_kernel(nb, sinks_ref, rq_ref, rk_ref, rv_ref, sq_ref, sk_ref, sv_ref, mq_ref, mk_ref, mv_ref,
                       decay_ref, rowdec_ref, wend_ref, gl_ref, gn_ref,
                       gn_out, swa_out, mem_out, s_out,
                       s_scr, kp_scr, kpr_scr, vp_scr, vpr_scr):
    c = pl.program_id(0)

    @pl.when(c == 0)
    def _():
        s_scr[...] = jnp.zeros_like(s_scr)
        for scr in (kp_scr, kpr_scr, vp_scr, vpr_scr):
            scr[...] = jnp.zeros_like(scr)

    pair_w = 2 * SWA_HD
    lane_lo = lax.broadcasted_iota(jnp.int32, (RET_CHUNK, pair_w), 1) < SWA_HD
    row_lo = lax.broadcasted_iota(jnp.int32, (2 * RET_DK, RET_DV), 0) < RET_DK
    lane_lo_kv = lax.broadcasted_iota(jnp.int32, (2 * WINDOW, pair_w), 1) < SWA_HD
    i = lax.broadcasted_iota(jnp.int32, (WINDOW, 2 * WINDOW), 0)
    j = lax.broadcasted_iota(jnp.int32, (WINDOW, 2 * WINDOW), 1)
    valid = (j <= i + WINDOW) & (j > i) & ((c > 0) | (j >= WINDOW))


    def ret_task(b, p):
        qk = slice(p * 2 * RET_DK, (p + 1) * 2 * RET_DK)
        vv = slice(p * 2 * RET_DV, (p + 1) * 2 * RET_DV)
        q2, k2, v2, s2 = rq_ref[b, :, qk], rk_ref[b, :, qk], rv_ref[b, :, vv], s_scr[b, p]
        zk = jnp.zeros_like(k2)
        k_rows = jnp.concatenate([jnp.where(lane_lo, k2, zk), jnp.where(lane_lo, zk, k2)], 0)
        sc_raw = lax.dot_general(q2, k_rows, (((1,), (1,)), ((), ())), preferred_element_type=F32)
        s2b = s2.astype(BF16)
        zs = jnp.zeros_like(s2b)
        s_bd = jnp.concatenate([jnp.where(row_lo, s2b, zs), jnp.where(row_lo, zs, s2b)], 1)
        os_raw = jnp.dot(q2, s_bd, preferred_element_type=F32)
        kw2 = (k2.astype(F32) * wend_ref[p]).astype(BF16)
        upd = lax.dot_general(kw2, v2, (((0,), (0,)), ((), ())), preferred_element_type=F32)
        yield
        zv = jnp.zeros((RET_CHUNK, RET_DV), v2.dtype)
        v_bd = jnp.concatenate([jnp.concatenate([v2[:, :RET_DV], zv], 1),
                                jnp.concatenate([zv, v2[:, RET_DV:]], 1)], 0)
        o_raw = jnp.dot((sc_raw * decay_ref[p]).astype(BF16), v_bd, preferred_element_type=F32)
        s_scr[b, p] = gl_ref[p] * s2 + jnp.where(row_lo, upd[:, :RET_DV], upd[:, RET_DV:])
        yield
        o2 = o_raw + os_raw * rowdec_ref[p]
        for u in range(2):
            lo = (2 * p + u) * RET_DV
            gn_out[b, :, lo:lo + RET_DV] = _group_norm(o2[:, u * RET_DV:(u + 1) * RET_DV],
                                                       gn_ref[2 * p + u:2 * p + u + 1, :])

    kv_ctx = {}

    def swa_prep(b):
        k_cur, v_cur = sk_ref[b], sv_ref[b]
        kb, kbr = k_cur.astype(BF16), pltpu.roll(k_cur, SWA_HD, 1).astype(BF16)
        vb, vbr = v_cur.astype(BF16), pltpu.roll(v_cur, SWA_HD, 1).astype(BF16)
        kv_ctx[b] = (jnp.concatenate([kp_scr[b], kb], 0), jnp.concatenate([kpr_scr[b], kbr], 0),
                     jnp.concatenate([vp_scr[b], vb], 0), jnp.concatenate([vpr_scr[b], vbr], 0))
        kp_scr[b], kpr_scr[b], vp_scr[b], vpr_scr[b] = kb, kbr, vb, vbr

    def swa_task(b, kvh):
        if b not in kv_ctx:
            swa_prep(b)
        kc, kcr, vc, vcr = kv_ctx[b]
        zkv = jnp.zeros_like(kc)
        k_lo, k_hi = (kc, kcr) if kvh == 0 else (kcr, kc)
        v_lo, v_hi = (vc, vcr) if kvh == 0 else (vcr, vc)
        k_rows = jnp.concatenate([jnp.where(lane_lo_kv, k_lo, zkv), jnp.where(lane_lo_kv, zkv, k_hi)], 0)
        v_rows = jnp.concatenate([jnp.where(lane_lo_kv, v_lo, zkv), jnp.where(lane_lo_kv, zkv, v_hi)], 0)
        n_pairs = SWA_GROUP // 2
        pairs = [kvh * n_pairs + jj for jj in range(n_pairs)]
        q4 = jnp.concatenate([sq_ref[b, :, pr * pair_w:(pr + 1) * pair_w] for pr in pairs], 0)
        s4 = lax.dot_general(q4, k_rows, (((1,), (1,)), ((), ())), preferred_element_type=F32)
        yield
        rows = []
        for jj, pr in enumerate(pairs):
            ps = []
            for u in range(2):
                s = jnp.where(valid, s4[jj * WINDOW:(jj + 1) * WINDOW, u * 2 * WINDOW:(u + 1) * 2 * WINDOW], -jnp.inf)
                sink = sinks_ref[2 * pr + u]
                m = jnp.maximum(jnp.max(s, -1, keepdims=True), sink)
                e = jnp.exp(s - m)
                den = jnp.sum(e, -1, keepdims=True) + jnp.exp(sink - m)
                ps.append((e / den).astype(BF16))
            rows.append(jnp.concatenate(ps, 1))
        o4 = jnp.dot(jnp.concatenate(rows, 0), v_rows, preferred_element_type=F32)
        yield
        for jj, pr in enumerate(pairs):
            swa_out[b, :, pr * pair_w:(pr + 1) * pair_w] = o4[jj * WINDOW:(jj + 1) * WINDOW].astype(swa_out.dtype)

    def mem_task(b, h):
        sl = slice(h * MEM_HD, (h + 1) * MEM_HD)
        s = lax.dot_general(mq_ref[b, :, sl], mk_ref[b, :, sl], (((1,), (1,)), ((), ())),
                            preferred_element_type=F32) * (MEM_HD ** -0.5)
        yield
        m = jnp.max(s, -1, keepdims=True)
        e = jnp.exp(s - m)
        p_att = (e / jnp.sum(e, -1, keepdims=True)).astype(BF16)
        o = jnp.dot(p_att, mv_ref[b, :, sl], preferred_element_type=F32)
        yield
        mem_out[b, :, sl] = o.astype(mem_out.dtype)

    def region(make_tasks):
        @pl.when(c >= 0)
        def _():
            _run_interleaved(make_tasks())

    for b in range(nb):
        region(lambda b=b: [ret_task(b, p) for p in range(RET_HEADS // 2)])
        region(lambda b=b: [swa_task(b, kvh) for kvh in range(SWA_KV_HEADS)])
        region(lambda b=b: [mem_task(b, h) for h in range(MEM_HEADS)])

    @pl.when(c == pl.num_programs(0) - 1)
    def _():
        s_out[...] = s_scr[...]


def _mix_prompt(rq, rk, rv, sq, sk, sv, mq, mk_bf, mv_bf, sinks, pair_tables, gn_g):
    nb, seq, _ = rq.shape
    decay2, rowdec2, wend2, gl2 = pair_tables
    chunk = lambda w: pl.BlockSpec((nb, RET_CHUNK, w), lambda c: (0, c, 0))
    st_shape = (nb, RET_HEADS // 2, 2 * RET_DK, RET_DV)
    kv_scr = pltpu.VMEM((nb, WINDOW, SWA_KV_W), BF16)
    return pl.pallas_call(
        functools.partial(_mix_prompt_kernel, nb),
        grid=(seq // RET_CHUNK,),
        in_specs=[pl.BlockSpec(memory_space=pltpu.SMEM),
                  chunk(RET_QK_W), chunk(RET_QK_W), chunk(RET_V_W), chunk(SWA_Q_W), chunk(SWA_KV_W), chunk(SWA_KV_W),
                  chunk(MEM_W), _const_spec(mk_bf.shape), _const_spec(mv_bf.shape),
                  _const_spec(decay2.shape), _const_spec(rowdec2.shape), _const_spec(wend2.shape),
                  _const_spec(gl2.shape), _const_spec(gn_g.shape)],
        out_specs=[chunk(RET_V_W), chunk(SWA_Q_W), chunk(MEM_W), _const_spec(st_shape)],
        out_shape=[jax.ShapeDtypeStruct((nb, seq, RET_V_W), F32), jax.ShapeDtypeStruct((nb, seq, SWA_Q_W), BF16),
                   jax.ShapeDtypeStruct((nb, seq, MEM_W), BF16), jax.ShapeDtypeStruct(st_shape, F32)],
        scratch_shapes=[pltpu.VMEM(st_shape, F32), kv_scr, kv_scr, kv_scr, kv_scr],
        compiler_params=_params(1),
        name="mix_prompt",
    )(sinks, rq, rk, rv, sq, sk, sv, mq, mk_bf, mv_bf, decay2, rowdec2, wend2, gl2, gn_g)


def _mem_kv_kernel(x_ref, w_ref, k_ref, v_ref, kb_ref, vb_ref):
    kv = jnp.dot(x_ref[...].astype(BF16), w_ref[0], preferred_element_type=F32)
    k_ref[...] = kv[:, :MEM_W]
    v_ref[...] = kv[:, MEM_W:]
    kb_ref[...] = kv[:, :MEM_W].astype(BF16)
    vb_ref[...] = kv[:, MEM_W:].astype(BF16)


def _mem_kv(mem2d, w_bf, layer, tm):
    m, k = mem2d.shape
    out = pl.BlockSpec((tm, MEM_W), lambda i: (i, 0))
    return pl.pallas_call(
        _mem_kv_kernel,
        grid=(m // tm,),
        in_specs=[pl.BlockSpec((tm, k), lambda i: (i, 0)), _layer_spec(w_bf.shape, layer)],
        out_specs=[out, out, out, out],
        out_shape=[jax.ShapeDtypeStruct((m, MEM_W), F32), jax.ShapeDtypeStruct((m, MEM_W), F32),
                   jax.ShapeDtypeStruct((m, MEM_W), BF16), jax.ShapeDtypeStruct((m, MEM_W), BF16)],
        compiler_params=_params(1),
        name="mem_kv",
    )(mem2d, w_bf)


def _layer_norm(x, g, b):
    mu = jnp.mean(x, -1, keepdims=True)
    d = x - mu
    var = jnp.mean(d * d, -1, keepdims=True)
    return d * lax.rsqrt(var + LN_EPS) * g + b


def _finish_kernel(x_ref, gn_ref, rg_ref, swa_ref, mem_ref, gr_ref, gs_ref, gm_ref,
                   wr_ref, ws_ref, wm_ref, wo_ref, l1g_ref, l1b_ref, wu_ref, wd_ref, l2g_ref, l2b_ref, o_ref):
    rg = rg_ref[...]
    ret_in = (rg * jax.nn.sigmoid(rg) * gn_ref[...]).astype(BF16)
    ret_b = jnp.dot(ret_in, wr_ref[0], preferred_element_type=F32)
    swa_b = jnp.dot(swa_ref[...], ws_ref[0], preferred_element_type=F32)
    mem_b = jnp.dot(mem_ref[...], wm_ref[0], preferred_element_type=F32)
    merged = (jax.nn.sigmoid(gr_ref[...]) * ret_b + jax.nn.sigmoid(gs_ref[...]) * swa_b
              + jax.nn.sigmoid(gm_ref[...]) * mem_b)
    y = jnp.dot(merged.astype(BF16), wo_ref[0], preferred_element_type=F32)
    x1 = _layer_norm(ALPHA * x_ref[...] + y, l1g_ref[0], l1b_ref[0])
    x1b = x1.astype(BF16)
    ff_chunk = D_FF // 4
    acc = jnp.zeros_like(x1)
    for c in range(D_FF // ff_chunk):
        h = jnp.dot(x1b, wu_ref[0, :, c * ff_chunk:(c + 1) * ff_chunk], preferred_element_type=F32)
        h = jnp.square(jnp.maximum(h, 0.0)).astype(BF16)
        acc = acc + jnp.dot(h, wd_ref[0, c * ff_chunk:(c + 1) * ff_chunk, :], preferred_element_type=F32)
    o_ref[...] = _layer_norm(ALPHA * x1 + acc, l2g_ref[0], l2b_ref[0])


def _finish(x2d, gn, rg, swa_o, mem_o, g_r, g_s, g_m, lw, layer, tm):
    m = x2d.shape[0]
    row = lambda w: pl.BlockSpec((tm, w), lambda i: (i, 0))
    lspec = lambda a: _layer_spec(a.shape, layer)
    return pl.pallas_call(
        _finish_kernel,
        grid=(m // tm,),
        in_specs=[row(D_MODEL), row(RET_V_W), row(RET_V_W), row(SWA_Q_W), row(MEM_W),
                  row(D_MODEL), row(D_MODEL), row(D_MODEL)] + [lspec(a) for a in lw],
        out_specs=row(D_MODEL),
        out_shape=jax.ShapeDtypeStruct((m, D_MODEL), F32),
        compiler_params=_params(1),
        name="finish",
    )(x2d, gn, rg, swa_o, mem_o, g_r, g_s, g_m, *lw)


def _rope_tables(pos):
    half = SWA_HD // 2
    inv = jnp.power(ROPE_THETA, -jnp.arange(half, dtype=F32) / half)
    ang = pos.astype(F32)[:, None] * inv[None, :]
    c, s = jnp.cos(ang), jnp.sin(ang)
    return jnp.concatenate([c, c, c, c], -1), jnp.concatenate([-s, s, -s, s], -1)


def kernel(x_prompt, x_sample, state_ret, cache_swa_k, cache_swa_v, cache_mem_k, cache_mem_v, mem_prompt,
           w_in, w_br_ret, w_br_swa, w_br_mem, w_out, w_mem_kv, attn_sinks, ret_gn_g,
           ln1_g, ln1_b, w_up, w_down, ln2_g, ln2_b):
    batch, seq, _ = x_prompt.shape
    dec_b, dec_t, _ = x_sample.shape
    tm_p, tm_s = 512, 256
    tm_fin = 256
    ret_seqs = RET_CHUNK // dec_t

    cos_p, sin_p = _rope_tables(jnp.arange(seq, dtype=jnp.int32))
    cos_s, sin_s = _rope_tables(PAST_LEN + jnp.arange(dec_t, dtype=jnp.int32))
    cos_s, sin_s = jnp.tile(cos_s, (tm_s // dec_t, 1)), jnp.tile(sin_s, (tm_s // dec_t, 1))
    lg = jnp.log1p(-jnp.exp2(-5.0 - jnp.arange(RET_HEADS, dtype=F32)))
    tab_p = _ret_tables(lg, RET_CHUNK, RET_CHUNK)
    tab_p2 = _pair_tables(tab_p)
    tab_s2 = _pair_tables(_ret_tables(lg, RET_CHUNK, dec_t))

    xp = x_prompt.reshape(batch * seq, D_MODEL)
    xs = x_sample.reshape(dec_b * dec_t, D_MODEL)
    mem2d = mem_prompt.reshape(batch * N_MEM, D_MODEL)
    cache_kt = jnp.transpose(cache_swa_k, (0, 1, 3, 4, 2))
    cache_vt = jnp.transpose(cache_swa_v, (0, 1, 3, 4, 2))
    cache_mk = cache_mem_k.reshape(DEPTH, dec_b, N_MEM * MEM_HEADS, MEM_HD)
    cache_mv = cache_mem_v.reshape(DEPTH, dec_b, N_MEM * MEM_HEADS, MEM_HD)

    w_in_bf = w_in.astype(BF16)
    w_mem_kv_bf = w_mem_kv.astype(BF16)
    ln_row = lambda a: a.reshape(DEPTH, 1, D_MODEL)
    lw = (w_br_ret.astype(BF16), w_br_swa.astype(BF16), w_br_mem.astype(BF16), w_out.astype(BF16),
          ln_row(ln1_g), ln_row(ln1_b), w_up.astype(BF16), w_down.astype(BF16), ln_row(ln2_g), ln_row(ln2_b))

    ret_p, swk_p, swv_p, mk_p, mv_p = [], [], [], [], []
    ret_s, swa_s = None, None
    for l in range(DEPTH):
        gn_g = ret_gn_g[l]
        sinks = attn_sinks[l]

        rq, rk, rv, rg, sq, sk, sv, mq, g_r, g_s, g_m = _inproj(xp, w_in_bf, l, cos_p, sin_p, tm_p, BF16)
        mk, mv, mk_bf, mv_bf = _mem_kv(mem2d, w_mem_kv_bf, l, 256)
        by_seq = lambda a: a.reshape(batch, seq, a.shape[-1])
        by_mem = lambda a: a.reshape(batch, N_MEM, MEM_W)
        gn, swa_o, mem_o, s_p = _mix_prompt(by_seq(rq), by_seq(rk), by_seq(rv), by_seq(sq), by_seq(sk), by_seq(sv),
                                            by_seq(mq), by_mem(mk_bf), by_mem(mv_bf), sinks, tab_p2, gn_g)
        flat = lambda a: a.reshape(batch * seq, a.shape[-1])
        xp = _finish(xp, flat(gn), rg, flat(swa_o), flat(mem_o), g_r, g_s, g_m, lw, l, tm_fin)
        ret_p.append(s_p.reshape(batch, RET_HEADS, RET_DK, RET_DV))
        swk_p.append(sk.reshape(batch, seq, SWA_KV_W)[:, -WINDOW:].reshape(batch, WINDOW, SWA_KV_HEADS, SWA_HD))
        swv_p.append(sv.reshape(batch, seq, SWA_KV_W)[:, -WINDOW:].reshape(batch, WINDOW, SWA_KV_HEADS, SWA_HD))
        mk_p.append(mk.reshape(batch, N_MEM, MEM_HEADS, MEM_HD))
        mv_p.append(mv.reshape(batch, N_MEM, MEM_HEADS, MEM_HD))

        rq, rk, rv, rg, sq, sk, sv, mq, g_r, g_s, g_m = _inproj(xs, w_in_bf, l, cos_s, sin_s, tm_s, F32)
        gn, ret_s = _ret_sample(rq, rk, rv, state_ret, ret_s, l, tab_s2, gn_g, dec_t, ret_seqs)
        swa_o, kto, vto = _swa_sample(sq, sk, sv, cache_kt, cache_vt, swa_s, sinks, l, dec_t, WINDOW // dec_t)
        swa_s = (kto, vto)
        mem_o = _mem_sample(mq, cache_mk, cache_mv, l, dec_t, 8)
        xs = _finish(xs, gn, rg, swa_o, mem_o, g_r, g_s, g_m, lw, l, tm_s)

    from_t = lambda a: jnp.transpose(a, (0, 1, 4, 2, 3))
    return (xp.reshape(batch, seq, D_MODEL), xs.reshape(dec_b, dec_t, D_MODEL),
            jnp.stack(ret_p), jnp.stack(swk_p), jnp.stack(swv_p), jnp.stack(mk_p), jnp.stack(mv_p),
            ret_s, from_t(swa_s[0]), from_t(swa_s[1]))
```

```python
import functools

import jax
import jax.numpy as jnp
from jax import lax
from jax.experimental import pallas as pl
from jax.experimental.pallas import tpu as pltpu

F32 = jnp.float32
BF16 = jnp.bfloat16

D_MODEL = 1024
DEPTH = 2
PAST_LEN = 16384
RET_HEADS = 8
RET_DK = 64
RET_DV = 128
RET_CHUNK = 128
SWA_HEADS = 8
SWA_KV_HEADS = 2
SWA_GROUP = SWA_HEADS // SWA_KV_HEADS
SWA_HD = 64
WINDOW = 128
MEM_HEADS = 4
MEM_HD = 128
N_MEM = 256
D_FF = 4 * D_MODEL
ROPE_THETA = 10000.0
LN_EPS = 1e-5
GN_EPS = 1e-5
ALPHA = (2 * DEPTH) ** 0.25

RET_QK_W = RET_HEADS * RET_DK
RET_V_W = RET_HEADS * RET_DV
SWA_Q_W = SWA_HEADS * SWA_HD
SWA_KV_W = SWA_KV_HEADS * SWA_HD
MEM_W = MEM_HEADS * MEM_HD
OFF_RQ = 0
OFF_RK = OFF_RQ + RET_QK_W
OFF_RV = OFF_RK + RET_QK_W
OFF_RG = OFF_RV + RET_V_W
OFF_SQ = OFF_RG + RET_V_W
OFF_SK = OFF_SQ + SWA_Q_W
OFF_SV = OFF_SK + SWA_KV_W
OFF_MQ = OFF_SV + SWA_KV_W
OFF_GR = OFF_MQ + MEM_W
OFF_GS = OFF_GR + D_MODEL
OFF_GM = OFF_GS + D_MODEL
IN_W = OFF_GM + D_MODEL

LANES = 128
V7X_VMEM_LIMIT = 56 * 1024 * 1024


def _const_spec(shape):
    nd = len(shape)
    return pl.BlockSpec(shape, lambda *_: (0,) * nd, pipeline_mode=pl.Buffered(1))


def _layer_spec(shape, layer):
    nd = len(shape)
    return pl.BlockSpec((1,) + tuple(shape[1:]), lambda *_: (layer,) + (0,) * (nd - 1), pipeline_mode=pl.Buffered(1))


def _params(n_grid):
    return pltpu.CompilerParams(dimension_semantics=("arbitrary",) * n_grid, vmem_limit_bytes=V7X_VMEM_LIMIT)


def _inproj_kernel(x_ref, w_ref, cos_ref, sin_ref,
                   rq_ref, rk_ref, rv_ref, rg_ref, sq_ref, sk_ref, sv_ref, mq_ref, gr_ref, gs_ref, gm_ref):
    xb = x_ref[...].astype(BF16)
    cos = cos_ref[...]
    sin = sin_ref[...]
    lane = lax.broadcasted_iota(jnp.int32, cos.shape, 1)
    first_half = (lane & (SWA_HD // 2)) == 0

    def proj(off, width):
        return jnp.dot(xb, w_ref[0, :, off:off + width], preferred_element_type=F32)

    def rope_store(off, width, out_ref, scale):
        y = proj(off, width)
        for j in range(width // LANES):
            yj = y[:, j * LANES:(j + 1) * LANES]
            sw = jnp.where(first_half, pltpu.roll(yj, LANES - SWA_HD // 2, 1), pltpu.roll(yj, SWA_HD // 2, 1))
            r = yj * cos + sw * sin
            if scale != 1.0:
                r = r * scale
            out_ref[:, j * LANES:(j + 1) * LANES] = r.astype(out_ref.dtype)

    def plain_store(off, width, out_ref):
        out_ref[...] = proj(off, width).astype(out_ref.dtype)

    rope_store(OFF_RQ, RET_QK_W, rq_ref, 1.0)
    rope_store(OFF_RK, RET_QK_W, rk_ref, RET_DK ** -0.5)
    plain_store(OFF_RV, RET_V_W, rv_ref)
    plain_store(OFF_RG, RET_V_W, rg_ref)
    rope_store(OFF_SQ, SWA_Q_W, sq_ref, SWA_HD ** -0.5)
    rope_store(OFF_SK, SWA_KV_W, sk_ref, 1.0)
    plain_store(OFF_SV, SWA_KV_W, sv_ref)
    plain_store(OFF_MQ, MEM_W, mq_ref)
    plain_store(OFF_GR, D_MODEL, gr_ref)
    plain_store(OFF_GS, D_MODEL, gs_ref)
    plain_store(OFF_GM, D_MODEL, gm_ref)


def _inproj(x2d, w_bf, layer, cos_tab, sin_tab, tm, qkv_dtype):
    m = x2d.shape[0]
    n_tab = cos_tab.shape[0] // tm
    row = lambda w: pl.BlockSpec((tm, w), lambda i: (i, 0))
    tab = pl.BlockSpec((tm, LANES), lambda i: (i % n_tab, 0))
    widths_dtypes = [(RET_QK_W, qkv_dtype), (RET_QK_W, qkv_dtype), (RET_V_W, qkv_dtype), (RET_V_W, F32),
                     (SWA_Q_W, qkv_dtype), (SWA_KV_W, F32), (SWA_KV_W, F32), (MEM_W, qkv_dtype),
                     (D_MODEL, F32), (D_MODEL, F32), (D_MODEL, F32)]
    return pl.pallas_call(
        _inproj_kernel,
        grid=(m // tm,),
        in_specs=[row(D_MODEL), _layer_spec(w_bf.shape, layer), tab, tab],
        out_specs=[row(w) for w, _ in widths_dtypes],
        out_shape=[jax.ShapeDtypeStruct((m, w), dt) for w, dt in widths_dtypes],
        compiler_params=_params(1),
        name="inproj",
    )(x2d, w_bf, cos_tab, sin_tab)


def _group_norm(o, g_row):
    mu = jnp.mean(o, -1, keepdims=True)
    d = o - mu
    var = jnp.mean(d * d, -1, keepdims=True)
    return d * lax.rsqrt(var + GN_EPS) * g_row


def _ret_tables(lg, n_rows, period):
    r = jnp.arange(n_rows)
    t = (r % period).astype(F32)
    same = (r[:, None] // period) == (r[None, :] // period)
    diff = t[:, None] - t[None, :]
    decay = jnp.where((diff >= 0) & same, jnp.exp(lg[:, None, None] * jnp.maximum(diff, 0.0)), 0.0)
    rowdec = jnp.exp(lg[:, None] * (t[None, :] + 1.0))
    wend = jnp.exp(lg[:, None] * (period - 1.0 - t[None, :]))
    gl = jnp.exp(lg * period)
    rowdec = jnp.broadcast_to(rowdec[:, :, None], (RET_HEADS, n_rows, RET_DV))
    wend = jnp.broadcast_to(wend[:, :, None], (RET_HEADS, n_rows, RET_DK))
    gl = jnp.broadcast_to(gl[:, None, None], (RET_HEADS, 1, RET_DV))
    return decay, rowdec, wend, gl


def _ret_sample_kernel(n_seq, t_len, has_prev, q_ref, k_ref, v_ref, s_ref, decay_ref, rowdec_ref, wend_ref, gl_ref,
                       *rest):
    o_ref, s_out_ref = rest[-2:]
    del has_prev
    rows = n_seq * t_len
    pair_dk, pair_dv = 2 * RET_DK, 2 * RET_DV
    lane_lo = lax.broadcasted_iota(jnp.int32, (rows, pair_dk), 1) < RET_DK
    row_lo = lax.broadcasted_iota(jnp.int32, (pair_dk, RET_DV), 0) < RET_DK
    for p in range(RET_HEADS // 2):
        qk = slice(p * pair_dk, (p + 1) * pair_dk)
        vv = slice(p * pair_dv, (p + 1) * pair_dv)
        q2f, k2f, v2f = q_ref[:, qk], k_ref[:, qk], v_ref[:, vv]
        q2, k2, v2 = q2f.astype(BF16), k2f.astype(BF16), v2f.astype(BF16)
        zk = jnp.zeros_like(k2)
        k_rows = jnp.concatenate([jnp.where(lane_lo, k2, zk), jnp.where(lane_lo, zk, k2)], 0)
        sc2 = lax.dot_general(q2, k_rows, (((1,), (1,)), ((), ())), preferred_element_type=F32) * decay_ref[p]
        zv = jnp.zeros((rows, RET_DV), BF16)
        v_bd = jnp.concatenate([jnp.concatenate([v2[:, :RET_DV], zv], 1),
                                jnp.concatenate([zv, v2[:, RET_DV:]], 1)], 0)
        o2 = jnp.dot(sc2.astype(BF16), v_bd, preferred_element_type=F32)
        kw2f = k2f * wend_ref[p]
        o_state = []
        for b in range(n_seq):
            r = slice(b * t_len, (b + 1) * t_len)
            s2 = s_ref[0, b, 2 * p:2 * p + 2].reshape(pair_dk, RET_DV)
            s2b = s2.astype(BF16)
            zs = jnp.zeros_like(s2b)
            s_bd = jnp.concatenate([jnp.where(row_lo, s2b, zs), jnp.where(row_lo, zs, s2b)], 1)
            o_state.append(jnp.dot(q2f[r].astype(BF16), s_bd, preferred_element_type=F32))
            upd = lax.dot_general(kw2f[r].astype(BF16), v2f[r].astype(BF16), (((0,), (0,)), ((), ())),
                                  preferred_element_type=F32)
            s_new = (gl_ref[p] * s2 + jnp.where(row_lo, upd[:, :RET_DV], upd[:, RET_DV:])).reshape(2, RET_DK, RET_DV)
            for d in range(s_out_ref.shape[0]):
                s_out_ref[d, b, 2 * p:2 * p + 2] = s_new
        o_ref[:, vv] = o2 + jnp.concatenate(o_state, 0) * rowdec_ref[p]


def _stacked_out_specs(shape, layer, n_seq):
    tail = tuple(shape[2:])
    zeros = (0,) * len(tail)
    if layer == 0:
        return pl.BlockSpec((shape[0], n_seq) + tail, lambda i: (0, i) + zeros)
    return pl.BlockSpec((1, n_seq) + tail, lambda i: (layer, i) + zeros)


def _ret_sample(rq, rk, rv, state, prev_out, layer, pair_tables, t_len, n_seq):
    m = rq.shape[0]
    rows = n_seq * t_len
    decay2, rowdec2, wend2, gl2 = pair_tables
    row = lambda w: pl.BlockSpec((rows, w), lambda i: (i, 0))
    st_in = pl.BlockSpec((1, n_seq, RET_HEADS, RET_DK, RET_DV), lambda i: (layer, i, 0, 0, 0))
    in_specs = [row(RET_QK_W), row(RET_QK_W), row(RET_V_W), st_in,
                _const_spec(decay2.shape), _const_spec(rowdec2.shape), _const_spec(wend2.shape),
                _const_spec(gl2.shape)]
    args = [rq, rk, rv, state, decay2, rowdec2, wend2, gl2]
    aliases = {}
    if prev_out is not None:
        in_specs.append(pl.BlockSpec(memory_space=pl.ANY))
        args.append(prev_out)
        aliases = {len(args) - 1: 1}
    return pl.pallas_call(
        functools.partial(_ret_sample_kernel, n_seq, t_len, prev_out is not None),
        grid=(m // rows,),
        in_specs=in_specs,
        out_specs=[row(RET_V_W), _stacked_out_specs(state.shape, layer, n_seq)],
        out_shape=[jax.ShapeDtypeStruct((m, RET_V_W), F32), jax.ShapeDtypeStruct(state.shape, F32)],
        input_output_aliases=aliases,
        compiler_params=_params(1),
        name="ret_sample",
    )(*args)


def _swa_sample_kernel(n_seq, t_len, sinks_ref, q_ref, kn_ref, vn_ref, kt_ref, vt_ref, *rest):
    o_ref, kto_ref, vto_ref = rest[-3:]
    grp_rows = SWA_GROUP * t_len
    n_all = n_seq * grp_rows
    q = q_ref[...]
    kn = kn_ref[...]
    vn = vn_ref[...]
    kn_t = kn.T
    vn_t = vn.T
    r = lax.broadcasted_iota(jnp.int32, (n_all, 1), 0)
    t_q = r % t_len
    g_row = (r // t_len) % SWA_GROUP
    b_row = r // grp_rows
    c = lax.broadcasted_iota(jnp.int32, (1, WINDOW), 1)
    valid_cache = c > t_q
    valid_new = ((c // t_len) == b_row) & ((c % t_len) <= t_q)
    lane = lax.broadcasted_iota(jnp.int32, (SWA_HD, WINDOW), 1)
    is_new_lane = lane >= WINDOW - t_len
    pieces = []
    for kvh in range(SWA_KV_HEADS):
        hd = slice(kvh * SWA_HD, (kvh + 1) * SWA_HD)
        qg = [q[:, (kvh * SWA_GROUP + g) * SWA_HD:(kvh * SWA_GROUP + g + 1) * SWA_HD] for g in range(SWA_GROUP)]
        q_all = jnp.concatenate([qg[g][b * t_len:(b + 1) * t_len] for b in range(n_seq) for g in range(SWA_GROUP)],
                                0).astype(BF16)
        s_new = jnp.dot(q_all, kn_t[hd].astype(BF16), preferred_element_type=F32)
        s_cache = jnp.concatenate(
            [jnp.dot(q_all[b * grp_rows:(b + 1) * grp_rows], kt_ref[0, b, kvh].astype(BF16),
                     preferred_element_type=F32) for b in range(n_seq)], 0)
        s_new = jnp.where(valid_new, s_new, -jnp.inf)
        s_cache = jnp.where(valid_cache, s_cache, -jnp.inf)
        sink = jnp.full((n_all, 1), sinks_ref[kvh * SWA_GROUP], F32)
        for g in range(1, SWA_GROUP):
            sink = jnp.where(g_row == g, sinks_ref[kvh * SWA_GROUP + g], sink)
        m = jnp.maximum(jnp.maximum(jnp.max(s_new, -1, keepdims=True), jnp.max(s_cache, -1, keepdims=True)), sink)
        e_new = jnp.exp(s_new - m)
        e_cache = jnp.exp(s_cache - m)
        den = jnp.sum(e_new, -1, keepdims=True) + jnp.sum(e_cache, -1, keepdims=True) + jnp.exp(sink - m)
        p_new = (e_new / den).astype(BF16)
        p_cache = (e_cache / den).astype(BF16)
        o = jnp.dot(p_new, vn[:, hd].astype(BF16), preferred_element_type=F32)
        o = o + jnp.concatenate(
            [lax.dot_general(p_cache[b * grp_rows:(b + 1) * grp_rows], vt_ref[0, b, kvh].astype(BF16),
                             (((1,), (1,)), ((), ())), preferred_element_type=F32) for b in range(n_seq)], 0)
        for g in range(SWA_GROUP):
            pieces.append(jnp.concatenate(
                [o[b * grp_rows + g * t_len:b * grp_rows + (g + 1) * t_len] for b in range(n_seq)], 0))
        for b in range(n_seq):
            shift_new = (WINDOW - t_len - b * t_len) % WINDOW
            k_slid = jnp.where(is_new_lane, pltpu.roll(kn_t[hd], shift_new, 1),
                               pltpu.roll(kt_ref[0, b, kvh], WINDOW - t_len, 1))
            v_slid = jnp.where(is_new_lane, pltpu.roll(vn_t[hd], shift_new, 1),
                               pltpu.roll(vt_ref[0, b, kvh], WINDOW - t_len, 1))
            for d in range(kto_ref.shape[0]):
                kto_ref[d, b, kvh] = k_slid
                vto_ref[d, b, kvh] = v_slid
    o_ref[...] = jnp.concatenate(pieces, -1).astype(o_ref.dtype)


def _swa_sample(sq, sk, sv, cache_kt, cache_vt, prev_out, sinks, layer, t_len, n_seq):
    m = sq.shape[0]
    rows = n_seq * t_len
    assert rows == WINDOW and cache_kt.shape[-1] == WINDOW
    row = lambda w: pl.BlockSpec((rows, w), lambda i: (i, 0))
    cin = pl.BlockSpec((1, n_seq, SWA_KV_HEADS, SWA_HD, WINDOW), lambda i: (layer, i, 0, 0, 0))
    cout = _stacked_out_specs(cache_kt.shape, layer, n_seq)
    cshape = jax.ShapeDtypeStruct(cache_kt.shape, F32)
    in_specs = [pl.BlockSpec(memory_space=pltpu.SMEM), row(SWA_Q_W), row(SWA_KV_W), row(SWA_KV_W), cin, cin]
    args = [sinks, sq, sk, sv, cache_kt, cache_vt]
    aliases = {}
    if prev_out is not None:
        in_specs += [pl.BlockSpec(memory_space=pl.ANY)] * 2
        args += list(prev_out)
        aliases = {len(args) - 2: 1, len(args) - 1: 2}
    return pl.pallas_call(
        functools.partial(_swa_sample_kernel, n_seq, t_len),
        grid=(m // rows,),
        in_specs=in_specs,
        out_specs=[row(SWA_Q_W), cout, cout],
        out_shape=[jax.ShapeDtypeStruct((m, SWA_Q_W), BF16), cshape, cshape],
        input_output_aliases=aliases,
        compiler_params=_params(1),
        name="swa_sample",
    )(*args)


def _mem_sample_kernel(n_seq, t_len, q_ref, k_ref, v_ref, o_ref):
    rows = MEM_HEADS * t_len
    head_of_row = lax.broadcasted_iota(jnp.int32, (rows, 1), 0) // t_len
    head_of_col = lax.broadcasted_iota(jnp.int32, (1, N_MEM * MEM_HEADS), 1) % MEM_HEADS
    valid = head_of_row == head_of_col
    outs = []
    for b in range(n_seq):
        qb = q_ref[b * t_len:(b + 1) * t_len, :]
        q_all = jnp.concatenate([qb[:, h * MEM_HD:(h + 1) * MEM_HD] for h in range(MEM_HEADS)], 0).astype(BF16)
        s = lax.dot_general(q_all, k_ref[0, b].astype(BF16), (((1,), (1,)), ((), ())),
                            preferred_element_type=F32) * (MEM_HD ** -0.5)
        s = jnp.where(valid, s, -jnp.inf)
        m = jnp.max(s, -1, keepdims=True)
        e = jnp.exp(s - m)
        p = e / jnp.sum(e, -1, keepdims=True)
        o = jnp.dot(p.astype(BF16), v_ref[0, b].astype(BF16), preferred_element_type=F32)
        outs.append(jnp.concatenate([o[h * t_len:(h + 1) * t_len] for h in range(MEM_HEADS)], -1))
    o_ref[...] = jnp.concatenate(outs, 0).astype(o_ref.dtype)


def _mem_sample(mq, cache_k, cache_v, layer, t_len, n_seq):
    m = mq.shape[0]
    rows = n_seq * t_len
    row = pl.BlockSpec((rows, MEM_W), lambda i: (i, 0))
    kv = pl.BlockSpec((1, n_seq, N_MEM * MEM_HEADS, MEM_HD), lambda i: (layer, i, 0, 0))
    return pl.pallas_call(
        functools.partial(_mem_sample_kernel, n_seq, t_len),
        grid=(m // rows,),
        in_specs=[row, kv, kv],
        out_specs=row,
        out_shape=jax.ShapeDtypeStruct((m, MEM_W), BF16),
        compiler_params=_params(1),
        name="mem_sample",
    )(mq, cache_k, cache_v)


def _pair_tables(tables):
    decay, rowdec, wend, gl = tables
    pair = lambda a: jnp.concatenate([a[0::2], a[1::2]], -1)
    gl_rows = jnp.concatenate([jnp.broadcast_to(gl[0::2], (RET_HEADS // 2, RET_DK, RET_DV)),
                               jnp.broadcast_to(gl[1::2], (RET_HEADS // 2, RET_DK, RET_DV))], 1)
    return pair(decay), pair(rowdec), pair(wend), gl_rows


def _run_interleaved(tasks):
    pending, active = list(tasks), []
    while pending or active:
        if pending:
            active.append(pending.pop(0))
        for t in list(active):
            try:
                next(t)
            except StopIteration:
                active.remove(t)


def _mix_prompt_kernel(nb, sinks_ref, rq_ref, rk_ref, rv_ref, sq_ref, sk_ref, sv_ref, mq_ref, mk_ref, mv_ref,
                       decay_ref, rowdec_ref, wend_ref, gl_ref,
                       ret_out, swa_out, mem_out, s_out,
                       s_scr, kp_scr, kpr_scr, vp_scr, vpr_scr):
    c = pl.program_id(0)

    @pl.when(c == 0)
    def _():
        s_scr[...] = jnp.zeros_like(s_scr)
        for scr in (kp_scr, kpr_scr, vp_scr, vpr_scr):
            scr[...] = jnp.zeros_like(scr)

    pair_w = 2 * SWA_HD
    lane_lo = lax.broadcasted_iota(jnp.int32, (RET_CHUNK, pair_w), 1) < SWA_HD
    row_lo = lax.broadcasted_iota(jnp.int32, (2 * RET_DK, RET_DV), 0) < RET_DK
    lane_lo_kv = lax.broadcasted_iota(jnp.int32, (2 * WINDOW, pair_w), 1) < SWA_HD
    upper = (lax.broadcasted_iota(jnp.int32, (WINDOW, WINDOW), 1)
             > lax.broadcasted_iota(jnp.int32, (WINDOW, WINDOW), 0))
    prev_bias = jnp.where(c > 0, 0.0, -jnp.inf)


    def ret_task(b, p):
        qk = slice(p * 2 * RET_DK, (p + 1) * 2 * RET_DK)
        vv = slice(p * 2 * RET_DV, (p + 1) * 2 * RET_DV)
        q2, k2, v2, s2 = rq_ref[b, :, qk], rk_ref[b, :, qk], rv_ref[b, :, vv], s_scr[b, p]
        zk = jnp.zeros_like(k2)
        k_rows = jnp.concatenate([jnp.where(lane_lo, k2, zk), jnp.where(lane_lo, zk, k2)], 0)
        sc_raw = lax.dot_general(q2, k_rows, (((1,), (1,)), ((), ())), preferred_element_type=F32)
        s2b = s2.astype(BF16)
        zs = jnp.zeros_like(s2b)
        s_bd = jnp.concatenate([jnp.where(row_lo, s2b, zs), jnp.where(row_lo, zs, s2b)], 1)
        os_raw = jnp.dot(q2, s_bd, preferred_element_type=F32)
        kw2 = (k2.astype(F32) * wend_ref[p]).astype(BF16)
        upd = lax.dot_general(kw2, v2, (((0,), (0,)), ((), ())), preferred_element_type=F32)
        yield
        zv = jnp.zeros((RET_CHUNK, RET_DV), v2.dtype)
        v_bd = jnp.concatenate([jnp.concatenate([v2[:, :RET_DV], zv], 1),
                                jnp.concatenate([zv, v2[:, RET_DV:]], 1)], 0)
        o_raw = jnp.dot((sc_raw * decay_ref[p]).astype(BF16), v_bd, preferred_element_type=F32)
        s_scr[b, p] = gl_ref[p] * s2 + jnp.where(row_lo, upd[:, :RET_DV], upd[:, RET_DV:])
        yield
        ret_out[b, :, vv] = o_raw + os_raw * rowdec_ref[p]

    kv_ctx = {}

    def swa_prep(b):
        k_cur, v_cur = sk_ref[b], sv_ref[b]
        kb, kbr = k_cur.astype(BF16), pltpu.roll(k_cur, SWA_HD, 1).astype(BF16)
        vb, vbr = v_cur.astype(BF16), pltpu.roll(v_cur, SWA_HD, 1).astype(BF16)
        kv_ctx[b] = (jnp.concatenate([kp_scr[b], kb], 0), jnp.concatenate([kpr_scr[b], kbr], 0),
                     jnp.concatenate([vp_scr[b], vb], 0), jnp.concatenate([vpr_scr[b], vbr], 0))
        kp_scr[b], kpr_scr[b], vp_scr[b], vpr_scr[b] = kb, kbr, vb, vbr

    def swa_task(b, kvh):
        if b not in kv_ctx:
            swa_prep(b)
        kc, kcr, vc, vcr = kv_ctx[b]
        zkv = jnp.zeros_like(kc)
        k_lo, k_hi = (kc, kcr) if kvh == 0 else (kcr, kc)
        v_lo, v_hi = (vc, vcr) if kvh == 0 else (vcr, vc)
        k_rows = jnp.concatenate([jnp.where(lane_lo_kv, k_lo, zkv), jnp.where(lane_lo_kv, zkv, k_hi)], 0)
        v_rows = jnp.concatenate([jnp.where(lane_lo_kv, v_lo, zkv), jnp.where(lane_lo_kv, zkv, v_hi)], 0)
        n_pairs = SWA_GROUP // 2
        pairs = [kvh * n_pairs + jj for jj in range(n_pairs)]
        q4 = jnp.concatenate([sq_ref[b, :, pr * pair_w:(pr + 1) * pair_w] for pr in pairs], 0)
        s4 = lax.dot_general(q4, k_rows, (((1,), (1,)), ((), ())), preferred_element_type=F32)
        yield
        rows, inv = [], []
        for jj, pr in enumerate(pairs):
            ps, inv_u = [], []
            for u in range(2):
                blk = s4[jj * WINDOW:(jj + 1) * WINDOW, u * 2 * WINDOW:(u + 1) * 2 * WINDOW]
                s = jnp.where(upper, blk[:, :WINDOW] + prev_bias, blk[:, WINDOW:])
                sink = sinks_ref[2 * pr + u]
                m = jnp.maximum(jnp.max(s, -1, keepdims=True), sink)
                e = jnp.exp(s - m)
                den = jnp.sum(e, -1, keepdims=True) + jnp.exp(sink - m)
                ps += [jnp.where(upper, e, 0.0).astype(BF16), jnp.where(upper, 0.0, e).astype(BF16)]
                inv_u.append(1.0 / den)
            rows.append(jnp.concatenate(ps, 1))
            inv.append(jnp.where(lane_lo, inv_u[0], inv_u[1]))
        o4 = jnp.dot(jnp.concatenate(rows, 0), v_rows, preferred_element_type=F32)
        yield
        for jj, pr in enumerate(pairs):
            swa_out[b, :, pr * pair_w:(pr + 1) * pair_w] = (o4[jj * WINDOW:(jj + 1) * WINDOW] * inv[jj]).astype(
                swa_out.dtype)

    def mem_task(b, h):
        sl = slice(h * MEM_HD, (h + 1) * MEM_HD)
        s = lax.dot_general(mq_ref[b, :, sl], mk_ref[b, :, sl], (((1,), (1,)), ((), ())),
                            preferred_element_type=F32) * (MEM_HD ** -0.5)
        yield
        m = jnp.max(s, -1, keepdims=True)
        e = jnp.exp(s - m)
        inv = 1.0 / jnp.sum(e, -1, keepdims=True)
        o = jnp.dot(e.astype(BF16), mv_ref[b, :, sl], preferred_element_type=F32)
        yield
        mem_out[b, :, sl] = (o * inv).astype(mem_out.dtype)

    def region(make_tasks):
        @pl.when(c >= 0)
        def _():
            _run_interleaved(make_tasks())

    for b in range(nb):
        region(lambda b=b: [ret_task(b, p) for p in range(RET_HEADS // 2)])
        region(lambda b=b: [swa_task(b, kvh) for kvh in range(SWA_KV_HEADS)])
        region(lambda b=b: [mem_task(b, h) for h in range(MEM_HEADS)])

    @pl.when(c == pl.num_programs(0) - 1)
    def _():
        s_out[...] = s_scr[...]


def _mix_prompt(rq, rk, rv, sq, sk, sv, mq, mk_bf, mv_bf, sinks, pair_tables):
    nb, seq, _ = rq.shape
    decay2, rowdec2, wend2, gl2 = pair_tables
    chunk = lambda w: pl.BlockSpec((nb, RET_CHUNK, w), lambda c: (0, c, 0))
    st_shape = (nb, RET_HEADS // 2, 2 * RET_DK, RET_DV)
    kv_scr = pltpu.VMEM((nb, WINDOW, SWA_KV_W), BF16)
    return pl.pallas_call(
        functools.partial(_mix_prompt_kernel, nb),
        grid=(seq // RET_CHUNK,),
        in_specs=[pl.BlockSpec(memory_space=pltpu.SMEM),
                  chunk(RET_QK_W), chunk(RET_QK_W), chunk(RET_V_W), chunk(SWA_Q_W), chunk(SWA_KV_W), chunk(SWA_KV_W),
                  chunk(MEM_W), _const_spec(mk_bf.shape), _const_spec(mv_bf.shape),
                  _const_spec(decay2.shape), _const_spec(rowdec2.shape), _const_spec(wend2.shape),
                  _const_spec(gl2.shape)],
        out_specs=[chunk(RET_V_W), chunk(SWA_Q_W), chunk(MEM_W), _const_spec(st_shape)],
        out_shape=[jax.ShapeDtypeStruct((nb, seq, RET_V_W), F32), jax.ShapeDtypeStruct((nb, seq, SWA_Q_W), BF16),
                   jax.ShapeDtypeStruct((nb, seq, MEM_W), BF16), jax.ShapeDtypeStruct(st_shape, F32)],
        scratch_shapes=[pltpu.VMEM(st_shape, F32), kv_scr, kv_scr, kv_scr, kv_scr],
        compiler_params=_params(1),
        name="mix_prompt",
    )(sinks, rq, rk, rv, sq, sk, sv, mq, mk_bf, mv_bf, decay2, rowdec2, wend2, gl2)


def _mem_kv_kernel(x_ref, w_ref, k_ref, v_ref, kb_ref, vb_ref):
    kv = jnp.dot(x_ref[...].astype(BF16), w_ref[0], preferred_element_type=F32)
    k_ref[...] = kv[:, :MEM_W]
    v_ref[...] = kv[:, MEM_W:]
    kb_ref[...] = kv[:, :MEM_W].astype(BF16)
    vb_ref[...] = kv[:, MEM_W:].astype(BF16)


def _mem_kv(mem2d, w_bf, layer, tm):
    m, k = mem2d.shape
    out = pl.BlockSpec((tm, MEM_W), lambda i: (i, 0))
    return pl.pallas_call(
        _mem_kv_kernel,
        grid=(m // tm,),
        in_specs=[pl.BlockSpec((tm, k), lambda i: (i, 0)), _layer_spec(w_bf.shape, layer)],
        out_specs=[out, out, out, out],
        out_shape=[jax.ShapeDtypeStruct((m, MEM_W), F32), jax.ShapeDtypeStruct((m, MEM_W), F32),
                   jax.ShapeDtypeStruct((m, MEM_W), BF16), jax.ShapeDtypeStruct((m, MEM_W), BF16)],
        compiler_params=_params(1),
        name="mem_kv",
    )(mem2d, w_bf)


def _layer_norm(x, g, b):
    mu = jnp.mean(x, -1, keepdims=True)
    d = x - mu
    var = jnp.mean(d * d, -1, keepdims=True)
    return d * lax.rsqrt(var + LN_EPS) * g + b


def _finish_kernel(x_ref, ret_ref, rg_ref, swa_ref, mem_ref, gr_ref, gs_ref, gm_ref, gng_ref,
                   wr_ref, ws_ref, wm_ref, wo_ref, l1g_ref, l1b_ref, wu_ref, wd_ref, l2g_ref, l2b_ref, o_ref):
    rg = rg_ref[...]
    gn = jnp.concatenate([_group_norm(ret_ref[:, h * RET_DV:(h + 1) * RET_DV], gng_ref[0, h:h + 1, :])
                          for h in range(RET_HEADS)], -1)
    ret_in = (rg * jax.nn.sigmoid(rg) * gn).astype(BF16)
    ret_b = jnp.dot(ret_in, wr_ref[0], preferred_element_type=F32)
    swa_b = jnp.dot(swa_ref[...], ws_ref[0], preferred_element_type=F32)
    mem_b = jnp.dot(mem_ref[...], wm_ref[0], preferred_element_type=F32)
    merged = (jax.nn.sigmoid(gr_ref[...]) * ret_b + jax.nn.sigmoid(gs_ref[...]) * swa_b
              + jax.nn.sigmoid(gm_ref[...]) * mem_b)
    y = jnp.dot(merged.astype(BF16), wo_ref[0], preferred_element_type=F32)
    x1 = _layer_norm(ALPHA * x_ref[...] + y, l1g_ref[0], l1b_ref[0])
    x1b = x1.astype(BF16)
    ff_chunk = D_FF // 4
    acc = jnp.zeros_like(x1)
    for c in range(D_FF // ff_chunk):
        h = jnp.dot(x1b, wu_ref[0, :, c * ff_chunk:(c + 1) * ff_chunk], preferred_element_type=F32)
        h = jnp.square(jnp.maximum(h, 0.0)).astype(BF16)
        acc = acc + jnp.dot(h, wd_ref[0, c * ff_chunk:(c + 1) * ff_chunk, :], preferred_element_type=F32)
    o_ref[...] = _layer_norm(ALPHA * x1 + acc, l2g_ref[0], l2b_ref[0])


def _finish(x2d, gn, rg, swa_o, mem_o, g_r, g_s, g_m, lw, layer, tm):
    m = x2d.shape[0]
    row = lambda w: pl.BlockSpec((tm, w), lambda i: (i, 0))
    lspec = lambda a: _layer_spec(a.shape, layer)
    return pl.pallas_call(
        _finish_kernel,
        grid=(m // tm,),
        in_specs=[row(D_MODEL), row(RET_V_W), row(RET_V_W), row(SWA_Q_W), row(MEM_W),
                  row(D_MODEL), row(D_MODEL), row(D_MODEL)] + [lspec(a) for a in lw],
        out_specs=row(D_MODEL),
        out_shape=jax.ShapeDtypeStruct((m, D_MODEL), F32),
        compiler_params=_params(1),
        name="finish",
    )(x2d, gn, rg, swa_o, mem_o, g_r, g_s, g_m, *lw)


def _rope_tables(pos):
    half = SWA_HD // 2
    inv = jnp.power(ROPE_THETA, -jnp.arange(half, dtype=F32) / half)
    ang = pos.astype(F32)[:, None] * inv[None, :]
    c, s = jnp.cos(ang), jnp.sin(ang)
    return jnp.concatenate([c, c, c, c], -1), jnp.concatenate([-s, s, -s, s], -1)


def kernel(x_prompt, x_sample, state_ret, cache_swa_k, cache_swa_v, cache_mem_k, cache_mem_v, mem_prompt,
           w_in, w_br_ret, w_br_swa, w_br_mem, w_out, w_mem_kv, attn_sinks, ret_gn_g,
           ln1_g, ln1_b, w_up, w_down, ln2_g, ln2_b):
    batch, seq, _ = x_prompt.shape
    dec_b, dec_t, _ = x_sample.shape
    tm_p, tm_s = 512, 256
    tm_fin = 256
    ret_seqs = RET_CHUNK // dec_t

    cos_p, sin_p = _rope_tables(jnp.arange(seq, dtype=jnp.int32))
    cos_s, sin_s = _rope_tables(PAST_LEN + jnp.arange(dec_t, dtype=jnp.int32))
    cos_s, sin_s = jnp.tile(cos_s, (tm_s // dec_t, 1)), jnp.tile(sin_s, (tm_s // dec_t, 1))
    lg = jnp.log1p(-jnp.exp2(-5.0 - jnp.arange(RET_HEADS, dtype=F32)))
    tab_p = _ret_tables(lg, RET_CHUNK, RET_CHUNK)
    tab_p2 = _pair_tables(tab_p)
    tab_s2 = _pair_tables(_ret_tables(lg, RET_CHUNK, dec_t))

    xp = x_prompt.reshape(batch * seq, D_MODEL)
    xs = x_sample.reshape(dec_b * dec_t, D_MODEL)
    mem2d = mem_prompt.reshape(batch * N_MEM, D_MODEL)
    cache_kt = jnp.transpose(cache_swa_k, (0, 1, 3, 4, 2))
    cache_vt = jnp.transpose(cache_swa_v, (0, 1, 3, 4, 2))
    cache_mk = cache_mem_k.reshape(DEPTH, dec_b, N_MEM * MEM_HEADS, MEM_HD)
    cache_mv = cache_mem_v.reshape(DEPTH, dec_b, N_MEM * MEM_HEADS, MEM_HD)

    w_in_bf = w_in.astype(BF16)
    w_mem_kv_bf = w_mem_kv.astype(BF16)
    ln_row = lambda a: a.reshape(DEPTH, 1, D_MODEL)
    lw = (ret_gn_g, w_br_ret.astype(BF16), w_br_swa.astype(BF16), w_br_mem.astype(BF16), w_out.astype(BF16),
          ln_row(ln1_g), ln_row(ln1_b), w_up.astype(BF16), w_down.astype(BF16), ln_row(ln2_g), ln_row(ln2_b))

    ret_p, swk_p, swv_p, mk_p, mv_p = [], [], [], [], []
    ret_s, swa_s = None, None
    for l in range(DEPTH):
        sinks = attn_sinks[l]

        rq, rk, rv, rg, sq, sk, sv, mq, g_r, g_s, g_m = _inproj(xp, w_in_bf, l, cos_p, sin_p, tm_p, BF16)
        mk, mv, mk_bf, mv_bf = _mem_kv(mem2d, w_mem_kv_bf, l, 256)
        by_seq = lambda a: a.reshape(batch, seq, a.shape[-1])
        by_mem = lambda a: a.reshape(batch, N_MEM, MEM_W)
        gn, swa_o, mem_o, s_p = _mix_prompt(by_seq(rq), by_seq(rk), by_seq(rv), by_seq(sq), by_seq(sk), by_seq(sv),
                                            by_seq(mq), by_mem(mk_bf), by_mem(mv_bf), sinks, tab_p2)
        flat = lambda a: a.reshape(batch * seq, a.shape[-1])
        xp = _finish(xp, flat(gn), rg, flat(swa_o), flat(mem_o), g_r, g_s, g_m, lw, l, tm_fin)
        ret_p.append(s_p.reshape(batch, RET_HEADS, RET_DK, RET_DV))
        swk_p.append(sk.reshape(batch, seq, SWA_KV_W)[:, -WINDOW:].reshape(batch, WINDOW, SWA_KV_HEADS, SWA_HD))
        swv_p.append(sv.reshape(batch, seq, SWA_KV_W)[:, -WINDOW:].reshape(batch, WINDOW, SWA_KV_HEADS, SWA_HD))
        mk_p.append(mk.reshape(batch, N_MEM, MEM_HEADS, MEM_HD))
        mv_p.append(mv.reshape(batch, N_MEM, MEM_HEADS, MEM_HD))

        rq, rk, rv, rg, sq, sk, sv, mq, g_r, g_s, g_m = _inproj(xs, w_in_bf, l, cos_s, sin_s, tm_s, F32)
        gn, ret_s = _ret_sample(rq, rk, rv, state_ret, ret_s, l, tab_s2, dec_t, ret_seqs)
        swa_o, kto, vto = _swa_sample(sq, sk, sv, cache_kt, cache_vt, swa_s, sinks, l, dec_t, WINDOW // dec_t)
        swa_s = (kto, vto)
        mem_o = _mem_sample(mq, cache_mk, cache_mv, l, dec_t, 8)
        xs = _finish(xs, gn, rg, swa_o, mem_o, g_r, g_s, g_m, lw, l, tm_s)

    from_t = lambda a: jnp.transpose(a, (0, 1, 4, 2, 3))
    return (xp.reshape(batch, seq, D_MODEL), xs.reshape(dec_b, dec_t, D_MODEL),
            jnp.stack(ret_p), jnp.stack(swk_p), jnp.stack(swv_p), jnp.stack(mk_p), jnp.stack(mv_p),
            ret_s, from_t(swa_s[0]), from_t(swa_s[1]))
```

```python
import functools

import jax
import jax.numpy as jnp
from jax import lax
from jax.experimental import pallas as pl
from jax.experimental.pallas import tpu as pltpu

F32 = jnp.float32
BF16 = jnp.bfloat16

D_MODEL = 1024
DEPTH = 2
PAST_LEN = 16384
RET_HEADS = 8
RET_DK = 64
RET_DV = 128
RET_CHUNK = 128
SWA_HEADS = 8
SWA_KV_HEADS = 2
SWA_GROUP = SWA_HEADS // SWA_KV_HEADS
SWA_HD = 64
WINDOW = 128
MEM_HEADS = 4
MEM_HD = 128
N_MEM = 256
D_FF = 4 * D_MODEL
ROPE_THETA = 10000.0
LN_EPS = 1e-5
GN_EPS = 1e-5
ALPHA = (2 * DEPTH) ** 0.25

RET_QK_W = RET_HEADS * RET_DK
RET_V_W = RET_HEADS * RET_DV
SWA_Q_W = SWA_HEADS * SWA_HD
SWA_KV_W = SWA_KV_HEADS * SWA_HD
MEM_W = MEM_HEADS * MEM_HD
OFF_RQ = 0
OFF_RK = OFF_RQ + RET_QK_W
OFF_RV = OFF_RK + RET_QK_W
OFF_RG = OFF_RV + RET_V_W
OFF_SQ = OFF_RG + RET_V_W
OFF_SK = OFF_SQ + SWA_Q_W
OFF_SV = OFF_SK + SWA_KV_W
OFF_MQ = OFF_SV + SWA_KV_W
OFF_GR = OFF_MQ + MEM_W
OFF_GS = OFF_GR + D_MODEL
OFF_GM = OFF_GS + D_MODEL
IN_W = OFF_GM + D_MODEL

LANES = 128
V7X_VMEM_LIMIT = 62 * 1024 * 1024


def _const_spec(shape):
    nd = len(shape)
    return pl.BlockSpec(shape, lambda *_: (0,) * nd, pipeline_mode=pl.Buffered(1))


def _layer_spec(shape, layer):
    nd = len(shape)
    return pl.BlockSpec((1,) + tuple(shape[1:]), lambda *_: (layer,) + (0,) * (nd - 1), pipeline_mode=pl.Buffered(1))


def _params(n_grid):
    return pltpu.CompilerParams(dimension_semantics=("arbitrary",) * n_grid, vmem_limit_bytes=V7X_VMEM_LIMIT)


def _inproj_kernel(x_ref, w_ref, cos_ref, sin_ref,
                   rq_ref, rk_ref, rv_ref, rg_ref, sq_ref, sk_ref, sv_ref, mq_ref, gr_ref, gs_ref, gm_ref):
    xb = x_ref[...].astype(BF16)
    cos = cos_ref[...]
    sin = sin_ref[...]
    lane = lax.broadcasted_iota(jnp.int32, cos.shape, 1)
    first_half = (lane & (SWA_HD // 2)) == 0

    def proj(off, width):
        return jnp.dot(xb, w_ref[0, :, off:off + width], preferred_element_type=F32)

    def rope_store(off, width, out_ref, scale):
        y = proj(off, width)
        for j in range(width // LANES):
            yj = y[:, j * LANES:(j + 1) * LANES]
            sw = jnp.where(first_half, pltpu.roll(yj, LANES - SWA_HD // 2, 1), pltpu.roll(yj, SWA_HD // 2, 1))
            r = yj * cos + sw * sin
            if scale != 1.0:
                r = r * scale
            out_ref[:, j * LANES:(j + 1) * LANES] = r.astype(out_ref.dtype)

    def plain_store(off, width, out_ref):
        out_ref[...] = proj(off, width).astype(out_ref.dtype)

    rope_store(OFF_RQ, RET_QK_W, rq_ref, 1.0)
    rope_store(OFF_RK, RET_QK_W, rk_ref, RET_DK ** -0.5)
    plain_store(OFF_RV, RET_V_W, rv_ref)
    plain_store(OFF_RG, RET_V_W, rg_ref)
    rope_store(OFF_SQ, SWA_Q_W, sq_ref, SWA_HD ** -0.5)
    rope_store(OFF_SK, SWA_KV_W, sk_ref, 1.0)
    plain_store(OFF_SV, SWA_KV_W, sv_ref)
    plain_store(OFF_MQ, MEM_W, mq_ref)
    plain_store(OFF_GR, D_MODEL, gr_ref)
    plain_store(OFF_GS, D_MODEL, gs_ref)
    plain_store(OFF_GM, D_MODEL, gm_ref)


def _inproj(x2d, w_bf, layer, cos_tab, sin_tab, tm, qkv_dtype):
    m = x2d.shape[0]
    n_tab = cos_tab.shape[0] // tm
    row = lambda w: pl.BlockSpec((tm, w), lambda i: (i, 0))
    tab = pl.BlockSpec((tm, LANES), lambda i: (i % n_tab, 0))
    widths_dtypes = [(RET_QK_W, qkv_dtype), (RET_QK_W, qkv_dtype), (RET_V_W, qkv_dtype), (RET_V_W, F32),
                     (SWA_Q_W, qkv_dtype), (SWA_KV_W, F32), (SWA_KV_W, F32), (MEM_W, qkv_dtype),
                     (D_MODEL, F32), (D_MODEL, F32), (D_MODEL, F32)]
    return pl.pallas_call(
        _inproj_kernel,
        grid=(m // tm,),
        in_specs=[row(D_MODEL), _layer_spec(w_bf.shape, layer), tab, tab],
        out_specs=[row(w) for w, _ in widths_dtypes],
        out_shape=[jax.ShapeDtypeStruct((m, w), dt) for w, dt in widths_dtypes],
        compiler_params=_params(1),
        name="inproj",
    )(x2d, w_bf, cos_tab, sin_tab)


def _group_norm(o, g_row):
    mu = jnp.mean(o, -1, keepdims=True)
    d = o - mu
    var = jnp.mean(d * d, -1, keepdims=True)
    return d * lax.rsqrt(var + GN_EPS) * g_row


def _ret_tables(lg, n_rows, period):
    r = jnp.arange(n_rows)
    t = (r % period).astype(F32)
    same = (r[:, None] // period) == (r[None, :] // period)
    diff = t[:, None] - t[None, :]
    decay = jnp.where((diff >= 0) & same, jnp.exp(lg[:, None, None] * jnp.maximum(diff, 0.0)), 0.0)
    rowdec = jnp.exp(lg[:, None] * (t[None, :] + 1.0))
    wend = jnp.exp(lg[:, None] * (period - 1.0 - t[None, :]))
    gl = jnp.exp(lg * period)
    rowdec = jnp.broadcast_to(rowdec[:, :, None], (RET_HEADS, n_rows, RET_DV))
    wend = jnp.broadcast_to(wend[:, :, None], (RET_HEADS, n_rows, RET_DK))
    gl = jnp.broadcast_to(gl[:, None, None], (RET_HEADS, 1, RET_DV))
    return decay, rowdec, wend, gl


def _ret_sample_kernel(n_seq, t_len, has_prev, q_ref, k_ref, v_ref, s_ref, decay_ref, rowdec_ref, wend_ref, gl_ref,
                       *rest):
    o_ref, s_out_ref = rest[-2:]
    del has_prev
    rows = n_seq * t_len
    pair_dk, pair_dv = 2 * RET_DK, 2 * RET_DV
    lane_lo = lax.broadcasted_iota(jnp.int32, (rows, pair_dk), 1) < RET_DK
    row_lo = lax.broadcasted_iota(jnp.int32, (pair_dk, RET_DV), 0) < RET_DK
    for p in range(RET_HEADS // 2):
        qk = slice(p * pair_dk, (p + 1) * pair_dk)
        vv = slice(p * pair_dv, (p + 1) * pair_dv)
        q2f, k2f, v2f = q_ref[:, qk], k_ref[:, qk], v_ref[:, vv]
        q2, k2, v2 = q2f.astype(BF16), k2f.astype(BF16), v2f.astype(BF16)
        zk = jnp.zeros_like(k2)
        k_rows = jnp.concatenate([jnp.where(lane_lo, k2, zk), jnp.where(lane_lo, zk, k2)], 0)
        sc2 = lax.dot_general(q2, k_rows, (((1,), (1,)), ((), ())), preferred_element_type=F32) * decay_ref[p]
        zv = jnp.zeros((rows, RET_DV), BF16)
        v_bd = jnp.concatenate([jnp.concatenate([v2[:, :RET_DV], zv], 1),
                                jnp.concatenate([zv, v2[:, RET_DV:]], 1)], 0)
        kw2f = k2f * wend_ref[p]
        o_state = []
        for b in range(n_seq):
            r = slice(b * t_len, (b + 1) * t_len)
            s2 = s_ref[0, b, 2 * p:2 * p + 2].reshape(pair_dk, RET_DV)
            s2b = s2.astype(BF16)
            zs = jnp.zeros_like(s2b)
            s_bd = jnp.concatenate([jnp.where(row_lo, s2b, zs), jnp.where(row_lo, zs, s2b)], 1)
            o_state.append(jnp.dot(q2f[r].astype(BF16), s_bd, preferred_element_type=F32))
            upd = lax.dot_general(kw2f[r].astype(BF16), v2f[r].astype(BF16), (((0,), (0,)), ((), ())),
                                  preferred_element_type=F32)
            s_new = (gl_ref[p] * s2 + jnp.where(row_lo, upd[:, :RET_DV], upd[:, RET_DV:])).reshape(2, RET_DK, RET_DV)
            for d in range(s_out_ref.shape[0]):
                s_out_ref[d, b, 2 * p:2 * p + 2] = s_new
        o2 = jnp.dot(sc2.astype(BF16), v_bd, preferred_element_type=F32)
        o_ref[:, vv] = o2 + jnp.concatenate(o_state, 0) * rowdec_ref[p]


def _stacked_out_specs(shape, layer, n_seq):
    tail = tuple(shape[2:])
    zeros = (0,) * len(tail)
    if layer == 0:
        return pl.BlockSpec((shape[0], n_seq) + tail, lambda i: (0, i) + zeros)
    return pl.BlockSpec((1, n_seq) + tail, lambda i: (layer, i) + zeros)


def _ret_sample(rq, rk, rv, state, prev_out, layer, pair_tables, t_len, n_seq):
    m = rq.shape[0]
    rows = n_seq * t_len
    decay2, rowdec2, wend2, gl2 = pair_tables
    row = lambda w: pl.BlockSpec((rows, w), lambda i: (i, 0))
    st_in = pl.BlockSpec((1, n_seq, RET_HEADS, RET_DK, RET_DV), lambda i: (layer, i, 0, 0, 0))
    in_specs = [row(RET_QK_W), row(RET_QK_W), row(RET_V_W), st_in,
                _const_spec(decay2.shape), _const_spec(rowdec2.shape), _const_spec(wend2.shape),
                _const_spec(gl2.shape)]
    args = [rq, rk, rv, state, decay2, rowdec2, wend2, gl2]
    aliases = {}
    if prev_out is not None:
        in_specs.append(pl.BlockSpec(memory_space=pl.ANY))
        args.append(prev_out)
        aliases = {len(args) - 1: 1}
    return pl.pallas_call(
        functools.partial(_ret_sample_kernel, n_seq, t_len, prev_out is not None),
        grid=(m // rows,),
        in_specs=in_specs,
        out_specs=[row(RET_V_W), _stacked_out_specs(state.shape, layer, n_seq)],
        out_shape=[jax.ShapeDtypeStruct((m, RET_V_W), F32), jax.ShapeDtypeStruct(state.shape, F32)],
        input_output_aliases=aliases,
        compiler_params=_params(1),
        name="ret_sample",
    )(*args)


def _swa_sample_kernel(n_seq, t_len, sinks_ref, q_ref, kn_ref, vn_ref, kt_ref, vt_ref, *rest):
    o_ref, kto_ref, vto_ref = rest[-3:]
    grp_rows = SWA_GROUP * t_len
    n_all = n_seq * grp_rows
    q = q_ref[...]
    kn = kn_ref[...]
    vn = vn_ref[...]
    kn_t = kn.T
    vn_t = vn.T
    r = lax.broadcasted_iota(jnp.int32, (n_all, 1), 0)
    t_q = r % t_len
    g_row = (r // t_len) % SWA_GROUP
    b_row = r // grp_rows
    c = lax.broadcasted_iota(jnp.int32, (1, WINDOW), 1)
    valid_cache = c > t_q
    valid_new = ((c // t_len) == b_row) & ((c % t_len) <= t_q)
    lane = lax.broadcasted_iota(jnp.int32, (SWA_HD, WINDOW), 1)
    is_new_lane = lane >= WINDOW - t_len
    head_dims = [slice(kvh * SWA_HD, (kvh + 1) * SWA_HD) for kvh in range(SWA_KV_HEADS)]
    scores = []
    for kvh, hd in enumerate(head_dims):
        qg = [q[:, (kvh * SWA_GROUP + g) * SWA_HD:(kvh * SWA_GROUP + g + 1) * SWA_HD] for g in range(SWA_GROUP)]
        q_all = jnp.concatenate([qg[g][b * t_len:(b + 1) * t_len] for b in range(n_seq) for g in range(SWA_GROUP)],
                                0).astype(BF16)
        s_new = jnp.dot(q_all, kn_t[hd].astype(BF16), preferred_element_type=F32)
        s_cache = jnp.concatenate(
            [jnp.dot(q_all[b * grp_rows:(b + 1) * grp_rows], kt_ref[0, b, kvh].astype(BF16),
                     preferred_element_type=F32) for b in range(n_seq)], 0)
        scores.append((s_new, s_cache))
    probs = []
    for kvh, (s_new, s_cache) in enumerate(scores):
        s_new = jnp.where(valid_new, s_new, -jnp.inf)
        s_cache = jnp.where(valid_cache, s_cache, -jnp.inf)
        sink = jnp.full((n_all, 1), sinks_ref[kvh * SWA_GROUP], F32)
        for g in range(1, SWA_GROUP):
            sink = jnp.where(g_row == g, sinks_ref[kvh * SWA_GROUP + g], sink)
        m = jnp.maximum(jnp.maximum(jnp.max(s_new, -1, keepdims=True), jnp.max(s_cache, -1, keepdims=True)), sink)
        e_new = jnp.exp(s_new - m)
        e_cache = jnp.exp(s_cache - m)
        den = jnp.sum(e_new, -1, keepdims=True) + jnp.sum(e_cache, -1, keepdims=True) + jnp.exp(sink - m)
        probs.append(((e_new / den).astype(BF16), (e_cache / den).astype(BF16)))
    pieces = []
    for kvh, hd in enumerate(head_dims):
        p_new, p_cache = probs[kvh]
        o = jnp.dot(p_new, vn[:, hd].astype(BF16), preferred_element_type=F32)
        o = o + jnp.concatenate(
            [lax.dot_general(p_cache[b * grp_rows:(b + 1) * grp_rows], vt_ref[0, b, kvh].astype(BF16),
                             (((1,), (1,)), ((), ())), preferred_element_type=F32) for b in range(n_seq)], 0)
        for g in range(SWA_GROUP):
            pieces.append(jnp.concatenate(
                [o[b * grp_rows + g * t_len:b * grp_rows + (g + 1) * t_len] for b in range(n_seq)], 0))
    for kvh, hd in enumerate(head_dims):
        for b in range(n_seq):
            shift_new = (WINDOW - t_len - b * t_len) % WINDOW
            k_slid = jnp.where(is_new_lane, pltpu.roll(kn_t[hd], shift_new, 1),
                               pltpu.roll(kt_ref[0, b, kvh], WINDOW - t_len, 1))
            v_slid = jnp.where(is_new_lane, pltpu.roll(vn_t[hd], shift_new, 1),
                               pltpu.roll(vt_ref[0, b, kvh], WINDOW - t_len, 1))
            for d in range(kto_ref.shape[0]):
                kto_ref[d, b, kvh] = k_slid
                vto_ref[d, b, kvh] = v_slid
    o_ref[...] = jnp.concatenate(pieces, -1).astype(o_ref.dtype)


def _swa_sample(sq, sk, sv, cache_kt, cache_vt, prev_out, sinks, layer, t_len, n_seq):
    m = sq.shape[0]
    rows = n_seq * t_len
    assert rows == WINDOW and cache_kt.shape[-1] == WINDOW
    row = lambda w: pl.BlockSpec((rows, w), lambda i: (i, 0))
    cin = pl.BlockSpec((1, n_seq, SWA_KV_HEADS, SWA_HD, WINDOW), lambda i: (layer, i, 0, 0, 0))
    cout = _stacked_out_specs(cache_kt.shape, layer, n_seq)
    cshape = jax.ShapeDtypeStruct(cache_kt.shape, F32)
    in_specs = [pl.BlockSpec(memory_space=pltpu.SMEM), row(SWA_Q_W), row(SWA_KV_W), row(SWA_KV_W), cin, cin]
    args = [sinks, sq, sk, sv, cache_kt, cache_vt]
    aliases = {}
    if prev_out is not None:
        in_specs += [pl.BlockSpec(memory_space=pl.ANY)] * 2
        args += list(prev_out)
        aliases = {len(args) - 2: 1, len(args) - 1: 2}
    return pl.pallas_call(
        functools.partial(_swa_sample_kernel, n_seq, t_len),
        grid=(m // rows,),
        in_specs=in_specs,
        out_specs=[row(SWA_Q_W), cout, cout],
        out_shape=[jax.ShapeDtypeStruct((m, SWA_Q_W), BF16), cshape, cshape],
        input_output_aliases=aliases,
        compiler_params=_params(1),
        name="swa_sample",
    )(*args)


def _mem_sample_kernel(n_seq, t_len, q_ref, k_ref, v_ref, o_ref):
    rows = MEM_HEADS * t_len
    head_of_row = lax.broadcasted_iota(jnp.int32, (rows, 1), 0) // t_len
    head_of_col = lax.broadcasted_iota(jnp.int32, (1, N_MEM * MEM_HEADS), 1) % MEM_HEADS
    valid = head_of_row == head_of_col
    scores = []
    for b in range(n_seq):
        qb = q_ref[b * t_len:(b + 1) * t_len, :]
        q_all = jnp.concatenate([qb[:, h * MEM_HD:(h + 1) * MEM_HD] for h in range(MEM_HEADS)], 0).astype(BF16)
        scores.append(lax.dot_general(q_all, k_ref[0, b].astype(BF16), (((1,), (1,)), ((), ())),
                                      preferred_element_type=F32) * (MEM_HD ** -0.5))
    probs = []
    for b in range(n_seq):
        s = jnp.where(valid, scores[b], -jnp.inf)
        m = jnp.max(s, -1, keepdims=True)
        e = jnp.exp(s - m)
        probs.append((e.astype(BF16), 1.0 / jnp.sum(e, -1, keepdims=True)))
    outs = []
    for b in range(n_seq):
        e, inv = probs[b]
        o = jnp.dot(e, v_ref[0, b].astype(BF16), preferred_element_type=F32) * inv
        outs.append(jnp.concatenate([o[h * t_len:(h + 1) * t_len] for h in range(MEM_HEADS)], -1))
    o_ref[...] = jnp.concatenate(outs, 0).astype(o_ref.dtype)


def _mem_sample(mq, cache_k, cache_v, layer, t_len, n_seq):
    m = mq.shape[0]
    rows = n_seq * t_len
    row = pl.BlockSpec((rows, MEM_W), lambda i: (i, 0))
    kv = pl.BlockSpec((1, n_seq, N_MEM * MEM_HEADS, MEM_HD), lambda i: (layer, i, 0, 0))
    return pl.pallas_call(
        functools.partial(_mem_sample_kernel, n_seq, t_len),
        grid=(m // rows,),
        in_specs=[row, kv, kv],
        out_specs=row,
        out_shape=jax.ShapeDtypeStruct((m, MEM_W), BF16),
        compiler_params=_params(1),
        name="mem_sample",
    )(mq, cache_k, cache_v)


def _pair_tables(tables):
    decay, rowdec, wend, gl = tables
    pair = lambda a: jnp.concatenate([a[0::2], a[1::2]], -1)
    gl_rows = jnp.concatenate([jnp.broadcast_to(gl[0::2], (RET_HEADS // 2, RET_DK, RET_DV)),
                               jnp.broadcast_to(gl[1::2], (RET_HEADS // 2, RET_DK, RET_DV))], 1)
    return pair(decay), pair(rowdec), pair(wend), gl_rows


def _run_staged(tasks):
    active = list(tasks)
    while active:
        for t in list(active):
            try:
                next(t)
            except StopIteration:
                active.remove(t)


def _mix_prompt_kernel(nb, sinks_ref, rq_ref, rk_ref, rv_ref, sq_ref, sk_ref, sv_ref, mq_ref, mk_ref, mv_ref,
                       decay_ref, rowdec_ref, wend_ref, gl_ref,
                       ret_out, swa_out, mem_out, s_out,
                       s_scr, kp_scr, kpr_scr, vp_scr, vpr_scr):
    c = pl.program_id(0)

    @pl.when(c == 0)
    def _():
        s_scr[...] = jnp.zeros_like(s_scr)
        for scr in (kp_scr, kpr_scr, vp_scr, vpr_scr):
            scr[...] = jnp.zeros_like(scr)

    pair_w = 2 * SWA_HD
    lane_lo = lax.broadcasted_iota(jnp.int32, (RET_CHUNK, pair_w), 1) < SWA_HD
    row_lo = lax.broadcasted_iota(jnp.int32, (2 * RET_DK, RET_DV), 0) < RET_DK
    lane_lo_kv = lax.broadcasted_iota(jnp.int32, (2 * WINDOW, pair_w), 1) < SWA_HD
    upper = (lax.broadcasted_iota(jnp.int32, (WINDOW, WINDOW), 1)
             > lax.broadcasted_iota(jnp.int32, (WINDOW, WINDOW), 0))
    prev_bias = jnp.where(c > 0, 0.0, -jnp.inf)


    def ret_task(b, p):
        qk = slice(p * 2 * RET_DK, (p + 1) * 2 * RET_DK)
        vv = slice(p * 2 * RET_DV, (p + 1) * 2 * RET_DV)
        q2, k2, v2, s2 = rq_ref[b, :, qk], rk_ref[b, :, qk], rv_ref[b, :, vv], s_scr[b, p]
        zk = jnp.zeros_like(k2)
        k_rows = jnp.concatenate([jnp.where(lane_lo, k2, zk), jnp.where(lane_lo, zk, k2)], 0)
        sc_raw = lax.dot_general(q2, k_rows, (((1,), (1,)), ((), ())), preferred_element_type=F32)
        s2b = s2.astype(BF16)
        zs = jnp.zeros_like(s2b)
        s_bd = jnp.concatenate([jnp.where(row_lo, s2b, zs), jnp.where(row_lo, zs, s2b)], 1)
        os_raw = jnp.dot(q2, s_bd, preferred_element_type=F32)
        kw2 = (k2.astype(F32) * wend_ref[p]).astype(BF16)
        upd = lax.dot_general(kw2, v2, (((0,), (0,)), ((), ())), preferred_element_type=F32)
        yield
        zv = jnp.zeros((RET_CHUNK, RET_DV), v2.dtype)
        v_bd = jnp.concatenate([jnp.concatenate([v2[:, :RET_DV], zv], 1),
                                jnp.concatenate([zv, v2[:, RET_DV:]], 1)], 0)
        o_raw = jnp.dot((sc_raw * decay_ref[p]).astype(BF16), v_bd, preferred_element_type=F32)
        s_scr[b, p] = gl_ref[p] * s2 + jnp.where(row_lo, upd[:, :RET_DV], upd[:, RET_DV:])
        yield
        ret_out[b, :, vv] = o_raw + os_raw * rowdec_ref[p]

    kv_ctx = {}

    def swa_prep(b):
        k_cur, v_cur = sk_ref[b], sv_ref[b]
        kb, kbr = k_cur.astype(BF16), pltpu.roll(k_cur, SWA_HD, 1).astype(BF16)
        vb, vbr = v_cur.astype(BF16), pltpu.roll(v_cur, SWA_HD, 1).astype(BF16)
        kv_ctx[b] = (jnp.concatenate([kp_scr[b], kb], 0), jnp.concatenate([kpr_scr[b], kbr], 0),
                     jnp.concatenate([vp_scr[b], vb], 0), jnp.concatenate([vpr_scr[b], vbr], 0))
        kp_scr[b], kpr_scr[b], vp_scr[b], vpr_scr[b] = kb, kbr, vb, vbr

    def swa_task(b, kvh):
        if b not in kv_ctx:
            swa_prep(b)
        kc, kcr, vc, vcr = kv_ctx[b]
        zkv = jnp.zeros_like(kc)
        k_lo, k_hi = (kc, kcr) if kvh == 0 else (kcr, kc)
        v_lo, v_hi = (vc, vcr) if kvh == 0 else (vcr, vc)
        k_rows = jnp.concatenate([jnp.where(lane_lo_kv, k_lo, zkv), jnp.where(lane_lo_kv, zkv, k_hi)], 0)
        v_rows = jnp.concatenate([jnp.where(lane_lo_kv, v_lo, zkv), jnp.where(lane_lo_kv, zkv, v_hi)], 0)
        n_pairs = SWA_GROUP // 2
        pairs = [kvh * n_pairs + jj for jj in range(n_pairs)]
        q4 = jnp.concatenate([sq_ref[b, :, pr * pair_w:(pr + 1) * pair_w] for pr in pairs], 0)
        s4 = lax.dot_general(q4, k_rows, (((1,), (1,)), ((), ())), preferred_element_type=F32)
        yield
        rows, inv = [], []
        for jj, pr in enumerate(pairs):
            ps, inv_u = [], []
            for u in range(2):
                blk = s4[jj * WINDOW:(jj + 1) * WINDOW, u * 2 * WINDOW:(u + 1) * 2 * WINDOW]
                s = jnp.where(upper, blk[:, :WINDOW] + prev_bias, blk[:, WINDOW:])
                sink = sinks_ref[2 * pr + u]
                m = jnp.maximum(jnp.max(s, -1, keepdims=True), sink)
                e = jnp.exp(s - m)
                den = jnp.sum(e, -1, keepdims=True) + jnp.exp(sink - m)
                ps += [jnp.where(upper, e, 0.0).astype(BF16), jnp.where(upper, 0.0, e).astype(BF16)]
                inv_u.append(1.0 / den)
            rows.append(jnp.concatenate(ps, 1))
            inv.append(jnp.where(lane_lo, inv_u[0], inv_u[1]))
        o4 = jnp.dot(jnp.concatenate(rows, 0), v_rows, preferred_element_type=F32)
        yield
        for jj, pr in enumerate(pairs):
            swa_out[b, :, pr * pair_w:(pr + 1) * pair_w] = (o4[jj * WINDOW:(jj + 1) * WINDOW] * inv[jj]).astype(
                swa_out.dtype)

    def mem_task(b, h):
        sl = slice(h * MEM_HD, (h + 1) * MEM_HD)
        s = lax.dot_general(mq_ref[b, :, sl], mk_ref[b, :, sl], (((1,), (1,)), ((), ())),
                            preferred_element_type=F32) * (MEM_HD ** -0.5)
        yield
        m = jnp.max(s, -1, keepdims=True)
        e = jnp.exp(s - m)
        inv = 1.0 / jnp.sum(e, -1, keepdims=True)
        o = jnp.dot(e.astype(BF16), mv_ref[b, :, sl], preferred_element_type=F32)
        yield
        mem_out[b, :, sl] = (o * inv).astype(mem_out.dtype)

    def region(make_tasks):
        @pl.when(c >= 0)
        def _():
            _run_staged(make_tasks())

    for b in range(nb):
        region(lambda b=b: [ret_task(b, 0), swa_task(b, 0), mem_task(b, 0), ret_task(b, 1), mem_task(b, 1),
                            ret_task(b, 2), swa_task(b, 1), mem_task(b, 2), ret_task(b, 3), mem_task(b, 3)])

    @pl.when(c == pl.num_programs(0) - 1)
    def _():
        s_out[...] = s_scr[...]


def _mix_prompt(rq, rk, rv, sq, sk, sv, mq, mk_bf, mv_bf, sinks, pair_tables):
    nb, seq, _ = rq.shape
    decay2, rowdec2, wend2, gl2 = pair_tables
    chunk = lambda w: pl.BlockSpec((nb, RET_CHUNK, w), lambda c: (0, c, 0))
    st_shape = (nb, RET_HEADS // 2, 2 * RET_DK, RET_DV)
    kv_scr = pltpu.VMEM((nb, WINDOW, SWA_KV_W), BF16)
    return pl.pallas_call(
        functools.partial(_mix_prompt_kernel, nb),
        grid=(seq // RET_CHUNK,),
        in_specs=[pl.BlockSpec(memory_space=pltpu.SMEM),
                  chunk(RET_QK_W), chunk(RET_QK_W), chunk(RET_V_W), chunk(SWA_Q_W), chunk(SWA_KV_W), chunk(SWA_KV_W),
                  chunk(MEM_W), _const_spec(mk_bf.shape), _const_spec(mv_bf.shape),
                  _const_spec(decay2.shape), _const_spec(rowdec2.shape), _const_spec(wend2.shape),
                  _const_spec(gl2.shape)],
        out_specs=[chunk(RET_V_W), chunk(SWA_Q_W), chunk(MEM_W), _const_spec(st_shape)],
        out_shape=[jax.ShapeDtypeStruct((nb, seq, RET_V_W), F32), jax.ShapeDtypeStruct((nb, seq, SWA_Q_W), BF16),
                   jax.ShapeDtypeStruct((nb, seq, MEM_W), BF16), jax.ShapeDtypeStruct(st_shape, F32)],
        scratch_shapes=[pltpu.VMEM(st_shape, F32), kv_scr, kv_scr, kv_scr, kv_scr],
        compiler_params=_params(1),
        name="mix_prompt",
    )(sinks, rq, rk, rv, sq, sk, sv, mq, mk_bf, mv_bf, decay2, rowdec2, wend2, gl2)


def _mem_kv_kernel(x_ref, w_ref, k_ref, v_ref, kb_ref, vb_ref):
    kv = jnp.dot(x_ref[...].astype(BF16), w_ref[0], preferred_element_type=F32)
    k_ref[...] = kv[:, :MEM_W]
    v_ref[...] = kv[:, MEM_W:]
    kb_ref[...] = kv[:, :MEM_W].astype(BF16)
    vb_ref[...] = kv[:, MEM_W:].astype(BF16)


def _mem_kv(mem2d, w_bf, layer, tm):
    m, k = mem2d.shape
    out = pl.BlockSpec((tm, MEM_W), lambda i: (i, 0))
    return pl.pallas_call(
        _mem_kv_kernel,
        grid=(m // tm,),
        in_specs=[pl.BlockSpec((tm, k), lambda i: (i, 0)), _layer_spec(w_bf.shape, layer)],
        out_specs=[out, out, out, out],
        out_shape=[jax.ShapeDtypeStruct((m, MEM_W), F32), jax.ShapeDtypeStruct((m, MEM_W), F32),
                   jax.ShapeDtypeStruct((m, MEM_W), BF16), jax.ShapeDtypeStruct((m, MEM_W), BF16)],
        compiler_params=_params(1),
        name="mem_kv",
    )(mem2d, w_bf)


def _layer_norm(x, g, b):
    mu = jnp.mean(x, -1, keepdims=True)
    d = x - mu
    var = jnp.mean(d * d, -1, keepdims=True)
    return d * lax.rsqrt(var + LN_EPS) * g + b


def _finish_kernel(x_ref, ret_ref, rg_ref, swa_ref, mem_ref, gr_ref, gs_ref, gm_ref, gng_ref,
                   wr_ref, ws_ref, wm_ref, wo_ref, l1g_ref, l1b_ref, wu_ref, wd_ref, l2g_ref, l2b_ref, o_ref):
    swa_b = jnp.dot(swa_ref[...], ws_ref[0], preferred_element_type=F32)
    mem_b = jnp.dot(mem_ref[...], wm_ref[0], preferred_element_type=F32)
    rg = rg_ref[...]
    gn = jnp.concatenate([_group_norm(ret_ref[:, h * RET_DV:(h + 1) * RET_DV], gng_ref[0, h:h + 1, :])
                          for h in range(RET_HEADS)], -1)
    ret_in = (rg * jax.nn.sigmoid(rg) * gn).astype(BF16)
    ret_b = jnp.dot(ret_in, wr_ref[0], preferred_element_type=F32)
    merged = (jax.nn.sigmoid(gr_ref[...]) * ret_b + jax.nn.sigmoid(gs_ref[...]) * swa_b
              + jax.nn.sigmoid(gm_ref[...]) * mem_b)
    y = jnp.dot(merged.astype(BF16), wo_ref[0], preferred_element_type=F32)
    x1 = _layer_norm(ALPHA * x_ref[...] + y, l1g_ref[0], l1b_ref[0])
    x1b = x1.astype(BF16)
    n_slabs = 4
    ff = D_FF // n_slabs
    up = lambda c: jnp.square(jnp.maximum(
        jnp.dot(x1b, wu_ref[0, :, c * ff:(c + 1) * ff], preferred_element_type=F32), 0.0)).astype(BF16)
    down = lambda c, h: jnp.dot(h, wd_ref[0, c * ff:(c + 1) * ff, :], preferred_element_type=F32)
    h_next = up(0)
    acc = None
    for c in range(n_slabs):
        h_cur, h_next = h_next, (up(c + 1) if c + 1 < n_slabs else None)
        d = down(c, h_cur)
        acc = d if acc is None else acc + d
    o_ref[...] = _layer_norm(ALPHA * x1 + acc, l2g_ref[0], l2b_ref[0])


def _finish(x2d, gn, rg, swa_o, mem_o, g_r, g_s, g_m, lw, layer, tm):
    m = x2d.shape[0]
    row = lambda w: pl.BlockSpec((tm, w), lambda i: (i, 0))
    lspec = lambda a: _layer_spec(a.shape, layer)
    return pl.pallas_call(
        _finish_kernel,
        grid=(m // tm,),
        in_specs=[row(D_MODEL), row(RET_V_W), row(RET_V_W), row(SWA_Q_W), row(MEM_W),
                  row(D_MODEL), row(D_MODEL), row(D_MODEL)] + [lspec(a) for a in lw],
        out_specs=row(D_MODEL),
        out_shape=jax.ShapeDtypeStruct((m, D_MODEL), F32),
        compiler_params=_params(1),
        name="finish",
    )(x2d, gn, rg, swa_o, mem_o, g_r, g_s, g_m, *lw)


def _rope_tables(pos):
    half = SWA_HD // 2
    inv = jnp.power(ROPE_THETA, -jnp.arange(half, dtype=F32) / half)
    ang = pos.astype(F32)[:, None] * inv[None, :]
    c, s = jnp.cos(ang), jnp.sin(ang)
    return jnp.concatenate([c, c, c, c], -1), jnp.concatenate([-s, s, -s, s], -1)


def kernel(x_prompt, x_sample, state_ret, cache_swa_k, cache_swa_v, cache_mem_k, cache_mem_v, mem_prompt,
           w_in, w_br_ret, w_br_swa, w_br_mem, w_out, w_mem_kv, attn_sinks, ret_gn_g,
           ln1_g, ln1_b, w_up, w_down, ln2_g, ln2_b):
    batch, seq, _ = x_prompt.shape
    dec_b, dec_t, _ = x_sample.shape
    tm_p, tm_s = 512, 256
    tm_fin = 512
    ret_seqs = RET_CHUNK // dec_t

    cos_p, sin_p = _rope_tables(jnp.arange(seq, dtype=jnp.int32))
    cos_s, sin_s = _rope_tables(PAST_LEN + jnp.arange(dec_t, dtype=jnp.int32))
    cos_s, sin_s = jnp.tile(cos_s, (tm_s // dec_t, 1)), jnp.tile(sin_s, (tm_s // dec_t, 1))
    lg = jnp.log1p(-jnp.exp2(-5.0 - jnp.arange(RET_HEADS, dtype=F32)))
    tab_p = _ret_tables(lg, RET_CHUNK, RET_CHUNK)
    tab_p2 = _pair_tables(tab_p)
    tab_s2 = _pair_tables(_ret_tables(lg, RET_CHUNK, dec_t))

    xp = x_prompt.reshape(batch * seq, D_MODEL)
    xs = x_sample.reshape(dec_b * dec_t, D_MODEL)
    mem2d = mem_prompt.reshape(batch * N_MEM, D_MODEL)
    cache_kt = jnp.transpose(cache_swa_k, (0, 1, 3, 4, 2))
    cache_vt = jnp.transpose(cache_swa_v, (0, 1, 3, 4, 2))
    cache_mk = cache_mem_k.reshape(DEPTH, dec_b, N_MEM * MEM_HEADS, MEM_HD)
    cache_mv = cache_mem_v.reshape(DEPTH, dec_b, N_MEM * MEM_HEADS, MEM_HD)

    w_in_bf = w_in.astype(BF16)
    w_mem_kv_bf = w_mem_kv.astype(BF16)
    ln_row = lambda a: a.reshape(DEPTH, 1, D_MODEL)
    lw = (ret_gn_g, w_br_ret.astype(BF16), w_br_swa.astype(BF16), w_br_mem.astype(BF16), w_out.astype(BF16),
          ln_row(ln1_g), ln_row(ln1_b), w_up.astype(BF16), w_down.astype(BF16), ln_row(ln2_g), ln_row(ln2_b))

    ret_p, swk_p, swv_p, mk_p, mv_p = [], [], [], [], []
    ret_s, swa_s = None, None
    for l in range(DEPTH):
        sinks = attn_sinks[l]

        rq, rk, rv, rg, sq, sk, sv, mq, g_r, g_s, g_m = _inproj(xp, w_in_bf, l, cos_p, sin_p, tm_p, BF16)
        mk, mv, mk_bf, mv_bf = _mem_kv(mem2d, w_mem_kv_bf, l, 256)
        by_seq = lambda a: a.reshape(batch, seq, a.shape[-1])
        by_mem = lambda a: a.reshape(batch, N_MEM, MEM_W)
        gn, swa_o, mem_o, s_p = _mix_prompt(by_seq(rq), by_seq(rk), by_seq(rv), by_seq(sq), by_seq(sk), by_seq(sv),
                                            by_seq(mq), by_mem(mk_bf), by_mem(mv_bf), sinks, tab_p2)
        flat = lambda a: a.reshape(batch * seq, a.shape[-1])
        xp = _finish(xp, flat(gn), rg, flat(swa_o), flat(mem_o), g_r, g_s, g_m, lw, l, tm_fin)
        ret_p.append(s_p.reshape(batch, RET_HEADS, RET_DK, RET_DV))
        swk_p.append(sk.reshape(batch, seq, SWA_KV_W)[:, -WINDOW:].reshape(batch, WINDOW, SWA_KV_HEADS, SWA_HD))
        swv_p.append(sv.reshape(batch, seq, SWA_KV_W)[:, -WINDOW:].reshape(batch, WINDOW, SWA_KV_HEADS, SWA_HD))
        mk_p.append(mk.reshape(batch, N_MEM, MEM_HEADS, MEM_HD))
        mv_p.append(mv.reshape(batch, N_MEM, MEM_HEADS, MEM_HD))

        rq, rk, rv, rg, sq, sk, sv, mq, g_r, g_s, g_m = _inproj(xs, w_in_bf, l, cos_s, sin_s, tm_s, F32)
        gn, ret_s = _ret_sample(rq, rk, rv, state_ret, ret_s, l, tab_s2, dec_t, ret_seqs)
        swa_o, kto, vto = _swa_sample(sq, sk, sv, cache_kt, cache_vt, swa_s, sinks, l, dec_t, WINDOW // dec_t)
        swa_s = (kto, vto)
        mem_o = _mem_sample(mq, cache_mk, cache_mv, l, dec_t, 8)
        xs = _finish(xs, gn, rg, swa_o, mem_o, g_r, g_s, g_m, lw, l, tm_s)

    from_t = lambda a: jnp.transpose(a, (0, 1, 4, 2, 3))
    return (xp.reshape(batch, seq, D_MODEL), xs.reshape(dec_b, dec_t, D_MODEL),
            jnp.stack(ret_p), jnp.stack(swk_p), jnp.stack(swv_p), jnp.stack(mk_p), jnp.stack(mv_p),
            ret_s, from_t(swa_s[0]), from_t(swa_s[1]))
```

```python
import functools

import jax
import jax.numpy as jnp
import numpy as np
from jax import lax
from jax.experimental import pallas as pl
from jax.experimental.pallas import tpu as pltpu

F32 = jnp.float32
BF16 = jnp.bfloat16

D_MODEL = 1024
DEPTH = 2
PAST_LEN = 16384
RET_HEADS = 8
RET_DK = 64
RET_DV = 128
RET_CHUNK = 128
SWA_HEADS = 8
SWA_KV_HEADS = 2
SWA_GROUP = SWA_HEADS // SWA_KV_HEADS
SWA_HD = 64
WINDOW = 128
MEM_HEADS = 4
MEM_HD = 128
N_MEM = 256
D_FF = 4 * D_MODEL
ROPE_THETA = 10000.0
LN_EPS = 1e-5
GN_EPS = 1e-5
ALPHA = (2 * DEPTH) ** 0.25

RET_QK_W = RET_HEADS * RET_DK
RET_V_W = RET_HEADS * RET_DV
SWA_Q_W = SWA_HEADS * SWA_HD
SWA_KV_W = SWA_KV_HEADS * SWA_HD
MEM_W = MEM_HEADS * MEM_HD
OFF_RQ = 0
OFF_RK = OFF_RQ + RET_QK_W
OFF_RV = OFF_RK + RET_QK_W
OFF_RG = OFF_RV + RET_V_W
OFF_SQ = OFF_RG + RET_V_W
OFF_SK = OFF_SQ + SWA_Q_W
OFF_SV = OFF_SK + SWA_KV_W
OFF_MQ = OFF_SV + SWA_KV_W
OFF_GR = OFF_MQ + MEM_W
OFF_GS = OFF_GR + D_MODEL
OFF_GM = OFF_GS + D_MODEL
IN_W = OFF_GM + D_MODEL

LANES = 128
V7X_VMEM_LIMIT = 62 * 1024 * 1024


def _const_spec(shape):
    nd = len(shape)
    return pl.BlockSpec(shape, lambda *_: (0,) * nd, pipeline_mode=pl.Buffered(1))


def _layer_spec(shape, layer):
    nd = len(shape)
    return pl.BlockSpec((1,) + tuple(shape[1:]), lambda *_: (layer,) + (0,) * (nd - 1), pipeline_mode=pl.Buffered(1))


def _params(n_grid):
    return pltpu.CompilerParams(dimension_semantics=("arbitrary",) * n_grid, vmem_limit_bytes=V7X_VMEM_LIMIT)


def _inproj_kernel(x_ref, w_ref, cos_ref, sin_ref,
                   rq_ref, rk_ref, rv_ref, rg_ref, sq_ref, sk_ref, sv_ref, mq_ref, gr_ref, gs_ref, gm_ref):
    xb = x_ref[...].astype(BF16)
    cos = cos_ref[...]
    sin = sin_ref[...]
    lane = lax.broadcasted_iota(jnp.int32, cos.shape, 1)
    first_half = (lane & (SWA_HD // 2)) == 0

    def proj(off, width):
        return jnp.dot(xb, w_ref[0, :, off:off + width], preferred_element_type=F32)

    def rope_store(off, width, out_ref, scale):
        y = proj(off, width)
        for j in range(width // LANES):
            yj = y[:, j * LANES:(j + 1) * LANES]
            sw = jnp.where(first_half, pltpu.roll(yj, LANES - SWA_HD // 2, 1), pltpu.roll(yj, SWA_HD // 2, 1))
            r = yj * cos + sw * sin
            if scale != 1.0:
                r = r * scale
            out_ref[:, j * LANES:(j + 1) * LANES] = r.astype(out_ref.dtype)

    def plain_store(off, width, out_ref):
        out_ref[...] = proj(off, width).astype(out_ref.dtype)

    rope_store(OFF_RQ, RET_QK_W, rq_ref, 1.0)
    rope_store(OFF_RK, RET_QK_W, rk_ref, RET_DK ** -0.5)
    plain_store(OFF_RV, RET_V_W, rv_ref)
    plain_store(OFF_RG, RET_V_W, rg_ref)
    rope_store(OFF_SQ, SWA_Q_W, sq_ref, SWA_HD ** -0.5)
    rope_store(OFF_SK, SWA_KV_W, sk_ref, 1.0)
    plain_store(OFF_SV, SWA_KV_W, sv_ref)
    plain_store(OFF_MQ, MEM_W, mq_ref)
    plain_store(OFF_GR, D_MODEL, gr_ref)
    plain_store(OFF_GS, D_MODEL, gs_ref)
    plain_store(OFF_GM, D_MODEL, gm_ref)


def _inproj(x2d, w_bf, layer, cos_tab, sin_tab, tm, qkv_dtype):
    m = x2d.shape[0]
    n_tab = cos_tab.shape[0] // tm
    row = lambda w: pl.BlockSpec((tm, w), lambda i: (i, 0))
    tab = pl.BlockSpec((tm, LANES), lambda i: (i % n_tab, 0))
    widths_dtypes = [(RET_QK_W, qkv_dtype), (RET_QK_W, qkv_dtype), (RET_V_W, qkv_dtype), (RET_V_W, F32),
                     (SWA_Q_W, qkv_dtype), (SWA_KV_W, F32), (SWA_KV_W, F32), (MEM_W, qkv_dtype),
                     (D_MODEL, F32), (D_MODEL, F32), (D_MODEL, F32)]
    return pl.pallas_call(
        _inproj_kernel,
        grid=(m // tm,),
        in_specs=[row(D_MODEL), _layer_spec(w_bf.shape, layer), tab, tab],
        out_specs=[row(w) for w, _ in widths_dtypes],
        out_shape=[jax.ShapeDtypeStruct((m, w), dt) for w, dt in widths_dtypes],
        compiler_params=_params(1),
        name="inproj",
    )(x2d, w_bf, cos_tab, sin_tab)


def _group_norm(o, g_row):
    mu = jnp.mean(o, -1, keepdims=True)
    d = o - mu
    var = jnp.mean(d * d, -1, keepdims=True)
    return d * lax.rsqrt(var + GN_EPS) * g_row


def _ret_tables(n_rows, period):
    lg = np.log1p(-np.exp2(-5.0 - np.arange(RET_HEADS, dtype=np.float64)))
    r = np.arange(n_rows)
    t = (r % period).astype(np.float64)
    same = (r[:, None] // period) == (r[None, :] // period)
    diff = t[:, None] - t[None, :]
    decay = np.where((diff >= 0) & same, np.exp(lg[:, None, None] * np.maximum(diff, 0.0)), 0.0)
    rowdec = np.exp(lg[:, None] * (t[None, :] + 1.0))
    wend = np.exp(lg[:, None] * (period - 1.0 - t[None, :]))
    gl = np.exp(lg * period)
    rowdec = np.broadcast_to(rowdec[:, :, None], (RET_HEADS, n_rows, RET_DV))
    wend = np.broadcast_to(wend[:, :, None], (RET_HEADS, n_rows, RET_DK))
    gl = np.broadcast_to(gl[:, None, None], (RET_HEADS, 1, RET_DV))
    return decay, rowdec, wend, gl


def _ret_sample_kernel(n_seq, t_len, has_prev, q_ref, k_ref, v_ref, s_ref, decay_ref, rowdec_ref, wend_ref, gl_ref,
                       *rest):
    o_ref, s_out_ref = rest[-2:]
    del has_prev
    rows = n_seq * t_len
    pair_dk, pair_dv = 2 * RET_DK, 2 * RET_DV
    lane_lo = lax.broadcasted_iota(jnp.int32, (rows, pair_dk), 1) < RET_DK
    row_lo = lax.broadcasted_iota(jnp.int32, (pair_dk, RET_DV), 0) < RET_DK
    for p in range(RET_HEADS // 2):
        qk = slice(p * pair_dk, (p + 1) * pair_dk)
        vv = slice(p * pair_dv, (p + 1) * pair_dv)
        q2f, k2f, v2f = q_ref[:, qk], k_ref[:, qk], v_ref[:, vv]
        q2, k2, v2 = q2f.astype(BF16), k2f.astype(BF16), v2f.astype(BF16)
        zk = jnp.zeros_like(k2)
        k_rows = jnp.concatenate([jnp.where(lane_lo, k2, zk), jnp.where(lane_lo, zk, k2)], 0)
        sc2 = lax.dot_general(q2, k_rows, (((1,), (1,)), ((), ())), preferred_element_type=F32) * decay_ref[p]
        zv = jnp.zeros((rows, RET_DV), BF16)
        v_bd = jnp.concatenate([jnp.concatenate([v2[:, :RET_DV], zv], 1),
                                jnp.concatenate([zv, v2[:, RET_DV:]], 1)], 0)
        kw2f = k2f * wend_ref[p]
        o_state = []
        for b in range(n_seq):
            r = slice(b * t_len, (b + 1) * t_len)
            s2 = s_ref[0, b, 2 * p:2 * p + 2].reshape(pair_dk, RET_DV)
            s2b = s2.astype(BF16)
            zs = jnp.zeros_like(s2b)
            s_bd = jnp.concatenate([jnp.where(row_lo, s2b, zs), jnp.where(row_lo, zs, s2b)], 1)
            o_state.append(jnp.dot(q2f[r].astype(BF16), s_bd, preferred_element_type=F32))
            upd = lax.dot_general(kw2f[r].astype(BF16), v2f[r].astype(BF16), (((0,), (0,)), ((), ())),
                                  preferred_element_type=F32)
            s_new = (gl_ref[p] * s2 + jnp.where(row_lo, upd[:, :RET_DV], upd[:, RET_DV:])).reshape(2, RET_DK, RET_DV)
            for d in range(s_out_ref.shape[0]):
                s_out_ref[d, b, 2 * p:2 * p + 2] = s_new
        o2 = jnp.dot(sc2.astype(BF16), v_bd, preferred_element_type=F32)
        o_ref[:, vv] = o2 + jnp.concatenate(o_state, 0) * rowdec_ref[p]


def _stacked_out_specs(shape, layer, n_seq):
    tail = tuple(shape[2:])
    zeros = (0,) * len(tail)
    if layer == 0:
        return pl.BlockSpec((shape[0], n_seq) + tail, lambda i: (0, i) + zeros)
    return pl.BlockSpec((1, n_seq) + tail, lambda i: (layer, i) + zeros)


def _ret_sample(rq, rk, rv, state, prev_out, layer, pair_tables, t_len, n_seq):
    m = rq.shape[0]
    rows = n_seq * t_len
    decay2, rowdec2, wend2, gl2 = pair_tables
    row = lambda w: pl.BlockSpec((rows, w), lambda i: (i, 0))
    st_in = pl.BlockSpec((1, n_seq, RET_HEADS, RET_DK, RET_DV), lambda i: (layer, i, 0, 0, 0))
    in_specs = [row(RET_QK_W), row(RET_QK_W), row(RET_V_W), st_in,
                _const_spec(decay2.shape), _const_spec(rowdec2.shape), _const_spec(wend2.shape),
                _const_spec(gl2.shape)]
    args = [rq, rk, rv, state, decay2, rowdec2, wend2, gl2]
    aliases = {}
    if prev_out is not None:
        in_specs.append(pl.BlockSpec(memory_space=pl.ANY))
        args.append(prev_out)
        aliases = {len(args) - 1: 1}
    return pl.pallas_call(
        functools.partial(_ret_sample_kernel, n_seq, t_len, prev_out is not None),
        grid=(m // rows,),
        in_specs=in_specs,
        out_specs=[row(RET_V_W), _stacked_out_specs(state.shape, layer, n_seq)],
        out_shape=[jax.ShapeDtypeStruct((m, RET_V_W), F32), jax.ShapeDtypeStruct(state.shape, F32)],
        input_output_aliases=aliases,
        compiler_params=_params(1),
        name="ret_sample",
    )(*args)


def _swa_sample_kernel(n_seq, t_len, sinks_ref, q_ref, kn_ref, vn_ref, kt_ref, vt_ref, *rest):
    o_ref, kto_ref, vto_ref = rest[-3:]
    grp_rows = SWA_GROUP * t_len
    n_all = n_seq * grp_rows
    q = q_ref[...]
    kn = kn_ref[...]
    vn = vn_ref[...]
    kn_t = kn.T
    vn_t = vn.T
    r = lax.broadcasted_iota(jnp.int32, (n_all, 1), 0)
    t_q = r % t_len
    g_row = (r // t_len) % SWA_GROUP
    b_row = r // grp_rows
    c = lax.broadcasted_iota(jnp.int32, (1, WINDOW), 1)
    valid_cache = c > t_q
    valid_new = ((c // t_len) == b_row) & ((c % t_len) <= t_q)
    lane = lax.broadcasted_iota(jnp.int32, (SWA_HD, WINDOW), 1)
    is_new_lane = lane >= WINDOW - t_len
    head_dims = [slice(kvh * SWA_HD, (kvh + 1) * SWA_HD) for kvh in range(SWA_KV_HEADS)]
    scores = []
    for kvh, hd in enumerate(head_dims):
        qg = [q[:, (kvh * SWA_GROUP + g) * SWA_HD:(kvh * SWA_GROUP + g + 1) * SWA_HD] for g in range(SWA_GROUP)]
        q_all = jnp.concatenate([qg[g][b * t_len:(b + 1) * t_len] for b in range(n_seq) for g in range(SWA_GROUP)],
                                0).astype(BF16)
        s_new = jnp.dot(q_all, kn_t[hd].astype(BF16), preferred_element_type=F32)
        s_cache = jnp.concatenate(
            [jnp.dot(q_all[b * grp_rows:(b + 1) * grp_rows], kt_ref[0, b, kvh].astype(BF16),
                     preferred_element_type=F32) for b in range(n_seq)], 0)
        scores.append((s_new, s_cache))
    probs = []
    for kvh, (s_new, s_cache) in enumerate(scores):
        s_new = jnp.where(valid_new, s_new, -jnp.inf)
        s_cache = jnp.where(valid_cache, s_cache, -jnp.inf)
        sink = jnp.full((n_all, 1), sinks_ref[kvh * SWA_GROUP], F32)
        for g in range(1, SWA_GROUP):
            sink = jnp.where(g_row == g, sinks_ref[kvh * SWA_GROUP + g], sink)
        m = jnp.maximum(jnp.maximum(jnp.max(s_new, -1, keepdims=True), jnp.max(s_cache, -1, keepdims=True)), sink)
        e_new = jnp.exp(s_new - m)
        e_cache = jnp.exp(s_cache - m)
        den = jnp.sum(e_new, -1, keepdims=True) + jnp.sum(e_cache, -1, keepdims=True) + jnp.exp(sink - m)
        probs.append(((e_new / den).astype(BF16), (e_cache / den).astype(BF16)))
    pieces = []
    for kvh, hd in enumerate(head_dims):
        p_new, p_cache = probs[kvh]
        o = jnp.dot(p_new, vn[:, hd].astype(BF16), preferred_element_type=F32)
        o = o + jnp.concatenate(
            [lax.dot_general(p_cache[b * grp_rows:(b + 1) * grp_rows], vt_ref[0, b, kvh].astype(BF16),
                             (((1,), (1,)), ((), ())), preferred_element_type=F32) for b in range(n_seq)], 0)
        for g in range(SWA_GROUP):
            pieces.append(jnp.concatenate(
                [o[b * grp_rows + g * t_len:b * grp_rows + (g + 1) * t_len] for b in range(n_seq)], 0))
    for kvh, hd in enumerate(head_dims):
        for b in range(n_seq):
            shift_new = (WINDOW - t_len - b * t_len) % WINDOW
            k_slid = jnp.where(is_new_lane, pltpu.roll(kn_t[hd], shift_new, 1),
                               pltpu.roll(kt_ref[0, b, kvh], WINDOW - t_len, 1))
            v_slid = jnp.where(is_new_lane, pltpu.roll(vn_t[hd], shift_new, 1),
                               pltpu.roll(vt_ref[0, b, kvh], WINDOW - t_len, 1))
            for d in range(kto_ref.shape[0]):
                kto_ref[d, b, kvh] = k_slid
                vto_ref[d, b, kvh] = v_slid
    o_ref[...] = jnp.concatenate(pieces, -1).astype(o_ref.dtype)


def _swa_sample(sq, sk, sv, cache_kt, cache_vt, prev_out, sinks, layer, t_len, n_seq):
    m = sq.shape[0]
    rows = n_seq * t_len
    assert rows == WINDOW and cache_kt.shape[-1] == WINDOW
    row = lambda w: pl.BlockSpec((rows, w), lambda i: (i, 0))
    cin = pl.BlockSpec((1, n_seq, SWA_KV_HEADS, SWA_HD, WINDOW), lambda i: (layer, i, 0, 0, 0))
    cout = _stacked_out_specs(cache_kt.shape, layer, n_seq)
    cshape = jax.ShapeDtypeStruct(cache_kt.shape, F32)
    in_specs = [pl.BlockSpec(memory_space=pltpu.SMEM), row(SWA_Q_W), row(SWA_KV_W), row(SWA_KV_W), cin, cin]
    args = [sinks, sq, sk, sv, cache_kt, cache_vt]
    aliases = {}
    if prev_out is not None:
        in_specs += [pl.BlockSpec(memory_space=pl.ANY)] * 2
        args += list(prev_out)
        aliases = {len(args) - 2: 1, len(args) - 1: 2}
    return pl.pallas_call(
        functools.partial(_swa_sample_kernel, n_seq, t_len),
        grid=(m // rows,),
        in_specs=in_specs,
        out_specs=[row(SWA_Q_W), cout, cout],
        out_shape=[jax.ShapeDtypeStruct((m, SWA_Q_W), BF16), cshape, cshape],
        input_output_aliases=aliases,
        compiler_params=_params(1),
        name="swa_sample",
    )(*args)


def _pair_tables(tables):
    decay, rowdec, wend, gl = tables
    pair = lambda a: np.concatenate([a[0::2], a[1::2]], -1)
    gl_rows = np.concatenate([np.broadcast_to(gl[0::2], (RET_HEADS // 2, RET_DK, RET_DV)),
                              np.broadcast_to(gl[1::2], (RET_HEADS // 2, RET_DK, RET_DV))], 1)
    return tuple(jnp.asarray(a, F32) for a in (pair(decay), pair(rowdec), pair(wend), gl_rows))


def _run_staged(tasks):
    active = list(tasks)
    while active:
        for t in list(active):
            try:
                next(t)
            except StopIteration:
                active.remove(t)


def _mix_prompt_kernel(nb, t_len, sinks_ref, rq_ref, rk_ref, rv_ref, sq_ref, sk_ref, sv_ref, mq_ref, mk_ref, mv_ref,
                       decay_ref, rowdec_ref, wend_ref, gl_ref, smq_ref, smk_ref, smv_ref,
                       ret_out, swa_out, mem_out, s_out, smem_out,
                       s_scr, kp_scr, kpr_scr, vp_scr, vpr_scr):
    c = pl.program_id(0)

    @pl.when(c == 0)
    def _():
        s_scr[...] = jnp.zeros_like(s_scr)
        for scr in (kp_scr, kpr_scr, vp_scr, vpr_scr):
            scr[...] = jnp.zeros_like(scr)

    pair_w = 2 * SWA_HD
    lane_lo = lax.broadcasted_iota(jnp.int32, (RET_CHUNK, pair_w), 1) < SWA_HD
    row_lo = lax.broadcasted_iota(jnp.int32, (2 * RET_DK, RET_DV), 0) < RET_DK
    lane_lo_kv = lax.broadcasted_iota(jnp.int32, (2 * WINDOW, pair_w), 1) < SWA_HD
    upper = (lax.broadcasted_iota(jnp.int32, (WINDOW, WINDOW), 1)
             > lax.broadcasted_iota(jnp.int32, (WINDOW, WINDOW), 0))
    prev_bias = jnp.where(c > 0, 0.0, -jnp.inf)


    def ret_task(b, p):
        qk = slice(p * 2 * RET_DK, (p + 1) * 2 * RET_DK)
        vv = slice(p * 2 * RET_DV, (p + 1) * 2 * RET_DV)
        q2, k2, v2, s2 = rq_ref[b, :, qk], rk_ref[b, :, qk], rv_ref[b, :, vv], s_scr[b, p]
        zk = jnp.zeros_like(k2)
        k_rows = jnp.concatenate([jnp.where(lane_lo, k2, zk), jnp.where(lane_lo, zk, k2)], 0)
        sc_raw = lax.dot_general(q2, k_rows, (((1,), (1,)), ((), ())), preferred_element_type=F32)
        s2b = s2.astype(BF16)
        zs = jnp.zeros_like(s2b)
        s_bd = jnp.concatenate([jnp.where(row_lo, s2b, zs), jnp.where(row_lo, zs, s2b)], 1)
        os_raw = jnp.dot(q2, s_bd, preferred_element_type=F32)
        kw2 = (k2.astype(F32) * wend_ref[p]).astype(BF16)
        upd = lax.dot_general(kw2, v2, (((0,), (0,)), ((), ())), preferred_element_type=F32)
        yield
        zv = jnp.zeros((RET_CHUNK, RET_DV), v2.dtype)
        v_bd = jnp.concatenate([jnp.concatenate([v2[:, :RET_DV], zv], 1),
                                jnp.concatenate([zv, v2[:, RET_DV:]], 1)], 0)
        o_raw = jnp.dot((sc_raw * decay_ref[p]).astype(BF16), v_bd, preferred_element_type=F32)
        s_scr[b, p] = gl_ref[p] * s2 + jnp.where(row_lo, upd[:, :RET_DV], upd[:, RET_DV:])
        yield
        ret_out[b, :, vv] = o_raw + os_raw * rowdec_ref[p]

    kv_ctx = {}

    def swa_prep(b):
        k_cur, v_cur = sk_ref[b], sv_ref[b]
        kb, kbr = k_cur.astype(BF16), pltpu.roll(k_cur, SWA_HD, 1).astype(BF16)
        vb, vbr = v_cur.astype(BF16), pltpu.roll(v_cur, SWA_HD, 1).astype(BF16)
        kv_ctx[b] = (jnp.concatenate([kp_scr[b], kb], 0), jnp.concatenate([kpr_scr[b], kbr], 0),
                     jnp.concatenate([vp_scr[b], vb], 0), jnp.concatenate([vpr_scr[b], vbr], 0))
        kp_scr[b], kpr_scr[b], vp_scr[b], vpr_scr[b] = kb, kbr, vb, vbr

    def swa_task(b, kvh):
        if b not in kv_ctx:
            swa_prep(b)
        kc, kcr, vc, vcr = kv_ctx[b]
        zkv = jnp.zeros_like(kc)
        k_lo, k_hi = (kc, kcr) if kvh == 0 else (kcr, kc)
        v_lo, v_hi = (vc, vcr) if kvh == 0 else (vcr, vc)
        k_rows = jnp.concatenate([jnp.where(lane_lo_kv, k_lo, zkv), jnp.where(lane_lo_kv, zkv, k_hi)], 0)
        v_rows = jnp.concatenate([jnp.where(lane_lo_kv, v_lo, zkv), jnp.where(lane_lo_kv, zkv, v_hi)], 0)
        n_pairs = SWA_GROUP // 2
        pairs = [kvh * n_pairs + jj for jj in range(n_pairs)]
        q4 = jnp.concatenate([sq_ref[b, :, pr * pair_w:(pr + 1) * pair_w] for pr in pairs], 0)
        s4 = lax.dot_general(q4, k_rows, (((1,), (1,)), ((), ())), preferred_element_type=F32)
        yield
        rows, inv = [], []
        for jj, pr in enumerate(pairs):
            ps, inv_u = [], []
            for u in range(2):
                blk = s4[jj * WINDOW:(jj + 1) * WINDOW, u * 2 * WINDOW:(u + 1) * 2 * WINDOW]
                s = jnp.where(upper, blk[:, :WINDOW] + prev_bias, blk[:, WINDOW:])
                sink = sinks_ref[2 * pr + u]
                m = jnp.maximum(jnp.max(s, -1, keepdims=True), sink)
                e = jnp.exp(s - m)
                den = jnp.sum(e, -1, keepdims=True) + jnp.exp(sink - m)
                ps += [jnp.where(upper, e, 0.0).astype(BF16), jnp.where(upper, 0.0, e).astype(BF16)]
                inv_u.append(1.0 / den)
            rows.append(jnp.concatenate(ps, 1))
            inv.append(jnp.where(lane_lo, inv_u[0], inv_u[1]))
        o4 = jnp.dot(jnp.concatenate(rows, 0), v_rows, preferred_element_type=F32)
        yield
        for jj, pr in enumerate(pairs):
            swa_out[b, :, pr * pair_w:(pr + 1) * pair_w] = (o4[jj * WINDOW:(jj + 1) * WINDOW] * inv[jj]).astype(
                swa_out.dtype)

    def mem_task(b, h):
        sl = slice(h * MEM_HD, (h + 1) * MEM_HD)
        s = lax.dot_general(mq_ref[b, :, sl], mk_ref[b, :, sl], (((1,), (1,)), ((), ())),
                            preferred_element_type=F32) * (MEM_HD ** -0.5)
        yield
        m = jnp.max(s, -1, keepdims=True)
        e = jnp.exp(s - m)
        inv = 1.0 / jnp.sum(e, -1, keepdims=True)
        o = jnp.dot(e.astype(BF16), mv_ref[b, :, sl], preferred_element_type=F32)
        yield
        mem_out[b, :, sl] = (o * inv).astype(mem_out.dtype)

    head_of_row = lax.broadcasted_iota(jnp.int32, (MEM_HEADS * t_len, 1), 0) // t_len
    head_of_col = lax.broadcasted_iota(jnp.int32, (1, N_MEM * MEM_HEADS), 1) % MEM_HEADS
    valid_smem = head_of_row == head_of_col

    def smem_task(j):
        r = slice(j * t_len, (j + 1) * t_len)
        qb = smq_ref[r, :]
        q_all = jnp.concatenate([qb[:, h * MEM_HD:(h + 1) * MEM_HD] for h in range(MEM_HEADS)], 0).astype(BF16)
        s = lax.dot_general(q_all, smk_ref[0, j].astype(BF16), (((1,), (1,)), ((), ())),
                            preferred_element_type=F32) * (MEM_HD ** -0.5)
        yield
        s = jnp.where(valid_smem, s, -jnp.inf)
        m = jnp.max(s, -1, keepdims=True)
        e = jnp.exp(s - m)
        inv = 1.0 / jnp.sum(e, -1, keepdims=True)
        o = jnp.dot(e.astype(BF16), smv_ref[0, j].astype(BF16), preferred_element_type=F32) * inv
        yield
        smem_out[r, :] = jnp.concatenate([o[h * t_len:(h + 1) * t_len] for h in range(MEM_HEADS)], -1)

    n_smem = smk_ref.shape[1]
    for b in range(nb):
        _run_staged([ret_task(b, 0), swa_task(b, 0), mem_task(b, 0), ret_task(b, 1), mem_task(b, 1),
                     ret_task(b, 2), swa_task(b, 1), mem_task(b, 2), ret_task(b, 3), mem_task(b, 3)]
                    + [smem_task(j) for j in range(b, n_smem, nb)])

    @pl.when(c == pl.num_programs(0) - 1)
    def _():
        s_out[...] = s_scr[...]


def _mix_prompt(rq, rk, rv, sq, sk, sv, mq, mk_bf, mv_bf, sinks, pair_tables, smq, cache_mk, cache_mv, layer, t_len):
    nb, seq, _ = rq.shape
    n_steps = seq // RET_CHUNK
    n_smem = cache_mk.shape[1] // n_steps
    decay2, rowdec2, wend2, gl2 = pair_tables
    chunk = lambda w: pl.BlockSpec((nb, RET_CHUNK, w), lambda c: (0, c, 0))
    srow = pl.BlockSpec((n_smem * t_len, MEM_W), lambda c: (c, 0))
    skv = pl.BlockSpec((1, n_smem, N_MEM * MEM_HEADS, MEM_HD), lambda c: (layer, c, 0, 0))
    st_shape = (nb, RET_HEADS // 2, 2 * RET_DK, RET_DV)
    kv_scr = pltpu.VMEM((nb, WINDOW, SWA_KV_W), BF16)
    return pl.pallas_call(
        functools.partial(_mix_prompt_kernel, nb, t_len),
        grid=(n_steps,),
        in_specs=[pl.BlockSpec(memory_space=pltpu.SMEM),
                  chunk(RET_QK_W), chunk(RET_QK_W), chunk(RET_V_W), chunk(SWA_Q_W), chunk(SWA_KV_W), chunk(SWA_KV_W),
                  chunk(MEM_W), _const_spec(mk_bf.shape), _const_spec(mv_bf.shape),
                  _const_spec(decay2.shape), _const_spec(rowdec2.shape), _const_spec(wend2.shape),
                  _const_spec(gl2.shape), srow, skv, skv],
        out_specs=[chunk(RET_V_W), chunk(SWA_Q_W), chunk(MEM_W), _const_spec(st_shape), srow],
        out_shape=[jax.ShapeDtypeStruct((nb, seq, RET_V_W), F32), jax.ShapeDtypeStruct((nb, seq, SWA_Q_W), BF16),
                   jax.ShapeDtypeStruct((nb, seq, MEM_W), BF16), jax.ShapeDtypeStruct(st_shape, F32),
                   jax.ShapeDtypeStruct(smq.shape, F32)],
        scratch_shapes=[pltpu.VMEM(st_shape, F32), kv_scr, kv_scr, kv_scr, kv_scr],
        compiler_params=_params(1),
        name="mix_prompt",
    )(sinks, rq, rk, rv, sq, sk, sv, mq, mk_bf, mv_bf, decay2, rowdec2, wend2, gl2, smq, cache_mk, cache_mv)


def _mem_kv_kernel(x_ref, w_ref, k_ref, v_ref, kb_ref, vb_ref):
    kv = jnp.dot(x_ref[...].astype(BF16), w_ref[0], preferred_element_type=F32)
    k_ref[...] = kv[:, :MEM_W]
    v_ref[...] = kv[:, MEM_W:]
    kb_ref[...] = kv[:, :MEM_W].astype(BF16)
    vb_ref[...] = kv[:, MEM_W:].astype(BF16)


def _mem_kv(mem2d, w_bf, layer, tm):
    m, k = mem2d.shape
    out = pl.BlockSpec((tm, MEM_W), lambda i: (i, 0))
    return pl.pallas_call(
        _mem_kv_kernel,
        grid=(m // tm,),
        in_specs=[pl.BlockSpec((tm, k), lambda i: (i, 0)), _layer_spec(w_bf.shape, layer)],
        out_specs=[out, out, out, out],
        out_shape=[jax.ShapeDtypeStruct((m, MEM_W), F32), jax.ShapeDtypeStruct((m, MEM_W), F32),
                   jax.ShapeDtypeStruct((m, MEM_W), BF16), jax.ShapeDtypeStruct((m, MEM_W), BF16)],
        compiler_params=_params(1),
        name="mem_kv",
    )(mem2d, w_bf)


def _layer_norm(x, g, b):
    mu = jnp.mean(x, -1, keepdims=True)
    d = x - mu
    var = jnp.mean(d * d, -1, keepdims=True)
    return d * lax.rsqrt(var + LN_EPS) * g + b


def _finish_kernel(x_ref, ret_ref, rg_ref, swa_ref, mem_ref, gr_ref, gs_ref, gm_ref, gng_ref,
                   wr_ref, ws_ref, wm_ref, wo_ref, l1g_ref, l1b_ref, wu_ref, wd_ref, l2g_ref, l2b_ref, o_ref):
    swa_b = jnp.dot(swa_ref[...].astype(BF16), ws_ref[0], preferred_element_type=F32)
    mem_b = jnp.dot(mem_ref[...].astype(BF16), wm_ref[0], preferred_element_type=F32)
    rg = rg_ref[...]
    gn = jnp.concatenate([_group_norm(ret_ref[:, h * RET_DV:(h + 1) * RET_DV], gng_ref[0, h:h + 1, :])
                          for h in range(RET_HEADS)], -1)
    ret_in = (rg * jax.nn.sigmoid(rg) * gn).astype(BF16)
    ret_b = jnp.dot(ret_in, wr_ref[0], preferred_element_type=F32)
    merged = (jax.nn.sigmoid(gr_ref[...]) * ret_b + jax.nn.sigmoid(gs_ref[...]) * swa_b
              + jax.nn.sigmoid(gm_ref[...]) * mem_b)
    y = jnp.dot(merged.astype(BF16), wo_ref[0], preferred_element_type=F32)
    x1 = _layer_norm(ALPHA * x_ref[...] + y, l1g_ref[0], l1b_ref[0])
    x1b = x1.astype(BF16)
    n_slabs = 4
    ff = D_FF // n_slabs
    up = lambda c: jnp.square(jnp.maximum(
        jnp.dot(x1b, wu_ref[0, :, c * ff:(c + 1) * ff], preferred_element_type=F32), 0.0)).astype(BF16)
    down = lambda c, h: jnp.dot(h, wd_ref[0, c * ff:(c + 1) * ff, :], preferred_element_type=F32)
    h_next = up(0)
    acc = None
    for c in range(n_slabs):
        h_cur, h_next = h_next, (up(c + 1) if c + 1 < n_slabs else None)
        d = down(c, h_cur)
        acc = d if acc is None else acc + d
    o_ref[...] = _layer_norm(ALPHA * x1 + acc, l2g_ref[0], l2b_ref[0])


def _finish(x2d, gn, rg, swa_o, mem_o, g_r, g_s, g_m, lw, layer, tm):
    m = x2d.shape[0]
    row = lambda w: pl.BlockSpec((tm, w), lambda i: (i, 0))
    lspec = lambda a: _layer_spec(a.shape, layer)
    return pl.pallas_call(
        _finish_kernel,
        grid=(m // tm,),
        in_specs=[row(D_MODEL), row(RET_V_W), row(RET_V_W), row(SWA_Q_W), row(MEM_W),
                  row(D_MODEL), row(D_MODEL), row(D_MODEL)] + [lspec(a) for a in lw],
        out_specs=row(D_MODEL),
        out_shape=jax.ShapeDtypeStruct((m, D_MODEL), F32),
        compiler_params=_params(1),
        name="finish",
    )(x2d, gn, rg, swa_o, mem_o, g_r, g_s, g_m, *lw)


def _rope_tables(pos, reps=1):
    half = SWA_HD // 2
    inv = np.power(ROPE_THETA, -np.arange(half, dtype=np.float64) / half)
    ang = np.asarray(pos, np.float64)[:, None] * inv[None, :]
    c, s = np.cos(ang), np.sin(ang)
    cos_t, sin_t = np.concatenate([c, c, c, c], -1), np.concatenate([-s, s, -s, s], -1)
    return jnp.asarray(np.tile(cos_t, (reps, 1)), F32), jnp.asarray(np.tile(sin_t, (reps, 1)), F32)


def kernel(x_prompt, x_sample, state_ret, cache_swa_k, cache_swa_v, cache_mem_k, cache_mem_v, mem_prompt,
           w_in, w_br_ret, w_br_swa, w_br_mem, w_out, w_mem_kv, attn_sinks, ret_gn_g,
           ln1_g, ln1_b, w_up, w_down, ln2_g, ln2_b):
    batch, seq, _ = x_prompt.shape
    dec_b, dec_t, _ = x_sample.shape
    tm_p, tm_s = 512, 256
    tm_fin = 512
    ret_seqs = RET_CHUNK // dec_t

    cos_p, sin_p = _rope_tables(np.arange(seq))
    cos_s, sin_s = _rope_tables(PAST_LEN + np.arange(dec_t), tm_s // dec_t)
    tab_p2 = _pair_tables(_ret_tables(RET_CHUNK, RET_CHUNK))
    tab_s2 = _pair_tables(_ret_tables(RET_CHUNK, dec_t))

    xp = x_prompt.reshape(batch * seq, D_MODEL)
    xs = x_sample.reshape(dec_b * dec_t, D_MODEL)
    mem2d = mem_prompt.reshape(batch * N_MEM, D_MODEL)
    cache_kt = jnp.transpose(cache_swa_k, (0, 1, 3, 4, 2))
    cache_vt = jnp.transpose(cache_swa_v, (0, 1, 3, 4, 2))
    cache_mk = cache_mem_k.reshape(DEPTH, dec_b, N_MEM * MEM_HEADS, MEM_HD)
    cache_mv = cache_mem_v.reshape(DEPTH, dec_b, N_MEM * MEM_HEADS, MEM_HD)

    w_in_bf = w_in.astype(BF16)
    w_mem_kv_bf = w_mem_kv.astype(BF16)
    ln_row = lambda a: a.reshape(DEPTH, 1, D_MODEL)
    lw = (ret_gn_g, w_br_ret.astype(BF16), w_br_swa.astype(BF16), w_br_mem.astype(BF16), w_out.astype(BF16),
          ln_row(ln1_g), ln_row(ln1_b), w_up.astype(BF16), w_down.astype(BF16), ln_row(ln2_g), ln_row(ln2_b))

    ret_p, swk_p, swv_p, mk_p, mv_p = [], [], [], [], []
    ret_s, swa_s = None, None
    for l in range(DEPTH):
        sinks = attn_sinks[l]
        rq, rk, rv, rg, sq, sk, sv, mq, g_r, g_s, g_m = _inproj(xp, w_in_bf, l, cos_p, sin_p, tm_p, BF16)
        s_rq, s_rk, s_rv, s_rg, s_sq, s_sk, s_sv, s_mq, s_g_r, s_g_s, s_g_m = _inproj(
            xs, w_in_bf, l, cos_s, sin_s, tm_s, F32)

        mk, mv, mk_bf, mv_bf = _mem_kv(mem2d, w_mem_kv_bf, l, 256)
        by_seq = lambda a: a.reshape(batch, seq, a.shape[-1])
        by_mem = lambda a: a.reshape(batch, N_MEM, MEM_W)
        gn, swa_o, mem_o, s_p, s_mem_o = _mix_prompt(
            by_seq(rq), by_seq(rk), by_seq(rv), by_seq(sq), by_seq(sk), by_seq(sv), by_seq(mq),
            by_mem(mk_bf), by_mem(mv_bf), sinks, tab_p2, s_mq, cache_mk, cache_mv, l, dec_t)
        flat = lambda a: a.reshape(batch * seq, a.shape[-1])
        xp = _finish(xp, flat(gn), rg, flat(swa_o), flat(mem_o), g_r, g_s, g_m, lw, l, tm_fin)
        ret_p.append(s_p.reshape(batch, RET_HEADS, RET_DK, RET_DV))
        swk_p.append(sk.reshape(batch, seq, SWA_KV_W)[:, -WINDOW:].reshape(batch, WINDOW, SWA_KV_HEADS, SWA_HD))
        swv_p.append(sv.reshape(batch, seq, SWA_KV_W)[:, -WINDOW:].reshape(batch, WINDOW, SWA_KV_HEADS, SWA_HD))
        mk_p.append(mk.reshape(batch, N_MEM, MEM_HEADS, MEM_HD))
        mv_p.append(mv.reshape(batch, N_MEM, MEM_HEADS, MEM_HD))

        s_gn, ret_s = _ret_sample(s_rq, s_rk, s_rv, state_ret, ret_s, l, tab_s2, dec_t, ret_seqs)
        s_swa_o, kto, vto = _swa_sample(s_sq, s_sk, s_sv, cache_kt, cache_vt, swa_s, sinks, l, dec_t,
                                        WINDOW // dec_t)
        swa_s = (kto, vto)
        xs = _finish(xs, s_gn, s_rg, s_swa_o, s_mem_o, s_g_r, s_g_s, s_g_m, lw, l, tm_s)

    from_t = lambda a: jnp.transpose(a, (0, 1, 4, 2, 3))
    return (xp.reshape(batch, seq, D_MODEL), xs.reshape(dec_b, dec_t, D_MODEL),
            jnp.stack(ret_p), jnp.stack(swk_p), jnp.stack(swv_p), jnp.stack(mk_p), jnp.stack(mv_p),
            ret_s, from_t(swa_s[0]), from_t(swa_s[1]))
```

```python
import functools

import jax
import jax.numpy as jnp
import numpy as np
from jax import lax
from jax.experimental import pallas as pl
from jax.experimental.pallas import tpu as pltpu

F32 = jnp.float32
BF16 = jnp.bfloat16

D_MODEL = 1024
DEPTH = 2
PAST_LEN = 16384
RET_HEADS = 8
RET_DK = 64
RET_DV = 128
RET_CHUNK = 128
SWA_HEADS = 8
SWA_KV_HEADS = 2
SWA_GROUP = SWA_HEADS // SWA_KV_HEADS
SWA_HD = 64
WINDOW = 128
MEM_HEADS = 4
MEM_HD = 128
N_MEM = 256
D_FF = 4 * D_MODEL
ROPE_THETA = 10000.0
LN_EPS = 1e-5
GN_EPS = 1e-5
ALPHA = (2 * DEPTH) ** 0.25

RET_QK_W = RET_HEADS * RET_DK
RET_V_W = RET_HEADS * RET_DV
SWA_Q_W = SWA_HEADS * SWA_HD
SWA_KV_W = SWA_KV_HEADS * SWA_HD
MEM_W = MEM_HEADS * MEM_HD
OFF_RQ = 0
OFF_RK = OFF_RQ + RET_QK_W
OFF_RV = OFF_RK + RET_QK_W
OFF_RG = OFF_RV + RET_V_W
OFF_SQ = OFF_RG + RET_V_W
OFF_SK = OFF_SQ + SWA_Q_W
OFF_SV = OFF_SK + SWA_KV_W
OFF_MQ = OFF_SV + SWA_KV_W
OFF_GR = OFF_MQ + MEM_W
OFF_GS = OFF_GR + D_MODEL
OFF_GM = OFF_GS + D_MODEL
IN_W = OFF_GM + D_MODEL

LANES = 128
V7X_VMEM_LIMIT = 62 * 1024 * 1024


def _const_spec(shape):
    nd = len(shape)
    return pl.BlockSpec(shape, lambda *_: (0,) * nd, pipeline_mode=pl.Buffered(1))


def _layer_spec(shape, layer):
    nd = len(shape)
    return pl.BlockSpec((1,) + tuple(shape[1:]), lambda *_: (layer,) + (0,) * (nd - 1), pipeline_mode=pl.Buffered(1))


def _params(n_grid):
    return pltpu.CompilerParams(dimension_semantics=("arbitrary",) * n_grid, vmem_limit_bytes=V7X_VMEM_LIMIT)


N_INPROJ_OUT = 11


def _inproj_kernel(n_cast, x_ref, w_ref, cos_ref, sin_ref, *refs):
    cast_in, outs, cast_out = refs[:n_cast], refs[n_cast:n_cast + N_INPROJ_OUT], refs[n_cast + N_INPROJ_OUT:]
    rq_ref, rk_ref, rv_ref, rg_ref, sq_ref, sk_ref, sv_ref, mq_ref, gr_ref, gs_ref, gm_ref = outs
    for src, dst in zip(cast_in, cast_out):
        dst[...] = src[...].astype(dst.dtype)
    xb = x_ref[...].astype(BF16)
    cos = cos_ref[...]
    sin = sin_ref[...]
    lane = lax.broadcasted_iota(jnp.int32, cos.shape, 1)
    first_half = (lane & (SWA_HD // 2)) == 0

    def proj(off, width):
        return jnp.dot(xb, w_ref[0, :, off:off + width], preferred_element_type=F32)

    def rope_store(off, width, out_ref, scale):
        y = proj(off, width)
        for j in range(width // LANES):
            yj = y[:, j * LANES:(j + 1) * LANES]
            sw = jnp.where(first_half, pltpu.roll(yj, LANES - SWA_HD // 2, 1), pltpu.roll(yj, SWA_HD // 2, 1))
            r = yj * cos + sw * sin
            if scale != 1.0:
                r = r * scale
            out_ref[:, j * LANES:(j + 1) * LANES] = r.astype(out_ref.dtype)

    def plain_store(off, width, out_ref):
        out_ref[...] = proj(off, width).astype(out_ref.dtype)

    rope_store(OFF_RQ, RET_QK_W, rq_ref, 1.0)
    rope_store(OFF_RK, RET_QK_W, rk_ref, RET_DK ** -0.5)
    plain_store(OFF_RV, RET_V_W, rv_ref)
    plain_store(OFF_RG, RET_V_W, rg_ref)
    rope_store(OFF_SQ, SWA_Q_W, sq_ref, SWA_HD ** -0.5)
    rope_store(OFF_SK, SWA_KV_W, sk_ref, 1.0)
    plain_store(OFF_SV, SWA_KV_W, sv_ref)
    plain_store(OFF_MQ, MEM_W, mq_ref)
    plain_store(OFF_GR, D_MODEL, gr_ref)
    plain_store(OFF_GS, D_MODEL, gs_ref)
    plain_store(OFF_GM, D_MODEL, gm_ref)


def _inproj(x2d, w_bf, cos_tab, sin_tab, tm, qkv_dtype, cast_jobs=()):
    m = x2d.shape[0]
    n_steps = m // tm
    n_tab = cos_tab.shape[0] // tm
    row = lambda w: pl.BlockSpec((tm, w), lambda i: (i, 0))
    tab = pl.BlockSpec((tm, LANES), lambda i: (i % n_tab, 0))
    widths_dtypes = [(RET_QK_W, qkv_dtype), (RET_QK_W, qkv_dtype), (RET_V_W, qkv_dtype), (RET_V_W, F32),
                     (SWA_Q_W, qkv_dtype), (SWA_KV_W, F32), (SWA_KV_W, F32), (MEM_W, qkv_dtype),
                     (D_MODEL, F32), (D_MODEL, F32), (D_MODEL, F32)]
    assert len(widths_dtypes) == N_INPROJ_OUT
    slab = lambda a: (1, a.shape[1] // n_steps, a.shape[2])
    cast_in = [pl.BlockSpec(slab(a), lambda i, layer=layer: (layer, i, 0)) for a, layer in cast_jobs]
    cast_out = [pl.BlockSpec(slab(a), lambda i: (0, i, 0)) for a, _ in cast_jobs]
    return pl.pallas_call(
        functools.partial(_inproj_kernel, len(cast_jobs)),
        grid=(n_steps,),
        in_specs=[row(D_MODEL), _layer_spec(w_bf[0].shape, w_bf[1]), tab, tab] + cast_in,
        out_specs=[row(w) for w, _ in widths_dtypes] + cast_out,
        out_shape=([jax.ShapeDtypeStruct((m, w), dt) for w, dt in widths_dtypes]
                   + [jax.ShapeDtypeStruct((1,) + a.shape[1:], BF16) for a, _ in cast_jobs]),
        compiler_params=_params(1),
        name="inproj",
    )(x2d, w_bf[0], cos_tab, sin_tab, *[a for a, _ in cast_jobs])


def _group_norm(o, g_row):
    mu = jnp.mean(o, -1, keepdims=True)
    d = o - mu
    var = jnp.mean(d * d, -1, keepdims=True)
    return d * lax.rsqrt(var + GN_EPS) * g_row


def _ret_tables(n_rows, period):
    lg = np.log1p(-np.exp2(-5.0 - np.arange(RET_HEADS, dtype=np.float64)))
    r = np.arange(n_rows)
    t = (r % period).astype(np.float64)
    same = (r[:, None] // period) == (r[None, :] // period)
    diff = t[:, None] - t[None, :]
    decay = np.where((diff >= 0) & same, np.exp(lg[:, None, None] * np.maximum(diff, 0.0)), 0.0)
    rowdec = np.exp(lg[:, None] * (t[None, :] + 1.0))
    wend = np.exp(lg[:, None] * (period - 1.0 - t[None, :]))
    gl = np.exp(lg * period)
    rowdec = np.broadcast_to(rowdec[:, :, None], (RET_HEADS, n_rows, RET_DV))
    wend = np.broadcast_to(wend[:, :, None], (RET_HEADS, n_rows, RET_DK))
    gl = np.broadcast_to(gl[:, None, None], (RET_HEADS, 1, RET_DV))
    return decay, rowdec, wend, gl


def _ret_sample_kernel(n_seq, t_len, has_prev, q_ref, k_ref, v_ref, s_ref, decay_ref, rowdec_ref, wend_ref, gl_ref,
                       *rest):
    o_ref, s_out_ref = rest[-2:]
    del has_prev
    rows = n_seq * t_len
    pair_dk, pair_dv = 2 * RET_DK, 2 * RET_DV
    lane_lo = lax.broadcasted_iota(jnp.int32, (rows, pair_dk), 1) < RET_DK
    row_lo = lax.broadcasted_iota(jnp.int32, (pair_dk, RET_DV), 0) < RET_DK
    for p in range(RET_HEADS // 2):
        qk = slice(p * pair_dk, (p + 1) * pair_dk)
        vv = slice(p * pair_dv, (p + 1) * pair_dv)
        q2f, k2f, v2f = q_ref[:, qk], k_ref[:, qk], v_ref[:, vv]
        q2, k2, v2 = q2f.astype(BF16), k2f.astype(BF16), v2f.astype(BF16)
        zk = jnp.zeros_like(k2)
        k_rows = jnp.concatenate([jnp.where(lane_lo, k2, zk), jnp.where(lane_lo, zk, k2)], 0)
        sc2 = lax.dot_general(q2, k_rows, (((1,), (1,)), ((), ())), preferred_element_type=F32) * decay_ref[p]
        zv = jnp.zeros((rows, RET_DV), BF16)
        v_bd = jnp.concatenate([jnp.concatenate([v2[:, :RET_DV], zv], 1),
                                jnp.concatenate([zv, v2[:, RET_DV:]], 1)], 0)
        kw2f = k2f * wend_ref[p]
        o_state = []
        for b in range(n_seq):
            r = slice(b * t_len, (b + 1) * t_len)
            s2 = s_ref[0, b, 2 * p:2 * p + 2].reshape(pair_dk, RET_DV)
            s2b = s2.astype(BF16)
            zs = jnp.zeros_like(s2b)
            s_bd = jnp.concatenate([jnp.where(row_lo, s2b, zs), jnp.where(row_lo, zs, s2b)], 1)
            o_state.append(jnp.dot(q2f[r].astype(BF16), s_bd, preferred_element_type=F32))
            upd = lax.dot_general(kw2f[r].astype(BF16), v2f[r].astype(BF16), (((0,), (0,)), ((), ())),
                                  preferred_element_type=F32)
            s_new = (gl_ref[p] * s2 + jnp.where(row_lo, upd[:, :RET_DV], upd[:, RET_DV:])).reshape(2, RET_DK, RET_DV)
            for d in range(s_out_ref.shape[0]):
                s_out_ref[d, b, 2 * p:2 * p + 2] = s_new
        o2 = jnp.dot(sc2.astype(BF16), v_bd, preferred_element_type=F32)
        o_ref[:, vv] = o2 + jnp.concatenate(o_state, 0) * rowdec_ref[p]


def _stacked_out_specs(shape, layer, n_seq):
    tail = tuple(shape[2:])
    zeros = (0,) * len(tail)
    if layer == 0:
        return pl.BlockSpec((shape[0], n_seq) + tail, lambda i: (0, i) + zeros)
    return pl.BlockSpec((1, n_seq) + tail, lambda i: (layer, i) + zeros)


def _ret_sample(rq, rk, rv, state, prev_out, layer, pair_tables, t_len, n_seq):
    m = rq.shape[0]
    rows = n_seq * t_len
    decay2, rowdec2, wend2, gl2 = pair_tables
    row = lambda w: pl.BlockSpec((rows, w), lambda i: (i, 0))
    st_in = pl.BlockSpec((1, n_seq, RET_HEADS, RET_DK, RET_DV), lambda i: (layer, i, 0, 0, 0))
    in_specs = [row(RET_QK_W), row(RET_QK_W), row(RET_V_W), st_in,
                _const_spec(decay2.shape), _const_spec(rowdec2.shape), _const_spec(wend2.shape),
                _const_spec(gl2.shape)]
    args = [rq, rk, rv, state, decay2, rowdec2, wend2, gl2]
    aliases = {}
    if prev_out is not None:
        in_specs.append(pl.BlockSpec(memory_space=pl.ANY))
        args.append(prev_out)
        aliases = {len(args) - 1: 1}
    return pl.pallas_call(
        functools.partial(_ret_sample_kernel, n_seq, t_len, prev_out is not None),
        grid=(m // rows,),
        in_specs=in_specs,
        out_specs=[row(RET_V_W), _stacked_out_specs(state.shape, layer, n_seq)],
        out_shape=[jax.ShapeDtypeStruct((m, RET_V_W), F32), jax.ShapeDtypeStruct(state.shape, F32)],
        input_output_aliases=aliases,
        compiler_params=_params(1),
        name="ret_sample",
    )(*args)


def _swa_sample_kernel(n_seq, t_len, sinks_ref, q_ref, kn_ref, vn_ref, kt_ref, vt_ref, *rest):
    o_ref, kto_ref, vto_ref = rest[-3:]
    grp_rows = SWA_GROUP * t_len
    n_all = n_seq * grp_rows
    q = q_ref[...]
    kn = kn_ref[...]
    vn = vn_ref[...]
    kn_t = kn.T
    vn_t = vn.T
    r = lax.broadcasted_iota(jnp.int32, (n_all, 1), 0)
    t_q = r % t_len
    g_row = (r // t_len) % SWA_GROUP
    b_row = r // grp_rows
    c = lax.broadcasted_iota(jnp.int32, (1, WINDOW), 1)
    valid_cache = c > t_q
    valid_new = ((c // t_len) == b_row) & ((c % t_len) <= t_q)
    lane = lax.broadcasted_iota(jnp.int32, (SWA_HD, WINDOW), 1)
    is_new_lane = lane >= WINDOW - t_len
    head_dims = [slice(kvh * SWA_HD, (kvh + 1) * SWA_HD) for kvh in range(SWA_KV_HEADS)]
    scores = []
    for kvh, hd in enumerate(head_dims):
        qg = [q[:, (kvh * SWA_GROUP + g) * SWA_HD:(kvh * SWA_GROUP + g + 1) * SWA_HD] for g in range(SWA_GROUP)]
        q_all = jnp.concatenate([qg[g][b * t_len:(b + 1) * t_len] for b in range(n_seq) for g in range(SWA_GROUP)],
                                0).astype(BF16)
        s_new = jnp.dot(q_all, kn_t[hd].astype(BF16), preferred_element_type=F32)
        s_cache = jnp.concatenate(
            [jnp.dot(q_all[b * grp_rows:(b + 1) * grp_rows], kt_ref[0, b, kvh].astype(BF16),
                     preferred_element_type=F32) for b in range(n_seq)], 0)
        scores.append((s_new, s_cache))
    probs = []
    for kvh, (s_new, s_cache) in enumerate(scores):
        s_new = jnp.where(valid_new, s_new, -jnp.inf)
        s_cache = jnp.where(valid_cache, s_cache, -jnp.inf)
        sink = jnp.full((n_all, 1), sinks_ref[kvh * SWA_GROUP], F32)
        for g in range(1, SWA_GROUP):
            sink = jnp.where(g_row == g, sinks_ref[kvh * SWA_GROUP + g], sink)
        m = jnp.maximum(jnp.maximum(jnp.max(s_new, -1, keepdims=True), jnp.max(s_cache, -1, keepdims=True)), sink)
        e_new = jnp.exp(s_new - m)
        e_cache = jnp.exp(s_cache - m)
        den = jnp.sum(e_new, -1, keepdims=True) + jnp.sum(e_cache, -1, keepdims=True) + jnp.exp(sink - m)
        probs.append(((e_new / den).astype(BF16), (e_cache / den).astype(BF16)))
    pieces = []
    for kvh, hd in enumerate(head_dims):
        p_new, p_cache = probs[kvh]
        o = jnp.dot(p_new, vn[:, hd].astype(BF16), preferred_element_type=F32)
        o = o + jnp.concatenate(
            [lax.dot_general(p_cache[b * grp_rows:(b + 1) * grp_rows], vt_ref[0, b, kvh].astype(BF16),
                             (((1,), (1,)), ((), ())), preferred_element_type=F32) for b in range(n_seq)], 0)
        for g in range(SWA_GROUP):
            pieces.append(jnp.concatenate(
                [o[b * grp_rows + g * t_len:b * grp_rows + (g + 1) * t_len] for b in range(n_seq)], 0))
    for kvh, hd in enumerate(head_dims):
        for b in range(n_seq):
            shift_new = (WINDOW - t_len - b * t_len) % WINDOW
            k_slid = jnp.where(is_new_lane, pltpu.roll(kn_t[hd], shift_new, 1),
                               pltpu.roll(kt_ref[0, b, kvh], WINDOW - t_len, 1))
            v_slid = jnp.where(is_new_lane, pltpu.roll(vn_t[hd], shift_new, 1),
                               pltpu.roll(vt_ref[0, b, kvh], WINDOW - t_len, 1))
            for d in range(kto_ref.shape[0]):
                kto_ref[d, b, kvh] = k_slid
                vto_ref[d, b, kvh] = v_slid
    o_ref[...] = jnp.concatenate(pieces, -1).astype(o_ref.dtype)


def _swa_sample(sq, sk, sv, cache_kt, cache_vt, prev_out, sinks, layer, t_len, n_seq):
    m = sq.shape[0]
    rows = n_seq * t_len
    assert rows == WINDOW and cache_kt.shape[-1] == WINDOW
    row = lambda w: pl.BlockSpec((rows, w), lambda i: (i, 0))
    cin = pl.BlockSpec((1, n_seq, SWA_KV_HEADS, SWA_HD, WINDOW), lambda i: (layer, i, 0, 0, 0))
    cout = _stacked_out_specs(cache_kt.shape, layer, n_seq)
    cshape = jax.ShapeDtypeStruct(cache_kt.shape, F32)
    in_specs = [pl.BlockSpec(memory_space=pltpu.SMEM), row(SWA_Q_W), row(SWA_KV_W), row(SWA_KV_W), cin, cin]
    args = [sinks, sq, sk, sv, cache_kt, cache_vt]
    aliases = {}
    if prev_out is not None:
        in_specs += [pl.BlockSpec(memory_space=pl.ANY)] * 2
        args += list(prev_out)
        aliases = {len(args) - 2: 1, len(args) - 1: 2}
    return pl.pallas_call(
        functools.partial(_swa_sample_kernel, n_seq, t_len),
        grid=(m // rows,),
        in_specs=in_specs,
        out_specs=[row(SWA_Q_W), cout, cout],
        out_shape=[jax.ShapeDtypeStruct((m, SWA_Q_W), BF16), cshape, cshape],
        input_output_aliases=aliases,
        compiler_params=_params(1),
        name="swa_sample",
    )(*args)


def _pair_tables(tables):
    decay, rowdec, wend, gl = tables
    pair = lambda a: np.concatenate([a[0::2], a[1::2]], -1)
    gl_rows = np.concatenate([np.broadcast_to(gl[0::2], (RET_HEADS // 2, RET_DK, RET_DV)),
                              np.broadcast_to(gl[1::2], (RET_HEADS // 2, RET_DK, RET_DV))], 1)
    return tuple(jnp.asarray(a, F32) for a in (pair(decay), pair(rowdec), pair(wend), gl_rows))


def _run_staged(tasks):
    active = list(tasks)
    while active:
        for t in list(active):
            try:
                next(t)
            except StopIteration:
                active.remove(t)


def _mix_prompt_kernel(nb, t_len, sinks_ref, rq_ref, rk_ref, rv_ref, sq_ref, sk_ref, sv_ref, mq_ref, mk_ref, mv_ref,
                       decay_ref, rowdec_ref, wend_ref, gl_ref, smq_ref, smk_ref, smv_ref,
                       ret_out, swa_out, mem_out, s_out, smem_out,
                       s_scr, kp_scr, kpr_scr, vp_scr, vpr_scr):
    c = pl.program_id(0)

    @pl.when(c == 0)
    def _():
        s_scr[...] = jnp.zeros_like(s_scr)
        for scr in (kp_scr, kpr_scr, vp_scr, vpr_scr):
            scr[...] = jnp.zeros_like(scr)

    pair_w = 2 * SWA_HD
    lane_lo = lax.broadcasted_iota(jnp.int32, (RET_CHUNK, pair_w), 1) < SWA_HD
    row_lo = lax.broadcasted_iota(jnp.int32, (2 * RET_DK, RET_DV), 0) < RET_DK
    lane_lo_kv = lax.broadcasted_iota(jnp.int32, (2 * WINDOW, pair_w), 1) < SWA_HD
    upper = (lax.broadcasted_iota(jnp.int32, (WINDOW, WINDOW), 1)
             > lax.broadcasted_iota(jnp.int32, (WINDOW, WINDOW), 0))
    prev_bias = jnp.where(c > 0, 0.0, -jnp.inf)


    def ret_task(b, p):
        qk = slice(p * 2 * RET_DK, (p + 1) * 2 * RET_DK)
        vv = slice(p * 2 * RET_DV, (p + 1) * 2 * RET_DV)
        q2, k2, v2, s2 = rq_ref[b, :, qk], rk_ref[b, :, qk], rv_ref[b, :, vv], s_scr[b, p]
        zk = jnp.zeros_like(k2)
        k_rows = jnp.concatenate([jnp.where(lane_lo, k2, zk), jnp.where(lane_lo, zk, k2)], 0)
        sc_raw = lax.dot_general(q2, k_rows, (((1,), (1,)), ((), ())), preferred_element_type=F32)
        s2b = s2.astype(BF16)
        zs = jnp.zeros_like(s2b)
        s_bd = jnp.concatenate([jnp.where(row_lo, s2b, zs), jnp.where(row_lo, zs, s2b)], 1)
        os_raw = jnp.dot(q2, s_bd, preferred_element_type=F32)
        kw2 = (k2.astype(F32) * wend_ref[p]).astype(BF16)
        upd = lax.dot_general(kw2, v2, (((0,), (0,)), ((), ())), preferred_element_type=F32)
        yield
        zv = jnp.zeros((RET_CHUNK, RET_DV), v2.dtype)
        v_bd = jnp.concatenate([jnp.concatenate([v2[:, :RET_DV], zv], 1),
                                jnp.concatenate([zv, v2[:, RET_DV:]], 1)], 0)
        o_raw = jnp.dot((sc_raw * decay_ref[p]).astype(BF16), v_bd, preferred_element_type=F32)
        s_scr[b, p] = gl_ref[p] * s2 + jnp.where(row_lo, upd[:, :RET_DV], upd[:, RET_DV:])
        yield
        ret_out[b, :, vv] = o_raw + os_raw * rowdec_ref[p]

    kv_ctx = {}

    def swa_prep(b):
        k_cur, v_cur = sk_ref[b], sv_ref[b]
        kb, kbr = k_cur.astype(BF16), pltpu.roll(k_cur, SWA_HD, 1).astype(BF16)
        vb, vbr = v_cur.astype(BF16), pltpu.roll(v_cur, SWA_HD, 1).astype(BF16)
        kv_ctx[b] = (jnp.concatenate([kp_scr[b], kb], 0), jnp.concatenate([kpr_scr[b], kbr], 0),
                     jnp.concatenate([vp_scr[b], vb], 0), jnp.concatenate([vpr_scr[b], vbr], 0))
        kp_scr[b], kpr_scr[b], vp_scr[b], vpr_scr[b] = kb, kbr, vb, vbr

    def swa_task(b, kvh):
        if b not in kv_ctx:
            swa_prep(b)
        kc, kcr, vc, vcr = kv_ctx[b]
        zkv = jnp.zeros_like(kc)
        k_lo, k_hi = (kc, kcr) if kvh == 0 else (kcr, kc)
        v_lo, v_hi = (vc, vcr) if kvh == 0 else (vcr, vc)
        k_rows = jnp.concatenate([jnp.where(lane_lo_kv, k_lo, zkv), jnp.where(lane_lo_kv, zkv, k_hi)], 0)
        v_rows = jnp.concatenate([jnp.where(lane_lo_kv, v_lo, zkv), jnp.where(lane_lo_kv, zkv, v_hi)], 0)
        n_pairs = SWA_GROUP // 2
        pairs = [kvh * n_pairs + jj for jj in range(n_pairs)]
        q4 = jnp.concatenate([sq_ref[b, :, pr * pair_w:(pr + 1) * pair_w] for pr in pairs], 0)
        s4 = lax.dot_general(q4, k_rows, (((1,), (1,)), ((), ())), preferred_element_type=F32)
        yield
        rows, inv = [], []
        for jj, pr in enumerate(pairs):
            ps, inv_u = [], []
            for u in range(2):
                blk = s4[jj * WINDOW:(jj + 1) * WINDOW, u * 2 * WINDOW:(u + 1) * 2 * WINDOW]
                s = jnp.where(upper, blk[:, :WINDOW] + prev_bias, blk[:, WINDOW:])
                sink = sinks_ref[2 * pr + u]
                m = jnp.maximum(jnp.max(s, -1, keepdims=True), sink)
                e = jnp.exp(s - m)
                den = jnp.sum(e, -1, keepdims=True) + jnp.exp(sink - m)
                ps += [jnp.where(upper, e, 0.0).astype(BF16), jnp.where(upper, 0.0, e).astype(BF16)]
                inv_u.append(1.0 / den)
            rows.append(jnp.concatenate(ps, 1))
            inv.append(jnp.where(lane_lo, inv_u[0], inv_u[1]))
        o4 = jnp.dot(jnp.concatenate(rows, 0), v_rows, preferred_element_type=F32)
        yield
        for jj, pr in enumerate(pairs):
            swa_out[b, :, pr * pair_w:(pr + 1) * pair_w] = (o4[jj * WINDOW:(jj + 1) * WINDOW] * inv[jj]).astype(
                swa_out.dtype)

    def mem_task(b, h):
        sl = slice(h * MEM_HD, (h + 1) * MEM_HD)
        s = lax.dot_general(mq_ref[b, :, sl], mk_ref[b, :, sl], (((1,), (1,)), ((), ())),
                            preferred_element_type=F32) * (MEM_HD ** -0.5)
        yield
        m = jnp.max(s, -1, keepdims=True)
        e = jnp.exp(s - m)
        inv = 1.0 / jnp.sum(e, -1, keepdims=True)
        o = jnp.dot(e.astype(BF16), mv_ref[b, :, sl], preferred_element_type=F32)
        yield
        mem_out[b, :, sl] = (o * inv).astype(mem_out.dtype)

    head_of_row = lax.broadcasted_iota(jnp.int32, (MEM_HEADS * t_len, 1), 0) // t_len
    head_of_col = lax.broadcasted_iota(jnp.int32, (1, N_MEM * MEM_HEADS), 1) % MEM_HEADS
    valid_smem = head_of_row == head_of_col

    def smem_task(j):
        r = slice(j * t_len, (j + 1) * t_len)
        qb = smq_ref[r, :]
        q_all = jnp.concatenate([qb[:, h * MEM_HD:(h + 1) * MEM_HD] for h in range(MEM_HEADS)], 0).astype(BF16)
        s = lax.dot_general(q_all, smk_ref[0, j].astype(BF16), (((1,), (1,)), ((), ())),
                            preferred_element_type=F32) * (MEM_HD ** -0.5)
        yield
        s = jnp.where(valid_smem, s, -jnp.inf)
        m = jnp.max(s, -1, keepdims=True)
        e = jnp.exp(s - m)
        inv = 1.0 / jnp.sum(e, -1, keepdims=True)
        o = jnp.dot(e.astype(BF16), smv_ref[0, j].astype(BF16), preferred_element_type=F32) * inv
        yield
        smem_out[r, :] = jnp.concatenate([o[h * t_len:(h + 1) * t_len] for h in range(MEM_HEADS)], -1)

    n_smem = smk_ref.shape[1]
    for b in range(nb):
        _run_staged([ret_task(b, 0), swa_task(b, 0), mem_task(b, 0), ret_task(b, 1), mem_task(b, 1),
                     ret_task(b, 2), swa_task(b, 1), mem_task(b, 2), ret_task(b, 3), mem_task(b, 3)]
                    + [smem_task(j) for j in range(b, n_smem, nb)])

    @pl.when(c == pl.num_programs(0) - 1)
    def _():
        s_out[...] = s_scr[...]


def _mix_prompt(rq, rk, rv, sq, sk, sv, mq, mk_bf, mv_bf, sinks, pair_tables, smq, cache_mk, cache_mv, layer, t_len):
    nb, seq, _ = rq.shape
    n_steps = seq // RET_CHUNK
    n_smem = cache_mk.shape[1] // n_steps
    decay2, rowdec2, wend2, gl2 = pair_tables
    chunk = lambda w: pl.BlockSpec((nb, RET_CHUNK, w), lambda c: (0, c, 0))
    srow = pl.BlockSpec((n_smem * t_len, MEM_W), lambda c: (c, 0))
    skv = pl.BlockSpec((1, n_smem, N_MEM * MEM_HEADS, MEM_HD), lambda c: (layer, c, 0, 0))
    st_shape = (nb, RET_HEADS // 2, 2 * RET_DK, RET_DV)
    kv_scr = pltpu.VMEM((nb, WINDOW, SWA_KV_W), BF16)
    return pl.pallas_call(
        functools.partial(_mix_prompt_kernel, nb, t_len),
        grid=(n_steps,),
        in_specs=[pl.BlockSpec(memory_space=pltpu.SMEM),
                  chunk(RET_QK_W), chunk(RET_QK_W), chunk(RET_V_W), chunk(SWA_Q_W), chunk(SWA_KV_W), chunk(SWA_KV_W),
                  chunk(MEM_W), _const_spec(mk_bf.shape), _const_spec(mv_bf.shape),
                  _const_spec(decay2.shape), _const_spec(rowdec2.shape), _const_spec(wend2.shape),
                  _const_spec(gl2.shape), srow, skv, skv],
        out_specs=[chunk(RET_V_W), chunk(SWA_Q_W), chunk(MEM_W), _const_spec(st_shape), srow],
        out_shape=[jax.ShapeDtypeStruct((nb, seq, RET_V_W), F32), jax.ShapeDtypeStruct((nb, seq, SWA_Q_W), BF16),
                   jax.ShapeDtypeStruct((nb, seq, MEM_W), BF16), jax.ShapeDtypeStruct(st_shape, F32),
                   jax.ShapeDtypeStruct(smq.shape, F32)],
        scratch_shapes=[pltpu.VMEM(st_shape, F32), kv_scr, kv_scr, kv_scr, kv_scr],
        compiler_params=_params(1),
        name="mix_prompt",
    )(sinks, rq, rk, rv, sq, sk, sv, mq, mk_bf, mv_bf, decay2, rowdec2, wend2, gl2, smq, cache_mk, cache_mv)


def _mem_kv_kernel(x_ref, w_ref, k_ref, v_ref, kb_ref, vb_ref):
    kv = jnp.dot(x_ref[...].astype(BF16), w_ref[0], preferred_element_type=F32)
    k_ref[...] = kv[:, :MEM_W]
    v_ref[...] = kv[:, MEM_W:]
    kb_ref[...] = kv[:, :MEM_W].astype(BF16)
    vb_ref[...] = kv[:, MEM_W:].astype(BF16)


def _mem_kv(mem2d, w_bf, tm):
    w_bf, layer = w_bf
    m, k = mem2d.shape
    out = pl.BlockSpec((tm, MEM_W), lambda i: (i, 0))
    return pl.pallas_call(
        _mem_kv_kernel,
        grid=(m // tm,),
        in_specs=[pl.BlockSpec((tm, k), lambda i: (i, 0)), _layer_spec(w_bf.shape, layer)],
        out_specs=[out, out, out, out],
        out_shape=[jax.ShapeDtypeStruct((m, MEM_W), F32), jax.ShapeDtypeStruct((m, MEM_W), F32),
                   jax.ShapeDtypeStruct((m, MEM_W), BF16), jax.ShapeDtypeStruct((m, MEM_W), BF16)],
        compiler_params=_params(1),
        name="mem_kv",
    )(mem2d, w_bf)


def _layer_norm(x, g, b):
    mu = jnp.mean(x, -1, keepdims=True)
    d = x - mu
    var = jnp.mean(d * d, -1, keepdims=True)
    return d * lax.rsqrt(var + LN_EPS) * g + b


def _finish_kernel(x_ref, ret_ref, rg_ref, swa_ref, mem_ref, gr_ref, gs_ref, gm_ref, gng_ref,
                   wr_ref, ws_ref, wm_ref, wo_ref, l1g_ref, l1b_ref, wu_ref, wd_ref, l2g_ref, l2b_ref, o_ref):
    swa_b = jnp.dot(swa_ref[...].astype(BF16), ws_ref[0], preferred_element_type=F32)
    mem_b = jnp.dot(mem_ref[...].astype(BF16), wm_ref[0], preferred_element_type=F32)
    rg = rg_ref[...]
    gn = jnp.concatenate([_group_norm(ret_ref[:, h * RET_DV:(h + 1) * RET_DV], gng_ref[0, h:h + 1, :])
                          for h in range(RET_HEADS)], -1)
    ret_in = (rg * jax.nn.sigmoid(rg) * gn).astype(BF16)
    ret_b = jnp.dot(ret_in, wr_ref[0], preferred_element_type=F32)
    merged = (jax.nn.sigmoid(gr_ref[...]) * ret_b + jax.nn.sigmoid(gs_ref[...]) * swa_b
              + jax.nn.sigmoid(gm_ref[...]) * mem_b)
    y = jnp.dot(merged.astype(BF16), wo_ref[0], preferred_element_type=F32)
    x1 = _layer_norm(ALPHA * x_ref[...] + y, l1g_ref[0], l1b_ref[0])
    x1b = x1.astype(BF16)
    n_slabs = 4
    ff = D_FF // n_slabs
    up = lambda c: jnp.square(jnp.maximum(
        jnp.dot(x1b, wu_ref[0, :, c * ff:(c + 1) * ff], preferred_element_type=F32), 0.0)).astype(BF16)
    down = lambda c, h: jnp.dot(h, wd_ref[0, c * ff:(c + 1) * ff, :], preferred_element_type=F32)
    h_next = up(0)
    acc = None
    for c in range(n_slabs):
        h_cur, h_next = h_next, (up(c + 1) if c + 1 < n_slabs else None)
        d = down(c, h_cur)
        acc = d if acc is None else acc + d
    o_ref[...] = _layer_norm(ALPHA * x1 + acc, l2g_ref[0], l2b_ref[0])


def _finish(x2d, gn, rg, swa_o, mem_o, g_r, g_s, g_m, lw, tm):
    m = x2d.shape[0]
    row = lambda w: pl.BlockSpec((tm, w), lambda i: (i, 0))
    return pl.pallas_call(
        _finish_kernel,
        grid=(m // tm,),
        in_specs=[row(D_MODEL), row(RET_V_W), row(RET_V_W), row(SWA_Q_W), row(MEM_W),
                  row(D_MODEL), row(D_MODEL), row(D_MODEL)] + [_layer_spec(a.shape, idx) for a, idx in lw],
        out_specs=row(D_MODEL),
        out_shape=jax.ShapeDtypeStruct((m, D_MODEL), F32),
        compiler_params=_params(1),
        name="finish",
    )(x2d, gn, rg, swa_o, mem_o, g_r, g_s, g_m, *[a for a, _ in lw])


def _rope_tables(pos, reps=1):
    half = SWA_HD // 2
    inv = np.power(ROPE_THETA, -np.arange(half, dtype=np.float64) / half)
    ang = np.asarray(pos, np.float64)[:, None] * inv[None, :]
    c, s = np.cos(ang), np.sin(ang)
    cos_t, sin_t = np.concatenate([c, c, c, c], -1), np.concatenate([-s, s, -s, s], -1)
    return jnp.asarray(np.tile(cos_t, (reps, 1)), F32), jnp.asarray(np.tile(sin_t, (reps, 1)), F32)


def kernel(x_prompt, x_sample, state_ret, cache_swa_k, cache_swa_v, cache_mem_k, cache_mem_v, mem_prompt,
           w_in, w_br_ret, w_br_swa, w_br_mem, w_out, w_mem_kv, attn_sinks, ret_gn_g,
           ln1_g, ln1_b, w_up, w_down, ln2_g, ln2_b):
    batch, seq, _ = x_prompt.shape
    dec_b, dec_t, _ = x_sample.shape
    tm_p, tm_s = 512, 256
    tm_fin = 512
    ret_seqs = RET_CHUNK // dec_t

    cos_p, sin_p = _rope_tables(np.arange(seq))
    cos_s, sin_s = _rope_tables(PAST_LEN + np.arange(dec_t), tm_s // dec_t)
    tab_p2 = _pair_tables(_ret_tables(RET_CHUNK, RET_CHUNK))
    tab_s2 = _pair_tables(_ret_tables(RET_CHUNK, dec_t))

    xp = x_prompt.reshape(batch * seq, D_MODEL)
    xs = x_sample.reshape(dec_b * dec_t, D_MODEL)
    mem2d = mem_prompt.reshape(batch * N_MEM, D_MODEL)
    cache_kt = jnp.transpose(cache_swa_k, (0, 1, 3, 4, 2))
    cache_vt = jnp.transpose(cache_swa_v, (0, 1, 3, 4, 2))
    cache_mk = cache_mem_k.reshape(DEPTH, dec_b, N_MEM * MEM_HEADS, MEM_HD)
    cache_mv = cache_mem_v.reshape(DEPTH, dec_b, N_MEM * MEM_HEADS, MEM_HD)

    ln_row = lambda a: a.reshape(DEPTH, 1, D_MODEL)
    late_weights = (w_br_ret, w_br_swa, w_br_mem, w_out, w_up, w_down)
    next_weights = (w_in, w_mem_kv)
    w_in_bf, w_mem_kv_bf = (w_in[:1].astype(BF16), 0), (w_mem_kv[:1].astype(BF16), 0)

    ret_p, swk_p, swv_p, mk_p, mv_p = [], [], [], [], []
    ret_s, swa_s = None, None
    for l in range(DEPTH):
        sinks = attn_sinks[l]
        cast_jobs = [(w, l) for w in late_weights] + ([(w, l + 1) for w in next_weights] if l + 1 < DEPTH else [])
        proj = _inproj(xp, w_in_bf, cos_p, sin_p, tm_p, BF16, cast_jobs)
        rq, rk, rv, rg, sq, sk, sv, mq, g_r, g_s, g_m = proj[:N_INPROJ_OUT]
        s_rq, s_rk, s_rv, s_rg, s_sq, s_sk, s_sv, s_mq, s_g_r, s_g_s, s_g_m = _inproj(
            xs, w_in_bf, cos_s, sin_s, tm_s, F32)
        cast = [(w, 0) for w in proj[N_INPROJ_OUT:]]
        wr_bf, ws_bf, wm_bf, wo_bf, wu_bf, wd_bf = cast[:len(late_weights)]
        lw = [(ret_gn_g, l), wr_bf, ws_bf, wm_bf, wo_bf, (ln_row(ln1_g), l), (ln_row(ln1_b), l),
              wu_bf, wd_bf, (ln_row(ln2_g), l), (ln_row(ln2_b), l)]

        mk, mv, mk_bf, mv_bf = _mem_kv(mem2d, w_mem_kv_bf, 256)
        if l + 1 < DEPTH:
            w_in_bf, w_mem_kv_bf = cast[len(late_weights):]
        by_seq = lambda a: a.reshape(batch, seq, a.shape[-1])
        by_mem = lambda a: a.reshape(batch, N_MEM, MEM_W)
        gn, swa_o, mem_o, s_p, s_mem_o = _mix_prompt(
            by_seq(rq), by_seq(rk), by_seq(rv), by_seq(sq), by_seq(sk), by_seq(sv), by_seq(mq),
            by_mem(mk_bf), by_mem(mv_bf), sinks, tab_p2, s_mq, cache_mk, cache_mv, l, dec_t)
        flat = lambda a: a.reshape(batch * seq, a.shape[-1])
        xp = _finish(xp, flat(gn), rg, flat(swa_o), flat(mem_o), g_r, g_s, g_m, lw, tm_fin)
        ret_p.append(s_p.reshape(batch, RET_HEADS, RET_DK, RET_DV))
        swk_p.append(sk.reshape(batch, seq, SWA_KV_W)[:, -WINDOW:].reshape(batch, WINDOW, SWA_KV_HEADS, SWA_HD))
        swv_p.append(sv.reshape(batch, seq, SWA_KV_W)[:, -WINDOW:].reshape(batch, WINDOW, SWA_KV_HEADS, SWA_HD))
        mk_p.append(mk.reshape(batch, N_MEM, MEM_HEADS, MEM_HD))
        mv_p.append(mv.reshape(batch, N_MEM, MEM_HEADS, MEM_HD))

        s_gn, ret_s = _ret_sample(s_rq, s_rk, s_rv, state_ret, ret_s, l, tab_s2, dec_t, ret_seqs)
        s_swa_o, kto, vto = _swa_sample(s_sq, s_sk, s_sv, cache_kt, cache_vt, swa_s, sinks, l, dec_t,
                                        WINDOW // dec_t)
        swa_s = (kto, vto)
        xs = _finish(xs, s_gn, s_rg, s_swa_o, s_mem_o, s_g_r, s_g_s, s_g_m, lw, tm_s)

    from_t = lambda a: jnp.transpose(a, (0, 1, 4, 2, 3))
    return (xp.reshape(batch, seq, D_MODEL), xs.reshape(dec_b, dec_t, D_MODEL),
            jnp.stack(ret_p), jnp.stack(swk_p), jnp.stack(swv_p), jnp.stack(mk_p), jnp.stack(mv_p),
            ret_s, from_t(swa_s[0]), from_t(swa_s[1]))
```

```python
import functools

import jax
import jax.numpy as jnp
import numpy as np
from jax import lax
from jax.experimental import pallas as pl
from jax.experimental.pallas import tpu as pltpu

F32 = jnp.float32
BF16 = jnp.bfloat16

D_MODEL = 1024
DEPTH = 2
PAST_LEN = 16384
RET_HEADS = 8
RET_DK = 64
RET_DV = 128
RET_CHUNK = 128
SWA_HEADS = 8
SWA_KV_HEADS = 2
SWA_GROUP = SWA_HEADS // SWA_KV_HEADS
SWA_HD = 64
WINDOW = 128
MEM_HEADS = 4
MEM_HD = 128
N_MEM = 256
D_FF = 4 * D_MODEL
ROPE_THETA = 10000.0
LN_EPS = 1e-5
GN_EPS = 1e-5
ALPHA = (2 * DEPTH) ** 0.25

RET_QK_W = RET_HEADS * RET_DK
RET_V_W = RET_HEADS * RET_DV
SWA_Q_W = SWA_HEADS * SWA_HD
SWA_KV_W = SWA_KV_HEADS * SWA_HD
MEM_W = MEM_HEADS * MEM_HD
OFF_RQ = 0
OFF_RK = OFF_RQ + RET_QK_W
OFF_RV = OFF_RK + RET_QK_W
OFF_RG = OFF_RV + RET_V_W
OFF_SQ = OFF_RG + RET_V_W
OFF_SK = OFF_SQ + SWA_Q_W
OFF_SV = OFF_SK + SWA_KV_W
OFF_MQ = OFF_SV + SWA_KV_W
OFF_GR = OFF_MQ + MEM_W
OFF_GS = OFF_GR + D_MODEL
OFF_GM = OFF_GS + D_MODEL
IN_W = OFF_GM + D_MODEL

LANES = 128
V7X_VMEM_LIMIT = 62 * 1024 * 1024


def _const_spec(shape):
    nd = len(shape)
    return pl.BlockSpec(shape, lambda *_: (0,) * nd, pipeline_mode=pl.Buffered(1))


def _layer_spec(shape, layer):
    nd = len(shape)
    return pl.BlockSpec((1,) + tuple(shape[1:]), lambda *_: (layer,) + (0,) * (nd - 1), pipeline_mode=pl.Buffered(1))


def _params(n_grid):
    return pltpu.CompilerParams(dimension_semantics=("arbitrary",) * n_grid, vmem_limit_bytes=V7X_VMEM_LIMIT)


N_INPROJ_OUT = 11


def _inproj_kernel(n_cast, x_ref, w_ref, cos_ref, sin_ref, *refs):
    cast_in, outs, cast_out = refs[:n_cast], refs[n_cast:n_cast + N_INPROJ_OUT], refs[n_cast + N_INPROJ_OUT:]
    rq_ref, rk_ref, rv_ref, rg_ref, sq_ref, sk_ref, sv_ref, mq_ref, gr_ref, gs_ref, gm_ref = outs
    for src, dst in zip(cast_in, cast_out):
        dst[...] = src[...].astype(dst.dtype)
    xb = x_ref[...].astype(BF16)
    cos = cos_ref[...]
    sin = sin_ref[...]
    lane = lax.broadcasted_iota(jnp.int32, cos.shape, 1)
    first_half = (lane & (SWA_HD // 2)) == 0

    def proj(off, width):
        return jnp.dot(xb, w_ref[0, :, off:off + width], preferred_element_type=F32)

    def rope_store(off, width, out_ref, scale):
        y = proj(off, width)
        for j in range(width // LANES):
            yj = y[:, j * LANES:(j + 1) * LANES]
            sw = jnp.where(first_half, pltpu.roll(yj, LANES - SWA_HD // 2, 1), pltpu.roll(yj, SWA_HD // 2, 1))
            r = yj * cos + sw * sin
            if scale != 1.0:
                r = r * scale
            out_ref[:, j * LANES:(j + 1) * LANES] = r.astype(out_ref.dtype)

    def plain_store(off, width, out_ref):
        out_ref[...] = proj(off, width).astype(out_ref.dtype)

    rope_store(OFF_RQ, RET_QK_W, rq_ref, 1.0)
    rope_store(OFF_RK, RET_QK_W, rk_ref, RET_DK ** -0.5)
    plain_store(OFF_RV, RET_V_W, rv_ref)
    plain_store(OFF_RG, RET_V_W, rg_ref)
    rope_store(OFF_SQ, SWA_Q_W, sq_ref, SWA_HD ** -0.5)
    rope_store(OFF_SK, SWA_KV_W, sk_ref, 1.0)
    plain_store(OFF_SV, SWA_KV_W, sv_ref)
    plain_store(OFF_MQ, MEM_W, mq_ref)
    plain_store(OFF_GR, D_MODEL, gr_ref)
    plain_store(OFF_GS, D_MODEL, gs_ref)
    plain_store(OFF_GM, D_MODEL, gm_ref)


def _inproj(x2d, w_bf, cos_tab, sin_tab, tm, qkv_dtype, cast_jobs=()):
    m = x2d.shape[0]
    n_steps = m // tm
    n_tab = cos_tab.shape[0] // tm
    row = lambda w: pl.BlockSpec((tm, w), lambda i: (i, 0))
    tab = pl.BlockSpec((tm, LANES), lambda i: (i % n_tab, 0))
    widths_dtypes = [(RET_QK_W, qkv_dtype), (RET_QK_W, qkv_dtype), (RET_V_W, qkv_dtype), (RET_V_W, F32),
                     (SWA_Q_W, qkv_dtype), (SWA_KV_W, F32), (SWA_KV_W, F32), (MEM_W, qkv_dtype),
                     (D_MODEL, F32), (D_MODEL, F32), (D_MODEL, F32)]
    assert len(widths_dtypes) == N_INPROJ_OUT
    slab = lambda a: (1, a.shape[1] // n_steps, a.shape[2])
    cast_in = [pl.BlockSpec(slab(a), lambda i, layer=layer: (layer, i, 0)) for a, layer in cast_jobs]
    cast_out = [pl.BlockSpec(slab(a), lambda i: (0, i, 0)) for a, _ in cast_jobs]
    return pl.pallas_call(
        functools.partial(_inproj_kernel, len(cast_jobs)),
        grid=(n_steps,),
        in_specs=[row(D_MODEL), _layer_spec(w_bf[0].shape, w_bf[1]), tab, tab] + cast_in,
        out_specs=[row(w) for w, _ in widths_dtypes] + cast_out,
        out_shape=([jax.ShapeDtypeStruct((m, w), dt) for w, dt in widths_dtypes]
                   + [jax.ShapeDtypeStruct((1,) + a.shape[1:], BF16) for a, _ in cast_jobs]),
        compiler_params=_params(1),
        name="inproj",
    )(x2d, w_bf[0], cos_tab, sin_tab, *[a for a, _ in cast_jobs])


def _group_norm(o, g_row):
    mu = jnp.mean(o, -1, keepdims=True)
    d = o - mu
    var = jnp.mean(d * d, -1, keepdims=True)
    return d * lax.rsqrt(var + GN_EPS) * g_row


def _ret_tables(n_rows, period):
    lg = np.log1p(-np.exp2(-5.0 - np.arange(RET_HEADS, dtype=np.float64)))
    r = np.arange(n_rows)
    t = (r % period).astype(np.float64)
    same = (r[:, None] // period) == (r[None, :] // period)
    diff = t[:, None] - t[None, :]
    decay = np.where((diff >= 0) & same, np.exp(lg[:, None, None] * np.maximum(diff, 0.0)), 0.0)
    rowdec = np.exp(lg[:, None] * (t[None, :] + 1.0))
    wend = np.exp(lg[:, None] * (period - 1.0 - t[None, :]))
    gl = np.exp(lg * period)
    rowdec = np.broadcast_to(rowdec[:, :, None], (RET_HEADS, n_rows, RET_DV))
    wend = np.broadcast_to(wend[:, :, None], (RET_HEADS, n_rows, RET_DK))
    gl = np.broadcast_to(gl[:, None, None], (RET_HEADS, 1, RET_DV))
    return decay, rowdec, wend, gl


def _ret_sample_kernel(n_seq, t_len, q_ref, k_ref, v_ref, s_ref, decay_ref, rowdec_ref, wend_ref, gl_ref,
                       o_ref, s_out_ref):
    rows = n_seq * t_len
    pair_dk, pair_dv = 2 * RET_DK, 2 * RET_DV
    lane_lo = lax.broadcasted_iota(jnp.int32, (rows, pair_dk), 1) < RET_DK
    row_lo = lax.broadcasted_iota(jnp.int32, (pair_dk, RET_DV), 0) < RET_DK
    for p in range(RET_HEADS // 2):
        qk = slice(p * pair_dk, (p + 1) * pair_dk)
        vv = slice(p * pair_dv, (p + 1) * pair_dv)
        q2f, k2f, v2f = q_ref[:, qk], k_ref[:, qk], v_ref[:, vv]
        q2, k2, v2 = q2f.astype(BF16), k2f.astype(BF16), v2f.astype(BF16)
        zk = jnp.zeros_like(k2)
        k_rows = jnp.concatenate([jnp.where(lane_lo, k2, zk), jnp.where(lane_lo, zk, k2)], 0)
        sc2 = lax.dot_general(q2, k_rows, (((1,), (1,)), ((), ())), preferred_element_type=F32) * decay_ref[p]
        zv = jnp.zeros((rows, RET_DV), BF16)
        v_bd = jnp.concatenate([jnp.concatenate([v2[:, :RET_DV], zv], 1),
                                jnp.concatenate([zv, v2[:, RET_DV:]], 1)], 0)
        kw2f = k2f * wend_ref[p]
        o_state = []
        for b in range(n_seq):
            r = slice(b * t_len, (b + 1) * t_len)
            s2 = s_ref[0, b, 2 * p:2 * p + 2].reshape(pair_dk, RET_DV)
            s2b = s2.astype(BF16)
            zs = jnp.zeros_like(s2b)
            s_bd = jnp.concatenate([jnp.where(row_lo, s2b, zs), jnp.where(row_lo, zs, s2b)], 1)
            o_state.append(jnp.dot(q2f[r].astype(BF16), s_bd, preferred_element_type=F32))
            upd = lax.dot_general(kw2f[r].astype(BF16), v2f[r].astype(BF16), (((0,), (0,)), ((), ())),
                                  preferred_element_type=F32)
            s_new = (gl_ref[p] * s2 + jnp.where(row_lo, upd[:, :RET_DV], upd[:, RET_DV:])).reshape(2, RET_DK, RET_DV)
            for d in range(s_out_ref.shape[0]):
                s_out_ref[d, b, 2 * p:2 * p + 2] = s_new
        o2 = jnp.dot(sc2.astype(BF16), v_bd, preferred_element_type=F32)
        o_ref[:, vv] = o2 + jnp.concatenate(o_state, 0) * rowdec_ref[p]


def _stacked_out_specs(shape, layer, n_seq):
    tail = tuple(shape[2:])
    zeros = (0,) * len(tail)
    if layer == 0:
        return pl.BlockSpec((shape[0], n_seq) + tail, lambda i: (0, i) + zeros)
    return pl.BlockSpec((1, n_seq) + tail, lambda i: (layer, i) + zeros)


def _swa_sample_kernel(n_seq, t_len, sinks_ref, q_ref, kn_ref, vn_ref, kt_ref, vt_ref, *rest):
    o_ref, kto_ref, vto_ref = rest[-3:]
    grp_rows = SWA_GROUP * t_len
    n_all = n_seq * grp_rows
    q = q_ref[...]
    kn = kn_ref[...]
    vn = vn_ref[...]
    kn_t = kn.T
    vn_t = vn.T
    r = lax.broadcasted_iota(jnp.int32, (n_all, 1), 0)
    t_q = r % t_len
    g_row = (r // t_len) % SWA_GROUP
    b_row = r // grp_rows
    c = lax.broadcasted_iota(jnp.int32, (1, WINDOW), 1)
    valid_cache = c > t_q
    valid_new = ((c // t_len) == b_row) & ((c % t_len) <= t_q)
    lane = lax.broadcasted_iota(jnp.int32, (SWA_HD, WINDOW), 1)
    is_new_lane = lane >= WINDOW - t_len
    head_dims = [slice(kvh * SWA_HD, (kvh + 1) * SWA_HD) for kvh in range(SWA_KV_HEADS)]
    scores = []
    for kvh, hd in enumerate(head_dims):
        qg = [q[:, (kvh * SWA_GROUP + g) * SWA_HD:(kvh * SWA_GROUP + g + 1) * SWA_HD] for g in range(SWA_GROUP)]
        q_all = jnp.concatenate([qg[g][b * t_len:(b + 1) * t_len] for b in range(n_seq) for g in range(SWA_GROUP)],
                                0).astype(BF16)
        s_new = jnp.dot(q_all, kn_t[hd].astype(BF16), preferred_element_type=F32)
        s_cache = jnp.concatenate(
            [jnp.dot(q_all[b * grp_rows:(b + 1) * grp_rows], kt_ref[0, b, kvh].astype(BF16),
                     preferred_element_type=F32) for b in range(n_seq)], 0)
        scores.append((s_new, s_cache))
    probs = []
    for kvh, (s_new, s_cache) in enumerate(scores):
        s_new = jnp.where(valid_new, s_new, -jnp.inf)
        s_cache = jnp.where(valid_cache, s_cache, -jnp.inf)
        sink = jnp.full((n_all, 1), sinks_ref[kvh * SWA_GROUP], F32)
        for g in range(1, SWA_GROUP):
            sink = jnp.where(g_row == g, sinks_ref[kvh * SWA_GROUP + g], sink)
        m = jnp.maximum(jnp.maximum(jnp.max(s_new, -1, keepdims=True), jnp.max(s_cache, -1, keepdims=True)), sink)
        e_new = jnp.exp(s_new - m)
        e_cache = jnp.exp(s_cache - m)
        den = jnp.sum(e_new, -1, keepdims=True) + jnp.sum(e_cache, -1, keepdims=True) + jnp.exp(sink - m)
        probs.append(((e_new / den).astype(BF16), (e_cache / den).astype(BF16)))
    pieces = []
    for kvh, hd in enumerate(head_dims):
        p_new, p_cache = probs[kvh]
        o = jnp.dot(p_new, vn[:, hd].astype(BF16), preferred_element_type=F32)
        o = o + jnp.concatenate(
            [lax.dot_general(p_cache[b * grp_rows:(b + 1) * grp_rows], vt_ref[0, b, kvh].astype(BF16),
                             (((1,), (1,)), ((), ())), preferred_element_type=F32) for b in range(n_seq)], 0)
        for g in range(SWA_GROUP):
            pieces.append(jnp.concatenate(
                [o[b * grp_rows + g * t_len:b * grp_rows + (g + 1) * t_len] for b in range(n_seq)], 0))
    for kvh, hd in enumerate(head_dims):
        for b in range(n_seq):
            shift_new = (WINDOW - t_len - b * t_len) % WINDOW
            k_slid = jnp.where(is_new_lane, pltpu.roll(kn_t[hd], shift_new, 1),
                               pltpu.roll(kt_ref[0, b, kvh], WINDOW - t_len, 1))
            v_slid = jnp.where(is_new_lane, pltpu.roll(vn_t[hd], shift_new, 1),
                               pltpu.roll(vt_ref[0, b, kvh], WINDOW - t_len, 1))
            for d in range(kto_ref.shape[0]):
                kto_ref[d, b, kvh] = k_slid
                vto_ref[d, b, kvh] = v_slid
    o_ref[...] = jnp.concatenate(pieces, -1).astype(o_ref.dtype)


N_MIX_SAMPLE_IN = 14


def _mix_sample_kernel(n_seq, t_len, sinks_ref, rq_ref, rk_ref, rv_ref, s_ref, decay_ref, rowdec_ref, wend_ref,
                       gl_ref, sq_ref, kn_ref, vn_ref, kt_ref, vt_ref, *rest):
    ret_o, s_out, swa_o, kto, vto = rest[-5:]
    _ret_sample_kernel(n_seq, t_len, rq_ref, rk_ref, rv_ref, s_ref, decay_ref, rowdec_ref, wend_ref, gl_ref,
                       ret_o, s_out)
    _swa_sample_kernel(n_seq, t_len, sinks_ref, sq_ref, kn_ref, vn_ref, kt_ref, vt_ref, swa_o, kto, vto)


def _mix_sample(rq, rk, rv, state, pair_tables, sq, sk, sv, cache_kt, cache_vt, sinks, prev_out, layer, t_len, n_seq):
    m = rq.shape[0]
    rows = n_seq * t_len
    assert rows == WINDOW and cache_kt.shape[-1] == WINDOW
    decay2, rowdec2, wend2, gl2 = pair_tables
    row = lambda w: pl.BlockSpec((rows, w), lambda i: (i, 0))
    st_in = pl.BlockSpec((1, n_seq, RET_HEADS, RET_DK, RET_DV), lambda i: (layer, i, 0, 0, 0))
    cin = pl.BlockSpec((1, n_seq, SWA_KV_HEADS, SWA_HD, WINDOW), lambda i: (layer, i, 0, 0, 0))
    in_specs = [pl.BlockSpec(memory_space=pltpu.SMEM), row(RET_QK_W), row(RET_QK_W), row(RET_V_W), st_in,
                _const_spec(decay2.shape), _const_spec(rowdec2.shape), _const_spec(wend2.shape),
                _const_spec(gl2.shape), row(SWA_Q_W), row(SWA_KV_W), row(SWA_KV_W), cin, cin]
    args = [sinks, rq, rk, rv, state, decay2, rowdec2, wend2, gl2, sq, sk, sv, cache_kt, cache_vt]
    assert len(args) == N_MIX_SAMPLE_IN
    stacked = (state, cache_kt, cache_vt)
    aliases = {}
    if prev_out is not None:
        in_specs += [pl.BlockSpec(memory_space=pl.ANY)] * len(stacked)
        args += list(prev_out)
        aliases = {N_MIX_SAMPLE_IN: 1, N_MIX_SAMPLE_IN + 1: 3, N_MIX_SAMPLE_IN + 2: 4}
    st_out, k_out, v_out = (_stacked_out_specs(a.shape, layer, n_seq) for a in stacked)
    shape = lambda a: jax.ShapeDtypeStruct(a.shape, F32)
    ret_o, s_new, swa_o, k_new, v_new = pl.pallas_call(
        functools.partial(_mix_sample_kernel, n_seq, t_len),
        grid=(m // rows,),
        in_specs=in_specs,
        out_specs=[row(RET_V_W), st_out, row(SWA_Q_W), k_out, v_out],
        out_shape=[jax.ShapeDtypeStruct((m, RET_V_W), F32), shape(state),
                   jax.ShapeDtypeStruct((m, SWA_Q_W), BF16), shape(cache_kt), shape(cache_vt)],
        input_output_aliases=aliases,
        compiler_params=_params(1),
        name="mix_sample",
    )(*args)
    return ret_o, swa_o, (s_new, k_new, v_new)


def _pair_tables(tables):
    decay, rowdec, wend, gl = tables
    pair = lambda a: np.concatenate([a[0::2], a[1::2]], -1)
    gl_rows = np.concatenate([np.broadcast_to(gl[0::2], (RET_HEADS // 2, RET_DK, RET_DV)),
                              np.broadcast_to(gl[1::2], (RET_HEADS // 2, RET_DK, RET_DV))], 1)
    return tuple(jnp.asarray(a, F32) for a in (pair(decay), pair(rowdec), pair(wend), gl_rows))


def _run_staged(tasks):
    active = list(tasks)
    while active:
        for t in list(active):
            try:
                next(t)
            except StopIteration:
                active.remove(t)


def _mix_prompt_kernel(nb, t_len, sinks_ref, rq_ref, rk_ref, rv_ref, sq_ref, sk_ref, sv_ref, mq_ref, mk_ref, mv_ref,
                       decay_ref, rowdec_ref, wend_ref, gl_ref, smq_ref, smk_ref, smv_ref,
                       ret_out, swa_out, mem_out, s_out, smem_out,
                       s_scr, kp_scr, kpr_scr, vp_scr, vpr_scr):
    c = pl.program_id(0)

    @pl.when(c == 0)
    def _():
        s_scr[...] = jnp.zeros_like(s_scr)
        for scr in (kp_scr, kpr_scr, vp_scr, vpr_scr):
            scr[...] = jnp.zeros_like(scr)

    pair_w = 2 * SWA_HD
    lane_lo = lax.broadcasted_iota(jnp.int32, (RET_CHUNK, pair_w), 1) < SWA_HD
    row_lo = lax.broadcasted_iota(jnp.int32, (2 * RET_DK, RET_DV), 0) < RET_DK
    lane_lo_kv = lax.broadcasted_iota(jnp.int32, (2 * WINDOW, pair_w), 1) < SWA_HD
    upper = (lax.broadcasted_iota(jnp.int32, (WINDOW, WINDOW), 1)
             > lax.broadcasted_iota(jnp.int32, (WINDOW, WINDOW), 0))
    prev_bias = jnp.where(c > 0, 0.0, -jnp.inf)


    def ret_task(b, p):
        qk = slice(p * 2 * RET_DK, (p + 1) * 2 * RET_DK)
        vv = slice(p * 2 * RET_DV, (p + 1) * 2 * RET_DV)
        q2, k2, v2, s2 = rq_ref[b, :, qk], rk_ref[b, :, qk], rv_ref[b, :, vv], s_scr[b, p]
        zk = jnp.zeros_like(k2)
        k_rows = jnp.concatenate([jnp.where(lane_lo, k2, zk), jnp.where(lane_lo, zk, k2)], 0)
        sc_raw = lax.dot_general(q2, k_rows, (((1,), (1,)), ((), ())), preferred_element_type=F32)
        s2b = s2.astype(BF16)
        zs = jnp.zeros_like(s2b)
        s_bd = jnp.concatenate([jnp.where(row_lo, s2b, zs), jnp.where(row_lo, zs, s2b)], 1)
        os_raw = jnp.dot(q2, s_bd, preferred_element_type=F32)
        kw2 = (k2.astype(F32) * wend_ref[p]).astype(BF16)
        upd = lax.dot_general(kw2, v2, (((0,), (0,)), ((), ())), preferred_element_type=F32)
        yield
        zv = jnp.zeros((RET_CHUNK, RET_DV), v2.dtype)
        v_bd = jnp.concatenate([jnp.concatenate([v2[:, :RET_DV], zv], 1),
                                jnp.concatenate([zv, v2[:, RET_DV:]], 1)], 0)
        o_raw = jnp.dot((sc_raw * decay_ref[p]).astype(BF16), v_bd, preferred_element_type=F32)
        s_scr[b, p] = gl_ref[p] * s2 + jnp.where(row_lo, upd[:, :RET_DV], upd[:, RET_DV:])
        yield
        ret_out[b, :, vv] = o_raw + os_raw * rowdec_ref[p]

    kv_ctx = {}

    def swa_prep(b):
        k_cur, v_cur = sk_ref[b], sv_ref[b]
        kb, kbr = k_cur.astype(BF16), pltpu.roll(k_cur, SWA_HD, 1).astype(BF16)
        vb, vbr = v_cur.astype(BF16), pltpu.roll(v_cur, SWA_HD, 1).astype(BF16)
        kv_ctx[b] = (jnp.concatenate([kp_scr[b], kb], 0), jnp.concatenate([kpr_scr[b], kbr], 0),
                     jnp.concatenate([vp_scr[b], vb], 0), jnp.concatenate([vpr_scr[b], vbr], 0))
        kp_scr[b], kpr_scr[b], vp_scr[b], vpr_scr[b] = kb, kbr, vb, vbr

    def swa_task(b, kvh):
        if b not in kv_ctx:
            swa_prep(b)
        kc, kcr, vc, vcr = kv_ctx[b]
        zkv = jnp.zeros_like(kc)
        k_lo, k_hi = (kc, kcr) if kvh == 0 else (kcr, kc)
        v_lo, v_hi = (vc, vcr) if kvh == 0 else (vcr, vc)
        k_rows = jnp.concatenate([jnp.where(lane_lo_kv, k_lo, zkv), jnp.where(lane_lo_kv, zkv, k_hi)], 0)
        v_rows = jnp.concatenate([jnp.where(lane_lo_kv, v_lo, zkv), jnp.where(lane_lo_kv, zkv, v_hi)], 0)
        n_pairs = SWA_GROUP // 2
        pairs = [kvh * n_pairs + jj for jj in range(n_pairs)]
        q4 = jnp.concatenate([sq_ref[b, :, pr * pair_w:(pr + 1) * pair_w] for pr in pairs], 0)
        s4 = lax.dot_general(q4, k_rows, (((1,), (1,)), ((), ())), preferred_element_type=F32)
        yield
        rows, inv = [], []
        for jj, pr in enumerate(pairs):
            ps, inv_u = [], []
            for u in range(2):
                blk = s4[jj * WINDOW:(jj + 1) * WINDOW, u * 2 * WINDOW:(u + 1) * 2 * WINDOW]
                s = jnp.where(upper, blk[:, :WINDOW] + prev_bias, blk[:, WINDOW:])
                sink = sinks_ref[2 * pr + u]
                m = jnp.maximum(jnp.max(s, -1, keepdims=True), sink)
                e = jnp.exp(s - m)
                den = jnp.sum(e, -1, keepdims=True) + jnp.exp(sink - m)
                ps += [jnp.where(upper, e, 0.0).astype(BF16), jnp.where(upper, 0.0, e).astype(BF16)]
                inv_u.append(1.0 / den)
            rows.append(jnp.concatenate(ps, 1))
            inv.append(jnp.where(lane_lo, inv_u[0], inv_u[1]))
        o4 = jnp.dot(jnp.concatenate(rows, 0), v_rows, preferred_element_type=F32)
        yield
        for jj, pr in enumerate(pairs):
            swa_out[b, :, pr * pair_w:(pr + 1) * pair_w] = (o4[jj * WINDOW:(jj + 1) * WINDOW] * inv[jj]).astype(
                swa_out.dtype)

    def mem_task(b, h):
        sl = slice(h * MEM_HD, (h + 1) * MEM_HD)
        s = lax.dot_general(mq_ref[b, :, sl], mk_ref[b, :, sl], (((1,), (1,)), ((), ())),
                            preferred_element_type=F32) * (MEM_HD ** -0.5)
        yield
        m = jnp.max(s, -1, keepdims=True)
        e = jnp.exp(s - m)
        inv = 1.0 / jnp.sum(e, -1, keepdims=True)
        o = jnp.dot(e.astype(BF16), mv_ref[b, :, sl], preferred_element_type=F32)
        yield
        mem_out[b, :, sl] = (o * inv).astype(mem_out.dtype)

    head_of_row = lax.broadcasted_iota(jnp.int32, (MEM_HEADS * t_len, 1), 0) // t_len
    head_of_col = lax.broadcasted_iota(jnp.int32, (1, N_MEM * MEM_HEADS), 1) % MEM_HEADS
    valid_smem = head_of_row == head_of_col

    def smem_task(j):
        r = slice(j * t_len, (j + 1) * t_len)
        qb = smq_ref[r, :]
        q_all = jnp.concatenate([qb[:, h * MEM_HD:(h + 1) * MEM_HD] for h in range(MEM_HEADS)], 0).astype(BF16)
        s = lax.dot_general(q_all, smk_ref[0, j].astype(BF16), (((1,), (1,)), ((), ())),
                            preferred_element_type=F32) * (MEM_HD ** -0.5)
        yield
        s = jnp.where(valid_smem, s, -jnp.inf)
        m = jnp.max(s, -1, keepdims=True)
        e = jnp.exp(s - m)
        inv = 1.0 / jnp.sum(e, -1, keepdims=True)
        o = jnp.dot(e.astype(BF16), smv_ref[0, j].astype(BF16), preferred_element_type=F32) * inv
        yield
        smem_out[r, :] = jnp.concatenate([o[h * t_len:(h + 1) * t_len] for h in range(MEM_HEADS)], -1)

    n_smem = smk_ref.shape[1]
    for b in range(nb):
        _run_staged([ret_task(b, 0), swa_task(b, 0), mem_task(b, 0), ret_task(b, 1), mem_task(b, 1),
                     ret_task(b, 2), swa_task(b, 1), mem_task(b, 2), ret_task(b, 3), mem_task(b, 3)]
                    + [smem_task(j) for j in range(b, n_smem, nb)])

    @pl.when(c == pl.num_programs(0) - 1)
    def _():
        s_out[...] = s_scr[...]


def _mix_prompt(rq, rk, rv, sq, sk, sv, mq, mk_bf, mv_bf, sinks, pair_tables, smq, cache_mk, cache_mv, layer, t_len):
    nb, seq, _ = rq.shape
    n_steps = seq // RET_CHUNK
    n_smem = cache_mk.shape[1] // n_steps
    decay2, rowdec2, wend2, gl2 = pair_tables
    chunk = lambda w: pl.BlockSpec((nb, RET_CHUNK, w), lambda c: (0, c, 0))
    srow = pl.BlockSpec((n_smem * t_len, MEM_W), lambda c: (c, 0))
    skv = pl.BlockSpec((1, n_smem, N_MEM * MEM_HEADS, MEM_HD), lambda c: (layer, c, 0, 0))
    st_shape = (nb, RET_HEADS // 2, 2 * RET_DK, RET_DV)
    kv_scr = pltpu.VMEM((nb, WINDOW, SWA_KV_W), BF16)
    return pl.pallas_call(
        functools.partial(_mix_prompt_kernel, nb, t_len),
        grid=(n_steps,),
        in_specs=[pl.BlockSpec(memory_space=pltpu.SMEM),
                  chunk(RET_QK_W), chunk(RET_QK_W), chunk(RET_V_W), chunk(SWA_Q_W), chunk(SWA_KV_W), chunk(SWA_KV_W),
                  chunk(MEM_W), _const_spec(mk_bf.shape), _const_spec(mv_bf.shape),
                  _const_spec(decay2.shape), _const_spec(rowdec2.shape), _const_spec(wend2.shape),
                  _const_spec(gl2.shape), srow, skv, skv],
        out_specs=[chunk(RET_V_W), chunk(SWA_Q_W), chunk(MEM_W), _const_spec(st_shape), srow],
        out_shape=[jax.ShapeDtypeStruct((nb, seq, RET_V_W), F32), jax.ShapeDtypeStruct((nb, seq, SWA_Q_W), BF16),
                   jax.ShapeDtypeStruct((nb, seq, MEM_W), BF16), jax.ShapeDtypeStruct(st_shape, F32),
                   jax.ShapeDtypeStruct(smq.shape, F32)],
        scratch_shapes=[pltpu.VMEM(st_shape, F32), kv_scr, kv_scr, kv_scr, kv_scr],
        compiler_params=_params(1),
        name="mix_prompt",
    )(sinks, rq, rk, rv, sq, sk, sv, mq, mk_bf, mv_bf, decay2, rowdec2, wend2, gl2, smq, cache_mk, cache_mv)


def _mem_kv_kernel(x_ref, w_ref, k_ref, v_ref, kb_ref, vb_ref):
    kv = jnp.dot(x_ref[...].astype(BF16), w_ref[0], preferred_element_type=F32)
    k_ref[...] = kv[:, :MEM_W]
    v_ref[...] = kv[:, MEM_W:]
    kb_ref[...] = kv[:, :MEM_W].astype(BF16)
    vb_ref[...] = kv[:, MEM_W:].astype(BF16)


def _mem_kv(mem2d, w_bf, tm):
    w_bf, layer = w_bf
    m, k = mem2d.shape
    out = pl.BlockSpec((tm, MEM_W), lambda i: (i, 0))
    return pl.pallas_call(
        _mem_kv_kernel,
        grid=(m // tm,),
        in_specs=[pl.BlockSpec((tm, k), lambda i: (i, 0)), _layer_spec(w_bf.shape, layer)],
        out_specs=[out, out, out, out],
        out_shape=[jax.ShapeDtypeStruct((m, MEM_W), F32), jax.ShapeDtypeStruct((m, MEM_W), F32),
                   jax.ShapeDtypeStruct((m, MEM_W), BF16), jax.ShapeDtypeStruct((m, MEM_W), BF16)],
        compiler_params=_params(1),
        name="mem_kv",
    )(mem2d, w_bf)


def _layer_norm(x, g, b):
    mu = jnp.mean(x, -1, keepdims=True)
    d = x - mu
    var = jnp.mean(d * d, -1, keepdims=True)
    return d * lax.rsqrt(var + LN_EPS) * g + b


def _finish_kernel(x_ref, ret_ref, rg_ref, swa_ref, mem_ref, gr_ref, gs_ref, gm_ref, gng_ref,
                   wr_ref, ws_ref, wm_ref, wo_ref, l1g_ref, l1b_ref, wu_ref, wd_ref, l2g_ref, l2b_ref, o_ref):
    swa_b = jnp.dot(swa_ref[...].astype(BF16), ws_ref[0], preferred_element_type=F32)
    mem_b = jnp.dot(mem_ref[...].astype(BF16), wm_ref[0], preferred_element_type=F32)
    rg = rg_ref[...]
    gn = jnp.concatenate([_group_norm(ret_ref[:, h * RET_DV:(h + 1) * RET_DV], gng_ref[0, h:h + 1, :])
                          for h in range(RET_HEADS)], -1)
    ret_in = (rg * jax.nn.sigmoid(rg) * gn).astype(BF16)
    ret_b = jnp.dot(ret_in, wr_ref[0], preferred_element_type=F32)
    merged = (jax.nn.sigmoid(gr_ref[...]) * ret_b + jax.nn.sigmoid(gs_ref[...]) * swa_b
              + jax.nn.sigmoid(gm_ref[...]) * mem_b)
    y = jnp.dot(merged.astype(BF16), wo_ref[0], preferred_element_type=F32)
    x1 = _layer_norm(ALPHA * x_ref[...] + y, l1g_ref[0], l1b_ref[0])
    x1b = x1.astype(BF16)
    n_slabs = 4
    ff = D_FF // n_slabs
    up = lambda c: jnp.square(jnp.maximum(
        jnp.dot(x1b, wu_ref[0, :, c * ff:(c + 1) * ff], preferred_element_type=F32), 0.0)).astype(BF16)
    down = lambda c, h: jnp.dot(h, wd_ref[0, c * ff:(c + 1) * ff, :], preferred_element_type=F32)
    h_next = up(0)
    acc = None
    for c in range(n_slabs):
        h_cur, h_next = h_next, (up(c + 1) if c + 1 < n_slabs else None)
        d = down(c, h_cur)
        acc = d if acc is None else acc + d
    o_ref[...] = _layer_norm(ALPHA * x1 + acc, l2g_ref[0], l2b_ref[0])


def _finish(x2d, gn, rg, swa_o, mem_o, g_r, g_s, g_m, lw, tm):
    m = x2d.shape[0]
    row = lambda w: pl.BlockSpec((tm, w), lambda i: (i, 0))
    return pl.pallas_call(
        _finish_kernel,
        grid=(m // tm,),
        in_specs=[row(D_MODEL), row(RET_V_W), row(RET_V_W), row(SWA_Q_W), row(MEM_W),
                  row(D_MODEL), row(D_MODEL), row(D_MODEL)] + [_layer_spec(a.shape, idx) for a, idx in lw],
        out_specs=row(D_MODEL),
        out_shape=jax.ShapeDtypeStruct((m, D_MODEL), F32),
        compiler_params=_params(1),
        name="finish",
    )(x2d, gn, rg, swa_o, mem_o, g_r, g_s, g_m, *[a for a, _ in lw])


def _rope_tables(pos, reps=1):
    half = SWA_HD // 2
    inv = np.power(ROPE_THETA, -np.arange(half, dtype=np.float64) / half)
    ang = np.asarray(pos, np.float64)[:, None] * inv[None, :]
    c, s = np.cos(ang), np.sin(ang)
    cos_t, sin_t = np.concatenate([c, c, c, c], -1), np.concatenate([-s, s, -s, s], -1)
    return jnp.asarray(np.tile(cos_t, (reps, 1)), F32), jnp.asarray(np.tile(sin_t, (reps, 1)), F32)


def kernel(x_prompt, x_sample, state_ret, cache_swa_k, cache_swa_v, cache_mem_k, cache_mem_v, mem_prompt,
           w_in, w_br_ret, w_br_swa, w_br_mem, w_out, w_mem_kv, attn_sinks, ret_gn_g,
           ln1_g, ln1_b, w_up, w_down, ln2_g, ln2_b):
    batch, seq, _ = x_prompt.shape
    dec_b, dec_t, _ = x_sample.shape
    tm_p, tm_s = 512, 256
    tm_fin = 512
    ret_seqs = RET_CHUNK // dec_t

    cos_p, sin_p = _rope_tables(np.arange(seq))
    cos_s, sin_s = _rope_tables(PAST_LEN + np.arange(dec_t), tm_s // dec_t)
    tab_p2 = _pair_tables(_ret_tables(RET_CHUNK, RET_CHUNK))
    tab_s2 = _pair_tables(_ret_tables(RET_CHUNK, dec_t))

    xp = x_prompt.reshape(batch * seq, D_MODEL)
    xs = x_sample.reshape(dec_b * dec_t, D_MODEL)
    mem2d = mem_prompt.reshape(batch * N_MEM, D_MODEL)
    cache_kt = jnp.transpose(cache_swa_k, (0, 1, 3, 4, 2))
    cache_vt = jnp.transpose(cache_swa_v, (0, 1, 3, 4, 2))
    cache_mk = cache_mem_k.reshape(DEPTH, dec_b, N_MEM * MEM_HEADS, MEM_HD)
    cache_mv = cache_mem_v.reshape(DEPTH, dec_b, N_MEM * MEM_HEADS, MEM_HD)

    ln_row = lambda a: a.reshape(DEPTH, 1, D_MODEL)
    late_weights = (w_br_ret, w_br_swa, w_br_mem, w_out, w_up, w_down)
    next_weights = (w_in, w_mem_kv)
    w_in_bf, w_mem_kv_bf = (w_in[:1].astype(BF16), 0), (w_mem_kv[:1].astype(BF16), 0)

    ret_p, swk_p, swv_p, mk_p, mv_p = [], [], [], [], []
    stacked_s = None
    for l in range(DEPTH):
        sinks = attn_sinks[l]
        cast_jobs = [(w, l) for w in late_weights] + ([(w, l + 1) for w in next_weights] if l + 1 < DEPTH else [])
        proj = _inproj(xp, w_in_bf, cos_p, sin_p, tm_p, BF16, cast_jobs)
        rq, rk, rv, rg, sq, sk, sv, mq, g_r, g_s, g_m = proj[:N_INPROJ_OUT]
        s_rq, s_rk, s_rv, s_rg, s_sq, s_sk, s_sv, s_mq, s_g_r, s_g_s, s_g_m = _inproj(
            xs, w_in_bf, cos_s, sin_s, tm_s, F32)
        cast = [(w, 0) for w in proj[N_INPROJ_OUT:]]
        wr_bf, ws_bf, wm_bf, wo_bf, wu_bf, wd_bf = cast[:len(late_weights)]
        lw = [(ret_gn_g, l), wr_bf, ws_bf, wm_bf, wo_bf, (ln_row(ln1_g), l), (ln_row(ln1_b), l),
              wu_bf, wd_bf, (ln_row(ln2_g), l), (ln_row(ln2_b), l)]

        mk, mv, mk_bf, mv_bf = _mem_kv(mem2d, w_mem_kv_bf, 256)
        if l + 1 < DEPTH:
            w_in_bf, w_mem_kv_bf = cast[len(late_weights):]
        by_seq = lambda a: a.reshape(batch, seq, a.shape[-1])
        by_mem = lambda a: a.reshape(batch, N_MEM, MEM_W)
        gn, swa_o, mem_o, s_p, s_mem_o = _mix_prompt(
            by_seq(rq), by_seq(rk), by_seq(rv), by_seq(sq), by_seq(sk), by_seq(sv), by_seq(mq),
            by_mem(mk_bf), by_mem(mv_bf), sinks, tab_p2, s_mq, cache_mk, cache_mv, l, dec_t)
        flat = lambda a: a.reshape(batch * seq, a.shape[-1])
        xp = _finish(xp, flat(gn), rg, flat(swa_o), flat(mem_o), g_r, g_s, g_m, lw, tm_fin)
        ret_p.append(s_p.reshape(batch, RET_HEADS, RET_DK, RET_DV))
        swk_p.append(sk.reshape(batch, seq, SWA_KV_W)[:, -WINDOW:].reshape(batch, WINDOW, SWA_KV_HEADS, SWA_HD))
        swv_p.append(sv.reshape(batch, seq, SWA_KV_W)[:, -WINDOW:].reshape(batch, WINDOW, SWA_KV_HEADS, SWA_HD))
        mk_p.append(mk.reshape(batch, N_MEM, MEM_HEADS, MEM_HD))
        mv_p.append(mv.reshape(batch, N_MEM, MEM_HEADS, MEM_HD))

        s_gn, s_swa_o, stacked_s = _mix_sample(s_rq, s_rk, s_rv, state_ret, tab_s2, s_sq, s_sk, s_sv, cache_kt, cache_vt,
                                               sinks, stacked_s, l, dec_t, ret_seqs)
        xs = _finish(xs, s_gn, s_rg, s_swa_o, s_mem_o, s_g_r, s_g_s, s_g_m, lw, tm_s)

    from_t = lambda a: jnp.transpose(a, (0, 1, 4, 2, 3))
    ret_s, swk_s, swv_s = stacked_s
    return (xp.reshape(batch, seq, D_MODEL), xs.reshape(dec_b, dec_t, D_MODEL),
            jnp.stack(ret_p), jnp.stack(swk_p), jnp.stack(swv_p), jnp.stack(mk_p), jnp.stack(mv_p),
            ret_s, from_t(swk_s), from_t(swv_s))
```

```python
import functools

import jax
import jax.numpy as jnp
import numpy as np
from jax import lax
from jax.experimental import pallas as pl
from jax.experimental.pallas import tpu as pltpu

F32 = jnp.float32
BF16 = jnp.bfloat16

D_MODEL = 1024
DEPTH = 2
PAST_LEN = 16384
RET_HEADS = 8
RET_DK = 64
RET_DV = 128
RET_CHUNK = 128
SWA_HEADS = 8
SWA_KV_HEADS = 2
SWA_GROUP = SWA_HEADS // SWA_KV_HEADS
SWA_HD = 64
WINDOW = 128
MEM_HEADS = 4
MEM_HD = 128
N_MEM = 256
D_FF = 4 * D_MODEL
ROPE_THETA = 10000.0
LN_EPS = 1e-5
GN_EPS = 1e-5
ALPHA = (2 * DEPTH) ** 0.25

RET_QK_W = RET_HEADS * RET_DK
RET_V_W = RET_HEADS * RET_DV
SWA_Q_W = SWA_HEADS * SWA_HD
SWA_KV_W = SWA_KV_HEADS * SWA_HD
MEM_W = MEM_HEADS * MEM_HD
OFF_RQ = 0
OFF_RK = OFF_RQ + RET_QK_W
OFF_RV = OFF_RK + RET_QK_W
OFF_RG = OFF_RV + RET_V_W
OFF_SQ = OFF_RG + RET_V_W
OFF_SK = OFF_SQ + SWA_Q_W
OFF_SV = OFF_SK + SWA_KV_W
OFF_MQ = OFF_SV + SWA_KV_W
OFF_GR = OFF_MQ + MEM_W
OFF_GS = OFF_GR + D_MODEL
OFF_GM = OFF_GS + D_MODEL
IN_W = OFF_GM + D_MODEL

assert RET_DK == SWA_HD
LANES = 128
V7X_VMEM_LIMIT = 62 * 1024 * 1024


def _const_spec(shape):
    nd = len(shape)
    return pl.BlockSpec(shape, lambda *_: (0,) * nd, pipeline_mode=pl.Buffered(1))


def _layer_spec(shape, layer):
    nd = len(shape)
    return pl.BlockSpec((1,) + tuple(shape[1:]), lambda *_: (layer,) + (0,) * (nd - 1), pipeline_mode=pl.Buffered(1))


def _params(n_grid):
    return pltpu.CompilerParams(dimension_semantics=("arbitrary",) * n_grid, vmem_limit_bytes=V7X_VMEM_LIMIT)


N_INPROJ_OUT = 11


def _inproj_kernel(n_cast, x_ref, w_ref, cos_ref, sin_ref, *refs):
    cast_in, outs, cast_out = refs[:n_cast], refs[n_cast:n_cast + N_INPROJ_OUT], refs[n_cast + N_INPROJ_OUT:]
    rq_ref, rk_ref, rv_ref, rg_ref, sq_ref, sk_ref, sv_ref, mq_ref, gr_ref, gs_ref, gm_ref = outs
    for src, dst in zip(cast_in, cast_out):
        dst[...] = src[...].astype(dst.dtype)
    xb = x_ref[...].astype(BF16)
    cos = cos_ref[...]
    sin = sin_ref[...]
    lane = lax.broadcasted_iota(jnp.int32, cos.shape, 1)
    first_half = (lane & (SWA_HD // 2)) == 0

    def proj(off, width):
        return jnp.dot(xb, w_ref[0, :, off:off + width], preferred_element_type=F32)

    def rope_store(off, width, out_ref, scale):
        y = proj(off, width)
        for j in range(width // LANES):
            yj = y[:, j * LANES:(j + 1) * LANES]
            sw = jnp.where(first_half, pltpu.roll(yj, LANES - SWA_HD // 2, 1), pltpu.roll(yj, SWA_HD // 2, 1))
            r = yj * cos + sw * sin
            if scale != 1.0:
                r = r * scale
            out_ref[:, j * LANES:(j + 1) * LANES] = r.astype(out_ref.dtype)

    def plain_store(off, width, out_ref):
        out_ref[...] = proj(off, width).astype(out_ref.dtype)

    rope_store(OFF_RQ, RET_QK_W, rq_ref, 1.0)
    rope_store(OFF_RK, RET_QK_W, rk_ref, RET_DK ** -0.5)
    plain_store(OFF_RV, RET_V_W, rv_ref)
    plain_store(OFF_RG, RET_V_W, rg_ref)
    rope_store(OFF_SQ, SWA_Q_W, sq_ref, SWA_HD ** -0.5)
    rope_store(OFF_SK, SWA_KV_W, sk_ref, 1.0)
    plain_store(OFF_SV, SWA_KV_W, sv_ref)
    plain_store(OFF_MQ, MEM_W, mq_ref)
    plain_store(OFF_GR, D_MODEL, gr_ref)
    plain_store(OFF_GS, D_MODEL, gs_ref)
    plain_store(OFF_GM, D_MODEL, gm_ref)


def _inproj(x2d, w_bf, cos_tab, sin_tab, tm, qkv_dtype, cast_jobs=()):
    m = x2d.shape[0]
    n_steps = m // tm
    n_tab = cos_tab.shape[0] // tm
    row = lambda w: pl.BlockSpec((tm, w), lambda i: (i, 0))
    tab = pl.BlockSpec((tm, LANES), lambda i: (i % n_tab, 0))
    widths_dtypes = [(RET_QK_W, qkv_dtype), (RET_QK_W, qkv_dtype), (RET_V_W, qkv_dtype), (RET_V_W, F32),
                     (SWA_Q_W, qkv_dtype), (SWA_KV_W, F32), (SWA_KV_W, F32), (MEM_W, qkv_dtype),
                     (D_MODEL, F32), (D_MODEL, F32), (D_MODEL, F32)]
    assert len(widths_dtypes) == N_INPROJ_OUT
    slab = lambda a: (1, a.shape[1] // n_steps, a.shape[2])
    cast_in = [pl.BlockSpec(slab(a), lambda i, layer=layer: (layer, i, 0)) for a, layer in cast_jobs]
    cast_out = [pl.BlockSpec(slab(a), lambda i: (0, i, 0)) for a, _ in cast_jobs]
    return pl.pallas_call(
        functools.partial(_inproj_kernel, len(cast_jobs)),
        grid=(n_steps,),
        in_specs=[row(D_MODEL), _layer_spec(w_bf[0].shape, w_bf[1]), tab, tab] + cast_in,
        out_specs=[row(w) for w, _ in widths_dtypes] + cast_out,
        out_shape=([jax.ShapeDtypeStruct((m, w), dt) for w, dt in widths_dtypes]
                   + [jax.ShapeDtypeStruct((1,) + a.shape[1:], BF16) for a, _ in cast_jobs]),
        compiler_params=_params(1),
        name="inproj",
    )(x2d, w_bf[0], cos_tab, sin_tab, *[a for a, _ in cast_jobs])


def _group_norm(o, g_row):
    mu = jnp.mean(o, -1, keepdims=True)
    d = o - mu
    var = jnp.mean(d * d, -1, keepdims=True)
    return d * lax.rsqrt(var + GN_EPS) * g_row


def _ret_tables(n_rows, period):
    lg = np.log1p(-np.exp2(-5.0 - np.arange(RET_HEADS, dtype=np.float64)))
    r = np.arange(n_rows)
    t = (r % period).astype(np.float64)
    same = (r[:, None] // period) == (r[None, :] // period)
    diff = t[:, None] - t[None, :]
    decay = np.where((diff >= 0) & same, np.exp(lg[:, None, None] * np.maximum(diff, 0.0)), 0.0)
    rowdec = np.exp(lg[:, None] * (t[None, :] + 1.0))
    wend = np.exp(lg[:, None] * (period - 1.0 - t[None, :]))
    gl = np.exp(lg * period)
    rowdec = np.broadcast_to(rowdec[:, :, None], (RET_HEADS, n_rows, RET_DV))
    wend = np.broadcast_to(wend[:, :, None], (RET_HEADS, n_rows, RET_DK))
    gl = np.broadcast_to(gl[:, None, None], (RET_HEADS, 1, RET_DV))
    return decay, rowdec, wend, gl


def _ret_sample_kernel(n_seq, t_len, q_ref, k_ref, v_ref, s_ref, decay_ref, rowdec_ref, wend_ref, gl_ref,
                       o_ref, s_out_ref):
    rows = n_seq * t_len
    pair_dk, pair_dv = 2 * RET_DK, 2 * RET_DV
    lane_lo = lax.broadcasted_iota(jnp.int32, (rows, pair_dk), 1) < RET_DK
    row_lo = lax.broadcasted_iota(jnp.int32, (pair_dk, RET_DV), 0) < RET_DK
    for p in range(RET_HEADS // 2):
        qk = slice(p * pair_dk, (p + 1) * pair_dk)
        vv = slice(p * pair_dv, (p + 1) * pair_dv)
        q2f, k2f, v2f = q_ref[:, qk], k_ref[:, qk], v_ref[:, vv]
        q2, k2, v2 = q2f.astype(BF16), k2f.astype(BF16), v2f.astype(BF16)
        zk = jnp.zeros_like(k2)
        k_rows = jnp.concatenate([jnp.where(lane_lo, k2, zk), jnp.where(lane_lo, zk, k2)], 0)
        sc2 = lax.dot_general(q2, k_rows, (((1,), (1,)), ((), ())), preferred_element_type=F32) * decay_ref[p]
        zv = jnp.zeros((rows, RET_DV), BF16)
        v_bd = jnp.concatenate([jnp.concatenate([v2[:, :RET_DV], zv], 1),
                                jnp.concatenate([zv, v2[:, RET_DV:]], 1)], 0)
        kw2f = k2f * wend_ref[p]
        o_state = []
        for b in range(n_seq):
            r = slice(b * t_len, (b + 1) * t_len)
            s2 = s_ref[0, b, 2 * p:2 * p + 2].reshape(pair_dk, RET_DV)
            s2b = s2.astype(BF16)
            zs = jnp.zeros_like(s2b)
            s_bd = jnp.concatenate([jnp.where(row_lo, s2b, zs), jnp.where(row_lo, zs, s2b)], 1)
            o_state.append(jnp.dot(q2f[r].astype(BF16), s_bd, preferred_element_type=F32))
            upd = lax.dot_general(kw2f[r].astype(BF16), v2f[r].astype(BF16), (((0,), (0,)), ((), ())),
                                  preferred_element_type=F32)
            s_new = (gl_ref[p] * s2 + jnp.where(row_lo, upd[:, :RET_DV], upd[:, RET_DV:])).reshape(2, RET_DK, RET_DV)
            for d in range(s_out_ref.shape[0]):
                s_out_ref[d, b, 2 * p:2 * p + 2] = s_new
        o2 = jnp.dot(sc2.astype(BF16), v_bd, preferred_element_type=F32)
        o_ref[:, vv] = o2 + jnp.concatenate(o_state, 0) * rowdec_ref[p]


def _stacked_out_specs(shape, layer, n_seq):
    tail = tuple(shape[2:])
    zeros = (0,) * len(tail)
    if layer == 0:
        return pl.BlockSpec((shape[0], n_seq) + tail, lambda i: (0, i) + zeros)
    return pl.BlockSpec((1, n_seq) + tail, lambda i: (layer, i) + zeros)


def _swa_sample_kernel(n_seq, t_len, sinks_ref, q_ref, kn_ref, vn_ref, kt_ref, vt_ref, *rest):
    o_ref, kto_ref, vto_ref = rest[-3:]
    grp_rows = SWA_GROUP * t_len
    n_all = n_seq * grp_rows
    q = q_ref[...]
    kn = kn_ref[...]
    vn = vn_ref[...]
    kn_t = kn.T
    vn_t = vn.T
    r = lax.broadcasted_iota(jnp.int32, (n_all, 1), 0)
    t_q = r % t_len
    g_row = (r // t_len) % SWA_GROUP
    b_row = r // grp_rows
    c = lax.broadcasted_iota(jnp.int32, (1, WINDOW), 1)
    valid_cache = c > t_q
    valid_new = ((c // t_len) == b_row) & ((c % t_len) <= t_q)
    lane = lax.broadcasted_iota(jnp.int32, (SWA_HD, WINDOW), 1)
    is_new_lane = lane >= WINDOW - t_len
    head_dims = [slice(kvh * SWA_HD, (kvh + 1) * SWA_HD) for kvh in range(SWA_KV_HEADS)]
    scores = []
    for kvh, hd in enumerate(head_dims):
        qg = [q[:, (kvh * SWA_GROUP + g) * SWA_HD:(kvh * SWA_GROUP + g + 1) * SWA_HD] for g in range(SWA_GROUP)]
        q_all = jnp.concatenate([qg[g][b * t_len:(b + 1) * t_len] for b in range(n_seq) for g in range(SWA_GROUP)],
                                0).astype(BF16)
        s_new = jnp.dot(q_all, kn_t[hd].astype(BF16), preferred_element_type=F32)
        s_cache = jnp.concatenate(
            [jnp.dot(q_all[b * grp_rows:(b + 1) * grp_rows], kt_ref[0, b, kvh].astype(BF16),
                     preferred_element_type=F32) for b in range(n_seq)], 0)
        scores.append((s_new, s_cache))
    probs = []
    for kvh, (s_new, s_cache) in enumerate(scores):
        s_new = jnp.where(valid_new, s_new, -jnp.inf)
        s_cache = jnp.where(valid_cache, s_cache, -jnp.inf)
        sink = jnp.full((n_all, 1), sinks_ref[kvh * SWA_GROUP], F32)
        for g in range(1, SWA_GROUP):
            sink = jnp.where(g_row == g, sinks_ref[kvh * SWA_GROUP + g], sink)
        m = jnp.maximum(jnp.maximum(jnp.max(s_new, -1, keepdims=True), jnp.max(s_cache, -1, keepdims=True)), sink)
        e_new = jnp.exp(s_new - m)
        e_cache = jnp.exp(s_cache - m)
        den = jnp.sum(e_new, -1, keepdims=True) + jnp.sum(e_cache, -1, keepdims=True) + jnp.exp(sink - m)
        probs.append(((e_new / den).astype(BF16), (e_cache / den).astype(BF16)))
    pieces = []
    for kvh, hd in enumerate(head_dims):
        p_new, p_cache = probs[kvh]
        o = jnp.dot(p_new, vn[:, hd].astype(BF16), preferred_element_type=F32)
        o = o + jnp.concatenate(
            [lax.dot_general(p_cache[b * grp_rows:(b + 1) * grp_rows], vt_ref[0, b, kvh].astype(BF16),
                             (((1,), (1,)), ((), ())), preferred_element_type=F32) for b in range(n_seq)], 0)
        for g in range(SWA_GROUP):
            pieces.append(jnp.concatenate(
                [o[b * grp_rows + g * t_len:b * grp_rows + (g + 1) * t_len] for b in range(n_seq)], 0))
    for kvh, hd in enumerate(head_dims):
        for b in range(n_seq):
            shift_new = (WINDOW - t_len - b * t_len) % WINDOW
            k_slid = jnp.where(is_new_lane, pltpu.roll(kn_t[hd], shift_new, 1),
                               pltpu.roll(kt_ref[0, b, kvh], WINDOW - t_len, 1))
            v_slid = jnp.where(is_new_lane, pltpu.roll(vn_t[hd], shift_new, 1),
                               pltpu.roll(vt_ref[0, b, kvh], WINDOW - t_len, 1))
            for d in range(kto_ref.shape[0]):
                kto_ref[d, b, kvh] = k_slid
                vto_ref[d, b, kvh] = v_slid
    o_ref[...] = jnp.concatenate(pieces, -1).astype(o_ref.dtype)


N_MIX_SAMPLE_IN = 14


def _mix_sample_kernel(n_seq, t_len, sinks_ref, rq_ref, rk_ref, rv_ref, s_ref, decay_ref, rowdec_ref, wend_ref,
                       gl_ref, sq_ref, kn_ref, vn_ref, kt_ref, vt_ref, *rest):
    ret_o, s_out, swa_o, kto, vto = rest[-5:]
    _ret_sample_kernel(n_seq, t_len, rq_ref, rk_ref, rv_ref, s_ref, decay_ref, rowdec_ref, wend_ref, gl_ref,
                       ret_o, s_out)
    _swa_sample_kernel(n_seq, t_len, sinks_ref, sq_ref, kn_ref, vn_ref, kt_ref, vt_ref, swa_o, kto, vto)


def _mix_sample(rq, rk, rv, state, pair_tables, sq, sk, sv, cache_kt, cache_vt, sinks, prev_out, layer, t_len, n_seq):
    m = rq.shape[0]
    rows = n_seq * t_len
    assert rows == WINDOW and cache_kt.shape[-1] == WINDOW
    decay2, rowdec2, wend2, gl2 = pair_tables
    row = lambda w: pl.BlockSpec((rows, w), lambda i: (i, 0))
    st_in = pl.BlockSpec((1, n_seq, RET_HEADS, RET_DK, RET_DV), lambda i: (layer, i, 0, 0, 0))
    cin = pl.BlockSpec((1, n_seq, SWA_KV_HEADS, SWA_HD, WINDOW), lambda i: (layer, i, 0, 0, 0))
    in_specs = [pl.BlockSpec(memory_space=pltpu.SMEM), row(RET_QK_W), row(RET_QK_W), row(RET_V_W), st_in,
                _const_spec(decay2.shape), _const_spec(rowdec2.shape), _const_spec(wend2.shape),
                _const_spec(gl2.shape), row(SWA_Q_W), row(SWA_KV_W), row(SWA_KV_W), cin, cin]
    args = [sinks, rq, rk, rv, state, decay2, rowdec2, wend2, gl2, sq, sk, sv, cache_kt, cache_vt]
    assert len(args) == N_MIX_SAMPLE_IN
    stacked = (state, cache_kt, cache_vt)
    aliases = {}
    if prev_out is not None:
        in_specs += [pl.BlockSpec(memory_space=pl.ANY)] * len(stacked)
        args += list(prev_out)
        aliases = {N_MIX_SAMPLE_IN: 1, N_MIX_SAMPLE_IN + 1: 3, N_MIX_SAMPLE_IN + 2: 4}
    st_out, k_out, v_out = (_stacked_out_specs(a.shape, layer, n_seq) for a in stacked)
    shape = lambda a: jax.ShapeDtypeStruct(a.shape, F32)
    ret_o, s_new, swa_o, k_new, v_new = pl.pallas_call(
        functools.partial(_mix_sample_kernel, n_seq, t_len),
        grid=(m // rows,),
        in_specs=in_specs,
        out_specs=[row(RET_V_W), st_out, row(SWA_Q_W), k_out, v_out],
        out_shape=[jax.ShapeDtypeStruct((m, RET_V_W), F32), shape(state),
                   jax.ShapeDtypeStruct((m, SWA_Q_W), BF16), shape(cache_kt), shape(cache_vt)],
        input_output_aliases=aliases,
        compiler_params=_params(1),
        name="mix_sample",
    )(*args)
    return ret_o, swa_o, (s_new, k_new, v_new)


def _pair_tables(tables):
    decay, rowdec, wend, gl = tables
    pair = lambda a: np.concatenate([a[0::2], a[1::2]], -1)
    gl_rows = np.concatenate([np.broadcast_to(gl[0::2], (RET_HEADS // 2, RET_DK, RET_DV)),
                              np.broadcast_to(gl[1::2], (RET_HEADS // 2, RET_DK, RET_DV))], 1)
    return tuple(jnp.asarray(a, F32) for a in (pair(decay), pair(rowdec), pair(wend), gl_rows))


def _run_staged(tasks):
    active = list(tasks)
    while active:
        for t in list(active):
            try:
                next(t)
            except StopIteration:
                active.remove(t)


def _mix_prompt_kernel(nb, t_len, sinks_ref, rq_ref, rk_ref, rv_ref, sq_ref, sk_ref, sv_ref, mq_ref, memx_ref, wkv_ref,
                       decay_ref, rowdec_ref, wend_ref, gl_ref, smq_ref, smk_ref, smv_ref,
                       ret_out, swa_out, mem_out, s_out, smem_out, mk_out, mv_out,
                       s_scr, kp_scr, kpr_scr, vp_scr, vpr_scr, mk_scr, mv_scr):
    c = pl.program_id(0)

    @pl.when(c == 0)
    def _():
        s_scr[...] = jnp.zeros_like(s_scr)
        for scr in (kp_scr, kpr_scr, vp_scr, vpr_scr):
            scr[...] = jnp.zeros_like(scr)
        kv = jnp.dot(memx_ref[...].astype(BF16), wkv_ref[0], preferred_element_type=F32)
        mk_out[...] = kv[:, :MEM_W]
        mv_out[...] = kv[:, MEM_W:]
        mk_scr[...] = kv[:, :MEM_W].astype(BF16).reshape(mk_scr.shape)
        mv_scr[...] = kv[:, MEM_W:].astype(BF16).reshape(mv_scr.shape)

    pair_w = 2 * SWA_HD
    lane_lo = lax.broadcasted_iota(jnp.int32, (RET_CHUNK, pair_w), 1) < SWA_HD
    row_lo = lax.broadcasted_iota(jnp.int32, (2 * RET_DK, RET_DV), 0) < RET_DK
    lane_lo_kv = lax.broadcasted_iota(jnp.int32, (2 * WINDOW, pair_w), 1) < SWA_HD
    upper = (lax.broadcasted_iota(jnp.int32, (WINDOW, WINDOW), 1)
             > lax.broadcasted_iota(jnp.int32, (WINDOW, WINDOW), 0))
    prev_bias = jnp.where(c > 0, 0.0, -jnp.inf)


    def ret_task(b, p):
        qk = slice(p * 2 * RET_DK, (p + 1) * 2 * RET_DK)
        vv = slice(p * 2 * RET_DV, (p + 1) * 2 * RET_DV)
        q2, k2, v2, s2 = rq_ref[b, :, qk], rk_ref[b, :, qk], rv_ref[b, :, vv], s_scr[b, p]
        zk = jnp.zeros_like(k2)
        k_rows = jnp.concatenate([jnp.where(lane_lo, k2, zk), jnp.where(lane_lo, zk, k2)], 0)
        sc_raw = lax.dot_general(q2, k_rows, (((1,), (1,)), ((), ())), preferred_element_type=F32)
        s2b = s2.astype(BF16)
        zs = jnp.zeros_like(s2b)
        s_bd = jnp.concatenate([jnp.where(row_lo, s2b, zs), jnp.where(row_lo, zs, s2b)], 1)
        os_raw = jnp.dot(q2, s_bd, preferred_element_type=F32)
        kw2 = (k2.astype(F32) * wend_ref[p]).astype(BF16)
        upd = lax.dot_general(kw2, v2, (((0,), (0,)), ((), ())), preferred_element_type=F32)
        yield
        zv = jnp.zeros((RET_CHUNK, RET_DV), v2.dtype)
        v_bd = jnp.concatenate([jnp.concatenate([v2[:, :RET_DV], zv], 1),
                                jnp.concatenate([zv, v2[:, RET_DV:]], 1)], 0)
        o_raw = jnp.dot((sc_raw * decay_ref[p]).astype(BF16), v_bd, preferred_element_type=F32)
        s_scr[b, p] = gl_ref[p] * s2 + jnp.where(row_lo, upd[:, :RET_DV], upd[:, RET_DV:])
        yield
        ret_out[b, :, vv] = o_raw + os_raw * rowdec_ref[p]

    kv_ctx = {}

    def swa_prep(b):
        k_cur, v_cur = sk_ref[b], sv_ref[b]
        kb, kbr = k_cur.astype(BF16), pltpu.roll(k_cur, SWA_HD, 1).astype(BF16)
        vb, vbr = v_cur.astype(BF16), pltpu.roll(v_cur, SWA_HD, 1).astype(BF16)
        kv_ctx[b] = (jnp.concatenate([kp_scr[b], kb], 0), jnp.concatenate([kpr_scr[b], kbr], 0),
                     jnp.concatenate([vp_scr[b], vb], 0), jnp.concatenate([vpr_scr[b], vbr], 0))
        kp_scr[b], kpr_scr[b], vp_scr[b], vpr_scr[b] = kb, kbr, vb, vbr

    def swa_task(b, kvh):
        if b not in kv_ctx:
            swa_prep(b)
        kc, kcr, vc, vcr = kv_ctx[b]
        zkv = jnp.zeros_like(kc)
        k_lo, k_hi = (kc, kcr) if kvh == 0 else (kcr, kc)
        v_lo, v_hi = (vc, vcr) if kvh == 0 else (vcr, vc)
        k_rows = jnp.concatenate([jnp.where(lane_lo_kv, k_lo, zkv), jnp.where(lane_lo_kv, zkv, k_hi)], 0)
        v_rows = jnp.concatenate([jnp.where(lane_lo_kv, v_lo, zkv), jnp.where(lane_lo_kv, zkv, v_hi)], 0)
        n_pairs = SWA_GROUP // 2
        pairs = [kvh * n_pairs + jj for jj in range(n_pairs)]
        q4 = jnp.concatenate([sq_ref[b, :, pr * pair_w:(pr + 1) * pair_w] for pr in pairs], 0)
        s4 = lax.dot_general(q4, k_rows, (((1,), (1,)), ((), ())), preferred_element_type=F32)
        yield
        rows, inv = [], []
        for jj, pr in enumerate(pairs):
            ps, inv_u = [], []
            for u in range(2):
                blk = s4[jj * WINDOW:(jj + 1) * WINDOW, u * 2 * WINDOW:(u + 1) * 2 * WINDOW]
                s = jnp.where(upper, blk[:, :WINDOW] + prev_bias, blk[:, WINDOW:])
                sink = sinks_ref[2 * pr + u]
                m = jnp.maximum(jnp.max(s, -1, keepdims=True), sink)
                e = jnp.exp(s - m)
                den = jnp.sum(e, -1, keepdims=True) + jnp.exp(sink - m)
                ps += [jnp.where(upper, e, 0.0).astype(BF16), jnp.where(upper, 0.0, e).astype(BF16)]
                inv_u.append(1.0 / den)
            rows.append(jnp.concatenate(ps, 1))
            inv.append(jnp.where(lane_lo, inv_u[0], inv_u[1]))
        o4 = jnp.dot(jnp.concatenate(rows, 0), v_rows, preferred_element_type=F32)
        yield
        for jj, pr in enumerate(pairs):
            swa_out[b, :, pr * pair_w:(pr + 1) * pair_w] = (o4[jj * WINDOW:(jj + 1) * WINDOW] * inv[jj]).astype(
                swa_out.dtype)

    def mem_task(b, h):
        sl = slice(h * MEM_HD, (h + 1) * MEM_HD)
        s = lax.dot_general(mq_ref[b, :, sl], mk_scr[b, :, sl], (((1,), (1,)), ((), ())),
                            preferred_element_type=F32) * (MEM_HD ** -0.5)
        yield
        m = jnp.max(s, -1, keepdims=True)
        e = jnp.exp(s - m)
        inv = 1.0 / jnp.sum(e, -1, keepdims=True)
        o = jnp.dot(e.astype(BF16), mv_scr[b, :, sl], preferred_element_type=F32)
        yield
        mem_out[b, :, sl] = (o * inv).astype(mem_out.dtype)

    head_of_row = lax.broadcasted_iota(jnp.int32, (MEM_HEADS * t_len, 1), 0) // t_len
    head_of_col = lax.broadcasted_iota(jnp.int32, (1, N_MEM * MEM_HEADS), 1) % MEM_HEADS
    valid_smem = head_of_row == head_of_col

    def smem_task(j):
        r = slice(j * t_len, (j + 1) * t_len)
        qb = smq_ref[r, :]
        q_all = jnp.concatenate([qb[:, h * MEM_HD:(h + 1) * MEM_HD] for h in range(MEM_HEADS)], 0).astype(BF16)
        s = lax.dot_general(q_all, smk_ref[0, j].astype(BF16), (((1,), (1,)), ((), ())),
                            preferred_element_type=F32) * (MEM_HD ** -0.5)
        yield
        s = jnp.where(valid_smem, s, -jnp.inf)
        m = jnp.max(s, -1, keepdims=True)
        e = jnp.exp(s - m)
        inv = 1.0 / jnp.sum(e, -1, keepdims=True)
        o = jnp.dot(e.astype(BF16), smv_ref[0, j].astype(BF16), preferred_element_type=F32) * inv
        yield
        smem_out[r, :] = jnp.concatenate([o[h * t_len:(h + 1) * t_len] for h in range(MEM_HEADS)], -1)

    n_smem = smk_ref.shape[1]
    for b in range(nb):
        _run_staged([ret_task(b, 0), swa_task(b, 0), mem_task(b, 0), ret_task(b, 1), mem_task(b, 1),
                     ret_task(b, 2), swa_task(b, 1), mem_task(b, 2), ret_task(b, 3), mem_task(b, 3)]
                    + [smem_task(j) for j in range(b, n_smem, nb)])

    @pl.when(c == pl.num_programs(0) - 1)
    def _():
        s_out[...] = s_scr[...]


def _mix_prompt(rq, rk, rv, sq, sk, sv, mq, mem2d, w_kv, sinks, pair_tables, smq, cache_mk, cache_mv, layer, t_len):
    nb, seq, _ = rq.shape
    n_steps = seq // RET_CHUNK
    n_smem = cache_mk.shape[1] // n_steps
    decay2, rowdec2, wend2, gl2 = pair_tables
    w_kv, kv_layer = w_kv
    chunk = lambda w: pl.BlockSpec((nb, RET_CHUNK, w), lambda c: (0, c, 0))
    srow = pl.BlockSpec((n_smem * t_len, MEM_W), lambda c: (c, 0))
    skv = pl.BlockSpec((1, n_smem, N_MEM * MEM_HEADS, MEM_HD), lambda c: (layer, c, 0, 0))
    st_shape = (nb, RET_HEADS // 2, 2 * RET_DK, RET_DV)
    mem_shape = (mem2d.shape[0], MEM_W)
    kv_scr = pltpu.VMEM((nb, WINDOW, SWA_KV_W), BF16)
    mem_scr = pltpu.VMEM((nb, mem2d.shape[0] // nb, MEM_W), BF16)
    return pl.pallas_call(
        functools.partial(_mix_prompt_kernel, nb, t_len),
        grid=(n_steps,),
        in_specs=[pl.BlockSpec(memory_space=pltpu.SMEM),
                  chunk(RET_QK_W), chunk(RET_QK_W), chunk(RET_V_W), chunk(SWA_Q_W), chunk(SWA_KV_W), chunk(SWA_KV_W),
                  chunk(MEM_W), _const_spec(mem2d.shape), _layer_spec(w_kv.shape, kv_layer),
                  _const_spec(decay2.shape), _const_spec(rowdec2.shape), _const_spec(wend2.shape),
                  _const_spec(gl2.shape), srow, skv, skv],
        out_specs=[chunk(RET_V_W), chunk(SWA_Q_W), chunk(MEM_W), _const_spec(st_shape), srow,
                   _const_spec(mem_shape), _const_spec(mem_shape)],
        out_shape=[jax.ShapeDtypeStruct((nb, seq, RET_V_W), F32), jax.ShapeDtypeStruct((nb, seq, SWA_Q_W), BF16),
                   jax.ShapeDtypeStruct((nb, seq, MEM_W), BF16), jax.ShapeDtypeStruct(st_shape, F32),
                   jax.ShapeDtypeStruct(smq.shape, F32),
                   jax.ShapeDtypeStruct(mem_shape, F32), jax.ShapeDtypeStruct(mem_shape, F32)],
        scratch_shapes=[pltpu.VMEM(st_shape, F32), kv_scr, kv_scr, kv_scr, kv_scr, mem_scr, mem_scr],
        compiler_params=_params(1),
        name="mix_prompt",
    )(sinks, rq, rk, rv, sq, sk, sv, mq, mem2d, w_kv, decay2, rowdec2, wend2, gl2, smq, cache_mk, cache_mv)


def _layer_norm(x, g, b):
    mu = jnp.mean(x, -1, keepdims=True)
    d = x - mu
    var = jnp.mean(d * d, -1, keepdims=True)
    return d * lax.rsqrt(var + LN_EPS) * g + b


def _finish_kernel(x_ref, ret_ref, rg_ref, swa_ref, mem_ref, gr_ref, gs_ref, gm_ref, gng_ref,
                   wr_ref, ws_ref, wm_ref, wo_ref, l1g_ref, l1b_ref, wu_ref, wd_ref, l2g_ref, l2b_ref, o_ref):
    swa_b = jnp.dot(swa_ref[...].astype(BF16), ws_ref[0], preferred_element_type=F32)
    mem_b = jnp.dot(mem_ref[...].astype(BF16), wm_ref[0], preferred_element_type=F32)
    half = x_ref.shape[0] // 2
    ret_parts = []
    for r in (slice(0, half), slice(half, 2 * half)):
        rg = rg_ref[r, :]
        gn = jnp.concatenate([_group_norm(ret_ref[r, h * RET_DV:(h + 1) * RET_DV], gng_ref[0, h:h + 1, :])
                              for h in range(RET_HEADS)], -1)
        ret_in = (rg * jax.nn.sigmoid(rg) * gn).astype(BF16)
        ret_parts.append(jnp.dot(ret_in, wr_ref[0], preferred_element_type=F32))
    ret_b = jnp.concatenate(ret_parts, 0)
    merged = (jax.nn.sigmoid(gr_ref[...]) * ret_b + jax.nn.sigmoid(gs_ref[...]) * swa_b
              + jax.nn.sigmoid(gm_ref[...]) * mem_b)
    y = jnp.dot(merged.astype(BF16), wo_ref[0], preferred_element_type=F32)
    x1 = _layer_norm(ALPHA * x_ref[...] + y, l1g_ref[0], l1b_ref[0])
    x1b = x1.astype(BF16)
    n_slabs = 4
    ff = D_FF // n_slabs
    up = lambda c: jnp.square(jnp.maximum(
        jnp.dot(x1b, wu_ref[0, :, c * ff:(c + 1) * ff], preferred_element_type=F32), 0.0)).astype(BF16)
    down = lambda c, h: jnp.dot(h, wd_ref[0, c * ff:(c + 1) * ff, :], preferred_element_type=F32)
    h_next = up(0)
    acc = None
    for c in range(n_slabs - 1):
        h_cur, h_next = h_next, up(c + 1)
        d = down(c, h_cur)
        acc = d if acc is None else acc + d
    for r in (slice(0, half), slice(half, 2 * half)):
        d = jnp.dot(h_next[r], wd_ref[0, (n_slabs - 1) * ff:, :], preferred_element_type=F32)
        o_ref[r, :] = _layer_norm(ALPHA * x1[r] + acc[r] + d, l2g_ref[0], l2b_ref[0])


def _finish(x2d, gn, rg, swa_o, mem_o, g_r, g_s, g_m, lw, tm):
    m = x2d.shape[0]
    row = lambda w: pl.BlockSpec((tm, w), lambda i: (i, 0))
    return pl.pallas_call(
        _finish_kernel,
        grid=(m // tm,),
        in_specs=[row(D_MODEL), row(RET_V_W), row(RET_V_W), row(SWA_Q_W), row(MEM_W),
                  row(D_MODEL), row(D_MODEL), row(D_MODEL)] + [_layer_spec(a.shape, idx) for a, idx in lw],
        out_specs=row(D_MODEL),
        out_shape=jax.ShapeDtypeStruct((m, D_MODEL), F32),
        compiler_params=_params(1),
        name="finish",
    )(x2d, gn, rg, swa_o, mem_o, g_r, g_s, g_m, *[a for a, _ in lw])


def _rope_tables(pos, reps=1):
    half = SWA_HD // 2
    inv = np.power(ROPE_THETA, -np.arange(half, dtype=np.float64) / half)
    ang = np.asarray(pos, np.float64)[:, None] * inv[None, :]
    c, s = np.cos(ang), np.sin(ang)
    cos_t, sin_t = np.concatenate([c, c, c, c], -1), np.concatenate([-s, s, -s, s], -1)
    return jnp.asarray(np.tile(cos_t, (reps, 1)), F32), jnp.asarray(np.tile(sin_t, (reps, 1)), F32)


def kernel(x_prompt, x_sample, state_ret, cache_swa_k, cache_swa_v, cache_mem_k, cache_mem_v, mem_prompt,
           w_in, w_br_ret, w_br_swa, w_br_mem, w_out, w_mem_kv, attn_sinks, ret_gn_g,
           ln1_g, ln1_b, w_up, w_down, ln2_g, ln2_b):
    batch, seq, _ = x_prompt.shape
    dec_b, dec_t, _ = x_sample.shape
    tm_p, tm_s = 512, 256
    tm_fin = 512
    ret_seqs = RET_CHUNK // dec_t

    cos_p, sin_p = _rope_tables(np.arange(seq))
    cos_s, sin_s = _rope_tables(PAST_LEN + np.arange(dec_t), tm_s // dec_t)
    tab_p2 = _pair_tables(_ret_tables(RET_CHUNK, RET_CHUNK))
    tab_s2 = _pair_tables(_ret_tables(RET_CHUNK, dec_t))

    xp = x_prompt.reshape(batch * seq, D_MODEL)
    xs = x_sample.reshape(dec_b * dec_t, D_MODEL)
    mem2d = mem_prompt.reshape(batch * N_MEM, D_MODEL)
    cache_kt = jnp.transpose(cache_swa_k, (0, 1, 3, 4, 2))
    cache_vt = jnp.transpose(cache_swa_v, (0, 1, 3, 4, 2))
    cache_mk = cache_mem_k.reshape(DEPTH, dec_b, N_MEM * MEM_HEADS, MEM_HD)
    cache_mv = cache_mem_v.reshape(DEPTH, dec_b, N_MEM * MEM_HEADS, MEM_HD)

    ln_row = lambda a: a.reshape(DEPTH, 1, D_MODEL)
    late_weights = (w_br_ret, w_br_swa, w_br_mem, w_out, w_up, w_down)
    next_weights = (w_in, w_mem_kv)
    w_in_bf, w_mem_kv_bf = (w_in[:1].astype(BF16), 0), (w_mem_kv[:1].astype(BF16), 0)

    ret_p, swk_p, swv_p, mk_p, mv_p = [], [], [], [], []
    stacked_s = None
    for l in range(DEPTH):
        sinks = attn_sinks[l]
        cast_jobs = [(w, l) for w in late_weights] + ([(w, l + 1) for w in next_weights] if l + 1 < DEPTH else [])
        proj = _inproj(xp, w_in_bf, cos_p, sin_p, tm_p, BF16, cast_jobs)
        rq, rk, rv, rg, sq, sk, sv, mq, g_r, g_s, g_m = proj[:N_INPROJ_OUT]
        s_rq, s_rk, s_rv, s_rg, s_sq, s_sk, s_sv, s_mq, s_g_r, s_g_s, s_g_m = _inproj(
            xs, w_in_bf, cos_s, sin_s, tm_s, F32)
        cast = [(w, 0) for w in proj[N_INPROJ_OUT:]]
        wr_bf, ws_bf, wm_bf, wo_bf, wu_bf, wd_bf = cast[:len(late_weights)]
        lw = [(ret_gn_g, l), wr_bf, ws_bf, wm_bf, wo_bf, (ln_row(ln1_g), l), (ln_row(ln1_b), l),
              wu_bf, wd_bf, (ln_row(ln2_g), l), (ln_row(ln2_b), l)]

        by_seq = lambda a: a.reshape(batch, seq, a.shape[-1])
        gn, swa_o, mem_o, s_p, s_mem_o, mk, mv = _mix_prompt(
            by_seq(rq), by_seq(rk), by_seq(rv), by_seq(sq), by_seq(sk), by_seq(sv), by_seq(mq),
            mem2d, w_mem_kv_bf, sinks, tab_p2, s_mq, cache_mk, cache_mv, l, dec_t)
        if l + 1 < DEPTH:
            w_in_bf, w_mem_kv_bf = cast[len(late_weights):]
        flat = lambda a: a.reshape(batch * seq, a.shape[-1])
        xp = _finish(xp, flat(gn), rg, flat(swa_o), flat(mem_o), g_r, g_s, g_m, lw, tm_fin)
        ret_p.append(s_p.reshape(batch, RET_HEADS, RET_DK, RET_DV))
        swk_p.append(sk.reshape(batch, seq, SWA_KV_W)[:, -WINDOW:].reshape(batch, WINDOW, SWA_KV_HEADS, SWA_HD))
        swv_p.append(sv.reshape(batch, seq, SWA_KV_W)[:, -WINDOW:].reshape(batch, WINDOW, SWA_KV_HEADS, SWA_HD))
        mk_p.append(mk.reshape(batch, N_MEM, MEM_HEADS, MEM_HD))
        mv_p.append(mv.reshape(batch, N_MEM, MEM_HEADS, MEM_HD))

        s_gn, s_swa_o, stacked_s = _mix_sample(s_rq, s_rk, s_rv, state_ret, tab_s2, s_sq, s_sk, s_sv, cache_kt, cache_vt,
                                               sinks, stacked_s, l, dec_t, ret_seqs)
        xs = _finish(xs, s_gn, s_rg, s_swa_o, s_mem_o, s_g_r, s_g_s, s_g_m, lw, tm_s)

    from_t = lambda a: jnp.transpose(a, (0, 1, 4, 2, 3))
    ret_s, swk_s, swv_s = stacked_s
    return (xp.reshape(batch, seq, D_MODEL), xs.reshape(dec_b, dec_t, D_MODEL),
            jnp.stack(ret_p), jnp.stack(swk_p), jnp.stack(swv_p), jnp.stack(mk_p), jnp.stack(mv_p),
            ret_s, from_t(swk_s), from_t(swv_s))
```

```python
import functools

import jax
import jax.numpy as jnp
import numpy as np
from jax import lax
from jax.experimental import pallas as pl
from jax.experimental.pallas import tpu as pltpu

F32 = jnp.float32
BF16 = jnp.bfloat16

D_MODEL = 1024
DEPTH = 2
PAST_LEN = 16384
RET_HEADS = 8
RET_DK = 64
RET_DV = 128
RET_CHUNK = 128
SWA_HEADS = 8
SWA_KV_HEADS = 2
SWA_GROUP = SWA_HEADS // SWA_KV_HEADS
SWA_HD = 64
WINDOW = 128
MEM_HEADS = 4
MEM_HD = 128
N_MEM = 256
D_FF = 4 * D_MODEL
ROPE_THETA = 10000.0
LN_EPS = 1e-5
GN_EPS = 1e-5
ALPHA = (2 * DEPTH) ** 0.25

RET_QK_W = RET_HEADS * RET_DK
RET_V_W = RET_HEADS * RET_DV
SWA_Q_W = SWA_HEADS * SWA_HD
SWA_KV_W = SWA_KV_HEADS * SWA_HD
MEM_W = MEM_HEADS * MEM_HD
OFF_RQ = 0
OFF_RK = OFF_RQ + RET_QK_W
OFF_RV = OFF_RK + RET_QK_W
OFF_RG = OFF_RV + RET_V_W
OFF_SQ = OFF_RG + RET_V_W
OFF_SK = OFF_SQ + SWA_Q_W
OFF_SV = OFF_SK + SWA_KV_W
OFF_MQ = OFF_SV + SWA_KV_W
OFF_GR = OFF_MQ + MEM_W
OFF_GS = OFF_GR + D_MODEL
OFF_GM = OFF_GS + D_MODEL
IN_W = OFF_GM + D_MODEL

assert RET_DK == SWA_HD
LANES = 128
V7X_VMEM_LIMIT = 62 * 1024 * 1024


def _const_spec(shape):
    nd = len(shape)
    return pl.BlockSpec(shape, lambda *_: (0,) * nd, pipeline_mode=pl.Buffered(1))


def _layer_spec(shape, layer):
    nd = len(shape)
    return pl.BlockSpec((1,) + tuple(shape[1:]), lambda *_: (layer,) + (0,) * (nd - 1), pipeline_mode=pl.Buffered(1))


def _params(n_grid):
    return pltpu.CompilerParams(dimension_semantics=("arbitrary",) * n_grid, vmem_limit_bytes=V7X_VMEM_LIMIT)


N_INPROJ_OUT = 11


def _inproj_kernel(n_cast, x_ref, w_ref, cos_ref, sin_ref, *refs):
    cast_in, outs, cast_out = refs[:n_cast], refs[n_cast:n_cast + N_INPROJ_OUT], refs[n_cast + N_INPROJ_OUT:]
    rq_ref, rk_ref, rv_ref, rg_ref, sq_ref, sk_ref, sv_ref, mq_ref, gr_ref, gs_ref, gm_ref = outs
    for src, dst in zip(cast_in, cast_out):
        dst[...] = src[...].astype(dst.dtype)
    xb = x_ref[...].astype(BF16)
    cos = cos_ref[...]
    sin = sin_ref[...]
    lane = lax.broadcasted_iota(jnp.int32, cos.shape, 1)
    first_half = (lane & (SWA_HD // 2)) == 0

    def proj(off, width):
        return jnp.dot(xb, w_ref[0, :, off:off + width], preferred_element_type=F32)

    def rope_store(off, width, out_ref, scale):
        y = proj(off, width)
        for j in range(width // LANES):
            yj = y[:, j * LANES:(j + 1) * LANES]
            sw = jnp.where(first_half, pltpu.roll(yj, LANES - SWA_HD // 2, 1), pltpu.roll(yj, SWA_HD // 2, 1))
            r = yj * cos + sw * sin
            if scale != 1.0:
                r = r * scale
            out_ref[:, j * LANES:(j + 1) * LANES] = r.astype(out_ref.dtype)

    def plain_store(off, width, out_ref):
        out_ref[...] = proj(off, width).astype(out_ref.dtype)

    rope_store(OFF_RQ, RET_QK_W, rq_ref, 1.0)
    rope_store(OFF_RK, RET_QK_W, rk_ref, RET_DK ** -0.5)
    plain_store(OFF_RV, RET_V_W, rv_ref)
    plain_store(OFF_RG, RET_V_W, rg_ref)
    rope_store(OFF_SQ, SWA_Q_W, sq_ref, SWA_HD ** -0.5)
    rope_store(OFF_SK, SWA_KV_W, sk_ref, 1.0)
    plain_store(OFF_SV, SWA_KV_W, sv_ref)
    plain_store(OFF_MQ, MEM_W, mq_ref)
    plain_store(OFF_GR, D_MODEL, gr_ref)
    plain_store(OFF_GS, D_MODEL, gs_ref)
    plain_store(OFF_GM, D_MODEL, gm_ref)


def _inproj(x2d, w_bf, cos_tab, sin_tab, tm, qkv_dtype, cast_jobs=()):
    m = x2d.shape[0]
    n_steps = m // tm
    n_tab = cos_tab.shape[0] // tm
    row = lambda w: pl.BlockSpec((tm, w), lambda i: (i, 0))
    tab = pl.BlockSpec((tm, LANES), lambda i: (i % n_tab, 0))
    widths_dtypes = [(RET_QK_W, qkv_dtype), (RET_QK_W, qkv_dtype), (RET_V_W, qkv_dtype), (RET_V_W, F32),
                     (SWA_Q_W, qkv_dtype), (SWA_KV_W, F32), (SWA_KV_W, F32), (MEM_W, qkv_dtype),
                     (D_MODEL, F32), (D_MODEL, F32), (D_MODEL, F32)]
    assert len(widths_dtypes) == N_INPROJ_OUT
    slab = lambda a: (1, a.shape[1] // n_steps, a.shape[2])
    cast_in = [pl.BlockSpec(slab(a), lambda i, layer=layer: (layer, i, 0)) for a, layer in cast_jobs]
    cast_out = [pl.BlockSpec(slab(a), lambda i: (0, i, 0)) for a, _ in cast_jobs]
    return pl.pallas_call(
        functools.partial(_inproj_kernel, len(cast_jobs)),
        grid=(n_steps,),
        in_specs=[row(D_MODEL), _layer_spec(w_bf[0].shape, w_bf[1]), tab, tab] + cast_in,
        out_specs=[row(w) for w, _ in widths_dtypes] + cast_out,
        out_shape=([jax.ShapeDtypeStruct((m, w), dt) for w, dt in widths_dtypes]
                   + [jax.ShapeDtypeStruct((1,) + a.shape[1:], BF16) for a, _ in cast_jobs]),
        compiler_params=_params(1),
        name="inproj",
    )(x2d, w_bf[0], cos_tab, sin_tab, *[a for a, _ in cast_jobs])


def _group_norm(o, g_row):
    mu = jnp.mean(o, -1, keepdims=True)
    d = o - mu
    var = jnp.mean(d * d, -1, keepdims=True)
    return d * lax.rsqrt(var + GN_EPS) * g_row


def _ret_tables(n_rows, period):
    lg = np.log1p(-np.exp2(-5.0 - np.arange(RET_HEADS, dtype=np.float64)))
    r = np.arange(n_rows)
    t = (r % period).astype(np.float64)
    same = (r[:, None] // period) == (r[None, :] // period)
    diff = t[:, None] - t[None, :]
    decay = np.where((diff >= 0) & same, np.exp(lg[:, None, None] * np.maximum(diff, 0.0)), 0.0)
    rowdec = np.exp(lg[:, None] * (t[None, :] + 1.0))
    wend = np.exp(lg[:, None] * (period - 1.0 - t[None, :]))
    gl = np.exp(lg * period)
    rowdec = np.broadcast_to(rowdec[:, :, None], (RET_HEADS, n_rows, RET_DV))
    wend = np.broadcast_to(wend[:, :, None], (RET_HEADS, n_rows, RET_DK))
    gl = np.broadcast_to(gl[:, None, None], (RET_HEADS, 1, RET_DV))
    return decay, rowdec, wend, gl


def _ret_sample_kernel(n_seq, t_len, q_ref, k_ref, v_ref, s_ref, decay_ref, rowdec_ref, wend_ref, gl_ref,
                       o_ref, s_out_ref):
    rows = n_seq * t_len
    pair_dk, pair_dv = 2 * RET_DK, 2 * RET_DV
    lane_lo = lax.broadcasted_iota(jnp.int32, (rows, pair_dk), 1) < RET_DK
    row_lo = lax.broadcasted_iota(jnp.int32, (pair_dk, RET_DV), 0) < RET_DK
    for p in range(RET_HEADS // 2):
        qk = slice(p * pair_dk, (p + 1) * pair_dk)
        vv = slice(p * pair_dv, (p + 1) * pair_dv)
        q2f, k2f, v2f = q_ref[:, qk], k_ref[:, qk], v_ref[:, vv]
        q2, k2, v2 = q2f.astype(BF16), k2f.astype(BF16), v2f.astype(BF16)
        zk = jnp.zeros_like(k2)
        k_rows = jnp.concatenate([jnp.where(lane_lo, k2, zk), jnp.where(lane_lo, zk, k2)], 0)
        sc2 = lax.dot_general(q2, k_rows, (((1,), (1,)), ((), ())), preferred_element_type=F32) * decay_ref[p]
        zv = jnp.zeros((rows, RET_DV), BF16)
        v_bd = jnp.concatenate([jnp.concatenate([v2[:, :RET_DV], zv], 1),
                                jnp.concatenate([zv, v2[:, RET_DV:]], 1)], 0)
        kw2f = k2f * wend_ref[p]
        o_state = []
        for b in range(n_seq):
            r = slice(b * t_len, (b + 1) * t_len)
            s2 = s_ref[0, b, 2 * p:2 * p + 2].reshape(pair_dk, RET_DV)
            s2b = s2.astype(BF16)
            zs = jnp.zeros_like(s2b)
            s_bd = jnp.concatenate([jnp.where(row_lo, s2b, zs), jnp.where(row_lo, zs, s2b)], 1)
            o_state.append(jnp.dot(q2f[r].astype(BF16), s_bd, preferred_element_type=F32))
            upd = lax.dot_general(kw2f[r].astype(BF16), v2f[r].astype(BF16), (((0,), (0,)), ((), ())),
                                  preferred_element_type=F32)
            s_new = (gl_ref[p] * s2 + jnp.where(row_lo, upd[:, :RET_DV], upd[:, RET_DV:])).reshape(2, RET_DK, RET_DV)
            for d in range(s_out_ref.shape[0]):
                s_out_ref[d, b, 2 * p:2 * p + 2] = s_new
        o2 = jnp.dot(sc2.astype(BF16), v_bd, preferred_element_type=F32)
        o_ref[:, vv] = o2 + jnp.concatenate(o_state, 0) * rowdec_ref[p]


def _stacked_out_specs(shape, layer, n_seq):
    tail = tuple(shape[2:])
    zeros = (0,) * len(tail)
    if layer == 0:
        return pl.BlockSpec((shape[0], n_seq) + tail, lambda i: (0, i) + zeros)
    return pl.BlockSpec((1, n_seq) + tail, lambda i: (layer, i) + zeros)


def _swa_sample_kernel(n_seq, t_len, sinks_ref, q_ref, kn_ref, vn_ref, kt_ref, vt_ref, *rest):
    o_ref, kto_ref, vto_ref = rest[-3:]
    grp_rows = SWA_GROUP * t_len
    n_all = n_seq * grp_rows
    q = q_ref[...]
    kn = kn_ref[...]
    vn = vn_ref[...]
    kn_t = kn.T
    vn_t = vn.T
    r = lax.broadcasted_iota(jnp.int32, (n_all, 1), 0)
    t_q = r % t_len
    g_row = (r // t_len) % SWA_GROUP
    b_row = r // grp_rows
    c = lax.broadcasted_iota(jnp.int32, (1, WINDOW), 1)
    valid_cache = c > t_q
    valid_new = ((c // t_len) == b_row) & ((c % t_len) <= t_q)
    lane = lax.broadcasted_iota(jnp.int32, (SWA_HD, WINDOW), 1)
    is_new_lane = lane >= WINDOW - t_len
    head_dims = [slice(kvh * SWA_HD, (kvh + 1) * SWA_HD) for kvh in range(SWA_KV_HEADS)]
    scores = []
    for kvh, hd in enumerate(head_dims):
        qg = [q[:, (kvh * SWA_GROUP + g) * SWA_HD:(kvh * SWA_GROUP + g + 1) * SWA_HD] for g in range(SWA_GROUP)]
        q_all = jnp.concatenate([qg[g][b * t_len:(b + 1) * t_len] for b in range(n_seq) for g in range(SWA_GROUP)],
                                0).astype(BF16)
        s_new = jnp.dot(q_all, kn_t[hd].astype(BF16), preferred_element_type=F32)
        s_cache = jnp.concatenate(
            [jnp.dot(q_all[b * grp_rows:(b + 1) * grp_rows], kt_ref[0, b, kvh].astype(BF16),
                     preferred_element_type=F32) for b in range(n_seq)], 0)
        scores.append((s_new, s_cache))
    probs = []
    for kvh, (s_new, s_cache) in enumerate(scores):
        s_new = jnp.where(valid_new, s_new, -jnp.inf)
        s_cache = jnp.where(valid_cache, s_cache, -jnp.inf)
        sink = jnp.full((n_all, 1), sinks_ref[kvh * SWA_GROUP], F32)
        for g in range(1, SWA_GROUP):
            sink = jnp.where(g_row == g, sinks_ref[kvh * SWA_GROUP + g], sink)
        m = jnp.maximum(jnp.maximum(jnp.max(s_new, -1, keepdims=True), jnp.max(s_cache, -1, keepdims=True)), sink)
        e_new = jnp.exp(s_new - m)
        e_cache = jnp.exp(s_cache - m)
        den = jnp.sum(e_new, -1, keepdims=True) + jnp.sum(e_cache, -1, keepdims=True) + jnp.exp(sink - m)
        probs.append(((e_new / den).astype(BF16), (e_cache / den).astype(BF16)))
    pieces = []
    for kvh, hd in enumerate(head_dims):
        p_new, p_cache = probs[kvh]
        o = jnp.dot(p_new, vn[:, hd].astype(BF16), preferred_element_type=F32)
        o = o + jnp.concatenate(
            [lax.dot_general(p_cache[b * grp_rows:(b + 1) * grp_rows], vt_ref[0, b, kvh].astype(BF16),
                             (((1,), (1,)), ((), ())), preferred_element_type=F32) for b in range(n_seq)], 0)
        for g in range(SWA_GROUP):
            pieces.append(jnp.concatenate(
                [o[b * grp_rows + g * t_len:b * grp_rows + (g + 1) * t_len] for b in range(n_seq)], 0))
    for kvh, hd in enumerate(head_dims):
        for b in range(n_seq):
            shift_new = (WINDOW - t_len - b * t_len) % WINDOW
            k_slid = jnp.where(is_new_lane, pltpu.roll(kn_t[hd], shift_new, 1),
                               pltpu.roll(kt_ref[0, b, kvh], WINDOW - t_len, 1))
            v_slid = jnp.where(is_new_lane, pltpu.roll(vn_t[hd], shift_new, 1),
                               pltpu.roll(vt_ref[0, b, kvh], WINDOW - t_len, 1))
            for d in range(kto_ref.shape[0]):
                kto_ref[d, b, kvh] = k_slid
                vto_ref[d, b, kvh] = v_slid
    o_ref[...] = jnp.concatenate(pieces, -1).astype(o_ref.dtype)


N_MIX_SAMPLE_IN = 14


def _mix_sample_kernel(n_seq, t_len, sinks_ref, rq_ref, rk_ref, rv_ref, s_ref, decay_ref, rowdec_ref, wend_ref,
                       gl_ref, sq_ref, kn_ref, vn_ref, kt_ref, vt_ref, *rest):
    ret_o, s_out, swa_o, kto, vto = rest[-5:]
    _ret_sample_kernel(n_seq, t_len, rq_ref, rk_ref, rv_ref, s_ref, decay_ref, rowdec_ref, wend_ref, gl_ref,
                       ret_o, s_out)
    _swa_sample_kernel(n_seq, t_len, sinks_ref, sq_ref, kn_ref, vn_ref, kt_ref, vt_ref, swa_o, kto, vto)


def _mix_sample(rq, rk, rv, state, pair_tables, sq, sk, sv, cache_kt, cache_vt, sinks, prev_out, layer, t_len, n_seq):
    m = rq.shape[0]
    rows = n_seq * t_len
    assert rows == WINDOW and cache_kt.shape[-1] == WINDOW
    decay2, rowdec2, wend2, gl2 = pair_tables
    row = lambda w: pl.BlockSpec((rows, w), lambda i: (i, 0))
    st_in = pl.BlockSpec((1, n_seq, RET_HEADS, RET_DK, RET_DV), lambda i: (layer, i, 0, 0, 0))
    cin = pl.BlockSpec((1, n_seq, SWA_KV_HEADS, SWA_HD, WINDOW), lambda i: (layer, i, 0, 0, 0))
    in_specs = [pl.BlockSpec(memory_space=pltpu.SMEM), row(RET_QK_W), row(RET_QK_W), row(RET_V_W), st_in,
                _const_spec(decay2.shape), _const_spec(rowdec2.shape), _const_spec(wend2.shape),
                _const_spec(gl2.shape), row(SWA_Q_W), row(SWA_KV_W), row(SWA_KV_W), cin, cin]
    args = [sinks, rq, rk, rv, state, decay2, rowdec2, wend2, gl2, sq, sk, sv, cache_kt, cache_vt]
    assert len(args) == N_MIX_SAMPLE_IN
    stacked = (state, cache_kt, cache_vt)
    aliases = {}
    if prev_out is not None:
        in_specs += [pl.BlockSpec(memory_space=pl.ANY)] * len(stacked)
        args += list(prev_out)
        aliases = {N_MIX_SAMPLE_IN: 1, N_MIX_SAMPLE_IN + 1: 3, N_MIX_SAMPLE_IN + 2: 4}
    st_out, k_out, v_out = (_stacked_out_specs(a.shape, layer, n_seq) for a in stacked)
    shape = lambda a: jax.ShapeDtypeStruct(a.shape, F32)
    ret_o, s_new, swa_o, k_new, v_new = pl.pallas_call(
        functools.partial(_mix_sample_kernel, n_seq, t_len),
        grid=(m // rows,),
        in_specs=in_specs,
        out_specs=[row(RET_V_W), st_out, row(SWA_Q_W), k_out, v_out],
        out_shape=[jax.ShapeDtypeStruct((m, RET_V_W), F32), shape(state),
                   jax.ShapeDtypeStruct((m, SWA_Q_W), BF16), shape(cache_kt), shape(cache_vt)],
        input_output_aliases=aliases,
        compiler_params=_params(1),
        name="mix_sample",
    )(*args)
    return ret_o, swa_o, (s_new, k_new, v_new)


def _pair_tables(tables):
    decay, rowdec, wend, gl = tables
    pair = lambda a: np.concatenate([a[0::2], a[1::2]], -1)
    gl_rows = np.concatenate([np.broadcast_to(gl[0::2], (RET_HEADS // 2, RET_DK, RET_DV)),
                              np.broadcast_to(gl[1::2], (RET_HEADS // 2, RET_DK, RET_DV))], 1)
    return tuple(jnp.asarray(a, F32) for a in (pair(decay), pair(rowdec), pair(wend), gl_rows))


def _run_staged(tasks):
    active = list(tasks)
    while active:
        for t in list(active):
            try:
                next(t)
            except StopIteration:
                active.remove(t)


def _mix_prompt_kernel(nb, t_len, sinks_ref, rq_ref, rk_ref, rv_ref, sq_ref, sk_ref, sv_ref, mq_ref, memx_ref, wkv_ref,
                       decay_ref, rowdec_ref, wend_ref, gl_ref, smq_ref, smk_ref, smv_ref,
                       ret_out, swa_out, mem_out, s_out, smem_out, mk_out, mv_out, skt_out, svt_out,
                       s_scr, kp_scr, kpr_scr, vp_scr, vpr_scr, mk_scr, mv_scr):
    c = pl.program_id(0)

    @pl.when(c == 0)
    def _():
        s_scr[...] = jnp.zeros_like(s_scr)
        for scr in (kp_scr, kpr_scr, vp_scr, vpr_scr):
            scr[...] = jnp.zeros_like(scr)
        kv = jnp.dot(memx_ref[...].astype(BF16), wkv_ref[0], preferred_element_type=F32)
        mk_out[...] = kv[:, :MEM_W]
        mv_out[...] = kv[:, MEM_W:]
        mk_scr[...] = kv[:, :MEM_W].astype(BF16).reshape(mk_scr.shape)
        mv_scr[...] = kv[:, MEM_W:].astype(BF16).reshape(mv_scr.shape)

    pair_w = 2 * SWA_HD
    lane_lo = lax.broadcasted_iota(jnp.int32, (RET_CHUNK, pair_w), 1) < SWA_HD
    row_lo = lax.broadcasted_iota(jnp.int32, (2 * RET_DK, RET_DV), 0) < RET_DK
    lane_lo_kv = lax.broadcasted_iota(jnp.int32, (2 * WINDOW, pair_w), 1) < SWA_HD
    upper = (lax.broadcasted_iota(jnp.int32, (WINDOW, WINDOW), 1)
             > lax.broadcasted_iota(jnp.int32, (WINDOW, WINDOW), 0))
    prev_bias = jnp.where(c > 0, 0.0, -jnp.inf)


    def ret_task(b, p):
        qk = slice(p * 2 * RET_DK, (p + 1) * 2 * RET_DK)
        vv = slice(p * 2 * RET_DV, (p + 1) * 2 * RET_DV)
        q2, k2, v2, s2 = rq_ref[b, :, qk], rk_ref[b, :, qk], rv_ref[b, :, vv], s_scr[b, p]
        zk = jnp.zeros_like(k2)
        k_rows = jnp.concatenate([jnp.where(lane_lo, k2, zk), jnp.where(lane_lo, zk, k2)], 0)
        sc_raw = lax.dot_general(q2, k_rows, (((1,), (1,)), ((), ())), preferred_element_type=F32)
        s2b = s2.astype(BF16)
        zs = jnp.zeros_like(s2b)
        s_bd = jnp.concatenate([jnp.where(row_lo, s2b, zs), jnp.where(row_lo, zs, s2b)], 1)
        os_raw = jnp.dot(q2, s_bd, preferred_element_type=F32)
        kw2 = (k2.astype(F32) * wend_ref[p]).astype(BF16)
        upd = lax.dot_general(kw2, v2, (((0,), (0,)), ((), ())), preferred_element_type=F32)
        yield
        zv = jnp.zeros((RET_CHUNK, RET_DV), v2.dtype)
        v_bd = jnp.concatenate([jnp.concatenate([v2[:, :RET_DV], zv], 1),
                                jnp.concatenate([zv, v2[:, RET_DV:]], 1)], 0)
        o_raw = jnp.dot((sc_raw * decay_ref[p]).astype(BF16), v_bd, preferred_element_type=F32)
        s_scr[b, p] = gl_ref[p] * s2 + jnp.where(row_lo, upd[:, :RET_DV], upd[:, RET_DV:])
        yield
        ret_out[b, :, vv] = o_raw + os_raw * rowdec_ref[p]

    kv_ctx = {}

    def swa_prep(b):
        k_cur, v_cur = sk_ref[b], sv_ref[b]
        kb, kbr = k_cur.astype(BF16), pltpu.roll(k_cur, SWA_HD, 1).astype(BF16)
        vb, vbr = v_cur.astype(BF16), pltpu.roll(v_cur, SWA_HD, 1).astype(BF16)
        kv_ctx[b] = (jnp.concatenate([kp_scr[b], kb], 0), jnp.concatenate([kpr_scr[b], kbr], 0),
                     jnp.concatenate([vp_scr[b], vb], 0), jnp.concatenate([vpr_scr[b], vbr], 0))
        kp_scr[b], kpr_scr[b], vp_scr[b], vpr_scr[b] = kb, kbr, vb, vbr

    def swa_task(b, kvh):
        if b not in kv_ctx:
            swa_prep(b)
        kc, kcr, vc, vcr = kv_ctx[b]
        zkv = jnp.zeros_like(kc)
        k_lo, k_hi = (kc, kcr) if kvh == 0 else (kcr, kc)
        v_lo, v_hi = (vc, vcr) if kvh == 0 else (vcr, vc)
        k_rows = jnp.concatenate([jnp.where(lane_lo_kv, k_lo, zkv), jnp.where(lane_lo_kv, zkv, k_hi)], 0)
        v_rows = jnp.concatenate([jnp.where(lane_lo_kv, v_lo, zkv), jnp.where(lane_lo_kv, zkv, v_hi)], 0)
        n_pairs = SWA_GROUP // 2
        pairs = [kvh * n_pairs + jj for jj in range(n_pairs)]
        q4 = jnp.concatenate([sq_ref[b, :, pr * pair_w:(pr + 1) * pair_w] for pr in pairs], 0)
        s4 = lax.dot_general(q4, k_rows, (((1,), (1,)), ((), ())), preferred_element_type=F32)
        yield
        rows, inv = [], []
        for jj, pr in enumerate(pairs):
            ps, inv_u = [], []
            for u in range(2):
                blk = s4[jj * WINDOW:(jj + 1) * WINDOW, u * 2 * WINDOW:(u + 1) * 2 * WINDOW]
                s = jnp.where(upper, blk[:, :WINDOW] + prev_bias, blk[:, WINDOW:])
                sink = sinks_ref[2 * pr + u]
                m = jnp.maximum(jnp.max(s, -1, keepdims=True), sink)
                e = jnp.exp(s - m)
                den = jnp.sum(e, -1, keepdims=True) + jnp.exp(sink - m)
                ps += [jnp.where(upper, e, 0.0).astype(BF16), jnp.where(upper, 0.0, e).astype(BF16)]
                inv_u.append(1.0 / den)
            rows.append(jnp.concatenate(ps, 1))
            inv.append(jnp.where(lane_lo, inv_u[0], inv_u[1]))
        o4 = jnp.dot(jnp.concatenate(rows, 0), v_rows, preferred_element_type=F32)
        yield
        for jj, pr in enumerate(pairs):
            swa_out[b, :, pr * pair_w:(pr + 1) * pair_w] = (o4[jj * WINDOW:(jj + 1) * WINDOW] * inv[jj]).astype(
                swa_out.dtype)

    def mem_task(b, h):
        sl = slice(h * MEM_HD, (h + 1) * MEM_HD)
        s = lax.dot_general(mq_ref[b, :, sl], mk_scr[b, :, sl], (((1,), (1,)), ((), ())),
                            preferred_element_type=F32) * (MEM_HD ** -0.5)
        yield
        m = jnp.max(s, -1, keepdims=True)
        e = jnp.exp(s - m)
        inv = 1.0 / jnp.sum(e, -1, keepdims=True)
        o = jnp.dot(e.astype(BF16), mv_scr[b, :, sl], preferred_element_type=F32)
        yield
        mem_out[b, :, sl] = (o * inv).astype(mem_out.dtype)

    head_of_row = lax.broadcasted_iota(jnp.int32, (MEM_HEADS * t_len, 1), 0) // t_len
    head_of_col = lax.broadcasted_iota(jnp.int32, (1, N_MEM * MEM_HEADS), 1) % MEM_HEADS
    valid_smem = head_of_row == head_of_col

    def smem_task(j):
        r = slice(j * t_len, (j + 1) * t_len)
        qb = smq_ref[r, :]
        q_all = jnp.concatenate([qb[:, h * MEM_HD:(h + 1) * MEM_HD] for h in range(MEM_HEADS)], 0).astype(BF16)
        s = lax.dot_general(q_all, smk_ref[0, j].astype(BF16), (((1,), (1,)), ((), ())),
                            preferred_element_type=F32) * (MEM_HD ** -0.5)
        yield
        s = jnp.where(valid_smem, s, -jnp.inf)
        m = jnp.max(s, -1, keepdims=True)
        e = jnp.exp(s - m)
        inv = 1.0 / jnp.sum(e, -1, keepdims=True)
        o = jnp.dot(e.astype(BF16), smv_ref[0, j].astype(BF16), preferred_element_type=F32) * inv
        yield
        smem_out[r, :] = jnp.concatenate([o[h * t_len:(h + 1) * t_len] for h in range(MEM_HEADS)], -1)

    n_smem = smk_ref.shape[1]
    for b in range(nb):
        _run_staged([ret_task(b, 0), swa_task(b, 0), mem_task(b, 0), ret_task(b, 1), mem_task(b, 1),
                     ret_task(b, 2), swa_task(b, 1), mem_task(b, 2), ret_task(b, 3), mem_task(b, 3)]
                    + [smem_task(j) for j in range(b, n_smem, nb)])

    @pl.when(c == pl.num_programs(0) - 1)
    def _():
        s_out[...] = s_scr[...]
        for b in range(nb):
            skt_out[b] = sk_ref[b].T.reshape(SWA_KV_HEADS, SWA_HD, WINDOW)
            svt_out[b] = sv_ref[b].T.reshape(SWA_KV_HEADS, SWA_HD, WINDOW)


def _mix_prompt(rq, rk, rv, sq, sk, sv, mq, mem2d, w_kv, sinks, pair_tables, smq, cache_mk, cache_mv, layer, t_len):
    nb, seq, _ = rq.shape
    n_steps = seq // RET_CHUNK
    n_smem = cache_mk.shape[1] // n_steps
    decay2, rowdec2, wend2, gl2 = pair_tables
    w_kv, kv_layer = w_kv
    chunk = lambda w: pl.BlockSpec((nb, RET_CHUNK, w), lambda c: (0, c, 0))
    srow = pl.BlockSpec((n_smem * t_len, MEM_W), lambda c: (c, 0))
    skv = pl.BlockSpec((1, n_smem, N_MEM * MEM_HEADS, MEM_HD), lambda c: (layer, c, 0, 0))
    st_shape = (nb, RET_HEADS // 2, 2 * RET_DK, RET_DV)
    mem_shape = (mem2d.shape[0], MEM_W)
    tail_shape = (nb, SWA_KV_HEADS, SWA_HD, WINDOW)
    assert RET_CHUNK == WINDOW
    kv_scr = pltpu.VMEM((nb, WINDOW, SWA_KV_W), BF16)
    mem_scr = pltpu.VMEM((nb, mem2d.shape[0] // nb, MEM_W), BF16)
    return pl.pallas_call(
        functools.partial(_mix_prompt_kernel, nb, t_len),
        grid=(n_steps,),
        in_specs=[pl.BlockSpec(memory_space=pltpu.SMEM),
                  chunk(RET_QK_W), chunk(RET_QK_W), chunk(RET_V_W), chunk(SWA_Q_W), chunk(SWA_KV_W), chunk(SWA_KV_W),
                  chunk(MEM_W), _const_spec(mem2d.shape), _layer_spec(w_kv.shape, kv_layer),
                  _const_spec(decay2.shape), _const_spec(rowdec2.shape), _const_spec(wend2.shape),
                  _const_spec(gl2.shape), srow, skv, skv],
        out_specs=[chunk(RET_V_W), chunk(SWA_Q_W), chunk(MEM_W), _const_spec(st_shape), srow,
                   _const_spec(mem_shape), _const_spec(mem_shape), _const_spec(tail_shape), _const_spec(tail_shape)],
        out_shape=[jax.ShapeDtypeStruct((nb, seq, RET_V_W), F32), jax.ShapeDtypeStruct((nb, seq, SWA_Q_W), BF16),
                   jax.ShapeDtypeStruct((nb, seq, MEM_W), BF16), jax.ShapeDtypeStruct(st_shape, F32),
                   jax.ShapeDtypeStruct(smq.shape, F32),
                   jax.ShapeDtypeStruct(mem_shape, F32), jax.ShapeDtypeStruct(mem_shape, F32),
                   jax.ShapeDtypeStruct(tail_shape, F32), jax.ShapeDtypeStruct(tail_shape, F32)],
        scratch_shapes=[pltpu.VMEM(st_shape, F32), kv_scr, kv_scr, kv_scr, kv_scr, mem_scr, mem_scr],
        compiler_params=_params(1),
        name="mix_prompt",
    )(sinks, rq, rk, rv, sq, sk, sv, mq, mem2d, w_kv, decay2, rowdec2, wend2, gl2, smq, cache_mk, cache_mv)


def _layer_norm(x, g, b):
    mu = jnp.mean(x, -1, keepdims=True)
    d = x - mu
    var = jnp.mean(d * d, -1, keepdims=True)
    return d * lax.rsqrt(var + LN_EPS) * g + b


def _finish_kernel(x_ref, ret_ref, rg_ref, swa_ref, mem_ref, gr_ref, gs_ref, gm_ref, gng_ref,
                   wr_ref, ws_ref, wm_ref, wo_ref, l1g_ref, l1b_ref, wu_ref, wd_ref, l2g_ref, l2b_ref, o_ref):
    swa_b = jnp.dot(swa_ref[...].astype(BF16), ws_ref[0], preferred_element_type=F32)
    mem_b = jnp.dot(mem_ref[...].astype(BF16), wm_ref[0], preferred_element_type=F32)
    half = x_ref.shape[0] // 2
    ret_parts = []
    for r in (slice(0, half), slice(half, 2 * half)):
        rg = rg_ref[r, :]
        gn = jnp.concatenate([_group_norm(ret_ref[r, h * RET_DV:(h + 1) * RET_DV], gng_ref[0, h:h + 1, :])
                              for h in range(RET_HEADS)], -1)
        ret_in = (rg * jax.nn.sigmoid(rg) * gn).astype(BF16)
        ret_parts.append(jnp.dot(ret_in, wr_ref[0], preferred_element_type=F32))
    ret_b = jnp.concatenate(ret_parts, 0)
    merged = (jax.nn.sigmoid(gr_ref[...]) * ret_b + jax.nn.sigmoid(gs_ref[...]) * swa_b
              + jax.nn.sigmoid(gm_ref[...]) * mem_b)
    y = jnp.dot(merged.astype(BF16), wo_ref[0], preferred_element_type=F32)
    x1 = _layer_norm(ALPHA * x_ref[...] + y, l1g_ref[0], l1b_ref[0])
    x1b = x1.astype(BF16)
    n_slabs = 4
    ff = D_FF // n_slabs
    up = lambda c: jnp.square(jnp.maximum(
        jnp.dot(x1b, wu_ref[0, :, c * ff:(c + 1) * ff], preferred_element_type=F32), 0.0)).astype(BF16)
    down = lambda c, h: jnp.dot(h, wd_ref[0, c * ff:(c + 1) * ff, :], preferred_element_type=F32)
    h_next = up(0)
    acc = None
    for c in range(n_slabs - 1):
        h_cur, h_next = h_next, up(c + 1)
        d = down(c, h_cur)
        acc = d if acc is None else acc + d
    for r in (slice(0, half), slice(half, 2 * half)):
        d = jnp.dot(h_next[r], wd_ref[0, (n_slabs - 1) * ff:, :], preferred_element_type=F32)
        o_ref[r, :] = _layer_norm(ALPHA * x1[r] + acc[r] + d, l2g_ref[0], l2b_ref[0])


def _finish(x2d, gn, rg, swa_o, mem_o, g_r, g_s, g_m, lw, tm):
    m = x2d.shape[0]
    row = lambda w: pl.BlockSpec((tm, w), lambda i: (i, 0))
    return pl.pallas_call(
        _finish_kernel,
        grid=(m // tm,),
        in_specs=[row(D_MODEL), row(RET_V_W), row(RET_V_W), row(SWA_Q_W), row(MEM_W),
                  row(D_MODEL), row(D_MODEL), row(D_MODEL)] + [_layer_spec(a.shape, idx) for a, idx in lw],
        out_specs=row(D_MODEL),
        out_shape=jax.ShapeDtypeStruct((m, D_MODEL), F32),
        compiler_params=_params(1),
        name="finish",
    )(x2d, gn, rg, swa_o, mem_o, g_r, g_s, g_m, *[a for a, _ in lw])


def _rope_tables(pos, reps=1):
    half = SWA_HD // 2
    inv = np.power(ROPE_THETA, -np.arange(half, dtype=np.float64) / half)
    ang = np.asarray(pos, np.float64)[:, None] * inv[None, :]
    c, s = np.cos(ang), np.sin(ang)
    cos_t, sin_t = np.concatenate([c, c, c, c], -1), np.concatenate([-s, s, -s, s], -1)
    return jnp.asarray(np.tile(cos_t, (reps, 1)), F32), jnp.asarray(np.tile(sin_t, (reps, 1)), F32)


def kernel(x_prompt, x_sample, state_ret, cache_swa_k, cache_swa_v, cache_mem_k, cache_mem_v, mem_prompt,
           w_in, w_br_ret, w_br_swa, w_br_mem, w_out, w_mem_kv, attn_sinks, ret_gn_g,
           ln1_g, ln1_b, w_up, w_down, ln2_g, ln2_b):
    batch, seq, _ = x_prompt.shape
    dec_b, dec_t, _ = x_sample.shape
    tm_p, tm_s = 512, 256
    tm_fin = 512
    ret_seqs = RET_CHUNK // dec_t

    cos_p, sin_p = _rope_tables(np.arange(seq))
    cos_s, sin_s = _rope_tables(PAST_LEN + np.arange(dec_t), tm_s // dec_t)
    tab_p2 = _pair_tables(_ret_tables(RET_CHUNK, RET_CHUNK))
    tab_s2 = _pair_tables(_ret_tables(RET_CHUNK, dec_t))

    xp = x_prompt.reshape(batch * seq, D_MODEL)
    xs = x_sample.reshape(dec_b * dec_t, D_MODEL)
    mem2d = mem_prompt.reshape(batch * N_MEM, D_MODEL)
    cache_kt = jnp.transpose(cache_swa_k, (0, 1, 3, 4, 2))
    cache_vt = jnp.transpose(cache_swa_v, (0, 1, 3, 4, 2))
    cache_mk = cache_mem_k.reshape(DEPTH, dec_b, N_MEM * MEM_HEADS, MEM_HD)
    cache_mv = cache_mem_v.reshape(DEPTH, dec_b, N_MEM * MEM_HEADS, MEM_HD)

    ln_row = lambda a: a.reshape(DEPTH, 1, D_MODEL)
    late_weights = (w_br_ret, w_br_swa, w_br_mem, w_out, w_up, w_down)
    next_weights = (w_in, w_mem_kv)
    w_in_bf, w_mem_kv_bf = (w_in[:1].astype(BF16), 0), (w_mem_kv[:1].astype(BF16), 0)

    ret_p, swk_p, swv_p, mk_p, mv_p = [], [], [], [], []
    stacked_s = None
    for l in range(DEPTH):
        sinks = attn_sinks[l]
        cast_jobs = [(w, l) for w in late_weights] + ([(w, l + 1) for w in next_weights] if l + 1 < DEPTH else [])
        proj = _inproj(xp, w_in_bf, cos_p, sin_p, tm_p, BF16, cast_jobs)
        rq, rk, rv, rg, sq, sk, sv, mq, g_r, g_s, g_m = proj[:N_INPROJ_OUT]
        s_rq, s_rk, s_rv, s_rg, s_sq, s_sk, s_sv, s_mq, s_g_r, s_g_s, s_g_m = _inproj(
            xs, w_in_bf, cos_s, sin_s, tm_s, F32)
        cast = [(w, 0) for w in proj[N_INPROJ_OUT:]]
        wr_bf, ws_bf, wm_bf, wo_bf, wu_bf, wd_bf = cast[:len(late_weights)]
        lw = [(ret_gn_g, l), wr_bf, ws_bf, wm_bf, wo_bf, (ln_row(ln1_g), l), (ln_row(ln1_b), l),
              wu_bf, wd_bf, (ln_row(ln2_g), l), (ln_row(ln2_b), l)]

        by_seq = lambda a: a.reshape(batch, seq, a.shape[-1])
        gn, swa_o, mem_o, s_p, s_mem_o, mk, mv, skt, svt = _mix_prompt(
            by_seq(rq), by_seq(rk), by_seq(rv), by_seq(sq), by_seq(sk), by_seq(sv), by_seq(mq),
            mem2d, w_mem_kv_bf, sinks, tab_p2, s_mq, cache_mk, cache_mv, l, dec_t)
        if l + 1 < DEPTH:
            w_in_bf, w_mem_kv_bf = cast[len(late_weights):]
        flat = lambda a: a.reshape(batch * seq, a.shape[-1])
        xp = _finish(xp, flat(gn), rg, flat(swa_o), flat(mem_o), g_r, g_s, g_m, lw, tm_fin)
        ret_p.append(s_p.reshape(batch, RET_HEADS, RET_DK, RET_DV))
        swk_p.append(skt)
        swv_p.append(svt)
        mk_p.append(mk.reshape(batch, N_MEM, MEM_HEADS, MEM_HD))
        mv_p.append(mv.reshape(batch, N_MEM, MEM_HEADS, MEM_HD))

        s_gn, s_swa_o, stacked_s = _mix_sample(s_rq, s_rk, s_rv, state_ret, tab_s2, s_sq, s_sk, s_sv, cache_kt, cache_vt,
                                               sinks, stacked_s, l, dec_t, ret_seqs)
        xs = _finish(xs, s_gn, s_rg, s_swa_o, s_mem_o, s_g_r, s_g_s, s_g_m, lw, tm_s)

    from_t = lambda a: jnp.transpose(a, (0, 1, 4, 2, 3))
    ret_s, swk_s, swv_s = stacked_s
    return (xp.reshape(batch, seq, D_MODEL), xs.reshape(dec_b, dec_t, D_MODEL),
            jnp.stack(ret_p), from_t(jnp.stack(swk_p)), from_t(jnp.stack(swv_p)), jnp.stack(mk_p), jnp.stack(mv_p),
            ret_s, from_t(swk_s), from_t(swv_s))
```

```python
import functools

import jax
import jax.numpy as jnp
import numpy as np
from jax import lax
from jax.experimental import pallas as pl
from jax.experimental.pallas import tpu as pltpu

F32 = jnp.float32
BF16 = jnp.bfloat16

D_MODEL = 1024
DEPTH = 2
PAST_LEN = 16384
RET_HEADS = 8
RET_DK = 64
RET_DV = 128
RET_CHUNK = 128
SWA_HEADS = 8
SWA_KV_HEADS = 2
SWA_GROUP = SWA_HEADS // SWA_KV_HEADS
SWA_HD = 64
WINDOW = 128
MEM_HEADS = 4
MEM_HD = 128
N_MEM = 256
D_FF = 4 * D_MODEL
ROPE_THETA = 10000.0
LN_EPS = 1e-5
GN_EPS = 1e-5
ALPHA = (2 * DEPTH) ** 0.25

RET_QK_W = RET_HEADS * RET_DK
RET_V_W = RET_HEADS * RET_DV
SWA_Q_W = SWA_HEADS * SWA_HD
SWA_KV_W = SWA_KV_HEADS * SWA_HD
MEM_W = MEM_HEADS * MEM_HD
OFF_RQ = 0
OFF_RK = OFF_RQ + RET_QK_W
OFF_RV = OFF_RK + RET_QK_W
OFF_RG = OFF_RV + RET_V_W
OFF_SQ = OFF_RG + RET_V_W
OFF_SK = OFF_SQ + SWA_Q_W
OFF_SV = OFF_SK + SWA_KV_W
OFF_MQ = OFF_SV + SWA_KV_W
OFF_GR = OFF_MQ + MEM_W
OFF_GS = OFF_GR + D_MODEL
OFF_GM = OFF_GS + D_MODEL
IN_W = OFF_GM + D_MODEL

assert RET_DK == SWA_HD
LANES = 128
V7X_VMEM_LIMIT = 62 * 1024 * 1024


def _const_spec(shape):
    nd = len(shape)
    return pl.BlockSpec(shape, lambda *_: (0,) * nd, pipeline_mode=pl.Buffered(1))


def _layer_spec(shape, layer):
    nd = len(shape)
    return pl.BlockSpec((1,) + tuple(shape[1:]), lambda *_: (layer,) + (0,) * (nd - 1), pipeline_mode=pl.Buffered(1))


def _params(n_grid):
    return pltpu.CompilerParams(dimension_semantics=("arbitrary",) * n_grid, vmem_limit_bytes=V7X_VMEM_LIMIT)


N_INPROJ_OUT = 11


def _inproj_kernel(n_cast, x_ref, w_ref, cos_ref, sin_ref, *refs):
    cast_in, outs, cast_out = refs[:n_cast], refs[n_cast:n_cast + N_INPROJ_OUT], refs[n_cast + N_INPROJ_OUT:]
    rq_ref, rk_ref, rv_ref, rg_ref, sq_ref, sk_ref, sv_ref, mq_ref, gr_ref, gs_ref, gm_ref = outs
    for src, dst in zip(cast_in, cast_out):
        dst[...] = src[...].astype(dst.dtype)
    xb = x_ref[...].astype(BF16)
    cos = cos_ref[...]
    sin = sin_ref[...]
    lane = lax.broadcasted_iota(jnp.int32, cos.shape, 1)
    first_half = (lane & (SWA_HD // 2)) == 0

    def proj(off, width):
        return jnp.dot(xb, w_ref[0, :, off:off + width], preferred_element_type=F32)

    def rope_store(off, width, out_ref, scale):
        y = proj(off, width)
        for j in range(width // LANES):
            yj = y[:, j * LANES:(j + 1) * LANES]
            sw = jnp.where(first_half, pltpu.roll(yj, LANES - SWA_HD // 2, 1), pltpu.roll(yj, SWA_HD // 2, 1))
            r = yj * cos + sw * sin
            if scale != 1.0:
                r = r * scale
            out_ref[:, j * LANES:(j + 1) * LANES] = r.astype(out_ref.dtype)

    def plain_store(off, width, out_ref):
        out_ref[...] = proj(off, width).astype(out_ref.dtype)

    rope_store(OFF_RQ, RET_QK_W, rq_ref, 1.0)
    rope_store(OFF_RK, RET_QK_W, rk_ref, RET_DK ** -0.5)
    plain_store(OFF_RV, RET_V_W, rv_ref)
    plain_store(OFF_RG, RET_V_W, rg_ref)
    rope_store(OFF_SQ, SWA_Q_W, sq_ref, SWA_HD ** -0.5)
    rope_store(OFF_SK, SWA_KV_W, sk_ref, 1.0)
    plain_store(OFF_SV, SWA_KV_W, sv_ref)
    plain_store(OFF_MQ, MEM_W, mq_ref)
    plain_store(OFF_GR, D_MODEL, gr_ref)
    plain_store(OFF_GS, D_MODEL, gs_ref)
    plain_store(OFF_GM, D_MODEL, gm_ref)


def _inproj(x2d, w_bf, cos_tab, sin_tab, tm, qkv_dtype, cast_jobs=()):
    m = x2d.shape[0]
    n_steps = m // tm
    n_tab = cos_tab.shape[0] // tm
    row = lambda w: pl.BlockSpec((tm, w), lambda i: (i, 0))
    tab = pl.BlockSpec((tm, LANES), lambda i: (i % n_tab, 0))
    widths_dtypes = [(RET_QK_W, qkv_dtype), (RET_QK_W, qkv_dtype), (RET_V_W, qkv_dtype), (RET_V_W, F32),
                     (SWA_Q_W, qkv_dtype), (SWA_KV_W, F32), (SWA_KV_W, F32), (MEM_W, qkv_dtype),
                     (D_MODEL, F32), (D_MODEL, F32), (D_MODEL, F32)]
    assert len(widths_dtypes) == N_INPROJ_OUT
    slab = lambda a: (1, a.shape[1] // n_steps, a.shape[2])
    cast_in = [pl.BlockSpec(slab(a), lambda i, layer=layer: (layer, i, 0)) for a, layer in cast_jobs]
    cast_out = [pl.BlockSpec(slab(a), lambda i: (0, i, 0)) for a, _ in cast_jobs]
    return pl.pallas_call(
        functools.partial(_inproj_kernel, len(cast_jobs)),
        grid=(n_steps,),
        in_specs=[row(D_MODEL), _layer_spec(w_bf[0].shape, w_bf[1]), tab, tab] + cast_in,
        out_specs=[row(w) for w, _ in widths_dtypes] + cast_out,
        out_shape=([jax.ShapeDtypeStruct((m, w), dt) for w, dt in widths_dtypes]
                   + [jax.ShapeDtypeStruct((1,) + a.shape[1:], BF16) for a, _ in cast_jobs]),
        compiler_params=_params(1),
        name="inproj",
    )(x2d, w_bf[0], cos_tab, sin_tab, *[a for a, _ in cast_jobs])


def _group_norm(o, g_row):
    mu = jnp.mean(o, -1, keepdims=True)
    d = o - mu
    var = jnp.mean(d * d, -1, keepdims=True)
    return d * lax.rsqrt(var + GN_EPS) * g_row


def _ret_tables(n_rows, period):
    lg = np.log1p(-np.exp2(-5.0 - np.arange(RET_HEADS, dtype=np.float64)))
    r = np.arange(n_rows)
    t = (r % period).astype(np.float64)
    same = (r[:, None] // period) == (r[None, :] // period)
    diff = t[:, None] - t[None, :]
    decay = np.where((diff >= 0) & same, np.exp(lg[:, None, None] * np.maximum(diff, 0.0)), 0.0)
    rowdec = np.exp(lg[:, None] * (t[None, :] + 1.0))
    wend = np.exp(lg[:, None] * (period - 1.0 - t[None, :]))
    gl = np.exp(lg * period)
    rowdec = np.broadcast_to(rowdec[:, :, None], (RET_HEADS, n_rows, RET_DV))
    wend = np.broadcast_to(wend[:, :, None], (RET_HEADS, n_rows, RET_DK))
    gl = np.broadcast_to(gl[:, None, None], (RET_HEADS, 1, RET_DV))
    return decay, rowdec, wend, gl


def _ret_sample_kernel(n_seq, t_len, q_ref, k_ref, v_ref, s_ref, decay_ref, rowdec_ref, wend_ref, gl_ref,
                       o_ref, s_out_ref):
    rows = n_seq * t_len
    pair_dk, pair_dv = 2 * RET_DK, 2 * RET_DV
    lane_lo = lax.broadcasted_iota(jnp.int32, (rows, pair_dk), 1) < RET_DK
    row_lo = lax.broadcasted_iota(jnp.int32, (pair_dk, RET_DV), 0) < RET_DK
    for p in range(RET_HEADS // 2):
        qk = slice(p * pair_dk, (p + 1) * pair_dk)
        vv = slice(p * pair_dv, (p + 1) * pair_dv)
        q2f, k2f, v2f = q_ref[:, qk], k_ref[:, qk], v_ref[:, vv]
        q2, k2, v2 = q2f.astype(BF16), k2f.astype(BF16), v2f.astype(BF16)
        zk = jnp.zeros_like(k2)
        k_rows = jnp.concatenate([jnp.where(lane_lo, k2, zk), jnp.where(lane_lo, zk, k2)], 0)
        sc2 = lax.dot_general(q2, k_rows, (((1,), (1,)), ((), ())), preferred_element_type=F32) * decay_ref[p]
        zv = jnp.zeros((rows, RET_DV), BF16)
        v_bd = jnp.concatenate([jnp.concatenate([v2[:, :RET_DV], zv], 1),
                                jnp.concatenate([zv, v2[:, RET_DV:]], 1)], 0)
        kw2f = k2f * wend_ref[p]
        o_state = []
        for b in range(n_seq):
            r = slice(b * t_len, (b + 1) * t_len)
            s2 = s_ref[0, b, 2 * p:2 * p + 2].reshape(pair_dk, RET_DV)
            s2b = s2.astype(BF16)
            zs = jnp.zeros_like(s2b)
            s_bd = jnp.concatenate([jnp.where(row_lo, s2b, zs), jnp.where(row_lo, zs, s2b)], 1)
            o_state.append(jnp.dot(q2f[r].astype(BF16), s_bd, preferred_element_type=F32))
            upd = lax.dot_general(kw2f[r].astype(BF16), v2f[r].astype(BF16), (((0,), (0,)), ((), ())),
                                  preferred_element_type=F32)
            s_new = (gl_ref[p] * s2 + jnp.where(row_lo, upd[:, :RET_DV], upd[:, RET_DV:])).reshape(2, RET_DK, RET_DV)
            for d in range(s_out_ref.shape[0]):
                s_out_ref[d, b, 2 * p:2 * p + 2] = s_new
        o2 = jnp.dot(sc2.astype(BF16), v_bd, preferred_element_type=F32)
        o_ref[:, vv] = o2 + jnp.concatenate(o_state, 0) * rowdec_ref[p]


def _stacked_out_specs(shape, layer, n_seq):
    tail = tuple(shape[2:])
    zeros = (0,) * len(tail)
    if layer == 0:
        return pl.BlockSpec((shape[0], n_seq) + tail, lambda i: (0, i) + zeros)
    return pl.BlockSpec((1, n_seq) + tail, lambda i: (layer, i) + zeros)


def _swa_sample_kernel(n_seq, t_len, sinks_ref, q_ref, kn_ref, vn_ref, kt_ref, vt_ref, *rest):
    o_ref, kto_ref, vto_ref = rest[-3:]
    grp_rows = SWA_GROUP * t_len
    n_all = n_seq * grp_rows
    q = q_ref[...]
    kn = kn_ref[...]
    vn = vn_ref[...]
    kn_t = kn.T
    vn_t = vn.T
    r = lax.broadcasted_iota(jnp.int32, (n_all, 1), 0)
    t_q = r % t_len
    g_row = (r // t_len) % SWA_GROUP
    b_row = r // grp_rows
    c = lax.broadcasted_iota(jnp.int32, (1, WINDOW), 1)
    valid_cache = c > t_q
    valid_new = ((c // t_len) == b_row) & ((c % t_len) <= t_q)
    lane = lax.broadcasted_iota(jnp.int32, (SWA_HD, WINDOW), 1)
    is_new_lane = lane >= WINDOW - t_len
    head_dims = [slice(kvh * SWA_HD, (kvh + 1) * SWA_HD) for kvh in range(SWA_KV_HEADS)]
    scores = []
    for kvh, hd in enumerate(head_dims):
        qg = [q[:, (kvh * SWA_GROUP + g) * SWA_HD:(kvh * SWA_GROUP + g + 1) * SWA_HD] for g in range(SWA_GROUP)]
        q_all = jnp.concatenate([qg[g][b * t_len:(b + 1) * t_len] for b in range(n_seq) for g in range(SWA_GROUP)],
                                0).astype(BF16)
        s_new = jnp.dot(q_all, kn_t[hd].astype(BF16), preferred_element_type=F32)
        s_cache = jnp.concatenate(
            [jnp.dot(q_all[b * grp_rows:(b + 1) * grp_rows], kt_ref[0, b, kvh].astype(BF16),
                     preferred_element_type=F32) for b in range(n_seq)], 0)
        scores.append((s_new, s_cache))
    probs = []
    for kvh, (s_new, s_cache) in enumerate(scores):
        s_new = jnp.where(valid_new, s_new, -jnp.inf)
        s_cache = jnp.where(valid_cache, s_cache, -jnp.inf)
        sink = jnp.full((n_all, 1), sinks_ref[kvh * SWA_GROUP], F32)
        for g in range(1, SWA_GROUP):
            sink = jnp.where(g_row == g, sinks_ref[kvh * SWA_GROUP + g], sink)
        m = jnp.maximum(jnp.maximum(jnp.max(s_new, -1, keepdims=True), jnp.max(s_cache, -1, keepdims=True)), sink)
        e_new = jnp.exp(s_new - m)
        e_cache = jnp.exp(s_cache - m)
        den = jnp.sum(e_new, -1, keepdims=True) + jnp.sum(e_cache, -1, keepdims=True) + jnp.exp(sink - m)
        probs.append(((e_new / den).astype(BF16), (e_cache / den).astype(BF16)))
    pieces = []
    for kvh, hd in enumerate(head_dims):
        p_new, p_cache = probs[kvh]
        o = jnp.dot(p_new, vn[:, hd].astype(BF16), preferred_element_type=F32)
        o = o + jnp.concatenate(
            [lax.dot_general(p_cache[b * grp_rows:(b + 1) * grp_rows], vt_ref[0, b, kvh].astype(BF16),
                             (((1,), (1,)), ((), ())), preferred_element_type=F32) for b in range(n_seq)], 0)
        for g in range(SWA_GROUP):
            pieces.append(jnp.concatenate(
                [o[b * grp_rows + g * t_len:b * grp_rows + (g + 1) * t_len] for b in range(n_seq)], 0))
    for kvh, hd in enumerate(head_dims):
        for b in range(n_seq):
            shift_new = (WINDOW - t_len - b * t_len) % WINDOW
            k_slid = jnp.where(is_new_lane, pltpu.roll(kn_t[hd], shift_new, 1),
                               pltpu.roll(kt_ref[0, b, kvh], WINDOW - t_len, 1))
            v_slid = jnp.where(is_new_lane, pltpu.roll(vn_t[hd], shift_new, 1),
                               pltpu.roll(vt_ref[0, b, kvh], WINDOW - t_len, 1))
            for d in range(kto_ref.shape[0]):
                kto_ref[d, b, kvh] = k_slid
                vto_ref[d, b, kvh] = v_slid
    o_ref[...] = jnp.concatenate(pieces, -1).astype(o_ref.dtype)


N_MIX_SAMPLE_IN = 14


def _mix_sample_kernel(n_seq, t_len, sinks_ref, rq_ref, rk_ref, rv_ref, s_ref, decay_ref, rowdec_ref, wend_ref,
                       gl_ref, sq_ref, kn_ref, vn_ref, kt_ref, vt_ref, *rest):
    ret_o, s_out, swa_o, kto, vto = rest[-5:]
    _ret_sample_kernel(n_seq, t_len, rq_ref, rk_ref, rv_ref, s_ref, decay_ref, rowdec_ref, wend_ref, gl_ref,
                       ret_o, s_out)
    _swa_sample_kernel(n_seq, t_len, sinks_ref, sq_ref, kn_ref, vn_ref, kt_ref, vt_ref, swa_o, kto, vto)


def _mix_sample(rq, rk, rv, state, pair_tables, sq, sk, sv, cache_kt, cache_vt, sinks, prev_out, layer, t_len, n_seq):
    m = rq.shape[0]
    rows = n_seq * t_len
    assert rows == WINDOW and cache_kt.shape[-1] == WINDOW
    decay2, rowdec2, wend2, gl2 = pair_tables
    row = lambda w: pl.BlockSpec((rows, w), lambda i: (i, 0))
    st_in = pl.BlockSpec((1, n_seq, RET_HEADS, RET_DK, RET_DV), lambda i: (layer, i, 0, 0, 0))
    cin = pl.BlockSpec((1, n_seq, SWA_KV_HEADS, SWA_HD, WINDOW), lambda i: (layer, i, 0, 0, 0))
    in_specs = [pl.BlockSpec(memory_space=pltpu.SMEM), row(RET_QK_W), row(RET_QK_W), row(RET_V_W), st_in,
                _const_spec(decay2.shape), _const_spec(rowdec2.shape), _const_spec(wend2.shape),
                _const_spec(gl2.shape), row(SWA_Q_W), row(SWA_KV_W), row(SWA_KV_W), cin, cin]
    args = [sinks, rq, rk, rv, state, decay2, rowdec2, wend2, gl2, sq, sk, sv, cache_kt, cache_vt]
    assert len(args) == N_MIX_SAMPLE_IN
    stacked = (state, cache_kt, cache_vt)
    aliases = {}
    if prev_out is not None:
        in_specs += [pl.BlockSpec(memory_space=pl.ANY)] * len(stacked)
        args += list(prev_out)
        aliases = {N_MIX_SAMPLE_IN: 1, N_MIX_SAMPLE_IN + 1: 3, N_MIX_SAMPLE_IN + 2: 4}
    st_out, k_out, v_out = (_stacked_out_specs(a.shape, layer, n_seq) for a in stacked)
    shape = lambda a: jax.ShapeDtypeStruct(a.shape, F32)
    ret_o, s_new, swa_o, k_new, v_new = pl.pallas_call(
        functools.partial(_mix_sample_kernel, n_seq, t_len),
        grid=(m // rows,),
        in_specs=in_specs,
        out_specs=[row(RET_V_W), st_out, row(SWA_Q_W), k_out, v_out],
        out_shape=[jax.ShapeDtypeStruct((m, RET_V_W), F32), shape(state),
                   jax.ShapeDtypeStruct((m, SWA_Q_W), BF16), shape(cache_kt), shape(cache_vt)],
        input_output_aliases=aliases,
        compiler_params=_params(1),
        name="mix_sample",
    )(*args)
    return ret_o, swa_o, (s_new, k_new, v_new)


def _pair_tables(tables):
    decay, rowdec, wend, gl = tables
    pair = lambda a: np.concatenate([a[0::2], a[1::2]], -1)
    gl_rows = np.concatenate([np.broadcast_to(gl[0::2], (RET_HEADS // 2, RET_DK, RET_DV)),
                              np.broadcast_to(gl[1::2], (RET_HEADS // 2, RET_DK, RET_DV))], 1)
    return tuple(jnp.asarray(a, F32) for a in (pair(decay), pair(rowdec), pair(wend), gl_rows))


def _run_staged(tasks):
    active = list(tasks)
    while active:
        for t in list(active):
            try:
                next(t)
            except StopIteration:
                active.remove(t)


def _mix_prompt_kernel(nb, t_len, sinks_ref, rq_ref, rk_ref, rv_ref, sq_ref, sk_ref, sv_ref, mq_ref, memx_ref, wkv_ref,
                       decay_ref, rowdec_ref, wend_ref, gl_ref, smq_ref, smk_ref, smv_ref,
                       ret_out, swa_out, mem_out, s_out, smem_out, mk_out, mv_out, skt_out, svt_out,
                       s_scr, kp_scr, kpr_scr, vp_scr, vpr_scr, mk_scr, mv_scr):
    c = pl.program_id(0)

    @pl.when(c == 0)
    def _():
        s_scr[...] = jnp.zeros_like(s_scr)
        for scr in (kp_scr, kpr_scr, vp_scr, vpr_scr):
            scr[...] = jnp.zeros_like(scr)
        kv = jnp.dot(memx_ref[...].astype(BF16), wkv_ref[0], preferred_element_type=F32)
        mk_out[...] = kv[:, :MEM_W]
        mv_out[...] = kv[:, MEM_W:]
        mk_scr[...] = kv[:, :MEM_W].astype(BF16).reshape(mk_scr.shape)
        mv_scr[...] = kv[:, MEM_W:].astype(BF16).reshape(mv_scr.shape)

    pair_w = 2 * SWA_HD
    lane_lo = lax.broadcasted_iota(jnp.int32, (RET_CHUNK, pair_w), 1) < SWA_HD
    row_lo = lax.broadcasted_iota(jnp.int32, (2 * RET_DK, RET_DV), 0) < RET_DK
    lane_lo_kv = lax.broadcasted_iota(jnp.int32, (2 * WINDOW, pair_w), 1) < SWA_HD
    upper = (lax.broadcasted_iota(jnp.int32, (WINDOW, WINDOW), 1)
             > lax.broadcasted_iota(jnp.int32, (WINDOW, WINDOW), 0))
    prev_bias = jnp.where(c > 0, 0.0, -jnp.inf)


    def ret_task(b, p):
        qk = slice(p * 2 * RET_DK, (p + 1) * 2 * RET_DK)
        vv = slice(p * 2 * RET_DV, (p + 1) * 2 * RET_DV)
        q2, k2, v2, s2 = rq_ref[b, :, qk], rk_ref[b, :, qk], rv_ref[b, :, vv], s_scr[b, p]
        zk = jnp.zeros_like(k2)
        k_rows = jnp.concatenate([jnp.where(lane_lo, k2, zk), jnp.where(lane_lo, zk, k2)], 0)
        sc_raw = lax.dot_general(q2, k_rows, (((1,), (1,)), ((), ())), preferred_element_type=F32)
        s2b = s2.astype(BF16)
        zs = jnp.zeros_like(s2b)
        s_bd = jnp.concatenate([jnp.where(row_lo, s2b, zs), jnp.where(row_lo, zs, s2b)], 1)
        os_raw = jnp.dot(q2, s_bd, preferred_element_type=F32)
        kw2 = (k2.astype(F32) * wend_ref[p]).astype(BF16)
        upd = lax.dot_general(kw2, v2, (((0,), (0,)), ((), ())), preferred_element_type=F32)
        yield
        zv = jnp.zeros((RET_CHUNK, RET_DV), v2.dtype)
        v_bd = jnp.concatenate([jnp.concatenate([v2[:, :RET_DV], zv], 1),
                                jnp.concatenate([zv, v2[:, RET_DV:]], 1)], 0)
        o_raw = jnp.dot((sc_raw * decay_ref[p]).astype(BF16), v_bd, preferred_element_type=F32)
        s_scr[b, p] = gl_ref[p] * s2 + jnp.where(row_lo, upd[:, :RET_DV], upd[:, RET_DV:])
        yield
        ret_out[b, :, vv] = o_raw + os_raw * rowdec_ref[p]

    kv_ctx = {}

    def swa_prep(b):
        k_cur, v_cur = sk_ref[b], sv_ref[b]
        kb, kbr = k_cur.astype(BF16), pltpu.roll(k_cur, SWA_HD, 1).astype(BF16)
        vb, vbr = v_cur.astype(BF16), pltpu.roll(v_cur, SWA_HD, 1).astype(BF16)
        kv_ctx[b] = (jnp.concatenate([kp_scr[b], kb], 0), jnp.concatenate([kpr_scr[b], kbr], 0),
                     jnp.concatenate([vp_scr[b], vb], 0), jnp.concatenate([vpr_scr[b], vbr], 0))
        kp_scr[b], kpr_scr[b], vp_scr[b], vpr_scr[b] = kb, kbr, vb, vbr

    def swa_task(b, kvh):
        if b not in kv_ctx:
            swa_prep(b)
        kc, kcr, vc, vcr = kv_ctx[b]
        zkv = jnp.zeros_like(kc)
        k_lo, k_hi = (kc, kcr) if kvh == 0 else (kcr, kc)
        v_lo, v_hi = (vc, vcr) if kvh == 0 else (vcr, vc)
        k_rows = jnp.concatenate([jnp.where(lane_lo_kv, k_lo, zkv), jnp.where(lane_lo_kv, zkv, k_hi)], 0)
        v_rows = jnp.concatenate([jnp.where(lane_lo_kv, v_lo, zkv), jnp.where(lane_lo_kv, zkv, v_hi)], 0)
        n_pairs = SWA_GROUP // 2
        pairs = [kvh * n_pairs + jj for jj in range(n_pairs)]
        q4 = jnp.concatenate([sq_ref[b, :, pr * pair_w:(pr + 1) * pair_w] for pr in pairs], 0)
        s4 = lax.dot_general(q4, k_rows, (((1,), (1,)), ((), ())), preferred_element_type=F32)
        yield
        rows, inv = [], []
        for jj, pr in enumerate(pairs):
            ps, inv_u = [], []
            for u in range(2):
                blk = s4[jj * WINDOW:(jj + 1) * WINDOW, u * 2 * WINDOW:(u + 1) * 2 * WINDOW]
                s = jnp.where(upper, blk[:, :WINDOW] + prev_bias, blk[:, WINDOW:])
                sink = sinks_ref[2 * pr + u]
                m = jnp.maximum(jnp.max(s, -1, keepdims=True), sink)
                e = jnp.exp(s - m)
                den = jnp.sum(e, -1, keepdims=True) + jnp.exp(sink - m)
                ps += [jnp.where(upper, e, 0.0).astype(BF16), jnp.where(upper, 0.0, e).astype(BF16)]
                inv_u.append(1.0 / den)
            rows.append(jnp.concatenate(ps, 1))
            inv.append(jnp.where(lane_lo, inv_u[0], inv_u[1]))
        o4 = jnp.dot(jnp.concatenate(rows, 0), v_rows, preferred_element_type=F32)
        yield
        for jj, pr in enumerate(pairs):
            swa_out[b, :, pr * pair_w:(pr + 1) * pair_w] = (o4[jj * WINDOW:(jj + 1) * WINDOW] * inv[jj]).astype(
                swa_out.dtype)

    def mem_task(b, h):
        sl = slice(h * MEM_HD, (h + 1) * MEM_HD)
        s = lax.dot_general(mq_ref[b, :, sl], mk_scr[b, :, sl], (((1,), (1,)), ((), ())),
                            preferred_element_type=F32) * (MEM_HD ** -0.5)
        yield
        m = jnp.max(s, -1, keepdims=True)
        e = jnp.exp(s - m)
        inv = 1.0 / jnp.sum(e, -1, keepdims=True)
        o = jnp.dot(e.astype(BF16), mv_scr[b, :, sl], preferred_element_type=F32)
        yield
        mem_out[b, :, sl] = (o * inv).astype(mem_out.dtype)

    head_of_row = lax.broadcasted_iota(jnp.int32, (MEM_HEADS * t_len, 1), 0) // t_len
    head_of_col = lax.broadcasted_iota(jnp.int32, (1, N_MEM * MEM_HEADS), 1) % MEM_HEADS
    valid_smem = head_of_row == head_of_col

    def smem_task(j):
        r = slice(j * t_len, (j + 1) * t_len)
        qb = smq_ref[r, :]
        q_all = jnp.concatenate([qb[:, h * MEM_HD:(h + 1) * MEM_HD] for h in range(MEM_HEADS)], 0).astype(BF16)
        s = lax.dot_general(q_all, smk_ref[0, j].astype(BF16), (((1,), (1,)), ((), ())),
                            preferred_element_type=F32) * (MEM_HD ** -0.5)
        yield
        s = jnp.where(valid_smem, s, -jnp.inf)
        m = jnp.max(s, -1, keepdims=True)
        e = jnp.exp(s - m)
        inv = 1.0 / jnp.sum(e, -1, keepdims=True)
        o = jnp.dot(e.astype(BF16), smv_ref[0, j].astype(BF16), preferred_element_type=F32) * inv
        yield
        smem_out[r, :] = jnp.concatenate([o[h * t_len:(h + 1) * t_len] for h in range(MEM_HEADS)], -1)

    n_smem = smk_ref.shape[1]
    for b in range(nb):
        _run_staged([ret_task(b, 0), swa_task(b, 0), mem_task(b, 0), ret_task(b, 1), mem_task(b, 1),
                     ret_task(b, 2), swa_task(b, 1), mem_task(b, 2), ret_task(b, 3), mem_task(b, 3)]
                    + [smem_task(j) for j in range(b, n_smem, nb)])

    @pl.when(c == pl.num_programs(0) - 1)
    def _():
        s_out[...] = s_scr[...]
        for b in range(nb):
            skt_out[b] = sk_ref[b].T.reshape(SWA_KV_HEADS, SWA_HD, WINDOW)
            svt_out[b] = sv_ref[b].T.reshape(SWA_KV_HEADS, SWA_HD, WINDOW)


def _mix_prompt(rq, rk, rv, sq, sk, sv, mq, mem2d, w_kv, sinks, pair_tables, smq, cache_mk, cache_mv, layer, t_len):
    nb, seq, _ = rq.shape
    n_steps = seq // RET_CHUNK
    n_smem = cache_mk.shape[1] // n_steps
    decay2, rowdec2, wend2, gl2 = pair_tables
    w_kv, kv_layer = w_kv
    chunk = lambda w: pl.BlockSpec((nb, RET_CHUNK, w), lambda c: (0, c, 0))
    srow = pl.BlockSpec((n_smem * t_len, MEM_W), lambda c: (c, 0))
    skv = pl.BlockSpec((1, n_smem, N_MEM * MEM_HEADS, MEM_HD), lambda c: (layer, c, 0, 0))
    st_shape = (nb, RET_HEADS // 2, 2 * RET_DK, RET_DV)
    mem_shape = (mem2d.shape[0], MEM_W)
    tail_shape = (nb, SWA_KV_HEADS, SWA_HD, WINDOW)
    assert RET_CHUNK == WINDOW
    kv_scr = pltpu.VMEM((nb, WINDOW, SWA_KV_W), BF16)
    mem_scr = pltpu.VMEM((nb, mem2d.shape[0] // nb, MEM_W), BF16)
    return pl.pallas_call(
        functools.partial(_mix_prompt_kernel, nb, t_len),
        grid=(n_steps,),
        in_specs=[pl.BlockSpec(memory_space=pltpu.SMEM),
                  chunk(RET_QK_W), chunk(RET_QK_W), chunk(RET_V_W), chunk(SWA_Q_W), chunk(SWA_KV_W), chunk(SWA_KV_W),
                  chunk(MEM_W), _const_spec(mem2d.shape), _layer_spec(w_kv.shape, kv_layer),
                  _const_spec(decay2.shape), _const_spec(rowdec2.shape), _const_spec(wend2.shape),
                  _const_spec(gl2.shape), srow, skv, skv],
        out_specs=[chunk(RET_V_W), chunk(SWA_Q_W), chunk(MEM_W), _const_spec(st_shape), srow,
                   _const_spec(mem_shape), _const_spec(mem_shape), _const_spec(tail_shape), _const_spec(tail_shape)],
        out_shape=[jax.ShapeDtypeStruct((nb, seq, RET_V_W), F32), jax.ShapeDtypeStruct((nb, seq, SWA_Q_W), BF16),
                   jax.ShapeDtypeStruct((nb, seq, MEM_W), BF16), jax.ShapeDtypeStruct(st_shape, F32),
                   jax.ShapeDtypeStruct(smq.shape, F32),
                   jax.ShapeDtypeStruct(mem_shape, F32), jax.ShapeDtypeStruct(mem_shape, F32),
                   jax.ShapeDtypeStruct(tail_shape, F32), jax.ShapeDtypeStruct(tail_shape, F32)],
        scratch_shapes=[pltpu.VMEM(st_shape, F32), kv_scr, kv_scr, kv_scr, kv_scr, mem_scr, mem_scr],
        compiler_params=_params(1),
        name="mix_prompt",
    )(sinks, rq, rk, rv, sq, sk, sv, mq, mem2d, w_kv, decay2, rowdec2, wend2, gl2, smq, cache_mk, cache_mv)


def _sigmoid(x):
    return 0.5 * jnp.tanh(0.5 * x) + 0.5


def _layer_norm(x, g, b):
    mu = jnp.mean(x, -1, keepdims=True)
    d = x - mu
    var = jnp.mean(d * d, -1, keepdims=True)
    return d * lax.rsqrt(var + LN_EPS) * g + b


def _finish_kernel(x_ref, ret_ref, rg_ref, swa_ref, mem_ref, gr_ref, gs_ref, gm_ref, gng_ref,
                   wr_ref, ws_ref, wm_ref, wo_ref, l1g_ref, l1b_ref, wu_ref, wd_ref, l2g_ref, l2b_ref, o_ref):
    swa_b = jnp.dot(swa_ref[...].astype(BF16), ws_ref[0], preferred_element_type=F32)
    mem_b = jnp.dot(mem_ref[...].astype(BF16), wm_ref[0], preferred_element_type=F32)
    half = x_ref.shape[0] // 2
    ret_parts = []
    for r in (slice(0, half), slice(half, 2 * half)):
        rg = rg_ref[r, :]
        gn = jnp.concatenate([_group_norm(ret_ref[r, h * RET_DV:(h + 1) * RET_DV], gng_ref[0, h:h + 1, :])
                              for h in range(RET_HEADS)], -1)
        ret_in = (rg * _sigmoid(rg) * gn).astype(BF16)
        ret_parts.append(jnp.dot(ret_in, wr_ref[0], preferred_element_type=F32))
    ret_b = jnp.concatenate(ret_parts, 0)
    merged = _sigmoid(gr_ref[...]) * ret_b + _sigmoid(gs_ref[...]) * swa_b + _sigmoid(gm_ref[...]) * mem_b
    y = jnp.dot(merged.astype(BF16), wo_ref[0], preferred_element_type=F32)
    x1 = _layer_norm(ALPHA * x_ref[...] + y, l1g_ref[0], l1b_ref[0])
    x1b = x1.astype(BF16)
    n_slabs = 4
    ff = D_FF // n_slabs
    up = lambda c: jnp.square(jnp.maximum(
        jnp.dot(x1b, wu_ref[0, :, c * ff:(c + 1) * ff], preferred_element_type=F32), 0.0)).astype(BF16)
    down = lambda c, h: jnp.dot(h, wd_ref[0, c * ff:(c + 1) * ff, :], preferred_element_type=F32)
    h_next = up(0)
    acc = None
    for c in range(n_slabs - 1):
        h_cur, h_next = h_next, up(c + 1)
        d = down(c, h_cur)
        acc = d if acc is None else acc + d
    for r in (slice(0, half), slice(half, 2 * half)):
        d = jnp.dot(h_next[r], wd_ref[0, (n_slabs - 1) * ff:, :], preferred_element_type=F32)
        o_ref[r, :] = _layer_norm(ALPHA * x1[r] + acc[r] + d, l2g_ref[0], l2b_ref[0])


def _finish(x2d, gn, rg, swa_o, mem_o, g_r, g_s, g_m, lw, tm):
    m = x2d.shape[0]
    row = lambda w: pl.BlockSpec((tm, w), lambda i: (i, 0))
    return pl.pallas_call(
        _finish_kernel,
        grid=(m // tm,),
        in_specs=[row(D_MODEL), row(RET_V_W), row(RET_V_W), row(SWA_Q_W), row(MEM_W),
                  row(D_MODEL), row(D_MODEL), row(D_MODEL)] + [_layer_spec(a.shape, idx) for a, idx in lw],
        out_specs=row(D_MODEL),
        out_shape=jax.ShapeDtypeStruct((m, D_MODEL), F32),
        compiler_params=_params(1),
        name="finish",
    )(x2d, gn, rg, swa_o, mem_o, g_r, g_s, g_m, *[a for a, _ in lw])


def _rope_tables(pos, reps=1):
    half = SWA_HD // 2
    inv = np.power(ROPE_THETA, -np.arange(half, dtype=np.float64) / half)
    ang = np.asarray(pos, np.float64)[:, None] * inv[None, :]
    c, s = np.cos(ang), np.sin(ang)
    cos_t, sin_t = np.concatenate([c, c, c, c], -1), np.concatenate([-s, s, -s, s], -1)
    return jnp.asarray(np.tile(cos_t, (reps, 1)), F32), jnp.asarray(np.tile(sin_t, (reps, 1)), F32)


def kernel(x_prompt, x_sample, state_ret, cache_swa_k, cache_swa_v, cache_mem_k, cache_mem_v, mem_prompt,
           w_in, w_br_ret, w_br_swa, w_br_mem, w_out, w_mem_kv, attn_sinks, ret_gn_g,
           ln1_g, ln1_b, w_up, w_down, ln2_g, ln2_b):
    batch, seq, _ = x_prompt.shape
    dec_b, dec_t, _ = x_sample.shape
    tm_p, tm_s = 512, 256
    tm_fin = 512
    ret_seqs = RET_CHUNK // dec_t

    cos_p, sin_p = _rope_tables(np.arange(seq))
    cos_s, sin_s = _rope_tables(PAST_LEN + np.arange(dec_t), tm_s // dec_t)
    tab_p2 = _pair_tables(_ret_tables(RET_CHUNK, RET_CHUNK))
    tab_s2 = _pair_tables(_ret_tables(RET_CHUNK, dec_t))

    xp = x_prompt.reshape(batch * seq, D_MODEL)
    xs = x_sample.reshape(dec_b * dec_t, D_MODEL)
    mem2d = mem_prompt.reshape(batch * N_MEM, D_MODEL)
    cache_kt = jnp.transpose(cache_swa_k, (0, 1, 3, 4, 2))
    cache_vt = jnp.transpose(cache_swa_v, (0, 1, 3, 4, 2))
    cache_mk = cache_mem_k.reshape(DEPTH, dec_b, N_MEM * MEM_HEADS, MEM_HD)
    cache_mv = cache_mem_v.reshape(DEPTH, dec_b, N_MEM * MEM_HEADS, MEM_HD)

    ln_row = lambda a: a.reshape(DEPTH, 1, D_MODEL)
    late_weights = (w_br_ret, w_br_swa, w_br_mem, w_out, w_up, w_down)
    next_weights = (w_in, w_mem_kv)
    w_in_bf, w_mem_kv_bf = (w_in[:1].astype(BF16), 0), (w_mem_kv[:1].astype(BF16), 0)

    ret_p, swk_p, swv_p, mk_p, mv_p = [], [], [], [], []
    stacked_s = None
    for l in range(DEPTH):
        sinks = attn_sinks[l]
        cast_jobs = [(w, l) for w in late_weights] + ([(w, l + 1) for w in next_weights] if l + 1 < DEPTH else [])
        proj = _inproj(xp, w_in_bf, cos_p, sin_p, tm_p, BF16, cast_jobs)
        rq, rk, rv, rg, sq, sk, sv, mq, g_r, g_s, g_m = proj[:N_INPROJ_OUT]
        s_rq, s_rk, s_rv, s_rg, s_sq, s_sk, s_sv, s_mq, s_g_r, s_g_s, s_g_m = _inproj(
            xs, w_in_bf, cos_s, sin_s, tm_s, F32)
        cast = [(w, 0) for w in proj[N_INPROJ_OUT:]]
        wr_bf, ws_bf, wm_bf, wo_bf, wu_bf, wd_bf = cast[:len(late_weights)]
        lw = [(ret_gn_g, l), wr_bf, ws_bf, wm_bf, wo_bf, (ln_row(ln1_g), l), (ln_row(ln1_b), l),
              wu_bf, wd_bf, (ln_row(ln2_g), l), (ln_row(ln2_b), l)]

        by_seq = lambda a: a.reshape(batch, seq, a.shape[-1])
        gn, swa_o, mem_o, s_p, s_mem_o, mk, mv, skt, svt = _mix_prompt(
            by_seq(rq), by_seq(rk), by_seq(rv), by_seq(sq), by_seq(sk), by_seq(sv), by_seq(mq),
            mem2d, w_mem_kv_bf, sinks, tab_p2, s_mq, cache_mk, cache_mv, l, dec_t)
        if l + 1 < DEPTH:
            w_in_bf, w_mem_kv_bf = cast[len(late_weights):]
        flat = lambda a: a.reshape(batch * seq, a.shape[-1])
        xp = _finish(xp, flat(gn), rg, flat(swa_o), flat(mem_o), g_r, g_s, g_m, lw, tm_fin)
        ret_p.append(s_p.reshape(batch, RET_HEADS, RET_DK, RET_DV))
        swk_p.append(skt)
        swv_p.append(svt)
        mk_p.append(mk.reshape(batch, N_MEM, MEM_HEADS, MEM_HD))
        mv_p.append(mv.reshape(batch, N_MEM, MEM_HEADS, MEM_HD))

        s_gn, s_swa_o, stacked_s = _mix_sample(s_rq, s_rk, s_rv, state_ret, tab_s2, s_sq, s_sk, s_sv, cache_kt, cache_vt,
                                               sinks, stacked_s, l, dec_t, ret_seqs)
        xs = _finish(xs, s_gn, s_rg, s_swa_o, s_mem_o, s_g_r, s_g_s, s_g_m, lw, tm_s)

    from_t = lambda a: jnp.transpose(a, (0, 1, 4, 2, 3))
    ret_s, swk_s, swv_s = stacked_s
    return (xp.reshape(batch, seq, D_MODEL), xs.reshape(dec_b, dec_t, D_MODEL),
            jnp.stack(ret_p), from_t(jnp.stack(swk_p)), from_t(jnp.stack(swv_p)), jnp.stack(mk_p), jnp.stack(mv_p),
            ret_s, from_t(swk_s), from_t(swv_s))
```

```python
import functools

import jax
import jax.numpy as jnp
import numpy as np
from jax import lax
from jax.experimental import pallas as pl
from jax.experimental.pallas import tpu as pltpu

F32 = jnp.float32
BF16 = jnp.bfloat16

D_MODEL = 1024
DEPTH = 2
PAST_LEN = 16384
RET_HEADS = 8
RET_DK = 64
RET_DV = 128
RET_CHUNK = 128
SWA_HEADS = 8
SWA_KV_HEADS = 2
SWA_GROUP = SWA_HEADS // SWA_KV_HEADS
SWA_HD = 64
WINDOW = 128
MEM_HEADS = 4
MEM_HD = 128
N_MEM = 256
D_FF = 4 * D_MODEL
ROPE_THETA = 10000.0
LN_EPS = 1e-5
GN_EPS = 1e-5
ALPHA = (2 * DEPTH) ** 0.25

RET_QK_W = RET_HEADS * RET_DK
RET_V_W = RET_HEADS * RET_DV
SWA_Q_W = SWA_HEADS * SWA_HD
SWA_KV_W = SWA_KV_HEADS * SWA_HD
MEM_W = MEM_HEADS * MEM_HD
OFF_RQ = 0
OFF_RK = OFF_RQ + RET_QK_W
OFF_RV = OFF_RK + RET_QK_W
OFF_RG = OFF_RV + RET_V_W
OFF_SQ = OFF_RG + RET_V_W
OFF_SK = OFF_SQ + SWA_Q_W
OFF_SV = OFF_SK + SWA_KV_W
OFF_MQ = OFF_SV + SWA_KV_W
OFF_GR = OFF_MQ + MEM_W
OFF_GS = OFF_GR + D_MODEL
OFF_GM = OFF_GS + D_MODEL
IN_W = OFF_GM + D_MODEL

assert RET_DK == SWA_HD
LANES = 128
V7X_VMEM_LIMIT = 62 * 1024 * 1024


def _const_spec(shape):
    nd = len(shape)
    return pl.BlockSpec(shape, lambda *_: (0,) * nd, pipeline_mode=pl.Buffered(1))


def _layer_spec(shape, layer):
    nd = len(shape)
    return pl.BlockSpec((1,) + tuple(shape[1:]), lambda *_: (layer,) + (0,) * (nd - 1), pipeline_mode=pl.Buffered(1))


def _params(n_grid):
    return pltpu.CompilerParams(dimension_semantics=("arbitrary",) * n_grid, vmem_limit_bytes=V7X_VMEM_LIMIT)


N_INPROJ_OUT = 11


def _inproj_kernel(n_cast, x_ref, w_ref, cos_ref, sin_ref, *refs):
    cast_in, outs, cast_out = refs[:n_cast], refs[n_cast:n_cast + N_INPROJ_OUT], refs[n_cast + N_INPROJ_OUT:]
    rq_ref, rk_ref, rv_ref, rg_ref, sq_ref, sk_ref, sv_ref, mq_ref, gr_ref, gs_ref, gm_ref = outs
    for src, dst in zip(cast_in, cast_out):
        dst[...] = src[...].astype(dst.dtype)
    xb = x_ref[...].astype(BF16)
    cos = cos_ref[...]
    sin = sin_ref[...]
    lane = lax.broadcasted_iota(jnp.int32, cos.shape, 1)
    first_half = (lane & (SWA_HD // 2)) == 0

    def proj(off, width):
        return jnp.dot(xb, w_ref[0, :, off:off + width], preferred_element_type=F32)

    def rope_store(off, width, out_ref, scale):
        y = proj(off, width)
        for j in range(width // LANES):
            yj = y[:, j * LANES:(j + 1) * LANES]
            sw = jnp.where(first_half, pltpu.roll(yj, LANES - SWA_HD // 2, 1), pltpu.roll(yj, SWA_HD // 2, 1))
            r = yj * cos + sw * sin
            if scale != 1.0:
                r = r * scale
            out_ref[:, j * LANES:(j + 1) * LANES] = r.astype(out_ref.dtype)

    def plain_store(off, width, out_ref):
        out_ref[...] = proj(off, width).astype(out_ref.dtype)

    rope_store(OFF_RQ, RET_QK_W, rq_ref, 1.0)
    rope_store(OFF_RK, RET_QK_W, rk_ref, RET_DK ** -0.5)
    plain_store(OFF_RV, RET_V_W, rv_ref)
    plain_store(OFF_RG, RET_V_W, rg_ref)
    rope_store(OFF_SQ, SWA_Q_W, sq_ref, SWA_HD ** -0.5)
    rope_store(OFF_SK, SWA_KV_W, sk_ref, 1.0)
    plain_store(OFF_SV, SWA_KV_W, sv_ref)
    plain_store(OFF_MQ, MEM_W, mq_ref)
    plain_store(OFF_GR, D_MODEL, gr_ref)
    plain_store(OFF_GS, D_MODEL, gs_ref)
    plain_store(OFF_GM, D_MODEL, gm_ref)


def _inproj(x2d, w_bf, cos_tab, sin_tab, tm, qkv_dtype, cast_jobs=()):
    m = x2d.shape[0]
    n_steps = m // tm
    n_tab = cos_tab.shape[0] // tm
    row = lambda w: pl.BlockSpec((tm, w), lambda i: (i, 0))
    tab = pl.BlockSpec((tm, LANES), lambda i: (i % n_tab, 0))
    widths_dtypes = [(RET_QK_W, qkv_dtype), (RET_QK_W, qkv_dtype), (RET_V_W, qkv_dtype), (RET_V_W, F32),
                     (SWA_Q_W, qkv_dtype), (SWA_KV_W, F32), (SWA_KV_W, F32), (MEM_W, qkv_dtype),
                     (D_MODEL, F32), (D_MODEL, F32), (D_MODEL, F32)]
    assert len(widths_dtypes) == N_INPROJ_OUT
    slab = lambda a: (1, a.shape[1] // n_steps, a.shape[2])
    cast_in = [pl.BlockSpec(slab(a), lambda i, layer=layer: (layer, i, 0)) for a, layer in cast_jobs]
    cast_out = [pl.BlockSpec(slab(a), lambda i: (0, i, 0)) for a, _ in cast_jobs]
    return pl.pallas_call(
        functools.partial(_inproj_kernel, len(cast_jobs)),
        grid=(n_steps,),
        in_specs=[row(D_MODEL), _layer_spec(w_bf[0].shape, w_bf[1]), tab, tab] + cast_in,
        out_specs=[row(w) for w, _ in widths_dtypes] + cast_out,
        out_shape=([jax.ShapeDtypeStruct((m, w), dt) for w, dt in widths_dtypes]
                   + [jax.ShapeDtypeStruct((1,) + a.shape[1:], BF16) for a, _ in cast_jobs]),
        compiler_params=_params(1),
        name="inproj",
    )(x2d, w_bf[0], cos_tab, sin_tab, *[a for a, _ in cast_jobs])


def _group_norm(o, g_row):
    mu = jnp.mean(o, -1, keepdims=True)
    d = o - mu
    var = jnp.mean(d * d, -1, keepdims=True)
    return d * lax.rsqrt(var + GN_EPS) * g_row


def _ret_tables(n_rows, period):
    lg = np.log1p(-np.exp2(-5.0 - np.arange(RET_HEADS, dtype=np.float64)))
    r = np.arange(n_rows)
    t = (r % period).astype(np.float64)
    same = (r[:, None] // period) == (r[None, :] // period)
    diff = t[:, None] - t[None, :]
    decay = np.where((diff >= 0) & same, np.exp(lg[:, None, None] * np.maximum(diff, 0.0)), 0.0)
    rowdec = np.exp(lg[:, None] * (t[None, :] + 1.0))
    wend = np.exp(lg[:, None] * (period - 1.0 - t[None, :]))
    gl = np.exp(lg * period)
    rowdec = np.broadcast_to(rowdec[:, :, None], (RET_HEADS, n_rows, RET_DV))
    wend = np.broadcast_to(wend[:, :, None], (RET_HEADS, n_rows, RET_DK))
    gl = np.broadcast_to(gl[:, None, None], (RET_HEADS, 1, RET_DV))
    return decay, rowdec, wend, gl


def _ret_sample_kernel(n_seq, t_len, q_ref, k_ref, v_ref, s_ref, decay_ref, rowdec_ref, wend_ref, gl_ref,
                       o_ref, s_out_ref):
    rows = n_seq * t_len
    pair_dk, pair_dv = 2 * RET_DK, 2 * RET_DV
    lane_lo = lax.broadcasted_iota(jnp.int32, (rows, pair_dk), 1) < RET_DK
    row_lo = lax.broadcasted_iota(jnp.int32, (pair_dk, RET_DV), 0) < RET_DK
    for p in range(RET_HEADS // 2):
        qk = slice(p * pair_dk, (p + 1) * pair_dk)
        vv = slice(p * pair_dv, (p + 1) * pair_dv)
        q2f, k2f, v2f = q_ref[:, qk], k_ref[:, qk], v_ref[:, vv]
        q2, k2, v2 = q2f.astype(BF16), k2f.astype(BF16), v2f.astype(BF16)
        zk = jnp.zeros_like(k2)
        k_rows = jnp.concatenate([jnp.where(lane_lo, k2, zk), jnp.where(lane_lo, zk, k2)], 0)
        sc2 = lax.dot_general(q2, k_rows, (((1,), (1,)), ((), ())), preferred_element_type=F32) * decay_ref[p]
        zv = jnp.zeros((rows, RET_DV), BF16)
        v_bd = jnp.concatenate([jnp.concatenate([v2[:, :RET_DV], zv], 1),
                                jnp.concatenate([zv, v2[:, RET_DV:]], 1)], 0)
        kw2f = k2f * wend_ref[p]
        o_state = []
        for b in range(n_seq):
            r = slice(b * t_len, (b + 1) * t_len)
            s2 = s_ref[0, b, 2 * p:2 * p + 2].reshape(pair_dk, RET_DV)
            s2b = s2.astype(BF16)
            zs = jnp.zeros_like(s2b)
            s_bd = jnp.concatenate([jnp.where(row_lo, s2b, zs), jnp.where(row_lo, zs, s2b)], 1)
            o_state.append(jnp.dot(q2f[r].astype(BF16), s_bd, preferred_element_type=F32))
            upd = lax.dot_general(kw2f[r].astype(BF16), v2f[r].astype(BF16), (((0,), (0,)), ((), ())),
                                  preferred_element_type=F32)
            s_new = (gl_ref[p] * s2 + jnp.where(row_lo, upd[:, :RET_DV], upd[:, RET_DV:])).reshape(2, RET_DK, RET_DV)
            for d in range(s_out_ref.shape[0]):
                s_out_ref[d, b, 2 * p:2 * p + 2] = s_new
        o2 = jnp.dot(sc2.astype(BF16), v_bd, preferred_element_type=F32)
        o_ref[:, vv] = o2 + jnp.concatenate(o_state, 0) * rowdec_ref[p]


def _stacked_out_specs(shape, layer, n_seq):
    tail = tuple(shape[2:])
    zeros = (0,) * len(tail)
    if layer == 0:
        return pl.BlockSpec((shape[0], n_seq) + tail, lambda i: (0, i) + zeros)
    return pl.BlockSpec((1, n_seq) + tail, lambda i: (layer, i) + zeros)


def _swa_sample_kernel(n_seq, t_len, sinks_ref, q_ref, kn_ref, vn_ref, kt_ref, vt_ref, *rest):
    o_ref, kto_ref, vto_ref = rest[-3:]
    grp_rows = SWA_GROUP * t_len
    n_all = n_seq * grp_rows
    q = q_ref[...]
    kn = kn_ref[...]
    vn = vn_ref[...]
    kn_t = kn.T
    vn_t = vn.T
    r = lax.broadcasted_iota(jnp.int32, (n_all, 1), 0)
    t_q = r % t_len
    g_row = (r // t_len) % SWA_GROUP
    b_row = r // grp_rows
    c = lax.broadcasted_iota(jnp.int32, (1, WINDOW), 1)
    valid_cache = c > t_q
    valid_new = ((c // t_len) == b_row) & ((c % t_len) <= t_q)
    lane = lax.broadcasted_iota(jnp.int32, (SWA_HD, WINDOW), 1)
    is_new_lane = lane >= WINDOW - t_len
    head_dims = [slice(kvh * SWA_HD, (kvh + 1) * SWA_HD) for kvh in range(SWA_KV_HEADS)]
    scores = []
    for kvh, hd in enumerate(head_dims):
        qg = [q[:, (kvh * SWA_GROUP + g) * SWA_HD:(kvh * SWA_GROUP + g + 1) * SWA_HD] for g in range(SWA_GROUP)]
        q_all = jnp.concatenate([qg[g][b * t_len:(b + 1) * t_len] for b in range(n_seq) for g in range(SWA_GROUP)],
                                0).astype(BF16)
        s_new = jnp.dot(q_all, kn_t[hd].astype(BF16), preferred_element_type=F32)
        s_cache = jnp.concatenate(
            [jnp.dot(q_all[b * grp_rows:(b + 1) * grp_rows], kt_ref[0, b, kvh].astype(BF16),
                     preferred_element_type=F32) for b in range(n_seq)], 0)
        scores.append((s_new, s_cache))
    probs = []
    for kvh, (s_new, s_cache) in enumerate(scores):
        s_new = jnp.where(valid_new, s_new, -jnp.inf)
        s_cache = jnp.where(valid_cache, s_cache, -jnp.inf)
        sink = jnp.full((n_all, 1), sinks_ref[kvh * SWA_GROUP], F32)
        for g in range(1, SWA_GROUP):
            sink = jnp.where(g_row == g, sinks_ref[kvh * SWA_GROUP + g], sink)
        m = jnp.maximum(jnp.maximum(jnp.max(s_new, -1, keepdims=True), jnp.max(s_cache, -1, keepdims=True)), sink)
        e_new = jnp.exp(s_new - m)
        e_cache = jnp.exp(s_cache - m)
        den = jnp.sum(e_new, -1, keepdims=True) + jnp.sum(e_cache, -1, keepdims=True) + jnp.exp(sink - m)
        probs.append(((e_new / den).astype(BF16), (e_cache / den).astype(BF16)))
    pieces = []
    for kvh, hd in enumerate(head_dims):
        p_new, p_cache = probs[kvh]
        o = jnp.dot(p_new, vn[:, hd].astype(BF16), preferred_element_type=F32)
        o = o + jnp.concatenate(
            [lax.dot_general(p_cache[b * grp_rows:(b + 1) * grp_rows], vt_ref[0, b, kvh].astype(BF16),
                             (((1,), (1,)), ((), ())), preferred_element_type=F32) for b in range(n_seq)], 0)
        for g in range(SWA_GROUP):
            pieces.append(jnp.concatenate(
                [o[b * grp_rows + g * t_len:b * grp_rows + (g + 1) * t_len] for b in range(n_seq)], 0))
    for kvh, hd in enumerate(head_dims):
        for b in range(n_seq):
            shift_new = (WINDOW - t_len - b * t_len) % WINDOW
            k_slid = jnp.where(is_new_lane, pltpu.roll(kn_t[hd], shift_new, 1),
                               pltpu.roll(kt_ref[0, b, kvh], WINDOW - t_len, 1))
            v_slid = jnp.where(is_new_lane, pltpu.roll(vn_t[hd], shift_new, 1),
                               pltpu.roll(vt_ref[0, b, kvh], WINDOW - t_len, 1))
            for d in range(kto_ref.shape[0]):
                kto_ref[d, b, kvh] = k_slid
                vto_ref[d, b, kvh] = v_slid
    o_ref[...] = jnp.concatenate(pieces, -1).astype(o_ref.dtype)


N_MIX_SAMPLE_IN = 14


def _mix_sample_kernel(n_seq, t_len, sinks_ref, rq_ref, rk_ref, rv_ref, s_ref, decay_ref, rowdec_ref, wend_ref,
                       gl_ref, sq_ref, kn_ref, vn_ref, kt_ref, vt_ref, *rest):
    ret_o, s_out, swa_o, kto, vto = rest[-5:]
    _ret_sample_kernel(n_seq, t_len, rq_ref, rk_ref, rv_ref, s_ref, decay_ref, rowdec_ref, wend_ref, gl_ref,
                       ret_o, s_out)
    _swa_sample_kernel(n_seq, t_len, sinks_ref, sq_ref, kn_ref, vn_ref, kt_ref, vt_ref, swa_o, kto, vto)


def _mix_sample(rq, rk, rv, state, pair_tables, sq, sk, sv, cache_kt, cache_vt, sinks, prev_out, layer, t_len, n_seq):
    m = rq.shape[0]
    rows = n_seq * t_len
    assert rows == WINDOW and cache_kt.shape[-1] == WINDOW
    decay2, rowdec2, wend2, gl2 = pair_tables
    row = lambda w: pl.BlockSpec((rows, w), lambda i: (i, 0))
    st_in = pl.BlockSpec((1, n_seq, RET_HEADS, RET_DK, RET_DV), lambda i: (layer, i, 0, 0, 0))
    cin = pl.BlockSpec((1, n_seq, SWA_KV_HEADS, SWA_HD, WINDOW), lambda i: (layer, i, 0, 0, 0))
    in_specs = [pl.BlockSpec(memory_space=pltpu.SMEM), row(RET_QK_W), row(RET_QK_W), row(RET_V_W), st_in,
                _const_spec(decay2.shape), _const_spec(rowdec2.shape), _const_spec(wend2.shape),
                _const_spec(gl2.shape), row(SWA_Q_W), row(SWA_KV_W), row(SWA_KV_W), cin, cin]
    args = [sinks, rq, rk, rv, state, decay2, rowdec2, wend2, gl2, sq, sk, sv, cache_kt, cache_vt]
    assert len(args) == N_MIX_SAMPLE_IN
    stacked = (state, cache_kt, cache_vt)
    aliases = {}
    if prev_out is not None:
        in_specs += [pl.BlockSpec(memory_space=pl.ANY)] * len(stacked)
        args += list(prev_out)
        aliases = {N_MIX_SAMPLE_IN: 1, N_MIX_SAMPLE_IN + 1: 3, N_MIX_SAMPLE_IN + 2: 4}
    st_out, k_out, v_out = (_stacked_out_specs(a.shape, layer, n_seq) for a in stacked)
    shape = lambda a: jax.ShapeDtypeStruct(a.shape, F32)
    ret_o, s_new, swa_o, k_new, v_new = pl.pallas_call(
        functools.partial(_mix_sample_kernel, n_seq, t_len),
        grid=(m // rows,),
        in_specs=in_specs,
        out_specs=[row(RET_V_W), st_out, row(SWA_Q_W), k_out, v_out],
        out_shape=[jax.ShapeDtypeStruct((m, RET_V_W), F32), shape(state),
                   jax.ShapeDtypeStruct((m, SWA_Q_W), BF16), shape(cache_kt), shape(cache_vt)],
        input_output_aliases=aliases,
        compiler_params=_params(1),
        name="mix_sample",
    )(*args)
    return ret_o, swa_o, (s_new, k_new, v_new)


def _pair_tables(tables):
    decay, rowdec, wend, gl = tables
    pair = lambda a: np.concatenate([a[0::2], a[1::2]], -1)
    gl_rows = np.concatenate([np.broadcast_to(gl[0::2], (RET_HEADS // 2, RET_DK, RET_DV)),
                              np.broadcast_to(gl[1::2], (RET_HEADS // 2, RET_DK, RET_DV))], 1)
    return tuple(jnp.asarray(a, F32) for a in (pair(decay), pair(rowdec), pair(wend), gl_rows))


def _run_staged(tasks):
    active = list(tasks)
    while active:
        for t in list(active):
            try:
                next(t)
            except StopIteration:
                active.remove(t)


def _mix_prompt_kernel(nb, t_len, sinks_ref, rq_ref, rk_ref, rv_ref, sq_ref, sk_ref, sv_ref, mq_ref, memx_ref, wkv_ref,
                       decay_ref, rowdec_ref, wend_ref, gl_ref, smq_ref, smk_ref, smv_ref,
                       ret_out, swa_out, mem_out, s_out, smem_out, mk_out, mv_out, skt_out, svt_out,
                       s_scr, kp_scr, kpr_scr, vp_scr, vpr_scr, mk_scr, mv_scr):
    c = pl.program_id(0)

    @pl.when(c == 0)
    def _():
        s_scr[...] = jnp.zeros_like(s_scr)
        for scr in (kp_scr, kpr_scr, vp_scr, vpr_scr):
            scr[...] = jnp.zeros_like(scr)
        kv = jnp.dot(memx_ref[...].astype(BF16), wkv_ref[0], preferred_element_type=F32)
        mk_out[...] = kv[:, :MEM_W]
        mv_out[...] = kv[:, MEM_W:]
        mk_scr[...] = kv[:, :MEM_W].astype(BF16).reshape(mk_scr.shape)
        mv_scr[...] = kv[:, MEM_W:].astype(BF16).reshape(mv_scr.shape)

    pair_w = 2 * SWA_HD
    lane_lo = lax.broadcasted_iota(jnp.int32, (RET_CHUNK, pair_w), 1) < SWA_HD
    row_lo = lax.broadcasted_iota(jnp.int32, (2 * RET_DK, RET_DV), 0) < RET_DK
    lane_lo_kv = lax.broadcasted_iota(jnp.int32, (2 * WINDOW, pair_w), 1) < SWA_HD
    upper = (lax.broadcasted_iota(jnp.int32, (WINDOW, WINDOW), 1)
             > lax.broadcasted_iota(jnp.int32, (WINDOW, WINDOW), 0))
    prev_bias = jnp.where(c > 0, 0.0, -jnp.inf)


    def ret_task(b, p):
        qk = slice(p * 2 * RET_DK, (p + 1) * 2 * RET_DK)
        vv = slice(p * 2 * RET_DV, (p + 1) * 2 * RET_DV)
        q2, k2, v2, s2 = rq_ref[b, :, qk], rk_ref[b, :, qk], rv_ref[b, :, vv], s_scr[b, p]
        zk = jnp.zeros_like(k2)
        k_rows = jnp.concatenate([jnp.where(lane_lo, k2, zk), jnp.where(lane_lo, zk, k2)], 0)
        sc_raw = lax.dot_general(q2, k_rows, (((1,), (1,)), ((), ())), preferred_element_type=F32)
        s2b = s2.astype(BF16)
        zs = jnp.zeros_like(s2b)
        s_bd = jnp.concatenate([jnp.where(row_lo, s2b, zs), jnp.where(row_lo, zs, s2b)], 1)
        os_raw = jnp.dot(q2, s_bd, preferred_element_type=F32)
        kw2 = (k2.astype(F32) * wend_ref[p]).astype(BF16)
        upd = lax.dot_general(kw2, v2, (((0,), (0,)), ((), ())), preferred_element_type=F32)
        yield
        zv = jnp.zeros((RET_CHUNK, RET_DV), v2.dtype)
        v_bd = jnp.concatenate([jnp.concatenate([v2[:, :RET_DV], zv], 1),
                                jnp.concatenate([zv, v2[:, RET_DV:]], 1)], 0)
        o_raw = jnp.dot((sc_raw * decay_ref[p]).astype(BF16), v_bd, preferred_element_type=F32)
        s_scr[b, p] = gl_ref[p] * s2 + jnp.where(row_lo, upd[:, :RET_DV], upd[:, RET_DV:])
        yield
        ret_out[b, :, vv] = o_raw + os_raw * rowdec_ref[p]

    kv_ctx = {}

    def swa_prep(b):
        k_cur, v_cur = sk_ref[b], sv_ref[b]
        kb, kbr = k_cur.astype(BF16), pltpu.roll(k_cur, SWA_HD, 1).astype(BF16)
        vb, vbr = v_cur.astype(BF16), pltpu.roll(v_cur, SWA_HD, 1).astype(BF16)
        kv_ctx[b] = (jnp.concatenate([kp_scr[b], kb], 0), jnp.concatenate([kpr_scr[b], kbr], 0),
                     jnp.concatenate([vp_scr[b], vb], 0), jnp.concatenate([vpr_scr[b], vbr], 0))
        kp_scr[b], kpr_scr[b], vp_scr[b], vpr_scr[b] = kb, kbr, vb, vbr

    def swa_task(b, kvh):
        if b not in kv_ctx:
            swa_prep(b)
        kc, kcr, vc, vcr = kv_ctx[b]
        zkv = jnp.zeros_like(kc)
        k_lo, k_hi = (kc, kcr) if kvh == 0 else (kcr, kc)
        v_lo, v_hi = (vc, vcr) if kvh == 0 else (vcr, vc)
        k_rows = jnp.concatenate([jnp.where(lane_lo_kv, k_lo, zkv), jnp.where(lane_lo_kv, zkv, k_hi)], 0)
        v_rows = jnp.concatenate([jnp.where(lane_lo_kv, v_lo, zkv), jnp.where(lane_lo_kv, zkv, v_hi)], 0)
        n_pairs = SWA_GROUP // 2
        pairs = [kvh * n_pairs + jj for jj in range(n_pairs)]
        q4 = jnp.concatenate([sq_ref[b, :, pr * pair_w:(pr + 1) * pair_w] for pr in pairs], 0)
        s4 = lax.dot_general(q4, k_rows, (((1,), (1,)), ((), ())), preferred_element_type=F32)
        yield
        rows, inv = [], []
        for jj, pr in enumerate(pairs):
            ps, inv_u = [], []
            for u in range(2):
                blk = s4[jj * WINDOW:(jj + 1) * WINDOW, u * 2 * WINDOW:(u + 1) * 2 * WINDOW]
                s = jnp.where(upper, blk[:, :WINDOW] + prev_bias, blk[:, WINDOW:])
                sink = sinks_ref[2 * pr + u]
                m = jnp.maximum(jnp.max(s, -1, keepdims=True), sink)
                e = jnp.exp(s - m)
                den = jnp.sum(e, -1, keepdims=True) + jnp.exp(sink - m)
                ps += [jnp.where(upper, e, 0.0).astype(BF16), jnp.where(upper, 0.0, e).astype(BF16)]
                inv_u.append(1.0 / den)
            rows.append(jnp.concatenate(ps, 1))
            inv.append(jnp.where(lane_lo, inv_u[0], inv_u[1]))
        o4 = jnp.dot(jnp.concatenate(rows, 0), v_rows, preferred_element_type=F32)
        yield
        for jj, pr in enumerate(pairs):
            swa_out[b, :, pr * pair_w:(pr + 1) * pair_w] = (o4[jj * WINDOW:(jj + 1) * WINDOW] * inv[jj]).astype(
                swa_out.dtype)

    def mem_task(b, h):
        sl = slice(h * MEM_HD, (h + 1) * MEM_HD)
        s = lax.dot_general(mq_ref[b, :, sl], mk_scr[b, :, sl], (((1,), (1,)), ((), ())),
                            preferred_element_type=F32) * (MEM_HD ** -0.5)
        yield
        m = jnp.max(s, -1, keepdims=True)
        e = jnp.exp(s - m)
        inv = 1.0 / jnp.sum(e, -1, keepdims=True)
        o = jnp.dot(e.astype(BF16), mv_scr[b, :, sl], preferred_element_type=F32)
        yield
        mem_out[b, :, sl] = (o * inv).astype(mem_out.dtype)

    head_of_row = lax.broadcasted_iota(jnp.int32, (MEM_HEADS * t_len, 1), 0) // t_len
    head_of_col = lax.broadcasted_iota(jnp.int32, (1, N_MEM * MEM_HEADS), 1) % MEM_HEADS
    valid_smem = head_of_row == head_of_col

    def smem_task(j):
        r = slice(j * t_len, (j + 1) * t_len)
        qb = smq_ref[r, :]
        q_all = jnp.concatenate([qb[:, h * MEM_HD:(h + 1) * MEM_HD] for h in range(MEM_HEADS)], 0).astype(BF16)
        s = lax.dot_general(q_all, smk_ref[0, j].astype(BF16), (((1,), (1,)), ((), ())),
                            preferred_element_type=F32) * (MEM_HD ** -0.5)
        yield
        s = jnp.where(valid_smem, s, -jnp.inf)
        m = jnp.max(s, -1, keepdims=True)
        e = jnp.exp(s - m)
        inv = 1.0 / jnp.sum(e, -1, keepdims=True)
        o = jnp.dot(e.astype(BF16), smv_ref[0, j].astype(BF16), preferred_element_type=F32) * inv
        yield
        smem_out[r, :] = jnp.concatenate([o[h * t_len:(h + 1) * t_len] for h in range(MEM_HEADS)], -1)

    n_smem = smk_ref.shape[1]
    for b in range(nb):
        _run_staged([ret_task(b, 0), swa_task(b, 0), mem_task(b, 0), ret_task(b, 1), mem_task(b, 1),
                     ret_task(b, 2), swa_task(b, 1), mem_task(b, 2), ret_task(b, 3), mem_task(b, 3)]
                    + [smem_task(j) for j in range(b, n_smem, nb)])

    @pl.when(c == pl.num_programs(0) - 1)
    def _():
        s_out[...] = s_scr[...]
        for b in range(nb):
            skt_out[b] = sk_ref[b].T.reshape(SWA_KV_HEADS, SWA_HD, WINDOW)
            svt_out[b] = sv_ref[b].T.reshape(SWA_KV_HEADS, SWA_HD, WINDOW)


def _mix_prompt(rq, rk, rv, sq, sk, sv, mq, mem2d, w_kv, sinks, pair_tables, smq, cache_mk, cache_mv, layer, t_len):
    nb, seq, _ = rq.shape
    n_steps = seq // RET_CHUNK
    n_smem = cache_mk.shape[1] // n_steps
    decay2, rowdec2, wend2, gl2 = pair_tables
    w_kv, kv_layer = w_kv
    chunk = lambda w: pl.BlockSpec((nb, RET_CHUNK, w), lambda c: (0, c, 0))
    srow = pl.BlockSpec((n_smem * t_len, MEM_W), lambda c: (c, 0))
    skv = pl.BlockSpec((1, n_smem, N_MEM * MEM_HEADS, MEM_HD), lambda c: (layer, c, 0, 0))
    st_shape = (nb, RET_HEADS // 2, 2 * RET_DK, RET_DV)
    mem_shape = (mem2d.shape[0], MEM_W)
    tail_shape = (nb, SWA_KV_HEADS, SWA_HD, WINDOW)
    assert RET_CHUNK == WINDOW
    kv_scr = pltpu.VMEM((nb, WINDOW, SWA_KV_W), BF16)
    mem_scr = pltpu.VMEM((nb, mem2d.shape[0] // nb, MEM_W), BF16)
    return pl.pallas_call(
        functools.partial(_mix_prompt_kernel, nb, t_len),
        grid=(n_steps,),
        in_specs=[pl.BlockSpec(memory_space=pltpu.SMEM),
                  chunk(RET_QK_W), chunk(RET_QK_W), chunk(RET_V_W), chunk(SWA_Q_W), chunk(SWA_KV_W), chunk(SWA_KV_W),
                  chunk(MEM_W), _const_spec(mem2d.shape), _layer_spec(w_kv.shape, kv_layer),
                  _const_spec(decay2.shape), _const_spec(rowdec2.shape), _const_spec(wend2.shape),
                  _const_spec(gl2.shape), srow, skv, skv],
        out_specs=[chunk(RET_V_W), chunk(SWA_Q_W), chunk(MEM_W), _const_spec(st_shape), srow,
                   _const_spec(mem_shape), _const_spec(mem_shape), _const_spec(tail_shape), _const_spec(tail_shape)],
        out_shape=[jax.ShapeDtypeStruct((nb, seq, RET_V_W), F32), jax.ShapeDtypeStruct((nb, seq, SWA_Q_W), BF16),
                   jax.ShapeDtypeStruct((nb, seq, MEM_W), BF16), jax.ShapeDtypeStruct(st_shape, F32),
                   jax.ShapeDtypeStruct(smq.shape, F32),
                   jax.ShapeDtypeStruct(mem_shape, F32), jax.ShapeDtypeStruct(mem_shape, F32),
                   jax.ShapeDtypeStruct(tail_shape, F32), jax.ShapeDtypeStruct(tail_shape, F32)],
        scratch_shapes=[pltpu.VMEM(st_shape, F32), kv_scr, kv_scr, kv_scr, kv_scr, mem_scr, mem_scr],
        compiler_params=_params(1),
        name="mix_prompt",
    )(sinks, rq, rk, rv, sq, sk, sv, mq, mem2d, w_kv, decay2, rowdec2, wend2, gl2, smq, cache_mk, cache_mv)


def _sigmoid(x):
    return 0.5 * jnp.tanh(0.5 * x) + 0.5


def _layer_norm(x, g, b):
    mu = jnp.mean(x, -1, keepdims=True)
    d = x - mu
    var = jnp.mean(d * d, -1, keepdims=True)
    return d * lax.rsqrt(var + LN_EPS) * g + b


def _finish_kernel(x_ref, ret_ref, rg_ref, swa_ref, mem_ref, gr_ref, gs_ref, gm_ref, gng_ref,
                   wr_ref, ws_ref, wm_ref, wo_ref, l1g_ref, l1b_ref, wu_ref, wd_ref, l2g_ref, l2b_ref, o_ref):
    swa_b = jnp.dot(swa_ref[...].astype(BF16), ws_ref[0], preferred_element_type=F32)
    mem_b = jnp.dot(mem_ref[...].astype(BF16), wm_ref[0], preferred_element_type=F32)
    half = x_ref.shape[0] // 2
    halves = (slice(0, half), slice(half, 2 * half))
    ret_parts = []
    for r in halves:
        rg = rg_ref[r, :]
        gn = jnp.concatenate([_group_norm(ret_ref[r, h * RET_DV:(h + 1) * RET_DV], gng_ref[0, h:h + 1, :])
                              for h in range(RET_HEADS)], -1)
        ret_in = (rg * _sigmoid(rg) * gn).astype(BF16)
        ret_parts.append(jnp.dot(ret_in, wr_ref[0], preferred_element_type=F32))
    n_slabs = 4
    ff = D_FF // n_slabs
    act = lambda h: jnp.square(jnp.maximum(h, 0.0)).astype(BF16)
    x1_parts, h_parts = [], []
    for r, ret_half in zip(halves, ret_parts):
        merged = (_sigmoid(gr_ref[r, :]) * ret_half + _sigmoid(gs_ref[r, :]) * swa_b[r]
                  + _sigmoid(gm_ref[r, :]) * mem_b[r])
        y = jnp.dot(merged.astype(BF16), wo_ref[0], preferred_element_type=F32)
        x1_parts.append(_layer_norm(ALPHA * x_ref[r, :] + y, l1g_ref[0], l1b_ref[0]))
    for x1_half in x1_parts:
        h_parts.append(act(jnp.dot(x1_half.astype(BF16), wu_ref[0, :, :ff], preferred_element_type=F32)))
    x1 = jnp.concatenate(x1_parts, 0)
    x1b = x1.astype(BF16)
    up = lambda c: act(jnp.dot(x1b, wu_ref[0, :, c * ff:(c + 1) * ff], preferred_element_type=F32))
    down = lambda c, h: jnp.dot(h, wd_ref[0, c * ff:(c + 1) * ff, :], preferred_element_type=F32)
    h_next = jnp.concatenate(h_parts, 0)
    acc = None
    for c in range(n_slabs - 1):
        h_cur, h_next = h_next, up(c + 1)
        d = down(c, h_cur)
        acc = d if acc is None else acc + d
    for r in halves:
        d = jnp.dot(h_next[r], wd_ref[0, (n_slabs - 1) * ff:, :], preferred_element_type=F32)
        o_ref[r, :] = _layer_norm(ALPHA * x1[r] + acc[r] + d, l2g_ref[0], l2b_ref[0])


def _finish(x2d, gn, rg, swa_o, mem_o, g_r, g_s, g_m, lw, tm):
    m = x2d.shape[0]
    row = lambda w: pl.BlockSpec((tm, w), lambda i: (i, 0))
    return pl.pallas_call(
        _finish_kernel,
        grid=(m // tm,),
        in_specs=[row(D_MODEL), row(RET_V_W), row(RET_V_W), row(SWA_Q_W), row(MEM_W),
                  row(D_MODEL), row(D_MODEL), row(D_MODEL)] + [_layer_spec(a.shape, idx) for a, idx in lw],
        out_specs=row(D_MODEL),
        out_shape=jax.ShapeDtypeStruct((m, D_MODEL), F32),
        compiler_params=_params(1),
        name="finish",
    )(x2d, gn, rg, swa_o, mem_o, g_r, g_s, g_m, *[a for a, _ in lw])


def _rope_tables(pos, reps=1):
    half = SWA_HD // 2
    inv = np.power(ROPE_THETA, -np.arange(half, dtype=np.float64) / half)
    ang = np.asarray(pos, np.float64)[:, None] * inv[None, :]
    c, s = np.cos(ang), np.sin(ang)
    cos_t, sin_t = np.concatenate([c, c, c, c], -1), np.concatenate([-s, s, -s, s], -1)
    return jnp.asarray(np.tile(cos_t, (reps, 1)), F32), jnp.asarray(np.tile(sin_t, (reps, 1)), F32)


def kernel(x_prompt, x_sample, state_ret, cache_swa_k, cache_swa_v, cache_mem_k, cache_mem_v, mem_prompt,
           w_in, w_br_ret, w_br_swa, w_br_mem, w_out, w_mem_kv, attn_sinks, ret_gn_g,
           ln1_g, ln1_b, w_up, w_down, ln2_g, ln2_b):
    batch, seq, _ = x_prompt.shape
    dec_b, dec_t, _ = x_sample.shape
    tm_p, tm_s = 512, 256
    tm_fin = 512
    ret_seqs = RET_CHUNK // dec_t

    cos_p, sin_p = _rope_tables(np.arange(seq))
    cos_s, sin_s = _rope_tables(PAST_LEN + np.arange(dec_t), tm_s // dec_t)
    tab_p2 = _pair_tables(_ret_tables(RET_CHUNK, RET_CHUNK))
    tab_s2 = _pair_tables(_ret_tables(RET_CHUNK, dec_t))

    xp = x_prompt.reshape(batch * seq, D_MODEL)
    xs = x_sample.reshape(dec_b * dec_t, D_MODEL)
    mem2d = mem_prompt.reshape(batch * N_MEM, D_MODEL)
    cache_kt = jnp.transpose(cache_swa_k, (0, 1, 3, 4, 2))
    cache_vt = jnp.transpose(cache_swa_v, (0, 1, 3, 4, 2))
    cache_mk = cache_mem_k.reshape(DEPTH, dec_b, N_MEM * MEM_HEADS, MEM_HD)
    cache_mv = cache_mem_v.reshape(DEPTH, dec_b, N_MEM * MEM_HEADS, MEM_HD)

    ln_row = lambda a: a.reshape(DEPTH, 1, D_MODEL)
    late_weights = (w_br_ret, w_br_swa, w_br_mem, w_out, w_up, w_down)
    next_weights = (w_in, w_mem_kv)
    w_in_bf, w_mem_kv_bf = (w_in[:1].astype(BF16), 0), (w_mem_kv[:1].astype(BF16), 0)

    ret_p, swk_p, swv_p, mk_p, mv_p = [], [], [], [], []
    stacked_s = None
    for l in range(DEPTH):
        sinks = attn_sinks[l]
        cast_jobs = [(w, l) for w in late_weights] + ([(w, l + 1) for w in next_weights] if l + 1 < DEPTH else [])
        proj = _inproj(xp, w_in_bf, cos_p, sin_p, tm_p, BF16, cast_jobs)
        rq, rk, rv, rg, sq, sk, sv, mq, g_r, g_s, g_m = proj[:N_INPROJ_OUT]
        s_rq, s_rk, s_rv, s_rg, s_sq, s_sk, s_sv, s_mq, s_g_r, s_g_s, s_g_m = _inproj(
            xs, w_in_bf, cos_s, sin_s, tm_s, F32)
        cast = [(w, 0) for w in proj[N_INPROJ_OUT:]]
        wr_bf, ws_bf, wm_bf, wo_bf, wu_bf, wd_bf = cast[:len(late_weights)]
        lw = [(ret_gn_g, l), wr_bf, ws_bf, wm_bf, wo_bf, (ln_row(ln1_g), l), (ln_row(ln1_b), l),
              wu_bf, wd_bf, (ln_row(ln2_g), l), (ln_row(ln2_b), l)]

        by_seq = lambda a: a.reshape(batch, seq, a.shape[-1])
        gn, swa_o, mem_o, s_p, s_mem_o, mk, mv, skt, svt = _mix_prompt(
            by_seq(rq), by_seq(rk), by_seq(rv), by_seq(sq), by_seq(sk), by_seq(sv), by_seq(mq),
            mem2d, w_mem_kv_bf, sinks, tab_p2, s_mq, cache_mk, cache_mv, l, dec_t)
        if l + 1 < DEPTH:
            w_in_bf, w_mem_kv_bf = cast[len(late_weights):]
        flat = lambda a: a.reshape(batch * seq, a.shape[-1])
        xp = _finish(xp, flat(gn), rg, flat(swa_o), flat(mem_o), g_r, g_s, g_m, lw, tm_fin)
        ret_p.append(s_p.reshape(batch, RET_HEADS, RET_DK, RET_DV))
        swk_p.append(skt)
        swv_p.append(svt)
        mk_p.append(mk.reshape(batch, N_MEM, MEM_HEADS, MEM_HD))
        mv_p.append(mv.reshape(batch, N_MEM, MEM_HEADS, MEM_HD))

        s_gn, s_swa_o, stacked_s = _mix_sample(s_rq, s_rk, s_rv, state_ret, tab_s2, s_sq, s_sk, s_sv, cache_kt, cache_vt,
                                               sinks, stacked_s, l, dec_t, ret_seqs)
        xs = _finish(xs, s_gn, s_rg, s_swa_o, s_mem_o, s_g_r, s_g_s, s_g_m, lw, tm_s)

    from_t = lambda a: jnp.transpose(a, (0, 1, 4, 2, 3))
    ret_s, swk_s, swv_s = stacked_s
    return (xp.reshape(batch, seq, D_MODEL), xs.reshape(dec_b, dec_t, D_MODEL),
            jnp.stack(ret_p), from_t(jnp.stack(swk_p)), from_t(jnp.stack(swv_p)), jnp.stack(mk_p), jnp.stack(mv_p),
            ret_s, from_t(swk_s), from_t(swv_s))
```

```python
import functools

import jax
import jax.numpy as jnp
import numpy as np
from jax import lax
from jax.experimental import pallas as pl
from jax.experimental.pallas import tpu as pltpu

F32 = jnp.float32
BF16 = jnp.bfloat16

D_MODEL = 1024
DEPTH = 2
PAST_LEN = 16384
RET_HEADS = 8
RET_DK = 64
RET_DV = 128
RET_CHUNK = 128
SWA_HEADS = 8
SWA_KV_HEADS = 2
SWA_GROUP = SWA_HEADS // SWA_KV_HEADS
SWA_HD = 64
WINDOW = 128
MEM_HEADS = 4
MEM_HD = 128
N_MEM = 256
D_FF = 4 * D_MODEL
ROPE_THETA = 10000.0
LN_EPS = 1e-5
GN_EPS = 1e-5
ALPHA = (2 * DEPTH) ** 0.25

RET_QK_W = RET_HEADS * RET_DK
RET_V_W = RET_HEADS * RET_DV
SWA_Q_W = SWA_HEADS * SWA_HD
SWA_KV_W = SWA_KV_HEADS * SWA_HD
MEM_W = MEM_HEADS * MEM_HD
OFF_RQ = 0
OFF_RK = OFF_RQ + RET_QK_W
OFF_RV = OFF_RK + RET_QK_W
OFF_RG = OFF_RV + RET_V_W
OFF_SQ = OFF_RG + RET_V_W
OFF_SK = OFF_SQ + SWA_Q_W
OFF_SV = OFF_SK + SWA_KV_W
OFF_MQ = OFF_SV + SWA_KV_W
OFF_GR = OFF_MQ + MEM_W
OFF_GS = OFF_GR + D_MODEL
OFF_GM = OFF_GS + D_MODEL
IN_W = OFF_GM + D_MODEL

assert RET_DK == SWA_HD
LANES = 128
V7X_VMEM_LIMIT = 62 * 1024 * 1024


def _const_spec(shape):
    nd = len(shape)
    return pl.BlockSpec(shape, lambda *_: (0,) * nd, pipeline_mode=pl.Buffered(1))


def _layer_spec(shape, layer):
    nd = len(shape)
    return pl.BlockSpec((1,) + tuple(shape[1:]), lambda *_: (layer,) + (0,) * (nd - 1), pipeline_mode=pl.Buffered(1))


def _params(n_grid):
    return pltpu.CompilerParams(dimension_semantics=("arbitrary",) * n_grid, vmem_limit_bytes=V7X_VMEM_LIMIT)


N_INPROJ_OUT = 11


def _inproj_kernel(n_cast, x_ref, w_ref, cos_ref, sin_ref, *refs):
    cast_in, outs, cast_out = refs[:n_cast], refs[n_cast:n_cast + N_INPROJ_OUT], refs[n_cast + N_INPROJ_OUT:]
    rq_ref, rk_ref, rv_ref, rg_ref, sq_ref, sk_ref, sv_ref, mq_ref, gr_ref, gs_ref, gm_ref = outs
    for src, dst in zip(cast_in, cast_out):
        dst[...] = src[...].astype(dst.dtype)
    xb = x_ref[...].astype(BF16)
    cos = cos_ref[...]
    sin = sin_ref[...]
    lane = lax.broadcasted_iota(jnp.int32, cos.shape, 1)
    first_half = (lane & (SWA_HD // 2)) == 0

    def proj(off, width):
        return jnp.dot(xb, w_ref[0, :, off:off + width], preferred_element_type=F32)

    def rope_store(off, width, out_ref, scale):
        y = proj(off, width)
        for j in range(width // LANES):
            yj = y[:, j * LANES:(j + 1) * LANES]
            sw = jnp.where(first_half, pltpu.roll(yj, LANES - SWA_HD // 2, 1), pltpu.roll(yj, SWA_HD // 2, 1))
            r = yj * cos + sw * sin
            if scale != 1.0:
                r = r * scale
            out_ref[:, j * LANES:(j + 1) * LANES] = r.astype(out_ref.dtype)

    def plain_store(off, width, out_ref):
        out_ref[...] = proj(off, width).astype(out_ref.dtype)

    rope_store(OFF_RQ, RET_QK_W, rq_ref, 1.0)
    rope_store(OFF_RK, RET_QK_W, rk_ref, RET_DK ** -0.5)
    plain_store(OFF_RV, RET_V_W, rv_ref)
    plain_store(OFF_RG, RET_V_W, rg_ref)
    rope_store(OFF_SQ, SWA_Q_W, sq_ref, SWA_HD ** -0.5)
    rope_store(OFF_SK, SWA_KV_W, sk_ref, 1.0)
    plain_store(OFF_SV, SWA_KV_W, sv_ref)
    plain_store(OFF_MQ, MEM_W, mq_ref)
    plain_store(OFF_GR, D_MODEL, gr_ref)
    plain_store(OFF_GS, D_MODEL, gs_ref)
    plain_store(OFF_GM, D_MODEL, gm_ref)


def _inproj(x2d, w_bf, cos_tab, sin_tab, tm, qkv_dtype, cast_jobs=()):
    m = x2d.shape[0]
    n_steps = m // tm
    n_tab = cos_tab.shape[0] // tm
    row = lambda w: pl.BlockSpec((tm, w), lambda i: (i, 0))
    tab = pl.BlockSpec((tm, LANES), lambda i: (i % n_tab, 0))
    widths_dtypes = [(RET_QK_W, qkv_dtype), (RET_QK_W, qkv_dtype), (RET_V_W, qkv_dtype), (RET_V_W, F32),
                     (SWA_Q_W, qkv_dtype), (SWA_KV_W, F32), (SWA_KV_W, F32), (MEM_W, qkv_dtype),
                     (D_MODEL, F32), (D_MODEL, F32), (D_MODEL, F32)]
    assert len(widths_dtypes) == N_INPROJ_OUT
    slab = lambda a: (1, a.shape[1] // n_steps, a.shape[2])
    cast_in = [pl.BlockSpec(slab(a), lambda i, layer=layer: (layer, i, 0)) for a, layer in cast_jobs]
    cast_out = [pl.BlockSpec(slab(a), lambda i: (0, i, 0)) for a, _ in cast_jobs]
    return pl.pallas_call(
        functools.partial(_inproj_kernel, len(cast_jobs)),
        grid=(n_steps,),
        in_specs=[row(D_MODEL), _layer_spec(w_bf[0].shape, w_bf[1]), tab, tab] + cast_in,
        out_specs=[row(w) for w, _ in widths_dtypes] + cast_out,
        out_shape=([jax.ShapeDtypeStruct((m, w), dt) for w, dt in widths_dtypes]
                   + [jax.ShapeDtypeStruct((1,) + a.shape[1:], BF16) for a, _ in cast_jobs]),
        compiler_params=_params(1),
        name="inproj",
    )(x2d, w_bf[0], cos_tab, sin_tab, *[a for a, _ in cast_jobs])


def _group_norm(o, g_row):
    mu = jnp.mean(o, -1, keepdims=True)
    d = o - mu
    var = jnp.mean(d * d, -1, keepdims=True)
    return d * lax.rsqrt(var + GN_EPS) * g_row


def _ret_tables(n_rows, period):
    lg = np.log1p(-np.exp2(-5.0 - np.arange(RET_HEADS, dtype=np.float64)))
    r = np.arange(n_rows)
    t = (r % period).astype(np.float64)
    same = (r[:, None] // period) == (r[None, :] // period)
    diff = t[:, None] - t[None, :]
    decay = np.where((diff >= 0) & same, np.exp(lg[:, None, None] * np.maximum(diff, 0.0)), 0.0)
    rowdec = np.exp(lg[:, None] * (t[None, :] + 1.0))
    wend = np.exp(lg[:, None] * (period - 1.0 - t[None, :]))
    gl = np.exp(lg * period)
    rowdec = np.broadcast_to(rowdec[:, :, None], (RET_HEADS, n_rows, RET_DV))
    wend = np.broadcast_to(wend[:, :, None], (RET_HEADS, n_rows, RET_DK))
    gl = np.broadcast_to(gl[:, None, None], (RET_HEADS, 1, RET_DV))
    return decay, rowdec, wend, gl


def _ret_sample_kernel(n_seq, t_len, q_ref, k_ref, v_ref, s_ref, decay_ref, rowdec_ref, wend_ref, gl_ref,
                       o_ref, s_out_ref):
    rows = n_seq * t_len
    pair_dk, pair_dv = 2 * RET_DK, 2 * RET_DV
    lane_lo = lax.broadcasted_iota(jnp.int32, (rows, pair_dk), 1) < RET_DK
    row_lo = lax.broadcasted_iota(jnp.int32, (pair_dk, RET_DV), 0) < RET_DK
    for p in range(RET_HEADS // 2):
        qk = slice(p * pair_dk, (p + 1) * pair_dk)
        vv = slice(p * pair_dv, (p + 1) * pair_dv)
        q2f, k2f, v2f = q_ref[:, qk], k_ref[:, qk], v_ref[:, vv]
        q2, k2, v2 = q2f.astype(BF16), k2f.astype(BF16), v2f.astype(BF16)
        zk = jnp.zeros_like(k2)
        k_rows = jnp.concatenate([jnp.where(lane_lo, k2, zk), jnp.where(lane_lo, zk, k2)], 0)
        sc2 = lax.dot_general(q2, k_rows, (((1,), (1,)), ((), ())), preferred_element_type=F32) * decay_ref[p]
        zv = jnp.zeros((rows, RET_DV), BF16)
        v_bd = jnp.concatenate([jnp.concatenate([v2[:, :RET_DV], zv], 1),
                                jnp.concatenate([zv, v2[:, RET_DV:]], 1)], 0)
        kw2f = k2f * wend_ref[p]
        o_state = []
        for b in range(n_seq):
            r = slice(b * t_len, (b + 1) * t_len)
            s2 = s_ref[0, b, 2 * p:2 * p + 2].reshape(pair_dk, RET_DV)
            s2b = s2.astype(BF16)
            zs = jnp.zeros_like(s2b)
            s_bd = jnp.concatenate([jnp.where(row_lo, s2b, zs), jnp.where(row_lo, zs, s2b)], 1)
            o_state.append(jnp.dot(q2f[r].astype(BF16), s_bd, preferred_element_type=F32))
            upd = lax.dot_general(kw2f[r].astype(BF16), v2f[r].astype(BF16), (((0,), (0,)), ((), ())),
                                  preferred_element_type=F32)
            s_new = (gl_ref[p] * s2 + jnp.where(row_lo, upd[:, :RET_DV], upd[:, RET_DV:])).reshape(2, RET_DK, RET_DV)
            for d in range(s_out_ref.shape[0]):
                s_out_ref[d, b, 2 * p:2 * p + 2] = s_new
        o2 = jnp.dot(sc2.astype(BF16), v_bd, preferred_element_type=F32)
        o_ref[:, vv] = o2 + jnp.concatenate(o_state, 0) * rowdec_ref[p]


def _stacked_out_specs(shape, layer, n_seq):
    tail = tuple(shape[2:])
    zeros = (0,) * len(tail)
    if layer == 0:
        return pl.BlockSpec((shape[0], n_seq) + tail, lambda i: (0, i) + zeros)
    return pl.BlockSpec((1, n_seq) + tail, lambda i: (layer, i) + zeros)


def _swa_sample_kernel(n_seq, t_len, sinks_ref, q_ref, kn_ref, vn_ref, kt_ref, vt_ref, *rest):
    o_ref, kto_ref, vto_ref = rest[-3:]
    grp_rows = SWA_GROUP * t_len
    n_all = n_seq * grp_rows
    q = q_ref[...]
    kn = kn_ref[...]
    vn = vn_ref[...]
    kn_t = kn.T
    vn_t = vn.T
    r = lax.broadcasted_iota(jnp.int32, (n_all, 1), 0)
    t_q = r % t_len
    g_row = (r // t_len) % SWA_GROUP
    b_row = r // grp_rows
    c = lax.broadcasted_iota(jnp.int32, (1, WINDOW), 1)
    valid_cache = c > t_q
    valid_new = ((c // t_len) == b_row) & ((c % t_len) <= t_q)
    lane = lax.broadcasted_iota(jnp.int32, (SWA_HD, WINDOW), 1)
    is_new_lane = lane >= WINDOW - t_len
    head_dims = [slice(kvh * SWA_HD, (kvh + 1) * SWA_HD) for kvh in range(SWA_KV_HEADS)]
    scores = []
    for kvh, hd in enumerate(head_dims):
        qg = [q[:, (kvh * SWA_GROUP + g) * SWA_HD:(kvh * SWA_GROUP + g + 1) * SWA_HD] for g in range(SWA_GROUP)]
        q_all = jnp.concatenate([qg[g][b * t_len:(b + 1) * t_len] for b in range(n_seq) for g in range(SWA_GROUP)],
                                0).astype(BF16)
        s_new = jnp.dot(q_all, kn_t[hd].astype(BF16), preferred_element_type=F32)
        s_cache = jnp.concatenate(
            [jnp.dot(q_all[b * grp_rows:(b + 1) * grp_rows], kt_ref[0, b, kvh].astype(BF16),
                     preferred_element_type=F32) for b in range(n_seq)], 0)
        scores.append((s_new, s_cache))
    probs = []
    for kvh, (s_new, s_cache) in enumerate(scores):
        s_new = jnp.where(valid_new, s_new, -jnp.inf)
        s_cache = jnp.where(valid_cache, s_cache, -jnp.inf)
        sink = jnp.full((n_all, 1), sinks_ref[kvh * SWA_GROUP], F32)
        for g in range(1, SWA_GROUP):
            sink = jnp.where(g_row == g, sinks_ref[kvh * SWA_GROUP + g], sink)
        m = jnp.maximum(jnp.maximum(jnp.max(s_new, -1, keepdims=True), jnp.max(s_cache, -1, keepdims=True)), sink)
        e_new = jnp.exp(s_new - m)
        e_cache = jnp.exp(s_cache - m)
        den = jnp.sum(e_new, -1, keepdims=True) + jnp.sum(e_cache, -1, keepdims=True) + jnp.exp(sink - m)
        probs.append(((e_new / den).astype(BF16), (e_cache / den).astype(BF16)))
    pieces = []
    for kvh, hd in enumerate(head_dims):
        p_new, p_cache = probs[kvh]
        o = jnp.dot(p_new, vn[:, hd].astype(BF16), preferred_element_type=F32)
        o = o + jnp.concatenate(
            [lax.dot_general(p_cache[b * grp_rows:(b + 1) * grp_rows], vt_ref[0, b, kvh].astype(BF16),
                             (((1,), (1,)), ((), ())), preferred_element_type=F32) for b in range(n_seq)], 0)
        for g in range(SWA_GROUP):
            pieces.append(jnp.concatenate(
                [o[b * grp_rows + g * t_len:b * grp_rows + (g + 1) * t_len] for b in range(n_seq)], 0))
    for kvh, hd in enumerate(head_dims):
        for b in range(n_seq):
            shift_new = (WINDOW - t_len - b * t_len) % WINDOW
            k_slid = jnp.where(is_new_lane, pltpu.roll(kn_t[hd], shift_new, 1),
                               pltpu.roll(kt_ref[0, b, kvh], WINDOW - t_len, 1))
            v_slid = jnp.where(is_new_lane, pltpu.roll(vn_t[hd], shift_new, 1),
                               pltpu.roll(vt_ref[0, b, kvh], WINDOW - t_len, 1))
            for d in range(kto_ref.shape[0]):
                kto_ref[d, b, kvh] = k_slid
                vto_ref[d, b, kvh] = v_slid
    o_ref[...] = jnp.concatenate(pieces, -1).astype(o_ref.dtype)


N_MIX_SAMPLE_IN = 14


def _mix_sample_kernel(n_seq, t_len, sinks_ref, rq_ref, rk_ref, rv_ref, s_ref, decay_ref, rowdec_ref, wend_ref,
                       gl_ref, sq_ref, kn_ref, vn_ref, kt_ref, vt_ref, *rest):
    ret_o, s_out, swa_o, kto, vto = rest[-5:]
    _ret_sample_kernel(n_seq, t_len, rq_ref, rk_ref, rv_ref, s_ref, decay_ref, rowdec_ref, wend_ref, gl_ref,
                       ret_o, s_out)
    _swa_sample_kernel(n_seq, t_len, sinks_ref, sq_ref, kn_ref, vn_ref, kt_ref, vt_ref, swa_o, kto, vto)


def _mix_sample(rq, rk, rv, state, pair_tables, sq, sk, sv, cache_kt, cache_vt, sinks, prev_out, layer, t_len, n_seq):
    m = rq.shape[0]
    rows = n_seq * t_len
    assert rows == WINDOW and cache_kt.shape[-1] == WINDOW
    decay2, rowdec2, wend2, gl2 = pair_tables
    row = lambda w: pl.BlockSpec((rows, w), lambda i: (i, 0))
    st_in = pl.BlockSpec((1, n_seq, RET_HEADS, RET_DK, RET_DV), lambda i: (layer, i, 0, 0, 0))
    cin = pl.BlockSpec((1, n_seq, SWA_KV_HEADS, SWA_HD, WINDOW), lambda i: (layer, i, 0, 0, 0))
    in_specs = [pl.BlockSpec(memory_space=pltpu.SMEM), row(RET_QK_W), row(RET_QK_W), row(RET_V_W), st_in,
                _const_spec(decay2.shape), _const_spec(rowdec2.shape), _const_spec(wend2.shape),
                _const_spec(gl2.shape), row(SWA_Q_W), row(SWA_KV_W), row(SWA_KV_W), cin, cin]
    args = [sinks, rq, rk, rv, state, decay2, rowdec2, wend2, gl2, sq, sk, sv, cache_kt, cache_vt]
    assert len(args) == N_MIX_SAMPLE_IN
    stacked = (state, cache_kt, cache_vt)
    aliases = {}
    if prev_out is not None:
        in_specs += [pl.BlockSpec(memory_space=pl.ANY)] * len(stacked)
        args += list(prev_out)
        aliases = {N_MIX_SAMPLE_IN: 1, N_MIX_SAMPLE_IN + 1: 3, N_MIX_SAMPLE_IN + 2: 4}
    st_out, k_out, v_out = (_stacked_out_specs(a.shape, layer, n_seq) for a in stacked)
    shape = lambda a: jax.ShapeDtypeStruct(a.shape, F32)
    ret_o, s_new, swa_o, k_new, v_new = pl.pallas_call(
        functools.partial(_mix_sample_kernel, n_seq, t_len),
        grid=(m // rows,),
        in_specs=in_specs,
        out_specs=[row(RET_V_W), st_out, row(SWA_Q_W), k_out, v_out],
        out_shape=[jax.ShapeDtypeStruct((m, RET_V_W), F32), shape(state),
                   jax.ShapeDtypeStruct((m, SWA_Q_W), BF16), shape(cache_kt), shape(cache_vt)],
        input_output_aliases=aliases,
        compiler_params=_params(1),
        name="mix_sample",
    )(*args)
    return ret_o, swa_o, (s_new, k_new, v_new)


def _pair_tables(tables):
    decay, rowdec, wend, gl = tables
    pair = lambda a: np.concatenate([a[0::2], a[1::2]], -1)
    gl_rows = np.concatenate([np.broadcast_to(gl[0::2], (RET_HEADS // 2, RET_DK, RET_DV)),
                              np.broadcast_to(gl[1::2], (RET_HEADS // 2, RET_DK, RET_DV))], 1)
    return tuple(jnp.asarray(a, F32) for a in (pair(decay), pair(rowdec), pair(wend), gl_rows))


def _run_zipped(groups, n_stages):
    def step(task):
        try:
            next(task)
        except StopIteration:
            pass

    longest = max(len(g) for g in groups)
    for tick in range(len(groups) + n_stages - 1):
        live = [g for i, g in enumerate(groups) if 0 <= tick - i < n_stages]
        for k in range(longest):
            for g in live:
                if k < len(g):
                    step(g[k])


def _mix_prompt_kernel(nb, t_len, sinks_ref, rq_ref, rk_ref, rv_ref, sq_ref, sk_ref, sv_ref, mq_ref, memx_ref, wkv_ref,
                       decay_ref, rowdec_ref, wend_ref, gl_ref, smq_ref, smk_ref, smv_ref,
                       ret_out, swa_out, mem_out, s_out, smem_out, mk_out, mv_out, skt_out, svt_out,
                       s_scr, kp_scr, kpr_scr, vp_scr, vpr_scr, mk_scr, mv_scr):
    c = pl.program_id(0)

    @pl.when(c == 0)
    def _():
        s_scr[...] = jnp.zeros_like(s_scr)
        for scr in (kp_scr, kpr_scr, vp_scr, vpr_scr):
            scr[...] = jnp.zeros_like(scr)
        kv = jnp.dot(memx_ref[...].astype(BF16), wkv_ref[0], preferred_element_type=F32)
        mk_out[...] = kv[:, :MEM_W]
        mv_out[...] = kv[:, MEM_W:]
        mk_scr[...] = kv[:, :MEM_W].astype(BF16).reshape(mk_scr.shape)
        mv_scr[...] = kv[:, MEM_W:].astype(BF16).reshape(mv_scr.shape)

    pair_w = 2 * SWA_HD
    lane_lo = lax.broadcasted_iota(jnp.int32, (RET_CHUNK, pair_w), 1) < SWA_HD
    row_lo = lax.broadcasted_iota(jnp.int32, (2 * RET_DK, RET_DV), 0) < RET_DK
    lane_lo_kv = lax.broadcasted_iota(jnp.int32, (2 * WINDOW, pair_w), 1) < SWA_HD
    upper = (lax.broadcasted_iota(jnp.int32, (WINDOW, WINDOW), 1)
             > lax.broadcasted_iota(jnp.int32, (WINDOW, WINDOW), 0))
    prev_bias = jnp.where(c > 0, 0.0, -jnp.inf)


    def ret_task(b, p):
        qk = slice(p * 2 * RET_DK, (p + 1) * 2 * RET_DK)
        vv = slice(p * 2 * RET_DV, (p + 1) * 2 * RET_DV)
        q2, k2, v2, s2 = rq_ref[b, :, qk], rk_ref[b, :, qk], rv_ref[b, :, vv], s_scr[b, p]
        zk = jnp.zeros_like(k2)
        k_rows = jnp.concatenate([jnp.where(lane_lo, k2, zk), jnp.where(lane_lo, zk, k2)], 0)
        sc_raw = lax.dot_general(q2, k_rows, (((1,), (1,)), ((), ())), preferred_element_type=F32)
        s2b = s2.astype(BF16)
        zs = jnp.zeros_like(s2b)
        s_bd = jnp.concatenate([jnp.where(row_lo, s2b, zs), jnp.where(row_lo, zs, s2b)], 1)
        os_raw = jnp.dot(q2, s_bd, preferred_element_type=F32)
        kw2 = (k2.astype(F32) * wend_ref[p]).astype(BF16)
        upd = lax.dot_general(kw2, v2, (((0,), (0,)), ((), ())), preferred_element_type=F32)
        yield
        zv = jnp.zeros((RET_CHUNK, RET_DV), v2.dtype)
        v_bd = jnp.concatenate([jnp.concatenate([v2[:, :RET_DV], zv], 1),
                                jnp.concatenate([zv, v2[:, RET_DV:]], 1)], 0)
        o_raw = jnp.dot((sc_raw * decay_ref[p]).astype(BF16), v_bd, preferred_element_type=F32)
        s_scr[b, p] = gl_ref[p] * s2 + jnp.where(row_lo, upd[:, :RET_DV], upd[:, RET_DV:])
        yield
        ret_out[b, :, vv] = o_raw + os_raw * rowdec_ref[p]

    kv_ctx = {}

    def swa_prep(b):
        k_cur, v_cur = sk_ref[b], sv_ref[b]
        kb, kbr = k_cur.astype(BF16), pltpu.roll(k_cur, SWA_HD, 1).astype(BF16)
        vb, vbr = v_cur.astype(BF16), pltpu.roll(v_cur, SWA_HD, 1).astype(BF16)
        kv_ctx[b] = (jnp.concatenate([kp_scr[b], kb], 0), jnp.concatenate([kpr_scr[b], kbr], 0),
                     jnp.concatenate([vp_scr[b], vb], 0), jnp.concatenate([vpr_scr[b], vbr], 0))
        kp_scr[b], kpr_scr[b], vp_scr[b], vpr_scr[b] = kb, kbr, vb, vbr

    def swa_task(b, kvh):
        if b not in kv_ctx:
            swa_prep(b)
        kc, kcr, vc, vcr = kv_ctx[b]
        zkv = jnp.zeros_like(kc)
        k_lo, k_hi = (kc, kcr) if kvh == 0 else (kcr, kc)
        v_lo, v_hi = (vc, vcr) if kvh == 0 else (vcr, vc)
        k_rows = jnp.concatenate([jnp.where(lane_lo_kv, k_lo, zkv), jnp.where(lane_lo_kv, zkv, k_hi)], 0)
        v_rows = jnp.concatenate([jnp.where(lane_lo_kv, v_lo, zkv), jnp.where(lane_lo_kv, zkv, v_hi)], 0)
        n_pairs = SWA_GROUP // 2
        pairs = [kvh * n_pairs + jj for jj in range(n_pairs)]
        q4 = jnp.concatenate([sq_ref[b, :, pr * pair_w:(pr + 1) * pair_w] for pr in pairs], 0)
        s4 = lax.dot_general(q4, k_rows, (((1,), (1,)), ((), ())), preferred_element_type=F32)
        yield
        rows, inv = [], []
        for jj, pr in enumerate(pairs):
            ps, inv_u = [], []
            for u in range(2):
                blk = s4[jj * WINDOW:(jj + 1) * WINDOW, u * 2 * WINDOW:(u + 1) * 2 * WINDOW]
                s = jnp.where(upper, blk[:, :WINDOW] + prev_bias, blk[:, WINDOW:])
                sink = sinks_ref[2 * pr + u]
                m = jnp.maximum(jnp.max(s, -1, keepdims=True), sink)
                e = jnp.exp(s - m)
                den = jnp.sum(e, -1, keepdims=True) + jnp.exp(sink - m)
                ps += [jnp.where(upper, e, 0.0).astype(BF16), jnp.where(upper, 0.0, e).astype(BF16)]
                inv_u.append(1.0 / den)
            rows.append(jnp.concatenate(ps, 1))
            inv.append(jnp.where(lane_lo, inv_u[0], inv_u[1]))
        o4 = jnp.dot(jnp.concatenate(rows, 0), v_rows, preferred_element_type=F32)
        yield
        for jj, pr in enumerate(pairs):
            swa_out[b, :, pr * pair_w:(pr + 1) * pair_w] = (o4[jj * WINDOW:(jj + 1) * WINDOW] * inv[jj]).astype(
                swa_out.dtype)

    def mem_task(b, h):
        sl = slice(h * MEM_HD, (h + 1) * MEM_HD)
        s = lax.dot_general(mq_ref[b, :, sl], mk_scr[b, :, sl], (((1,), (1,)), ((), ())),
                            preferred_element_type=F32) * (MEM_HD ** -0.5)
        yield
        m = jnp.max(s, -1, keepdims=True)
        e = jnp.exp(s - m)
        inv = 1.0 / jnp.sum(e, -1, keepdims=True)
        o = jnp.dot(e.astype(BF16), mv_scr[b, :, sl], preferred_element_type=F32)
        yield
        mem_out[b, :, sl] = (o * inv).astype(mem_out.dtype)

    head_of_row = lax.broadcasted_iota(jnp.int32, (MEM_HEADS * t_len, 1), 0) // t_len
    head_of_col = lax.broadcasted_iota(jnp.int32, (1, N_MEM * MEM_HEADS), 1) % MEM_HEADS
    valid_smem = head_of_row == head_of_col

    def smem_task(j):
        r = slice(j * t_len, (j + 1) * t_len)
        qb = smq_ref[r, :]
        q_all = jnp.concatenate([qb[:, h * MEM_HD:(h + 1) * MEM_HD] for h in range(MEM_HEADS)], 0).astype(BF16)
        s = lax.dot_general(q_all, smk_ref[0, j].astype(BF16), (((1,), (1,)), ((), ())),
                            preferred_element_type=F32) * (MEM_HD ** -0.5)
        yield
        s = jnp.where(valid_smem, s, -jnp.inf)
        m = jnp.max(s, -1, keepdims=True)
        e = jnp.exp(s - m)
        inv = 1.0 / jnp.sum(e, -1, keepdims=True)
        o = jnp.dot(e.astype(BF16), smv_ref[0, j].astype(BF16), preferred_element_type=F32) * inv
        yield
        smem_out[r, :] = jnp.concatenate([o[h * t_len:(h + 1) * t_len] for h in range(MEM_HEADS)], -1)

    n_smem = smk_ref.shape[1]
    groups = [[ret_task(b, 0), swa_task(b, 0), mem_task(b, 0), ret_task(b, 1), mem_task(b, 1),
               ret_task(b, 2), swa_task(b, 1), mem_task(b, 2), ret_task(b, 3), mem_task(b, 3)]
              + [smem_task(j) for j in range(b, n_smem, nb)] for b in range(nb)]
    _run_zipped(groups, n_stages=3)

    @pl.when(c == pl.num_programs(0) - 1)
    def _():
        s_out[...] = s_scr[...]
        for b in range(nb):
            skt_out[b] = sk_ref[b].T.reshape(SWA_KV_HEADS, SWA_HD, WINDOW)
            svt_out[b] = sv_ref[b].T.reshape(SWA_KV_HEADS, SWA_HD, WINDOW)


def _mix_prompt(rq, rk, rv, sq, sk, sv, mq, mem2d, w_kv, sinks, pair_tables, smq, cache_mk, cache_mv, layer, t_len):
    nb, seq, _ = rq.shape
    n_steps = seq // RET_CHUNK
    n_smem = cache_mk.shape[1] // n_steps
    decay2, rowdec2, wend2, gl2 = pair_tables
    w_kv, kv_layer = w_kv
    chunk = lambda w: pl.BlockSpec((nb, RET_CHUNK, w), lambda c: (0, c, 0))
    srow = pl.BlockSpec((n_smem * t_len, MEM_W), lambda c: (c, 0))
    skv = pl.BlockSpec((1, n_smem, N_MEM * MEM_HEADS, MEM_HD), lambda c: (layer, c, 0, 0))
    st_shape = (nb, RET_HEADS // 2, 2 * RET_DK, RET_DV)
    mem_shape = (mem2d.shape[0], MEM_W)
    tail_shape = (nb, SWA_KV_HEADS, SWA_HD, WINDOW)
    assert RET_CHUNK == WINDOW
    kv_scr = pltpu.VMEM((nb, WINDOW, SWA_KV_W), BF16)
    mem_scr = pltpu.VMEM((nb, mem2d.shape[0] // nb, MEM_W), BF16)
    return pl.pallas_call(
        functools.partial(_mix_prompt_kernel, nb, t_len),
        grid=(n_steps,),
        in_specs=[pl.BlockSpec(memory_space=pltpu.SMEM),
                  chunk(RET_QK_W), chunk(RET_QK_W), chunk(RET_V_W), chunk(SWA_Q_W), chunk(SWA_KV_W), chunk(SWA_KV_W),
                  chunk(MEM_W), _const_spec(mem2d.shape), _layer_spec(w_kv.shape, kv_layer),
                  _const_spec(decay2.shape), _const_spec(rowdec2.shape), _const_spec(wend2.shape),
                  _const_spec(gl2.shape), srow, skv, skv],
        out_specs=[chunk(RET_V_W), chunk(SWA_Q_W), chunk(MEM_W), _const_spec(st_shape), srow,
                   _const_spec(mem_shape), _const_spec(mem_shape), _const_spec(tail_shape), _const_spec(tail_shape)],
        out_shape=[jax.ShapeDtypeStruct((nb, seq, RET_V_W), F32), jax.ShapeDtypeStruct((nb, seq, SWA_Q_W), BF16),
                   jax.ShapeDtypeStruct((nb, seq, MEM_W), BF16), jax.ShapeDtypeStruct(st_shape, F32),
                   jax.ShapeDtypeStruct(smq.shape, F32),
                   jax.ShapeDtypeStruct(mem_shape, F32), jax.ShapeDtypeStruct(mem_shape, F32),
                   jax.ShapeDtypeStruct(tail_shape, F32), jax.ShapeDtypeStruct(tail_shape, F32)],
        scratch_shapes=[pltpu.VMEM(st_shape, F32), kv_scr, kv_scr, kv_scr, kv_scr, mem_scr, mem_scr],
        compiler_params=_params(1),
        name="mix_prompt",
    )(sinks, rq, rk, rv, sq, sk, sv, mq, mem2d, w_kv, decay2, rowdec2, wend2, gl2, smq, cache_mk, cache_mv)


def _sigmoid(x):
    return 0.5 * jnp.tanh(0.5 * x) + 0.5


def _layer_norm(x, g, b):
    mu = jnp.mean(x, -1, keepdims=True)
    d = x - mu
    var = jnp.mean(d * d, -1, keepdims=True)
    return d * lax.rsqrt(var + LN_EPS) * g + b


def _finish_kernel(x_ref, ret_ref, rg_ref, swa_ref, mem_ref, gr_ref, gs_ref, gm_ref, gng_ref,
                   wr_ref, ws_ref, wm_ref, wo_ref, l1g_ref, l1b_ref, wu_ref, wd_ref, l2g_ref, l2b_ref, o_ref):
    swa_b = jnp.dot(swa_ref[...].astype(BF16), ws_ref[0], preferred_element_type=F32)
    mem_b = jnp.dot(mem_ref[...].astype(BF16), wm_ref[0], preferred_element_type=F32)
    half = x_ref.shape[0] // 2
    halves = (slice(0, half), slice(half, 2 * half))
    ret_parts = []
    for r in halves:
        rg = rg_ref[r, :]
        gn = jnp.concatenate([_group_norm(ret_ref[r, h * RET_DV:(h + 1) * RET_DV], gng_ref[0, h:h + 1, :])
                              for h in range(RET_HEADS)], -1)
        ret_in = (rg * _sigmoid(rg) * gn).astype(BF16)
        ret_parts.append(jnp.dot(ret_in, wr_ref[0], preferred_element_type=F32))
    n_slabs = 4
    ff = D_FF // n_slabs
    act = lambda h: jnp.square(jnp.maximum(h, 0.0)).astype(BF16)
    x1_parts, h_parts = [], []
    for r, ret_half in zip(halves, ret_parts):
        merged = (_sigmoid(gr_ref[r, :]) * ret_half + _sigmoid(gs_ref[r, :]) * swa_b[r]
                  + _sigmoid(gm_ref[r, :]) * mem_b[r])
        y = jnp.dot(merged.astype(BF16), wo_ref[0], preferred_element_type=F32)
        x1_parts.append(_layer_norm(ALPHA * x_ref[r, :] + y, l1g_ref[0], l1b_ref[0]))
    for x1_half in x1_parts:
        h_parts.append(act(jnp.dot(x1_half.astype(BF16), wu_ref[0, :, :ff], preferred_element_type=F32)))
    x1 = jnp.concatenate(x1_parts, 0)
    x1b = x1.astype(BF16)
    up = lambda c: act(jnp.dot(x1b, wu_ref[0, :, c * ff:(c + 1) * ff], preferred_element_type=F32))
    down = lambda c, h: jnp.dot(h, wd_ref[0, c * ff:(c + 1) * ff, :], preferred_element_type=F32)
    h_next = jnp.concatenate(h_parts, 0)
    acc = None
    for c in range(n_slabs - 1):
        h_cur, h_next = h_next, up(c + 1)
        d = down(c, h_cur)
        acc = d if acc is None else acc + d
    for r in halves:
        d = jnp.dot(h_next[r], wd_ref[0, (n_slabs - 1) * ff:, :], preferred_element_type=F32)
        o_ref[r, :] = _layer_norm(ALPHA * x1[r] + acc[r] + d, l2g_ref[0], l2b_ref[0])


def _finish(x2d, gn, rg, swa_o, mem_o, g_r, g_s, g_m, lw, tm):
    m = x2d.shape[0]
    row = lambda w: pl.BlockSpec((tm, w), lambda i: (i, 0))
    return pl.pallas_call(
        _finish_kernel,
        grid=(m // tm,),
        in_specs=[row(D_MODEL), row(RET_V_W), row(RET_V_W), row(SWA_Q_W), row(MEM_W),
                  row(D_MODEL), row(D_MODEL), row(D_MODEL)] + [_layer_spec(a.shape, idx) for a, idx in lw],
        out_specs=row(D_MODEL),
        out_shape=jax.ShapeDtypeStruct((m, D_MODEL), F32),
        compiler_params=_params(1),
        name="finish",
    )(x2d, gn, rg, swa_o, mem_o, g_r, g_s, g_m, *[a for a, _ in lw])


def _rope_tables(pos, reps=1):
    half = SWA_HD // 2
    inv = np.power(ROPE_THETA, -np.arange(half, dtype=np.float64) / half)
    ang = np.asarray(pos, np.float64)[:, None] * inv[None, :]
    c, s = np.cos(ang), np.sin(ang)
    cos_t, sin_t = np.concatenate([c, c, c, c], -1), np.concatenate([-s, s, -s, s], -1)
    return jnp.asarray(np.tile(cos_t, (reps, 1)), F32), jnp.asarray(np.tile(sin_t, (reps, 1)), F32)


def kernel(x_prompt, x_sample, state_ret, cache_swa_k, cache_swa_v, cache_mem_k, cache_mem_v, mem_prompt,
           w_in, w_br_ret, w_br_swa, w_br_mem, w_out, w_mem_kv, attn_sinks, ret_gn_g,
           ln1_g, ln1_b, w_up, w_down, ln2_g, ln2_b):
    batch, seq, _ = x_prompt.shape
    dec_b, dec_t, _ = x_sample.shape
    tm_p, tm_s = 512, 256
    tm_fin = 512
    ret_seqs = RET_CHUNK // dec_t

    cos_p, sin_p = _rope_tables(np.arange(seq))
    cos_s, sin_s = _rope_tables(PAST_LEN + np.arange(dec_t), tm_s // dec_t)
    tab_p2 = _pair_tables(_ret_tables(RET_CHUNK, RET_CHUNK))
    tab_s2 = _pair_tables(_ret_tables(RET_CHUNK, dec_t))

    xp = x_prompt.reshape(batch * seq, D_MODEL)
    xs = x_sample.reshape(dec_b * dec_t, D_MODEL)
    mem2d = mem_prompt.reshape(batch * N_MEM, D_MODEL)
    cache_kt = jnp.transpose(cache_swa_k, (0, 1, 3, 4, 2))
    cache_vt = jnp.transpose(cache_swa_v, (0, 1, 3, 4, 2))
    cache_mk = cache_mem_k.reshape(DEPTH, dec_b, N_MEM * MEM_HEADS, MEM_HD)
    cache_mv = cache_mem_v.reshape(DEPTH, dec_b, N_MEM * MEM_HEADS, MEM_HD)

    ln_row = lambda a: a.reshape(DEPTH, 1, D_MODEL)
    late_weights = (w_br_ret, w_br_swa, w_br_mem, w_out, w_up, w_down)
    next_weights = (w_in, w_mem_kv)
    w_in_bf, w_mem_kv_bf = (w_in[:1].astype(BF16), 0), (w_mem_kv[:1].astype(BF16), 0)

    ret_p, swk_p, swv_p, mk_p, mv_p = [], [], [], [], []
    stacked_s = None
    for l in range(DEPTH):
        sinks = attn_sinks[l]
        cast_jobs = [(w, l) for w in late_weights] + ([(w, l + 1) for w in next_weights] if l + 1 < DEPTH else [])
        proj = _inproj(xp, w_in_bf, cos_p, sin_p, tm_p, BF16, cast_jobs)
        rq, rk, rv, rg, sq, sk, sv, mq, g_r, g_s, g_m = proj[:N_INPROJ_OUT]
        s_rq, s_rk, s_rv, s_rg, s_sq, s_sk, s_sv, s_mq, s_g_r, s_g_s, s_g_m = _inproj(
            xs, w_in_bf, cos_s, sin_s, tm_s, F32)
        cast = [(w, 0) for w in proj[N_INPROJ_OUT:]]
        wr_bf, ws_bf, wm_bf, wo_bf, wu_bf, wd_bf = cast[:len(late_weights)]
        lw = [(ret_gn_g, l), wr_bf, ws_bf, wm_bf, wo_bf, (ln_row(ln1_g), l), (ln_row(ln1_b), l),
              wu_bf, wd_bf, (ln_row(ln2_g), l), (ln_row(ln2_b), l)]

        by_seq = lambda a: a.reshape(batch, seq, a.shape[-1])
        gn, swa_o, mem_o, s_p, s_mem_o, mk, mv, skt, svt = _mix_prompt(
            by_seq(rq), by_seq(rk), by_seq(rv), by_seq(sq), by_seq(sk), by_seq(sv), by_seq(mq),
            mem2d, w_mem_kv_bf, sinks, tab_p2, s_mq, cache_mk, cache_mv, l, dec_t)
        if l + 1 < DEPTH:
            w_in_bf, w_mem_kv_bf = cast[len(late_weights):]
        flat = lambda a: a.reshape(batch * seq, a.shape[-1])
        xp = _finish(xp, flat(gn), rg, flat(swa_o), flat(mem_o), g_r, g_s, g_m, lw, tm_fin)
        ret_p.append(s_p.reshape(batch, RET_HEADS, RET_DK, RET_DV))
        swk_p.append(skt)
        swv_p.append(svt)
        mk_p.append(mk.reshape(batch, N_MEM, MEM_HEADS, MEM_HD))
        mv_p.append(mv.reshape(batch, N_MEM, MEM_HEADS, MEM_HD))

        s_gn, s_swa_o, stacked_s = _mix_sample(s_rq, s_rk, s_rv, state_ret, tab_s2, s_sq, s_sk, s_sv, cache_kt, cache_vt,
                                               sinks, stacked_s, l, dec_t, ret_seqs)
        xs = _finish(xs, s_gn, s_rg, s_swa_o, s_mem_o, s_g_r, s_g_s, s_g_m, lw, tm_s)

    from_t = lambda a: jnp.transpose(a, (0, 1, 4, 2, 3))
    ret_s, swk_s, swv_s = stacked_s
    return (xp.reshape(batch, seq, D_MODEL), xs.reshape(dec_b, dec_t, D_MODEL),
            jnp.stack(ret_p), from_t(jnp.stack(swk_p)), from_t(jnp.stack(swv_p)), jnp.stack(mk_p), jnp.stack(mv_p),
            ret_s, from_t(swk_s), from_t(swv_s))
```
